```python
import math
import jax, jax.numpy as jnp
from jax import lax
import numpy as np

D_MODEL = 2048
BATCH = 2
SEQ = 4096
DEPTH = 1

MLA_HEADS = 8
MLA_NOPE = 128
MLA_ROPE = 64
MLA_V = 128
KV_RANK = 512
ROPE_THETA = 10000.0
MLA_WIDTH = MLA_HEADS * MLA_V

DIFF_HEADS = 8
DIFF_QK = 64
DIFF_V = 2 * DIFF_QK
DIFF_WIDTH = DIFF_HEADS * DIFF_V

Q_BLOCK = 128
EPS = 1e-6
NEG_INF = -1e30

SPLITS = (
    MLA_HEADS * (MLA_NOPE + MLA_ROPE),
    KV_RANK,
    MLA_ROPE,
    DIFF_HEADS * 2 * DIFF_QK,
    DIFF_HEADS * 2 * DIFF_QK,
    DIFF_WIDTH,
    MLA_WIDTH,
    DIFF_WIDTH,
    D_MODEL,
    D_MODEL,
)
IN_WIDTH = sum(SPLITS)

kernel_name = "hybrid_mla_diffattn_gated_block"


def rms_norm(x, g):
    xf = x.astype(jnp.float32)
    y = xf * lax.rsqrt(jnp.mean(xf * xf, axis=-1, keepdims=True) + EPS)
    return (y * g.astype(jnp.float32)).astype(x.dtype)


def rope(x, pos):
    half = x.shape[-1] // 2
    inv = ROPE_THETA ** (-jnp.arange(half, dtype=jnp.float32) / half)
    ang = pos.astype(jnp.float32)[..., None] * inv
    cos = jnp.cos(ang)[:, :, None, :]
    sin = jnp.sin(ang)[:, :, None, :]
    x1 = x[..., :half].astype(jnp.float32)
    x2 = x[..., half:].astype(jnp.float32)
    out = jnp.concatenate([x1 * cos - x2 * sin, x2 * cos + x1 * sin], axis=-1)
    return out.astype(x.dtype)


def mla_attention(q_nope, q_rope, k_nope, k_rope, v):
    B, S, H, dv = v.shape
    scale = (MLA_NOPE + MLA_ROPE) ** -0.5
    k_idx = jnp.arange(S)

    def block(i):
        start = i * Q_BLOCK
        qn = lax.dynamic_slice_in_dim(q_nope, start, Q_BLOCK, axis=1)
        qr = lax.dynamic_slice_in_dim(q_rope, start, Q_BLOCK, axis=1)
        s = (jnp.einsum('bqhd,bkhd->bhqk', qn, k_nope).astype(jnp.float32)
             + jnp.einsum('bqhd,bkd->bhqk', qr, k_rope).astype(jnp.float32)) * scale
        q_idx = start + jnp.arange(Q_BLOCK)
        causal = k_idx[None, :] <= q_idx[:, None]
        s = jnp.where(causal[None, None], s, NEG_INF)
        p = jax.nn.softmax(s, axis=-1)
        return jnp.einsum('bhqk,bkhd->bqhd', p.astype(v.dtype), v)

    out = lax.map(block, jnp.arange(S // Q_BLOCK))
    return out.transpose(1, 0, 2, 3, 4).reshape(B, S, H, dv)


def diff_attention(q, k, v, pos, lam):
    B, S, H, dv = v.shape
    scale = DIFF_QK ** -0.5
    slopes = 2.0 ** (-8.0 * jnp.arange(1, H + 1, dtype=jnp.float32) / H)
    posf = pos.astype(jnp.float32)
    k_idx = jnp.arange(S)

    def block(i):
        start = i * Q_BLOCK
        qb = lax.dynamic_slice_in_dim(q, start, Q_BLOCK, axis=1)
        pq = lax.dynamic_slice_in_dim(posf, start, Q_BLOCK, axis=1)
        s = jnp.einsum('bqhcd,bkhcd->bchqk', qb, k).astype(jnp.float32) * scale
        dist = jnp.abs(pq[:, :, None] - posf[:, None, :])
        s = s - slopes[None, None, :, None, None] * dist[:, None, None]
        q_idx = start + jnp.arange(Q_BLOCK)
        causal = k_idx[None, :] <= q_idx[:, None]
        s = jnp.where(causal[None, None, None], s, NEG_INF)
        p = jax.nn.softmax(s, axis=-1)
        a = p[:, 0] - lam * p[:, 1]
        return jnp.einsum('bhqk,bkhd->bqhd', a.astype(v.dtype), v)

    out = lax.map(block, jnp.arange(S // Q_BLOCK))
    return out.transpose(1, 0, 2, 3, 4).reshape(B, S, H, dv)


def setup_inputs(seed: int = 0) -> dict:
    key = jax.random.key(seed)
    ks = jax.random.split(key, 20)
    f32 = jnp.float32

    def w(k, shape, fan_in, mult=1.0):
        return jax.random.normal(k, shape, f32) * (mult * fan_in ** -0.5)

    def gain(k, shape):
        return 1.0 + 0.02 * jax.random.normal(k, shape, f32)

    x = jax.random.normal(ks[0], (BATCH, SEQ, D_MODEL), f32)
    c = jax.random.normal(ks[1], (BATCH, D_MODEL), f32)
    positions = jnp.broadcast_to(jnp.arange(SEQ, dtype=jnp.int32)[None, :], (BATCH, SEQ))
    return {
        "x": x,
        "c": c,
        "positions": positions,
        "w_ada": w(ks[2], (DEPTH, D_MODEL, 3 * D_MODEL), D_MODEL, 0.2),
        "b_ada": 0.01 * jax.random.normal(ks[3], (DEPTH, 3 * D_MODEL), f32),
        "g_pre": gain(ks[4], (DEPTH, D_MODEL)),
        "w_in": w(ks[5], (DEPTH, D_MODEL, IN_WIDTH), D_MODEL),
        "g_kv": gain(ks[6], (DEPTH, KV_RANK)),
        "w_ukv": w(ks[7], (DEPTH, KV_RANK, MLA_HEADS * (MLA_NOPE + MLA_V)), KV_RANK),
        "lambda_q1": 0.1 * jax.random.normal(ks[8], (DEPTH, DIFF_QK), f32),
        "lambda_k1": 0.1 * jax.random.normal(ks[9], (DEPTH, DIFF_QK), f32),
        "lambda_q2": 0.1 * jax.random.normal(ks[10], (DEPTH, DIFF_QK), f32),
        "lambda_k2": 0.1 * jax.random.normal(ks[11], (DEPTH, DIFF_QK), f32),
        "g_subln": gain(ks[12], (DEPTH, DIFF_V)),
        "w_o_mla": w(ks[13], (DEPTH, MLA_WIDTH, D_MODEL), MLA_WIDTH),
        "w_o_diff": w(ks[14], (DEPTH, DIFF_WIDTH, D_MODEL), DIFF_WIDTH),
        "w_out": w(ks[15], (DEPTH, D_MODEL, D_MODEL), D_MODEL),
        "g_post": gain(ks[16], (DEPTH, D_MODEL)),
    }


def reference(x, c, positions, w_ada, b_ada, g_pre, w_in, g_kv, w_ukv,
              lambda_q1, lambda_k1, lambda_q2, lambda_k2, g_subln,
              w_o_mla, w_o_diff, w_out, g_post):
    B, S, _ = x.shape
    split_points = []
    acc = 0
    for n in SPLITS[:-1]:
        acc += n
        split_points.append(acc)

    for l in range(DEPTH):
        lambda_init = 0.8 - 0.6 * math.exp(-0.3 * l)

        ada = c @ w_ada[l] + b_ada[l]
        shift, scale, gate = jnp.split(ada, 3, axis=-1)
        h = rms_norm(x, g_pre[l]) * (1.0 + scale[:, None]) + shift[:, None]

        proj = h @ w_in[l]
        (q_mla, c_kv, k_rope, q_diff, k_diff, v_diff,
         gate_mla, gate_diff, mg_mla, mg_diff) = jnp.split(proj, split_points, axis=-1)

        q_mla = q_mla.reshape(B, S, MLA_HEADS, MLA_NOPE + MLA_ROPE)
        q_nope = q_mla[..., :MLA_NOPE]
        q_rope = rope(q_mla[..., MLA_NOPE:], positions)
        kv = rms_norm(c_kv, g_kv[l]) @ w_ukv[l]
        kv = kv.reshape(B, S, MLA_HEADS, MLA_NOPE + MLA_V)
        k_nope = kv[..., :MLA_NOPE]
        v_mla = kv[..., MLA_NOPE:]
        k_rope = rope(k_rope[:, :, None, :], positions)[:, :, 0]
        o_mla = mla_attention(q_nope, q_rope, k_nope, k_rope, v_mla).reshape(B, S, MLA_WIDTH)
        y_mla = (o_mla * jax.nn.silu(gate_mla)) @ w_o_mla[l]

        lam = (jnp.exp(jnp.sum(lambda_q1[l].astype(jnp.float32) * lambda_k1[l].astype(jnp.float32)))
               - jnp.exp(jnp.sum(lambda_q2[l].astype(jnp.float32) * lambda_k2[l].astype(jnp.float32)))
               + lambda_init)
        qd = q_diff.reshape(B, S, DIFF_HEADS, 2, DIFF_QK)
        kd = k_diff.reshape(B, S, DIFF_HEADS, 2, DIFF_QK)
        vd = v_diff.reshape(B, S, DIFF_HEADS, DIFF_V)
        o_diff = diff_attention(qd, kd, vd, positions, lam)
        o_diff = (rms_norm(o_diff, g_subln[l]) * (1.0 - lambda_init)).reshape(B, S, DIFF_WIDTH)
        y_diff = (o_diff * jax.nn.silu(gate_diff)) @ w_o_diff[l]

        merged = jax.nn.sigmoid(mg_mla) * y_mla + jax.nn.sigmoid(mg_diff) * y_diff
        y = merged @ w_out[l]

        x = x + gate[:, None] * rms_norm(y, g_post[l])
    return x
```

```python
import functools
import math

import jax
import jax.numpy as jnp
from jax import lax
from jax.experimental import pallas as pl
from jax.experimental.pallas import tpu as pltpu

F32 = jnp.float32
BF16 = jnp.bfloat16

D_MODEL = 2048
MLA_HEADS = 8
MLA_NOPE = 128
MLA_ROPE = 64
MLA_V = 128
KV_RANK = 512
ROPE_THETA = 10000.0
DIFF_HEADS = 8
DIFF_QK = 64
DIFF_V = 128
EPS = 1e-6
NEG = -1e30
LAMBDA_INIT = 0.8 - 0.6 * math.exp(-0.3 * 0)

LANES = 128
SUBLANES = 8
VMEM_LIMIT = 56 * 1024 * 1024
ATTN_TQ = 512

Q_MLA_W = MLA_HEADS * (MLA_NOPE + MLA_ROPE)
KVR_W = KV_RANK + MLA_ROPE
KVR_PAD = 640
REST_OFF = Q_MLA_W + KVR_W
QD_OFF, KD_OFF, VD_OFF, GM_OFF, GD_OFF, MGM_OFF, MGD_OFF = 0, 1024, 2048, 3072, 4096, 5120, 7168
REST_W = 9216


def _cparams(sem):
    return pltpu.CompilerParams(dimension_semantics=sem, vmem_limit_bytes=VMEM_LIMIT)


def _ada_kernel(cb_ref, w_ref, b_ref, o_ref):
    k_dim, tn = w_ref.shape
    nb = cb_ref.shape[0]
    nchunk = tn // LANES

    def body(i, accs):
        k0 = pl.multiple_of(i * SUBLANES, SUBLANES)
        out = []
        for b in range(nb):
            cv = cb_ref[b, pl.ds(k0, SUBLANES), :]
            for j in range(nchunk):
                wv = w_ref[pl.ds(k0, SUBLANES), j * LANES:(j + 1) * LANES]
                out.append(accs[b * nchunk + j] + wv * cv)
        return tuple(out)

    init = tuple(jnp.zeros((SUBLANES, LANES), F32) for _ in range(nb * nchunk))
    accs = lax.fori_loop(0, k_dim // SUBLANES, body, init, unroll=8)
    for b in range(nb):
        row = jnp.concatenate(
            [jnp.sum(accs[b * nchunk + j], axis=0, keepdims=True) for j in range(nchunk)], axis=1)
        o_ref[b:b + 1, :] = row + b_ref[...]


def _ada(c, w, bias):
    nb, k_dim = c.shape
    n = w.shape[1]
    tn = 512
    cb = jnp.broadcast_to(c[:, :, None], (nb, k_dim, LANES))
    return pl.pallas_call(
        _ada_kernel,
        grid=(n // tn,),
        in_specs=[pl.BlockSpec((nb, k_dim, LANES), lambda j: (0, 0, 0)),
                  pl.BlockSpec((k_dim, tn), lambda j: (0, j)),
                  pl.BlockSpec((1, tn), lambda j: (0, j))],
        out_specs=pl.BlockSpec((nb, tn), lambda j: (0, j)),
        out_shape=jax.ShapeDtypeStruct((nb, n), F32),
        compiler_params=_cparams(("arbitrary",)),
        name="ada",
    )(cb, w, bias.reshape(1, n))


def _norm_kernel(x_ref, ada_ref, g_ref, pos_ref, inv_ref, h_ref, cos_ref, sin_ref):
    x = x_ref[0]
    ms = jnp.mean(x * x, axis=-1, keepdims=True)
    y = x * lax.rsqrt(ms + EPS) * g_ref[...]
    shift = ada_ref[0, 0:1, :]
    scale = ada_ref[0, 1:2, :]
    h_ref[...] = (y * (1.0 + scale) + shift).astype(BF16)
    ang = pos_ref[...].astype(F32) * inv_ref[...]
    lane = lax.broadcasted_iota(jnp.int32, ang.shape, 1)
    sign = jnp.where((lane % MLA_ROPE) < MLA_ROPE // 2, -1.0, 1.0).astype(F32)
    cos_ref[...] = jnp.cos(ang)
    sin_ref[...] = jnp.sin(ang) * sign


def _norm(x, ada3, g_pre, pos_col, inv_tab):
    nb, s, d = x.shape
    ts = 512
    ns = s // ts
    row = lambda b, i: (b * ns + i, 0)
    return pl.pallas_call(
        _norm_kernel,
        grid=(nb, ns),
        in_specs=[pl.BlockSpec((1, ts, d), lambda b, i: (b, i, 0)),
                  pl.BlockSpec((1, 3, d), lambda b, i: (b, 0, 0)),
                  pl.BlockSpec((1, d), lambda b, i: (0, 0)),
                  pl.BlockSpec((ts, 1), row),
                  pl.BlockSpec((1, LANES), lambda b, i: (0, 0))],
        out_specs=[pl.BlockSpec((ts, d), row),
                   pl.BlockSpec((ts, LANES), row),
                   pl.BlockSpec((ts, LANES), row)],
        out_shape=[jax.ShapeDtypeStruct((nb * s, d), BF16),
                   jax.ShapeDtypeStruct((nb * s, LANES), F32),
                   jax.ShapeDtypeStruct((nb * s, LANES), F32)],
        compiler_params=_cparams(("arbitrary", "arbitrary")),
        name="prenorm",
    )(x, ada3, g_pre, pos_col, inv_tab)


def _rope_cols(r, cos, sin_signed):
    lane = lax.broadcasted_iota(jnp.int32, r.shape, 1)
    half = MLA_ROPE // 2
    partner = jnp.where((lane % MLA_ROPE) < half,
                        pltpu.roll(r, LANES - half, 1), pltpu.roll(r, half, 1))
    return r * cos + partner * sin_signed


def _qmla_kernel(a_ref, w_ref, cos_ref, sin_ref, o_ref, *, scale):
    acc = jnp.dot(a_ref[...], w_ref[...], preferred_element_type=F32)
    rr = _rope_cols(acc[:, 2 * MLA_NOPE:], cos_ref[...], sin_ref[...])
    o_ref[:, :2 * MLA_NOPE] = (acc[:, :2 * MLA_NOPE] * scale).astype(BF16)
    o_ref[:, 2 * MLA_NOPE:] = (rr * scale).astype(BF16)


def _plain_mm_kernel(a_ref, w_ref, o_ref):
    o_ref[...] = jnp.dot(a_ref[...], w_ref[...], preferred_element_type=F32).astype(o_ref.dtype)


def _rest_kernel(a_ref, w_ref, o_ref, *, tn):
    j = pl.program_id(1)
    acc = jnp.dot(a_ref[...], w_ref[...], preferred_element_type=F32)

    @pl.when(j < GM_OFF // tn)
    def _():
        o_ref[...] = acc.astype(BF16)

    @pl.when(jnp.logical_and(j >= GM_OFF // tn, j < MGM_OFF // tn))
    def _():
        o_ref[...] = (acc * jax.nn.sigmoid(acc)).astype(BF16)

    @pl.when(j >= MGM_OFF // tn)
    def _():
        o_ref[...] = jax.nn.sigmoid(acc).astype(BF16)


def _proj(h, w_q, w_kvr, w_rest, cos_tab, sin_tab):
    m, k_dim = h.shape
    tm = 1024
    pair_w = 2 * (MLA_NOPE + MLA_ROPE)
    q_scale = (MLA_NOPE + MLA_ROPE) ** -0.5
    qm = pl.pallas_call(
        functools.partial(_qmla_kernel, scale=q_scale),
        grid=(m // tm, Q_MLA_W // pair_w),
        in_specs=[pl.BlockSpec((tm, k_dim), lambda i, j: (i, 0)),
                  pl.BlockSpec((k_dim, pair_w), lambda i, j: (0, j)),
                  pl.BlockSpec((tm, LANES), lambda i, j: (i, 0)),
                  pl.BlockSpec((tm, LANES), lambda i, j: (i, 0))],
        out_specs=pl.BlockSpec((tm, pair_w), lambda i, j: (i, j)),
        out_shape=jax.ShapeDtypeStruct((m, Q_MLA_W), BF16),
        compiler_params=_cparams(("arbitrary", "arbitrary")),
        name="proj_qmla",
    )(h, w_q, cos_tab, sin_tab)
    kvr = pl.pallas_call(
        _plain_mm_kernel,
        grid=(m // tm,),
        in_specs=[pl.BlockSpec((tm, k_dim), lambda i: (i, 0)),
                  pl.BlockSpec((k_dim, KVR_PAD), lambda i: (0, 0))],
        out_specs=pl.BlockSpec((tm, KVR_PAD), lambda i: (i, 0)),
        out_shape=jax.ShapeDtypeStruct((m, KVR_PAD), F32),
        compiler_params=_cparams(("arbitrary",)),
        name="proj_kvr",
    )(h, w_kvr)
    tn = 1024
    rest = pl.pallas_call(
        functools.partial(_rest_kernel, tn=tn),
        grid=(m // tm, REST_W // tn),
        in_specs=[pl.BlockSpec((tm, k_dim), lambda i, j: (i, 0)),
                  pl.BlockSpec((k_dim, tn), lambda i, j: (0, j))],
        out_specs=pl.BlockSpec((tm, tn), lambda i, j: (i, j)),
        out_shape=jax.ShapeDtypeStruct((m, REST_W), BF16),
        compiler_params=_cparams(("arbitrary", "arbitrary")),
        name="proj_rest",
    )(h, w_rest)
    return qm, kvr, rest


def _kv_kernel(p_ref, g_ref, wk_ref, wv_ref, cos_ref, sin_ref, k_ref, v_ref):
    p = p_ref[...]
    ckv = p[:, :KV_RANK]
    ms = jnp.mean(ckv * ckv, axis=-1, keepdims=True)
    n = (ckv * lax.rsqrt(ms + EPS) * g_ref[...]).astype(BF16)
    kn = jnp.dot(n, wk_ref[...], preferred_element_type=F32)
    vv = jnp.dot(n, wv_ref[...], preferred_element_type=F32)
    kr_even = _rope_cols(p[:, KV_RANK:], cos_ref[...], sin_ref[...])
    kr_odd = pltpu.roll(kr_even, MLA_ROPE, 1)
    lane = lax.broadcasted_iota(jnp.int32, kr_even.shape, 1)
    ones_col = jnp.where(lane == 0, 1.0, 0.0).astype(BF16)
    kw = 2 * MLA_NOPE
    for hd in range(MLA_HEADS):
        k_ref[:, hd * kw:hd * kw + MLA_NOPE] = kn[:, hd * MLA_NOPE:(hd + 1) * MLA_NOPE].astype(BF16)
        k_ref[:, hd * kw + MLA_NOPE:(hd + 1) * kw] = (kr_even if hd % 2 == 0 else kr_odd).astype(BF16)
        v_ref[:, hd * kw:hd * kw + MLA_V] = vv[:, hd * MLA_V:(hd + 1) * MLA_V].astype(BF16)
        v_ref[:, hd * kw + MLA_V:(hd + 1) * kw] = ones_col


def _kv(kvr, g_kv, w_kn, w_v, cos_tab, sin_tab):
    m = kvr.shape[0]
    tm = 512
    kw = MLA_HEADS * 2 * MLA_NOPE
    return pl.pallas_call(
        _kv_kernel,
        grid=(m // tm,),
        in_specs=[pl.BlockSpec((tm, KVR_PAD), lambda i: (i, 0)),
                  pl.BlockSpec((1, KV_RANK), lambda i: (0, 0)),
                  pl.BlockSpec((KV_RANK, MLA_HEADS * MLA_NOPE), lambda i: (0, 0)),
                  pl.BlockSpec((KV_RANK, MLA_HEADS * MLA_V), lambda i: (0, 0)),
                  pl.BlockSpec((tm, LANES), lambda i: (i, 0)),
                  pl.BlockSpec((tm, LANES), lambda i: (i, 0))],
        out_specs=[pl.BlockSpec((tm, kw), lambda i: (i, 0)),
                   pl.BlockSpec((tm, kw), lambda i: (i, 0))],
        out_shape=[jax.ShapeDtypeStruct((m, kw), BF16),
                   jax.ShapeDtypeStruct((m, kw), BF16)],
        compiler_params=_cparams(("arbitrary",)),
        name="kv_up",
    )(kvr, g_kv, w_kn, w_v, cos_tab, sin_tab)


def _online_step(s, m, acc_ref, v):
    m_new = jnp.maximum(m, jnp.max(s, axis=-1, keepdims=True))
    p = jnp.exp(s - m_new)
    alpha = jnp.exp(m - m_new)
    acc_ref[...] = alpha * acc_ref[...] + jnp.dot(p.astype(BF16), v, preferred_element_type=F32)
    return m_new


def _causal_mask(tq, tk):
    row = lax.broadcasted_iota(jnp.int32, (tq, tk), 0)
    col = lax.broadcasted_iota(jnp.int32, (tq, tk), 1)
    return col <= row


_NT = (((1,), (1,)), ((), ()))


def _mla_attn_kernel(q_ref, k_ref, v_ref, g_ref, o_ref, acc_ref, qs_ref, *, tq):
    qi = pl.program_id(2)
    kw = 2 * MLA_NOPE
    for hp in range(2):
        qs_ref[:, :MLA_NOPE] = q_ref[:, hp * MLA_NOPE:(hp + 1) * MLA_NOPE]
        qs_ref[:, MLA_NOPE:] = q_ref[:, 2 * MLA_NOPE:]
        q = qs_ref[...]
        acc_ref[...] = jnp.zeros_like(acc_ref)

        def body(kc, m, q=q, hp=hp):
            k0 = pl.multiple_of(kc * tq, tq)
            k = k_ref[pl.ds(k0, tq), hp * kw:(hp + 1) * kw]
            v = v_ref[pl.ds(k0, tq), hp * kw:(hp + 1) * kw]
            s = lax.dot_general(q, k, _NT, preferred_element_type=F32)
            return _online_step(s, m, acc_ref, v)

        m = lax.fori_loop(0, qi, body, jnp.full((tq, 1), NEG, F32))
        k0 = pl.multiple_of(qi * tq, tq)
        k = k_ref[pl.ds(k0, tq), hp * kw:(hp + 1) * kw]
        v = v_ref[pl.ds(k0, tq), hp * kw:(hp + 1) * kw]
        s = lax.dot_general(q, k, _NT, preferred_element_type=F32)
        s = jnp.where(_causal_mask(tq, tq), s, NEG)
        _online_step(s, m, acc_ref, v)
        acc = acc_ref[...]
        o = acc[:, :MLA_V] / acc[:, MLA_V:MLA_V + 1]
        gate = g_ref[:, hp * MLA_V:(hp + 1) * MLA_V].astype(F32)
        o_ref[:, hp * MLA_V:(hp + 1) * MLA_V] = (o * gate).astype(BF16)


def _mla_attn(qm, kk, vv, rest, nb, s):
    tq = ATTN_TQ
    nq = s // tq
    pair_w = 2 * (MLA_NOPE + MLA_ROPE)
    kw = 4 * MLA_NOPE
    gate_blk = GM_OFF // (2 * MLA_V)
    return pl.pallas_call(
        functools.partial(_mla_attn_kernel, tq=tq),
        grid=(nb, MLA_HEADS // 2, nq),
        in_specs=[pl.BlockSpec((tq, pair_w), lambda b, hh, i: (b * nq + i, hh)),
                  pl.BlockSpec((s, kw), lambda b, hh, i: (b, hh)),
                  pl.BlockSpec((s, kw), lambda b, hh, i: (b, hh)),
                  pl.BlockSpec((tq, 2 * MLA_V), lambda b, hh, i: (b * nq + i, gate_blk + hh))],
        out_specs=pl.BlockSpec((tq, 2 * MLA_V), lambda b, hh, i: (b * nq + i, hh)),
        out_shape=jax.ShapeDtypeStruct((nb * s, MLA_HEADS * MLA_V), BF16),
        scratch_shapes=[pltpu.VMEM((tq, 2 * MLA_V), F32), pltpu.VMEM((tq, 2 * MLA_NOPE), BF16)],
        compiler_params=_cparams(("arbitrary", "arbitrary", "arbitrary")),
        name="mla_attn",
    )(qm, kk, vv, rest)


def _diff_attn_kernel(q_ref, k_ref, v_ref, g_ref, pq_ref, pk_ref, sl_ref, lam_ref, gs_ref,
                      o_ref, vaug_ref, acc1_ref, acc2_ref, *, tq):
    qi = pl.program_id(2)

    @pl.when(qi == 0)
    def _():
        lane = lax.broadcasted_iota(jnp.int32, (vaug_ref.shape[1], DIFF_V), 1)
        ones_col = jnp.where(lane == 0, 1.0, 0.0).astype(BF16)
        for hp in range(2):
            vaug_ref[hp, :, :DIFF_V] = v_ref[:, hp * DIFF_V:(hp + 1) * DIFF_V]
            vaug_ref[hp, :, DIFF_V:] = ones_col

    lq = lam_ref[...]
    lam = (jnp.exp(jnp.sum(lq[0:1] * lq[1:2], axis=-1, keepdims=True))
           - jnp.exp(jnp.sum(lq[2:3] * lq[3:4], axis=-1, keepdims=True)) + LAMBDA_INIT)
    blk = q_ref[...]
    pq = pq_ref[...]
    lane_q = lax.broadcasted_iota(jnp.int32, (tq, 2 * DIFF_QK), 1)
    qk_scale = DIFF_QK ** -0.5
    for hp in range(2):
        q = blk[:, hp * 2 * DIFF_QK:(hp + 1) * 2 * DIFF_QK] * qk_scale
        q1 = jnp.where(lane_q < DIFF_QK, q, 0).astype(BF16)
        q2 = jnp.where(lane_q >= DIFF_QK, q, 0).astype(BF16)
        slope = sl_ref[0, hp:hp + 1, 0:1]
        aq = slope * pq
        acc1_ref[...] = jnp.zeros_like(acc1_ref)
        acc2_ref[...] = jnp.zeros_like(acc2_ref)

        def scores(kc, q1=q1, q2=q2, aq=aq, slope=slope, hp=hp):
            k0 = pl.multiple_of(kc * tq, tq)
            k = k_ref[pl.ds(k0, tq), hp * 2 * DIFF_QK:(hp + 1) * 2 * DIFF_QK]
            v = vaug_ref[hp, pl.ds(k0, tq), :]
            bias = jnp.abs(aq - slope * pk_ref[0, pl.ds(kc, 1), :])
            s1 = lax.dot_general(q1, k, _NT, preferred_element_type=F32) - bias
            s2 = lax.dot_general(q2, k, _NT, preferred_element_type=F32) - bias
            return s1, s2, v

        def body(kc, ms):
            s1, s2, v = scores(kc)
            return (_online_step(s1, ms[0], acc1_ref, v), _online_step(s2, ms[1], acc2_ref, v))

        m0 = jnp.full((tq, 1), NEG, F32)
        ms = lax.fori_loop(0, qi, body, (m0, m0))
        s1, s2, v = scores(qi)
        mask = _causal_mask(tq, tq)
        _online_step(jnp.where(mask, s1, NEG), ms[0], acc1_ref, v)
        _online_step(jnp.where(mask, s2, NEG), ms[1], acc2_ref, v)
        a1 = acc1_ref[...]
        a2 = acc2_ref[...]
        o = a1[:, :DIFF_V] / a1[:, DIFF_V:DIFF_V + 1] - lam * (a2[:, :DIFF_V] / a2[:, DIFF_V:DIFF_V + 1])
        ms_o = jnp.mean(o * o, axis=-1, keepdims=True)
        o = o * lax.rsqrt(ms_o + EPS) * gs_ref[...] * (1.0 - LAMBDA_INIT)
        gate = g_ref[:, hp * DIFF_V:(hp + 1) * DIFF_V].astype(F32)
        o_ref[:, hp * DIFF_V:(hp + 1) * DIFF_V] = (o * gate).astype(BF16)


def _diff_attn(rest, pos_col, pos_row, slopes, lam_par, g_subln, nb, s):
    tq = ATTN_TQ
    nq = s // tq
    pw = 2 * DIFF_V
    return pl.pallas_call(
        functools.partial(_diff_attn_kernel, tq=tq),
        grid=(nb, DIFF_HEADS // 2, nq),
        in_specs=[pl.BlockSpec((tq, pw), lambda b, hh, i: (b * nq + i, QD_OFF // pw + hh)),
                  pl.BlockSpec((s, pw), lambda b, hh, i: (b, KD_OFF // pw + hh)),
                  pl.BlockSpec((s, pw), lambda b, hh, i: (b, VD_OFF // pw + hh)),
                  pl.BlockSpec((tq, pw), lambda b, hh, i: (b * nq + i, GD_OFF // pw + hh)),
                  pl.BlockSpec((tq, 1), lambda b, hh, i: (b * nq + i, 0)),
                  pl.BlockSpec((1, nq, tq), lambda b, hh, i: (b, 0, 0)),
                  pl.BlockSpec((1, 2, LANES), lambda b, hh, i: (hh, 0, 0)),
                  pl.BlockSpec((4, DIFF_QK), lambda b, hh, i: (0, 0)),
                  pl.BlockSpec((1, DIFF_V), lambda b, hh, i: (0, 0))],
        out_specs=pl.BlockSpec((tq, pw), lambda b, hh, i: (b * nq + i, hh)),
        out_shape=jax.ShapeDtypeStruct((nb * s, DIFF_HEADS * DIFF_V), BF16),
        scratch_shapes=[pltpu.VMEM((2, s, 2 * DIFF_V), BF16),
                        pltpu.VMEM((tq, 2 * DIFF_V), F32),
                        pltpu.VMEM((tq, 2 * DIFF_V), F32)],
        compiler_params=_cparams(("arbitrary", "arbitrary", "arbitrary")),
        name="diff_attn",
    )(rest, rest, rest, rest, pos_col, pos_row, slopes, lam_par, g_subln)


def _merge_kernel(a1_ref, a2_ref, w1_ref, w2_ref, s1_ref, s2_ref, o_ref):
    y1 = jnp.dot(a1_ref[...], w1_ref[...], preferred_element_type=F32)
    y2 = jnp.dot(a2_ref[...], w2_ref[...], preferred_element_type=F32)
    o_ref[...] = (s1_ref[...].astype(F32) * y1 + s2_ref[...].astype(F32) * y2).astype(BF16)


def _merge(og_mla, og_diff, w1, w2, rest):
    m, k_dim = og_mla.shape
    n = w1.shape[1]
    tm, tn = 1024, 1024
    return pl.pallas_call(
        _merge_kernel,
        grid=(m // tm, n // tn),
        in_specs=[pl.BlockSpec((tm, k_dim), lambda i, j: (i, 0)),
                  pl.BlockSpec((tm, k_dim), lambda i, j: (i, 0)),
                  pl.BlockSpec((k_dim, tn), lambda i, j: (0, j)),
                  pl.BlockSpec((k_dim, tn), lambda i, j: (0, j)),
                  pl.BlockSpec((tm, tn), lambda i, j: (i, MGM_OFF // tn + j)),
                  pl.BlockSpec((tm, tn), lambda i, j: (i, MGD_OFF // tn + j))],
        out_specs=pl.BlockSpec((tm, tn), lambda i, j: (i, j)),
        out_shape=jax.ShapeDtypeStruct((m, n), BF16),
        compiler_params=_cparams(("arbitrary", "arbitrary")),
        name="merge",
    )(og_mla, og_diff, w1, w2, rest, rest)


def _out_kernel(a_ref, w_ref, x_ref, ada_ref, g_ref, o_ref):
    y = jnp.dot(a_ref[...], w_ref[...], preferred_element_type=F32)
    ms = jnp.mean(y * y, axis=-1, keepdims=True)
    yn = y * lax.rsqrt(ms + EPS) * g_ref[...]
    o_ref[0] = x_ref[0] + ada_ref[0, 2:3, :] * yn


def _out(merged, w_out, x, ada3, g_post):
    nb, s, d = x.shape
    tm = 512
    ns = s // tm
    return pl.pallas_call(
        _out_kernel,
        grid=(nb, ns),
        in_specs=[pl.BlockSpec((tm, d), lambda b, i: (b * ns + i, 0)),
                  pl.BlockSpec((d, d), lambda b, i: (0, 0)),
                  pl.BlockSpec((1, tm, d), lambda b, i: (b, i, 0)),
                  pl.BlockSpec((1, 3, d), lambda b, i: (b, 0, 0)),
                  pl.BlockSpec((1, d), lambda b, i: (0, 0))],
        out_specs=pl.BlockSpec((1, tm, d), lambda b, i: (b, i, 0)),
        out_shape=jax.ShapeDtypeStruct((nb, s, d), F32),
        compiler_params=_cparams(("arbitrary", "arbitrary")),
        name="out_proj",
    )(merged, w_out, x, ada3, g_post)


def _prep_weights(w_in, w_ukv):
    qcols = w_in[:, :Q_MLA_W].reshape(D_MODEL, MLA_HEADS // 2, 2, MLA_NOPE + MLA_ROPE)
    nope = qcols[..., :MLA_NOPE].reshape(D_MODEL, MLA_HEADS // 2, 2 * MLA_NOPE)
    rope = qcols[..., MLA_NOPE:].reshape(D_MODEL, MLA_HEADS // 2, 2 * MLA_ROPE)
    w_q = jnp.concatenate([nope, rope], axis=-1).reshape(D_MODEL, Q_MLA_W).astype(BF16)
    w_kvr = jnp.pad(w_in[:, Q_MLA_W:REST_OFF], ((0, 0), (0, KVR_PAD - KVR_W))).astype(BF16)
    w_rest = w_in[:, REST_OFF:].astype(BF16)
    ukv = w_ukv.reshape(KV_RANK, MLA_HEADS, MLA_NOPE + MLA_V)
    w_kn = ukv[..., :MLA_NOPE].reshape(KV_RANK, MLA_HEADS * MLA_NOPE).astype(BF16)
    w_v = ukv[..., MLA_NOPE:].reshape(KV_RANK, MLA_HEADS * MLA_V).astype(BF16)
    return w_q, w_kvr, w_rest, w_kn, w_v


def kernel(x, c, positions, w_ada, b_ada, g_pre, w_in, g_kv, w_ukv, lambda_q1, lambda_k1,
           lambda_q2, lambda_k2, g_subln, w_o_mla, w_o_diff, w_out, g_post):
    nb, s, d = x.shape
    depth = w_in.shape[0]
    half = MLA_ROPE // 2
    inv = ROPE_THETA ** (-jnp.arange(half, dtype=F32) / half)
    inv_tab = jnp.tile(inv, LANES // half).reshape(1, LANES)
    slopes = 2.0 ** (-8.0 * jnp.arange(1, DIFF_HEADS + 1, dtype=F32) / DIFF_HEADS)
    slopes = jnp.broadcast_to(slopes.reshape(DIFF_HEADS // 2, 2, 1), (DIFF_HEADS // 2, 2, LANES))
    pos_col = positions.reshape(nb * s, 1)
    pos_colf = pos_col.astype(F32)
    pos_row = positions.astype(F32).reshape(nb, s // ATTN_TQ, ATTN_TQ)

    for l in range(depth):
        w_q, w_kvr, w_rest, w_kn, w_v = _prep_weights(w_in[l], w_ukv[l])
        ada3 = _ada(c, w_ada[l], b_ada[l]).reshape(nb, 3, d)
        h, cos_tab, sin_tab = _norm(x, ada3, g_pre[l].reshape(1, d), pos_col, inv_tab)
        qm, kvr, rest = _proj(h, w_q, w_kvr, w_rest, cos_tab, sin_tab)
        kk, vv = _kv(kvr, g_kv[l].reshape(1, KV_RANK), w_kn, w_v, cos_tab, sin_tab)
        og_mla = _mla_attn(qm, kk, vv, rest, nb, s)
        lam_par = jnp.stack([lambda_q1[l], lambda_k1[l], lambda_q2[l], lambda_k2[l]]).astype(F32)
        og_diff = _diff_attn(rest, pos_colf, pos_row, slopes, lam_par,
                             g_subln[l].reshape(1, DIFF_V), nb, s)
        merged = _merge(og_mla, og_diff, w_o_mla[l].astype(BF16), w_o_diff[l].astype(BF16), rest)
        x = _out(merged, w_out[l].astype(BF16), x, ada3, g_post[l].reshape(1, d))
    return x
```

```python
import functools
import math

import jax
import jax.numpy as jnp
from jax import lax
from jax.experimental import pallas as pl
from jax.experimental.pallas import tpu as pltpu

F32 = jnp.float32
BF16 = jnp.bfloat16

D_MODEL = 2048
MLA_HEADS = 8
MLA_NOPE = 128
MLA_ROPE = 64
MLA_V = 128
KV_RANK = 512
ROPE_THETA = 10000.0
DIFF_HEADS = 8
DIFF_QK = 64
DIFF_V = 128
EPS = 1e-6
NEG = -1e30
LAMBDA_INIT = 0.8 - 0.6 * math.exp(-0.3 * 0)

LANES = 128
SUBLANES = 8
VMEM_LIMIT = 56 * 1024 * 1024
ATTN_TQ = 512
MLA_ROW_SPLIT = 2
LOG2E = math.log2(math.e)

Q_MLA_W = MLA_HEADS * (MLA_NOPE + MLA_ROPE)
KVR_W = KV_RANK + MLA_ROPE
KVR_PAD = 640
REST_OFF = Q_MLA_W + KVR_W
QD_OFF, KD_OFF, VD_OFF, GM_OFF, GD_OFF, MGM_OFF, MGD_OFF = 0, 1024, 2048, 3072, 4096, 5120, 7168
REST_W = 9216


def _cparams(sem):
    return pltpu.CompilerParams(dimension_semantics=sem, vmem_limit_bytes=VMEM_LIMIT)


def _ada_kernel(cb_ref, w_ref, b_ref, o_ref):
    k_dim, tn = w_ref.shape
    nb = cb_ref.shape[0]
    nchunk = tn // LANES

    def body(i, accs):
        k0 = pl.multiple_of(i * SUBLANES, SUBLANES)
        out = []
        for b in range(nb):
            cv = cb_ref[b, pl.ds(k0, SUBLANES), :]
            for j in range(nchunk):
                wv = w_ref[pl.ds(k0, SUBLANES), j * LANES:(j + 1) * LANES]
                out.append(accs[b * nchunk + j] + wv * cv)
        return tuple(out)

    init = tuple(jnp.zeros((SUBLANES, LANES), F32) for _ in range(nb * nchunk))
    accs = lax.fori_loop(0, k_dim // SUBLANES, body, init, unroll=8)
    for b in range(nb):
        row = jnp.concatenate(
            [jnp.sum(accs[b * nchunk + j], axis=0, keepdims=True) for j in range(nchunk)], axis=1)
        o_ref[b:b + 1, :] = row + b_ref[...]


def _ada(c, w, bias):
    nb, k_dim = c.shape
    n = w.shape[1]
    tn = 512
    cb = jnp.broadcast_to(c[:, :, None], (nb, k_dim, LANES))
    return pl.pallas_call(
        _ada_kernel,
        grid=(n // tn,),
        in_specs=[pl.BlockSpec((nb, k_dim, LANES), lambda j: (0, 0, 0)),
                  pl.BlockSpec((k_dim, tn), lambda j: (0, j)),
                  pl.BlockSpec((1, tn), lambda j: (0, j))],
        out_specs=pl.BlockSpec((nb, tn), lambda j: (0, j)),
        out_shape=jax.ShapeDtypeStruct((nb, n), F32),
        compiler_params=_cparams(("arbitrary",)),
        name="ada",
    )(cb, w, bias.reshape(1, n))


def _norm_kernel(x_ref, ada_ref, g_ref, pos_ref, inv_ref, h_ref, cos_ref, sin_ref):
    x = x_ref[0]
    ms = jnp.mean(x * x, axis=-1, keepdims=True)
    y = x * lax.rsqrt(ms + EPS) * g_ref[...]
    shift = ada_ref[0, 0:1, :]
    scale = ada_ref[0, 1:2, :]
    h_ref[...] = (y * (1.0 + scale) + shift).astype(BF16)
    ang = pos_ref[...].astype(F32) * inv_ref[...]
    lane = lax.broadcasted_iota(jnp.int32, ang.shape, 1)
    sign = jnp.where((lane % MLA_ROPE) < MLA_ROPE // 2, -1.0, 1.0).astype(F32)
    cos_ref[...] = jnp.cos(ang)
    sin_ref[...] = jnp.sin(ang) * sign


def _norm(x, ada3, g_pre, pos_col, inv_tab):
    nb, s, d = x.shape
    ts = 512
    ns = s // ts
    row = lambda b, i: (b * ns + i, 0)
    return pl.pallas_call(
        _norm_kernel,
        grid=(nb, ns),
        in_specs=[pl.BlockSpec((1, ts, d), lambda b, i: (b, i, 0)),
                  pl.BlockSpec((1, 3, d), lambda b, i: (b, 0, 0)),
                  pl.BlockSpec((1, d), lambda b, i: (0, 0)),
                  pl.BlockSpec((ts, 1), row),
                  pl.BlockSpec((1, LANES), lambda b, i: (0, 0))],
        out_specs=[pl.BlockSpec((ts, d), row),
                   pl.BlockSpec((ts, LANES), row),
                   pl.BlockSpec((ts, LANES), row)],
        out_shape=[jax.ShapeDtypeStruct((nb * s, d), BF16),
                   jax.ShapeDtypeStruct((nb * s, LANES), F32),
                   jax.ShapeDtypeStruct((nb * s, LANES), F32)],
        compiler_params=_cparams(("arbitrary", "arbitrary")),
        name="prenorm",
    )(x, ada3, g_pre, pos_col, inv_tab)


def _rope_cols(r, cos, sin_signed):
    lane = lax.broadcasted_iota(jnp.int32, r.shape, 1)
    half = MLA_ROPE // 2
    partner = jnp.where((lane % MLA_ROPE) < half,
                        pltpu.roll(r, LANES - half, 1), pltpu.roll(r, half, 1))
    return r * cos + partner * sin_signed


def _qmla_kernel(a_ref, w_ref, cos_ref, sin_ref, o_ref, *, scale):
    acc = jnp.dot(a_ref[...], w_ref[...], preferred_element_type=F32)
    rr = _rope_cols(acc[:, 2 * MLA_NOPE:], cos_ref[...], sin_ref[...])
    o_ref[:, :2 * MLA_NOPE] = (acc[:, :2 * MLA_NOPE] * scale).astype(BF16)
    o_ref[:, 2 * MLA_NOPE:] = (rr * scale).astype(BF16)


def _plain_mm_kernel(a_ref, w_ref, o_ref):
    o_ref[...] = jnp.dot(a_ref[...], w_ref[...], preferred_element_type=F32).astype(o_ref.dtype)


def _rest_kernel(a_ref, w_ref, o_ref, *, tn):
    j = pl.program_id(1)
    acc = jnp.dot(a_ref[...], w_ref[...], preferred_element_type=F32)

    @pl.when(j < KD_OFF // tn)
    def _():
        o_ref[...] = (acc * (DIFF_QK ** -0.5 * LOG2E)).astype(BF16)

    @pl.when(jnp.logical_and(j >= KD_OFF // tn, j < GM_OFF // tn))
    def _():
        o_ref[...] = acc.astype(BF16)

    @pl.when(jnp.logical_and(j >= GM_OFF // tn, j < MGM_OFF // tn))
    def _():
        o_ref[...] = (acc * jax.nn.sigmoid(acc)).astype(BF16)

    @pl.when(j >= MGM_OFF // tn)
    def _():
        o_ref[...] = jax.nn.sigmoid(acc).astype(BF16)


def _proj(h, w_q, w_kvr, w_rest, cos_tab, sin_tab):
    m, k_dim = h.shape
    tm = 1024
    pair_w = 2 * (MLA_NOPE + MLA_ROPE)
    q_scale = (MLA_NOPE + MLA_ROPE) ** -0.5 * LOG2E
    qm = pl.pallas_call(
        functools.partial(_qmla_kernel, scale=q_scale),
        grid=(m // tm, Q_MLA_W // pair_w),
        in_specs=[pl.BlockSpec((tm, k_dim), lambda i, j: (i, 0)),
                  pl.BlockSpec((k_dim, pair_w), lambda i, j: (0, j)),
                  pl.BlockSpec((tm, LANES), lambda i, j: (i, 0)),
                  pl.BlockSpec((tm, LANES), lambda i, j: (i, 0))],
        out_specs=pl.BlockSpec((tm, pair_w), lambda i, j: (i, j)),
        out_shape=jax.ShapeDtypeStruct((m, Q_MLA_W), BF16),
        compiler_params=_cparams(("arbitrary", "arbitrary")),
        name="proj_qmla",
    )(h, w_q, cos_tab, sin_tab)
    kvr = pl.pallas_call(
        _plain_mm_kernel,
        grid=(m // tm,),
        in_specs=[pl.BlockSpec((tm, k_dim), lambda i: (i, 0)),
                  pl.BlockSpec((k_dim, KVR_PAD), lambda i: (0, 0))],
        out_specs=pl.BlockSpec((tm, KVR_PAD), lambda i: (i, 0)),
        out_shape=jax.ShapeDtypeStruct((m, KVR_PAD), F32),
        compiler_params=_cparams(("arbitrary",)),
        name="proj_kvr",
    )(h, w_kvr)
    tn = 1024
    rest = pl.pallas_call(
        functools.partial(_rest_kernel, tn=tn),
        grid=(m // tm, REST_W // tn),
        in_specs=[pl.BlockSpec((tm, k_dim), lambda i, j: (i, 0)),
                  pl.BlockSpec((k_dim, tn), lambda i, j: (0, j))],
        out_specs=pl.BlockSpec((tm, tn), lambda i, j: (i, j)),
        out_shape=jax.ShapeDtypeStruct((m, REST_W), BF16),
        compiler_params=_cparams(("arbitrary", "arbitrary")),
        name="proj_rest",
    )(h, w_rest)
    return qm, kvr, rest


def _kv_kernel(p_ref, g_ref, wk_ref, wv_ref, cos_ref, sin_ref, k_ref, v_ref):
    p = p_ref[...]
    ckv = p[:, :KV_RANK]
    ms = jnp.mean(ckv * ckv, axis=-1, keepdims=True)
    n = (ckv * lax.rsqrt(ms + EPS) * g_ref[...]).astype(BF16)
    kn = jnp.dot(n, wk_ref[...], preferred_element_type=F32)
    vv = jnp.dot(n, wv_ref[...], preferred_element_type=F32)
    kr_even = _rope_cols(p[:, KV_RANK:], cos_ref[...], sin_ref[...])
    kr_odd = pltpu.roll(kr_even, MLA_ROPE, 1)
    lane = lax.broadcasted_iota(jnp.int32, kr_even.shape, 1)
    ones_col = jnp.where(lane == 0, 1.0, 0.0).astype(BF16)
    kw = 2 * MLA_NOPE
    for hd in range(MLA_HEADS):
        k_ref[:, hd * kw:hd * kw + MLA_NOPE] = kn[:, hd * MLA_NOPE:(hd + 1) * MLA_NOPE].astype(BF16)
        k_ref[:, hd * kw + MLA_NOPE:(hd + 1) * kw] = (kr_even if hd % 2 == 0 else kr_odd).astype(BF16)
        v_ref[:, hd * kw:hd * kw + MLA_V] = vv[:, hd * MLA_V:(hd + 1) * MLA_V].astype(BF16)
        v_ref[:, hd * kw + MLA_V:(hd + 1) * kw] = ones_col


def _kv(kvr, g_kv, w_kn, w_v, cos_tab, sin_tab):
    m = kvr.shape[0]
    tm = 512
    kw = MLA_HEADS * 2 * MLA_NOPE
    return pl.pallas_call(
        _kv_kernel,
        grid=(m // tm,),
        in_specs=[pl.BlockSpec((tm, KVR_PAD), lambda i: (i, 0)),
                  pl.BlockSpec((1, KV_RANK), lambda i: (0, 0)),
                  pl.BlockSpec((KV_RANK, MLA_HEADS * MLA_NOPE), lambda i: (0, 0)),
                  pl.BlockSpec((KV_RANK, MLA_HEADS * MLA_V), lambda i: (0, 0)),
                  pl.BlockSpec((tm, LANES), lambda i: (i, 0)),
                  pl.BlockSpec((tm, LANES), lambda i: (i, 0))],
        out_specs=[pl.BlockSpec((tm, kw), lambda i: (i, 0)),
                   pl.BlockSpec((tm, kw), lambda i: (i, 0))],
        out_shape=[jax.ShapeDtypeStruct((m, kw), BF16),
                   jax.ShapeDtypeStruct((m, kw), BF16)],
        compiler_params=_cparams(("arbitrary",)),
        name="kv_up",
    )(kvr, g_kv, w_kn, w_v, cos_tab, sin_tab)


def _online_step(s, m, acc_ref, v):
    m_new = jnp.maximum(m, jnp.max(s, axis=-1, keepdims=True))
    p = jnp.exp2(s - m_new)
    alpha = jnp.exp2(m - m_new)
    acc_ref[...] = alpha * acc_ref[...] + jnp.dot(p.astype(BF16), v, preferred_element_type=F32)
    return m_new


def _causal_mask(rows, cols, row0):
    row = lax.broadcasted_iota(jnp.int32, (rows, cols), 0) + row0
    col = lax.broadcasted_iota(jnp.int32, (rows, cols), 1)
    return col <= row


_NT = (((1,), (1,)), ((), ()))


def _mla_attn_kernel(q_ref, k_ref, v_ref, g_ref, o_ref, acc_ref, qs_ref, *, tq, rs):
    qi = pl.program_id(2)
    kw = 2 * MLA_NOPE
    tr = tq // rs
    for hp in range(2):
        qs_ref[hp, :, :MLA_NOPE] = q_ref[:, hp * MLA_NOPE:(hp + 1) * MLA_NOPE]
        qs_ref[hp, :, MLA_NOPE:] = q_ref[:, 2 * MLA_NOPE:]
    acc_ref[...] = jnp.zeros_like(acc_ref)
    chains = [(hp, r) for hp in range(2) for r in range(rs)]

    def step(kc, ms, diag):
        k0 = pl.multiple_of(kc * tq, tq)
        out = []
        for ci, (hp, r) in enumerate(chains):
            nk = (r + 1) * tr if diag else tq
            q = qs_ref[hp, r * tr:(r + 1) * tr, :]
            k = k_ref[pl.ds(k0, nk), hp * kw:(hp + 1) * kw]
            v = v_ref[pl.ds(k0, nk), hp * kw:(hp + 1) * kw]
            s = lax.dot_general(q, k, _NT, preferred_element_type=F32)
            if diag:
                s = jnp.where(_causal_mask(tr, nk, r * tr), s, NEG)
            out.append(_online_step(s, ms[ci], acc_ref.at[ci], v))
        return tuple(out)

    m0 = tuple(jnp.full((tr, 1), NEG, F32) for _ in chains)
    ms = lax.fori_loop(0, qi, lambda kc, ms: step(kc, ms, False), m0)
    step(qi, ms, True)
    for ci, (hp, r) in enumerate(chains):
        acc = acc_ref[ci]
        o = acc[:, :MLA_V] / acc[:, MLA_V:MLA_V + 1]
        gate = g_ref[r * tr:(r + 1) * tr, hp * MLA_V:(hp + 1) * MLA_V].astype(F32)
        o_ref[r * tr:(r + 1) * tr, hp * MLA_V:(hp + 1) * MLA_V] = (o * gate).astype(BF16)


def _mla_attn(qm, kk, vv, rest, nb, s):
    tq = ATTN_TQ
    nq = s // tq
    pair_w = 2 * (MLA_NOPE + MLA_ROPE)
    kw = 4 * MLA_NOPE
    gate_blk = GM_OFF // (2 * MLA_V)
    rs = MLA_ROW_SPLIT
    return pl.pallas_call(
        functools.partial(_mla_attn_kernel, tq=tq, rs=rs),
        grid=(nb, MLA_HEADS // 2, nq),
        in_specs=[pl.BlockSpec((tq, pair_w), lambda b, hh, i: (b * nq + i, hh)),
                  pl.BlockSpec((s, kw), lambda b, hh, i: (b, hh)),
                  pl.BlockSpec((s, kw), lambda b, hh, i: (b, hh)),
                  pl.BlockSpec((tq, 2 * MLA_V), lambda b, hh, i: (b * nq + i, gate_blk + hh))],
        out_specs=pl.BlockSpec((tq, 2 * MLA_V), lambda b, hh, i: (b * nq + i, hh)),
        out_shape=jax.ShapeDtypeStruct((nb * s, MLA_HEADS * MLA_V), BF16),
        scratch_shapes=[pltpu.VMEM((2 * rs, tq // rs, 2 * MLA_V), F32),
                        pltpu.VMEM((2, tq, 2 * MLA_NOPE), BF16)],
        compiler_params=_cparams(("arbitrary", "arbitrary", "arbitrary")),
        name="mla_attn",
    )(qm, kk, vv, rest)


def _diff_attn_kernel(q_ref, k_ref, v_ref, g_ref, pq_ref, pk_ref, sl_ref, lam_ref, gs_ref,
                      o_ref, vaug_ref, acc_ref, qz_ref, *, tq):
    qi = pl.program_id(2)

    @pl.when(qi == 0)
    def _():
        lane = lax.broadcasted_iota(jnp.int32, (vaug_ref.shape[1], DIFF_V), 1)
        ones_col = jnp.where(lane == 0, 1.0, 0.0).astype(BF16)
        for hp in range(2):
            vaug_ref[hp, :, :DIFF_V] = v_ref[:, hp * DIFF_V:(hp + 1) * DIFF_V]
            vaug_ref[hp, :, DIFF_V:] = ones_col

    lq = lam_ref[...]
    lam = (jnp.exp(jnp.sum(lq[0:1] * lq[1:2], axis=-1, keepdims=True))
           - jnp.exp(jnp.sum(lq[2:3] * lq[3:4], axis=-1, keepdims=True)) + LAMBDA_INIT)
    pq = pq_ref[...]
    lane_q = lax.broadcasted_iota(jnp.int32, (tq, 2 * DIFF_QK), 1)
    slopes2 = []
    aqs = []
    for hp in range(2):
        q = q_ref[:, hp * 2 * DIFF_QK:(hp + 1) * 2 * DIFF_QK]
        qz_ref[2 * hp] = jnp.where(lane_q < DIFF_QK, q, 0).astype(BF16)
        qz_ref[2 * hp + 1] = jnp.where(lane_q >= DIFF_QK, q, 0).astype(BF16)
        slopes2.append(sl_ref[0, hp:hp + 1, 0:1] * LOG2E)
        aqs.append(slopes2[hp] * pq)
    acc_ref[...] = jnp.zeros_like(acc_ref)

    def step(kc, ms, diag):
        k0 = pl.multiple_of(kc * tq, tq)
        pk = pk_ref[0, pl.ds(kc, 1), :]
        mask = _causal_mask(tq, tq, 0) if diag else None
        out = []
        for hp in range(2):
            k = k_ref[pl.ds(k0, tq), hp * 2 * DIFF_QK:(hp + 1) * 2 * DIFF_QK]
            v = vaug_ref[hp, pl.ds(k0, tq), :]
            bias = jnp.abs(aqs[hp] - slopes2[hp] * pk)
            for c in range(2):
                ci = 2 * hp + c
                s = lax.dot_general(qz_ref[ci], k, _NT, preferred_element_type=F32) - bias
                if diag:
                    s = jnp.where(mask, s, NEG)
                out.append(_online_step(s, ms[ci], acc_ref.at[ci], v))
        return tuple(out)

    m0 = tuple(jnp.full((tq, 1), NEG, F32) for _ in range(4))
    ms = lax.fori_loop(0, qi, lambda kc, ms: step(kc, ms, False), m0)
    step(qi, ms, True)
    for hp in range(2):
        a1 = acc_ref[2 * hp]
        a2 = acc_ref[2 * hp + 1]
        o = a1[:, :DIFF_V] / a1[:, DIFF_V:DIFF_V + 1] - lam * (a2[:, :DIFF_V] / a2[:, DIFF_V:DIFF_V + 1])
        ms_o = jnp.mean(o * o, axis=-1, keepdims=True)
        o = o * lax.rsqrt(ms_o + EPS) * gs_ref[...] * (1.0 - LAMBDA_INIT)
        gate = g_ref[:, hp * DIFF_V:(hp + 1) * DIFF_V].astype(F32)
        o_ref[:, hp * DIFF_V:(hp + 1) * DIFF_V] = (o * gate).astype(BF16)


def _diff_attn(rest, pos_col, pos_row, slopes, lam_par, g_subln, nb, s):
    tq = ATTN_TQ
    nq = s // tq
    pw = 2 * DIFF_V
    return pl.pallas_call(
        functools.partial(_diff_attn_kernel, tq=tq),
        grid=(nb, DIFF_HEADS // 2, nq),
        in_specs=[pl.BlockSpec((tq, pw), lambda b, hh, i: (b * nq + i, QD_OFF // pw + hh)),
                  pl.BlockSpec((s, pw), lambda b, hh, i: (b, KD_OFF // pw + hh)),
                  pl.BlockSpec((s, pw), lambda b, hh, i: (b, VD_OFF // pw + hh)),
                  pl.BlockSpec((tq, pw), lambda b, hh, i: (b * nq + i, GD_OFF // pw + hh)),
                  pl.BlockSpec((tq, 1), lambda b, hh, i: (b * nq + i, 0)),
                  pl.BlockSpec((1, nq, tq), lambda b, hh, i: (b, 0, 0)),
                  pl.BlockSpec((1, 2, LANES), lambda b, hh, i: (hh, 0, 0)),
                  pl.BlockSpec((4, DIFF_QK), lambda b, hh, i: (0, 0)),
                  pl.BlockSpec((1, DIFF_V), lambda b, hh, i: (0, 0))],
        out_specs=pl.BlockSpec((tq, pw), lambda b, hh, i: (b * nq + i, hh)),
        out_shape=jax.ShapeDtypeStruct((nb * s, DIFF_HEADS * DIFF_V), BF16),
        scratch_shapes=[pltpu.VMEM((2, s, 2 * DIFF_V), BF16),
                        pltpu.VMEM((4, tq, 2 * DIFF_V), F32),
                        pltpu.VMEM((4, tq, 2 * DIFF_QK), BF16)],
        compiler_params=_cparams(("arbitrary", "arbitrary", "arbitrary")),
        name="diff_attn",
    )(rest, rest, rest, rest, pos_col, pos_row, slopes, lam_par, g_subln)


def _merge_kernel(a1_ref, a2_ref, w1_ref, w2_ref, s1_ref, s2_ref, o_ref):
    y1 = jnp.dot(a1_ref[...], w1_ref[...], preferred_element_type=F32)
    y2 = jnp.dot(a2_ref[...], w2_ref[...], preferred_element_type=F32)
    o_ref[...] = (s1_ref[...].astype(F32) * y1 + s2_ref[...].astype(F32) * y2).astype(BF16)


def _merge(og_mla, og_diff, w1, w2, rest):
    m, k_dim = og_mla.shape
    n = w1.shape[1]
    tm, tn = 1024, 1024
    return pl.pallas_call(
        _merge_kernel,
        grid=(m // tm, n // tn),
        in_specs=[pl.BlockSpec((tm, k_dim), lambda i, j: (i, 0)),
                  pl.BlockSpec((tm, k_dim), lambda i, j: (i, 0)),
                  pl.BlockSpec((k_dim, tn), lambda i, j: (0, j)),
                  pl.BlockSpec((k_dim, tn), lambda i, j: (0, j)),
                  pl.BlockSpec((tm, tn), lambda i, j: (i, MGM_OFF // tn + j)),
                  pl.BlockSpec((tm, tn), lambda i, j: (i, MGD_OFF // tn + j))],
        out_specs=pl.BlockSpec((tm, tn), lambda i, j: (i, j)),
        out_shape=jax.ShapeDtypeStruct((m, n), BF16),
        compiler_params=_cparams(("arbitrary", "arbitrary")),
        name="merge",
    )(og_mla, og_diff, w1, w2, rest, rest)


def _out_kernel(a_ref, w_ref, x_ref, ada_ref, g_ref, o_ref):
    y = jnp.dot(a_ref[...], w_ref[...], preferred_element_type=F32)
    ms = jnp.mean(y * y, axis=-1, keepdims=True)
    yn = y * lax.rsqrt(ms + EPS) * g_ref[...]
    o_ref[0] = x_ref[0] + ada_ref[0, 2:3, :] * yn


def _out(merged, w_out, x, ada3, g_post):
    nb, s, d = x.shape
    tm = 512
    ns = s // tm
    return pl.pallas_call(
        _out_kernel,
        grid=(nb, ns),
        in_specs=[pl.BlockSpec((tm, d), lambda b, i: (b * ns + i, 0)),
                  pl.BlockSpec((d, d), lambda b, i: (0, 0)),
                  pl.BlockSpec((1, tm, d), lambda b, i: (b, i, 0)),
                  pl.BlockSpec((1, 3, d), lambda b, i: (b, 0, 0)),
                  pl.BlockSpec((1, d), lambda b, i: (0, 0))],
        out_specs=pl.BlockSpec((1, tm, d), lambda b, i: (b, i, 0)),
        out_shape=jax.ShapeDtypeStruct((nb, s, d), F32),
        compiler_params=_cparams(("arbitrary", "arbitrary")),
        name="out_proj",
    )(merged, w_out, x, ada3, g_post)


def _prep_weights(w_in, w_ukv):
    qcols = w_in[:, :Q_MLA_W].reshape(D_MODEL, MLA_HEADS // 2, 2, MLA_NOPE + MLA_ROPE)
    nope = qcols[..., :MLA_NOPE].reshape(D_MODEL, MLA_HEADS // 2, 2 * MLA_NOPE)
    rope = qcols[..., MLA_NOPE:].reshape(D_MODEL, MLA_HEADS // 2, 2 * MLA_ROPE)
    w_q = jnp.concatenate([nope, rope], axis=-1).reshape(D_MODEL, Q_MLA_W).astype(BF16)
    w_kvr = jnp.pad(w_in[:, Q_MLA_W:REST_OFF], ((0, 0), (0, KVR_PAD - KVR_W))).astype(BF16)
    w_rest = w_in[:, REST_OFF:].astype(BF16)
    ukv = w_ukv.reshape(KV_RANK, MLA_HEADS, MLA_NOPE + MLA_V)
    w_kn = ukv[..., :MLA_NOPE].reshape(KV_RANK, MLA_HEADS * MLA_NOPE).astype(BF16)
    w_v = ukv[..., MLA_NOPE:].reshape(KV_RANK, MLA_HEADS * MLA_V).astype(BF16)
    return w_q, w_kvr, w_rest, w_kn, w_v


def kernel(x, c, positions, w_ada, b_ada, g_pre, w_in, g_kv, w_ukv, lambda_q1, lambda_k1,
           lambda_q2, lambda_k2, g_subln, w_o_mla, w_o_diff, w_out, g_post):
    nb, s, d = x.shape
    depth = w_in.shape[0]
    half = MLA_ROPE // 2
    inv = ROPE_THETA ** (-jnp.arange(half, dtype=F32) / half)
    inv_tab = jnp.tile(inv, LANES // half).reshape(1, LANES)
    slopes = 2.0 ** (-8.0 * jnp.arange(1, DIFF_HEADS + 1, dtype=F32) / DIFF_HEADS)
    slopes = jnp.broadcast_to(slopes.reshape(DIFF_HEADS // 2, 2, 1), (DIFF_HEADS // 2, 2, LANES))
    pos_col = positions.reshape(nb * s, 1)
    pos_colf = pos_col.astype(F32)
    pos_row = positions.astype(F32).reshape(nb, s // ATTN_TQ, ATTN_TQ)

    for l in range(depth):
        w_q, w_kvr, w_rest, w_kn, w_v = _prep_weights(w_in[l], w_ukv[l])
        ada3 = _ada(c, w_ada[l], b_ada[l]).reshape(nb, 3, d)
        h, cos_tab, sin_tab = _norm(x, ada3, g_pre[l].reshape(1, d), pos_col, inv_tab)
        qm, kvr, rest = _proj(h, w_q, w_kvr, w_rest, cos_tab, sin_tab)
        kk, vv = _kv(kvr, g_kv[l].reshape(1, KV_RANK), w_kn, w_v, cos_tab, sin_tab)
        og_mla = _mla_attn(qm, kk, vv, rest, nb, s)
        lam_par = jnp.stack([lambda_q1[l], lambda_k1[l], lambda_q2[l], lambda_k2[l]]).astype(F32)
        og_diff = _diff_attn(rest, pos_colf, pos_row, slopes, lam_par,
                             g_subln[l].reshape(1, DIFF_V), nb, s)
        merged = _merge(og_mla, og_diff, w_o_mla[l].astype(BF16), w_o_diff[l].astype(BF16), rest)
        x = _out(merged, w_out[l].astype(BF16), x, ada3, g_post[l].reshape(1, d))
    return x
```

```python
import functools
import math

import jax
import jax.numpy as jnp
from jax import lax
from jax.experimental import pallas as pl
from jax.experimental.pallas import tpu as pltpu

F32 = jnp.float32
BF16 = jnp.bfloat16

D_MODEL = 2048
MLA_HEADS = 8
MLA_NOPE = 128
MLA_ROPE = 64
MLA_V = 128
KV_RANK = 512
ROPE_THETA = 10000.0
DIFF_HEADS = 8
DIFF_QK = 64
DIFF_V = 128
EPS = 1e-6
NEG = -1e30
LAMBDA_INIT = 0.8 - 0.6 * math.exp(-0.3 * 0)

LANES = 128
SUBLANES = 8
VMEM_LIMIT = 56 * 1024 * 1024
ATTN_TQ = 512
MLA_ROW_SPLIT = 2
LOG2E = math.log2(math.e)

Q_MLA_W = MLA_HEADS * (MLA_NOPE + MLA_ROPE)
KVR_W = KV_RANK + MLA_ROPE
KVR_PAD = 640
REST_OFF = Q_MLA_W + KVR_W
QD_OFF, KD_OFF, VD_OFF, GM_OFF, GD_OFF, MGM_OFF, MGD_OFF = 0, 1024, 2048, 3072, 4096, 5120, 7168
REST_W = 9216


def _cparams(sem):
    return pltpu.CompilerParams(dimension_semantics=sem, vmem_limit_bytes=VMEM_LIMIT)


def _ada_kernel(cb_ref, w_ref, b_ref, o_ref):
    k_dim, tn = w_ref.shape
    nb = cb_ref.shape[0]
    nchunk = tn // LANES

    def body(i, accs):
        k0 = pl.multiple_of(i * SUBLANES, SUBLANES)
        out = []
        for b in range(nb):
            cv = cb_ref[b, pl.ds(k0, SUBLANES), :]
            for j in range(nchunk):
                wv = w_ref[pl.ds(k0, SUBLANES), j * LANES:(j + 1) * LANES]
                out.append(accs[b * nchunk + j] + wv * cv)
        return tuple(out)

    init = tuple(jnp.zeros((SUBLANES, LANES), F32) for _ in range(nb * nchunk))
    accs = lax.fori_loop(0, k_dim // SUBLANES, body, init, unroll=8)
    for b in range(nb):
        row = jnp.concatenate(
            [jnp.sum(accs[b * nchunk + j], axis=0, keepdims=True) for j in range(nchunk)], axis=1)
        o_ref[b:b + 1, :] = row + b_ref[...]


def _ada(c, w, bias):
    nb, k_dim = c.shape
    n = w.shape[1]
    tn = 512
    cb = jnp.broadcast_to(c[:, :, None], (nb, k_dim, LANES))
    return pl.pallas_call(
        _ada_kernel,
        grid=(n // tn,),
        in_specs=[pl.BlockSpec((nb, k_dim, LANES), lambda j: (0, 0, 0)),
                  pl.BlockSpec((k_dim, tn), lambda j: (0, j)),
                  pl.BlockSpec((1, tn), lambda j: (0, j))],
        out_specs=pl.BlockSpec((nb, tn), lambda j: (0, j)),
        out_shape=jax.ShapeDtypeStruct((nb, n), F32),
        compiler_params=_cparams(("arbitrary",)),
        name="ada",
    )(cb, w, bias.reshape(1, n))


def _norm_kernel(x_ref, ada_ref, g_ref, pos_ref, inv_ref, h_ref, cos_ref, sin_ref):
    x = x_ref[0]
    ms = jnp.mean(x * x, axis=-1, keepdims=True)
    y = x * lax.rsqrt(ms + EPS) * g_ref[...]
    shift = ada_ref[0, 0:1, :]
    scale = ada_ref[0, 1:2, :]
    h_ref[...] = (y * (1.0 + scale) + shift).astype(BF16)
    ang = pos_ref[...].astype(F32) * inv_ref[...]
    lane = lax.broadcasted_iota(jnp.int32, ang.shape, 1)
    sign = jnp.where((lane % MLA_ROPE) < MLA_ROPE // 2, -1.0, 1.0).astype(F32)
    cos_ref[...] = jnp.cos(ang)
    sin_ref[...] = jnp.sin(ang) * sign


def _norm(x, ada3, g_pre, pos_col, inv_tab):
    nb, s, d = x.shape
    ts = 512
    ns = s // ts
    row = lambda b, i: (b * ns + i, 0)
    return pl.pallas_call(
        _norm_kernel,
        grid=(nb, ns),
        in_specs=[pl.BlockSpec((1, ts, d), lambda b, i: (b, i, 0)),
                  pl.BlockSpec((1, 3, d), lambda b, i: (b, 0, 0)),
                  pl.BlockSpec((1, d), lambda b, i: (0, 0)),
                  pl.BlockSpec((ts, 1), row),
                  pl.BlockSpec((1, LANES), lambda b, i: (0, 0))],
        out_specs=[pl.BlockSpec((ts, d), row),
                   pl.BlockSpec((ts, LANES), row),
                   pl.BlockSpec((ts, LANES), row)],
        out_shape=[jax.ShapeDtypeStruct((nb * s, d), BF16),
                   jax.ShapeDtypeStruct((nb * s, LANES), F32),
                   jax.ShapeDtypeStruct((nb * s, LANES), F32)],
        compiler_params=_cparams(("arbitrary", "arbitrary")),
        name="prenorm",
    )(x, ada3, g_pre, pos_col, inv_tab)


def _rope_cols(r, cos, sin_signed):
    lane = lax.broadcasted_iota(jnp.int32, r.shape, 1)
    half = MLA_ROPE // 2
    partner = jnp.where((lane % MLA_ROPE) < half,
                        pltpu.roll(r, LANES - half, 1), pltpu.roll(r, half, 1))
    return r * cos + partner * sin_signed


def _qmla_kernel(a_ref, w_ref, cos_ref, sin_ref, o_ref, *, scale):
    acc = jnp.dot(a_ref[...], w_ref[...], preferred_element_type=F32)
    rr = _rope_cols(acc[:, 2 * MLA_NOPE:], cos_ref[...], sin_ref[...])
    o_ref[:, :2 * MLA_NOPE] = (acc[:, :2 * MLA_NOPE] * scale).astype(BF16)
    o_ref[:, 2 * MLA_NOPE:] = (rr * scale).astype(BF16)


def _plain_mm_kernel(a_ref, w_ref, o_ref):
    o_ref[...] = jnp.dot(a_ref[...], w_ref[...], preferred_element_type=F32).astype(o_ref.dtype)


def _rest_kernel(a_ref, w_ref, o_ref, *, tn):
    j = pl.program_id(1)
    acc = jnp.dot(a_ref[...], w_ref[...], preferred_element_type=F32)

    @pl.when(j < KD_OFF // tn)
    def _():
        o_ref[...] = (acc * (DIFF_QK ** -0.5 * LOG2E)).astype(BF16)

    @pl.when(jnp.logical_and(j >= KD_OFF // tn, j < GM_OFF // tn))
    def _():
        o_ref[...] = acc.astype(BF16)

    @pl.when(jnp.logical_and(j >= GM_OFF // tn, j < MGM_OFF // tn))
    def _():
        o_ref[...] = (acc * jax.nn.sigmoid(acc)).astype(BF16)

    @pl.when(j >= MGM_OFF // tn)
    def _():
        o_ref[...] = jax.nn.sigmoid(acc).astype(BF16)


def _proj(h, w_q, w_kvr, w_rest, cos_tab, sin_tab):
    m, k_dim = h.shape
    tm = 1024
    pair_w = 2 * (MLA_NOPE + MLA_ROPE)
    q_scale = (MLA_NOPE + MLA_ROPE) ** -0.5 * LOG2E
    qm = pl.pallas_call(
        functools.partial(_qmla_kernel, scale=q_scale),
        grid=(m // tm, Q_MLA_W // pair_w),
        in_specs=[pl.BlockSpec((tm, k_dim), lambda i, j: (i, 0)),
                  pl.BlockSpec((k_dim, pair_w), lambda i, j: (0, j)),
                  pl.BlockSpec((tm, LANES), lambda i, j: (i, 0)),
                  pl.BlockSpec((tm, LANES), lambda i, j: (i, 0))],
        out_specs=pl.BlockSpec((tm, pair_w), lambda i, j: (i, j)),
        out_shape=jax.ShapeDtypeStruct((m, Q_MLA_W), BF16),
        compiler_params=_cparams(("arbitrary", "arbitrary")),
        name="proj_qmla",
    )(h, w_q, cos_tab, sin_tab)
    kvr = pl.pallas_call(
        _plain_mm_kernel,
        grid=(m // tm,),
        in_specs=[pl.BlockSpec((tm, k_dim), lambda i: (i, 0)),
                  pl.BlockSpec((k_dim, KVR_PAD), lambda i: (0, 0))],
        out_specs=pl.BlockSpec((tm, KVR_PAD), lambda i: (i, 0)),
        out_shape=jax.ShapeDtypeStruct((m, KVR_PAD), F32),
        compiler_params=_cparams(("arbitrary",)),
        name="proj_kvr",
    )(h, w_kvr)
    tn = 1024
    rest = pl.pallas_call(
        functools.partial(_rest_kernel, tn=tn),
        grid=(m // tm, REST_W // tn),
        in_specs=[pl.BlockSpec((tm, k_dim), lambda i, j: (i, 0)),
                  pl.BlockSpec((k_dim, tn), lambda i, j: (0, j))],
        out_specs=pl.BlockSpec((tm, tn), lambda i, j: (i, j)),
        out_shape=jax.ShapeDtypeStruct((m, REST_W), BF16),
        compiler_params=_cparams(("arbitrary", "arbitrary")),
        name="proj_rest",
    )(h, w_rest)
    return qm, kvr, rest


def _kv_kernel(p_ref, g_ref, wk_ref, wv_ref, cos_ref, sin_ref, k_ref, v_ref):
    p = p_ref[...]
    ckv = p[:, :KV_RANK]
    ms = jnp.mean(ckv * ckv, axis=-1, keepdims=True)
    n = (ckv * lax.rsqrt(ms + EPS) * g_ref[...]).astype(BF16)
    kn = jnp.dot(n, wk_ref[...], preferred_element_type=F32)
    vv = jnp.dot(n, wv_ref[...], preferred_element_type=F32)
    kr_even = _rope_cols(p[:, KV_RANK:], cos_ref[...], sin_ref[...])
    kr_odd = pltpu.roll(kr_even, MLA_ROPE, 1)
    lane = lax.broadcasted_iota(jnp.int32, kr_even.shape, 1)
    ones_col = jnp.where(lane == 0, 1.0, 0.0).astype(BF16)
    kw = 2 * MLA_NOPE
    for hd in range(MLA_HEADS):
        k_ref[:, hd * kw:hd * kw + MLA_NOPE] = kn[:, hd * MLA_NOPE:(hd + 1) * MLA_NOPE].astype(BF16)
        k_ref[:, hd * kw + MLA_NOPE:(hd + 1) * kw] = (kr_even if hd % 2 == 0 else kr_odd).astype(BF16)
        v_ref[:, hd * kw:hd * kw + MLA_V] = vv[:, hd * MLA_V:(hd + 1) * MLA_V].astype(BF16)
        v_ref[:, hd * kw + MLA_V:(hd + 1) * kw] = ones_col


def _kv(kvr, g_kv, w_kn, w_v, cos_tab, sin_tab):
    m = kvr.shape[0]
    tm = 512
    kw = MLA_HEADS * 2 * MLA_NOPE
    return pl.pallas_call(
        _kv_kernel,
        grid=(m // tm,),
        in_specs=[pl.BlockSpec((tm, KVR_PAD), lambda i: (i, 0)),
                  pl.BlockSpec((1, KV_RANK), lambda i: (0, 0)),
                  pl.BlockSpec((KV_RANK, MLA_HEADS * MLA_NOPE), lambda i: (0, 0)),
                  pl.BlockSpec((KV_RANK, MLA_HEADS * MLA_V), lambda i: (0, 0)),
                  pl.BlockSpec((tm, LANES), lambda i: (i, 0)),
                  pl.BlockSpec((tm, LANES), lambda i: (i, 0))],
        out_specs=[pl.BlockSpec((tm, kw), lambda i: (i, 0)),
                   pl.BlockSpec((tm, kw), lambda i: (i, 0))],
        out_shape=[jax.ShapeDtypeStruct((m, kw), BF16),
                   jax.ShapeDtypeStruct((m, kw), BF16)],
        compiler_params=_cparams(("arbitrary",)),
        name="kv_up",
    )(kvr, g_kv, w_kn, w_v, cos_tab, sin_tab)


def _flash_pipeline(n_chains, qi, score_fn, value_fn, s_ref, acc_ref):
    rows = s_ref.shape[1]

    def qk_phase(kc, m_run, diag):
        out = []
        for ci, s in enumerate(score_fn(kc, diag)):
            s_ref[ci] = s
            out.append(jnp.maximum(m_run[ci], jnp.max(s, axis=-1, keepdims=True)))
        return tuple(out)

    def pv_phase(kc, m_acc, m_run):
        for ci in range(n_chains):
            p = jnp.exp2(s_ref[ci] - m_run[ci])
            alpha = jnp.exp2(m_acc[ci] - m_run[ci])
            pv = jnp.dot(p.astype(BF16), value_fn(ci, kc), preferred_element_type=F32)
            acc_ref[ci] = alpha * acc_ref[ci] + pv

    neg = tuple(jnp.full((rows, 1), NEG, F32) for _ in range(n_chains))
    acc_ref[...] = jnp.zeros_like(acc_ref)
    m_run = qk_phase(qi, neg, True)

    def body(kc, carry):
        cur, m_acc, m_run = carry
        pv_phase(cur, m_acc, m_run)
        return kc, m_run, qk_phase(kc, m_run, False)

    cur, m_acc, m_run = lax.fori_loop(0, qi, body, (qi, neg, m_run))
    pv_phase(cur, m_acc, m_run)


def _causal_mask(rows, cols, row0):
    row = lax.broadcasted_iota(jnp.int32, (rows, cols), 0) + row0
    col = lax.broadcasted_iota(jnp.int32, (rows, cols), 1)
    return col <= row


_NT = (((1,), (1,)), ((), ()))


def _mla_attn_kernel(q_ref, k_ref, v_ref, g_ref, o_ref, acc_ref, qs_ref, s_ref, *, tq, rs):
    qi = pl.program_id(2)
    kw = 2 * MLA_NOPE
    tr = tq // rs
    for hp in range(2):
        qs_ref[hp, :, :MLA_NOPE] = q_ref[:, hp * MLA_NOPE:(hp + 1) * MLA_NOPE]
        qs_ref[hp, :, MLA_NOPE:] = q_ref[:, 2 * MLA_NOPE:]
    chains = [(hp, r) for hp in range(2) for r in range(rs)]

    def scores(kc, diag):
        k0 = pl.multiple_of(kc * tq, tq)
        out = []
        for hp, r in chains:
            q = qs_ref[hp, r * tr:(r + 1) * tr, :]
            k = k_ref[pl.ds(k0, tq), hp * kw:(hp + 1) * kw]
            s = lax.dot_general(q, k, _NT, preferred_element_type=F32)
            if diag:
                s = jnp.where(_causal_mask(tr, tq, r * tr), s, NEG)
            out.append(s)
        return out

    def values(ci, kc):
        hp = chains[ci][0]
        return v_ref[pl.ds(pl.multiple_of(kc * tq, tq), tq), hp * kw:(hp + 1) * kw]

    _flash_pipeline(len(chains), qi, scores, values, s_ref, acc_ref)
    for ci, (hp, r) in enumerate(chains):
        acc = acc_ref[ci]
        o = acc[:, :MLA_V] / acc[:, MLA_V:MLA_V + 1]
        gate = g_ref[r * tr:(r + 1) * tr, hp * MLA_V:(hp + 1) * MLA_V].astype(F32)
        o_ref[r * tr:(r + 1) * tr, hp * MLA_V:(hp + 1) * MLA_V] = (o * gate).astype(BF16)


def _mla_attn(qm, kk, vv, rest, nb, s):
    tq = ATTN_TQ
    nq = s // tq
    pair_w = 2 * (MLA_NOPE + MLA_ROPE)
    kw = 4 * MLA_NOPE
    gate_blk = GM_OFF // (2 * MLA_V)
    rs = MLA_ROW_SPLIT
    return pl.pallas_call(
        functools.partial(_mla_attn_kernel, tq=tq, rs=rs),
        grid=(nb, MLA_HEADS // 2, nq),
        in_specs=[pl.BlockSpec((tq, pair_w), lambda b, hh, i: (b * nq + i, hh)),
                  pl.BlockSpec((s, kw), lambda b, hh, i: (b, hh)),
                  pl.BlockSpec((s, kw), lambda b, hh, i: (b, hh)),
                  pl.BlockSpec((tq, 2 * MLA_V), lambda b, hh, i: (b * nq + i, gate_blk + hh))],
        out_specs=pl.BlockSpec((tq, 2 * MLA_V), lambda b, hh, i: (b * nq + i, hh)),
        out_shape=jax.ShapeDtypeStruct((nb * s, MLA_HEADS * MLA_V), BF16),
        scratch_shapes=[pltpu.VMEM((2 * rs, tq // rs, 2 * MLA_V), F32),
                        pltpu.VMEM((2, tq, 2 * MLA_NOPE), BF16),
                        pltpu.VMEM((2 * rs, tq // rs, tq), F32)],
        compiler_params=_cparams(("arbitrary", "arbitrary", "arbitrary")),
        name="mla_attn",
    )(qm, kk, vv, rest)


def _diff_attn_kernel(q_ref, k_ref, v_ref, g_ref, pq_ref, pk_ref, sl_ref, lam_ref, gs_ref,
                      o_ref, vaug_ref, acc_ref, qz_ref, s_ref, *, tq):
    qi = pl.program_id(2)

    @pl.when(qi == 0)
    def _():
        lane = lax.broadcasted_iota(jnp.int32, (vaug_ref.shape[1], DIFF_V), 1)
        ones_col = jnp.where(lane == 0, 1.0, 0.0).astype(BF16)
        for hp in range(2):
            vaug_ref[hp, :, :DIFF_V] = v_ref[:, hp * DIFF_V:(hp + 1) * DIFF_V]
            vaug_ref[hp, :, DIFF_V:] = ones_col

    lq = lam_ref[...]
    lam = (jnp.exp(jnp.sum(lq[0:1] * lq[1:2], axis=-1, keepdims=True))
           - jnp.exp(jnp.sum(lq[2:3] * lq[3:4], axis=-1, keepdims=True)) + LAMBDA_INIT)
    pq = pq_ref[...]
    lane_q = lax.broadcasted_iota(jnp.int32, (tq, 2 * DIFF_QK), 1)
    slopes2 = []
    aqs = []
    for hp in range(2):
        q = q_ref[:, hp * 2 * DIFF_QK:(hp + 1) * 2 * DIFF_QK]
        qz_ref[2 * hp] = jnp.where(lane_q < DIFF_QK, q, 0).astype(BF16)
        qz_ref[2 * hp + 1] = jnp.where(lane_q >= DIFF_QK, q, 0).astype(BF16)
        slopes2.append(sl_ref[0, hp:hp + 1, 0:1] * LOG2E)
        aqs.append(slopes2[hp] * pq)

    def scores(kc, diag):
        k0 = pl.multiple_of(kc * tq, tq)
        pk = pk_ref[0, pl.ds(kc, 1), :]
        mask = _causal_mask(tq, tq, 0) if diag else None
        out = []
        for hp in range(2):
            k = k_ref[pl.ds(k0, tq), hp * 2 * DIFF_QK:(hp + 1) * 2 * DIFF_QK]
            bias = jnp.abs(aqs[hp] - slopes2[hp] * pk)
            for c in range(2):
                s = lax.dot_general(qz_ref[2 * hp + c], k, _NT, preferred_element_type=F32) - bias
                out.append(jnp.where(mask, s, NEG) if diag else s)
        return out

    def values(ci, kc):
        return vaug_ref[ci // 2, pl.ds(pl.multiple_of(kc * tq, tq), tq), :]

    _flash_pipeline(4, qi, scores, values, s_ref, acc_ref)
    for hp in range(2):
        a1 = acc_ref[2 * hp]
        a2 = acc_ref[2 * hp + 1]
        o = a1[:, :DIFF_V] / a1[:, DIFF_V:DIFF_V + 1] - lam * (a2[:, :DIFF_V] / a2[:, DIFF_V:DIFF_V + 1])
        ms_o = jnp.mean(o * o, axis=-1, keepdims=True)
        o = o * lax.rsqrt(ms_o + EPS) * gs_ref[...] * (1.0 - LAMBDA_INIT)
        gate = g_ref[:, hp * DIFF_V:(hp + 1) * DIFF_V].astype(F32)
        o_ref[:, hp * DIFF_V:(hp + 1) * DIFF_V] = (o * gate).astype(BF16)


def _diff_attn(rest, pos_col, pos_row, slopes, lam_par, g_subln, nb, s):
    tq = ATTN_TQ
    nq = s // tq
    pw = 2 * DIFF_V
    return pl.pallas_call(
        functools.partial(_diff_attn_kernel, tq=tq),
        grid=(nb, DIFF_HEADS // 2, nq),
        in_specs=[pl.BlockSpec((tq, pw), lambda b, hh, i: (b * nq + i, QD_OFF // pw + hh)),
                  pl.BlockSpec((s, pw), lambda b, hh, i: (b, KD_OFF // pw + hh)),
                  pl.BlockSpec((s, pw), lambda b, hh, i: (b, VD_OFF // pw + hh)),
                  pl.BlockSpec((tq, pw), lambda b, hh, i: (b * nq + i, GD_OFF // pw + hh)),
                  pl.BlockSpec((tq, 1), lambda b, hh, i: (b * nq + i, 0)),
                  pl.BlockSpec((1, nq, tq), lambda b, hh, i: (b, 0, 0)),
                  pl.BlockSpec((1, 2, LANES), lambda b, hh, i: (hh, 0, 0)),
                  pl.BlockSpec((4, DIFF_QK), lambda b, hh, i: (0, 0)),
                  pl.BlockSpec((1, DIFF_V), lambda b, hh, i: (0, 0))],
        out_specs=pl.BlockSpec((tq, pw), lambda b, hh, i: (b * nq + i, hh)),
        out_shape=jax.ShapeDtypeStruct((nb * s, DIFF_HEADS * DIFF_V), BF16),
        scratch_shapes=[pltpu.VMEM((2, s, 2 * DIFF_V), BF16),
                        pltpu.VMEM((4, tq, 2 * DIFF_V), F32),
                        pltpu.VMEM((4, tq, 2 * DIFF_QK), BF16),
                        pltpu.VMEM((4, tq, tq), F32)],
        compiler_params=_cparams(("arbitrary", "arbitrary", "arbitrary")),
        name="diff_attn",
    )(rest, rest, rest, rest, pos_col, pos_row, slopes, lam_par, g_subln)


def _merge_kernel(a1_ref, a2_ref, w1_ref, w2_ref, s1_ref, s2_ref, o_ref):
    y1 = jnp.dot(a1_ref[...], w1_ref[...], preferred_element_type=F32)
    y2 = jnp.dot(a2_ref[...], w2_ref[...], preferred_element_type=F32)
    o_ref[...] = (s1_ref[...].astype(F32) * y1 + s2_ref[...].astype(F32) * y2).astype(BF16)


def _merge(og_mla, og_diff, w1, w2, rest):
    m, k_dim = og_mla.shape
    n = w1.shape[1]
    tm, tn = 1024, 1024
    return pl.pallas_call(
        _merge_kernel,
        grid=(m // tm, n // tn),
        in_specs=[pl.BlockSpec((tm, k_dim), lambda i, j: (i, 0)),
                  pl.BlockSpec((tm, k_dim), lambda i, j: (i, 0)),
                  pl.BlockSpec((k_dim, tn), lambda i, j: (0, j)),
                  pl.BlockSpec((k_dim, tn), lambda i, j: (0, j)),
                  pl.BlockSpec((tm, tn), lambda i, j: (i, MGM_OFF // tn + j)),
                  pl.BlockSpec((tm, tn), lambda i, j: (i, MGD_OFF // tn + j))],
        out_specs=pl.BlockSpec((tm, tn), lambda i, j: (i, j)),
        out_shape=jax.ShapeDtypeStruct((m, n), BF16),
        compiler_params=_cparams(("arbitrary", "arbitrary")),
        name="merge",
    )(og_mla, og_diff, w1, w2, rest, rest)


def _out_kernel(a_ref, w_ref, x_ref, ada_ref, g_ref, o_ref):
    y = jnp.dot(a_ref[...], w_ref[...], preferred_element_type=F32)
    ms = jnp.mean(y * y, axis=-1, keepdims=True)
    yn = y * lax.rsqrt(ms + EPS) * g_ref[...]
    o_ref[0] = x_ref[0] + ada_ref[0, 2:3, :] * yn


def _out(merged, w_out, x, ada3, g_post):
    nb, s, d = x.shape
    tm = 512
    ns = s // tm
    return pl.pallas_call(
        _out_kernel,
        grid=(nb, ns),
        in_specs=[pl.BlockSpec((tm, d), lambda b, i: (b * ns + i, 0)),
                  pl.BlockSpec((d, d), lambda b, i: (0, 0)),
                  pl.BlockSpec((1, tm, d), lambda b, i: (b, i, 0)),
                  pl.BlockSpec((1, 3, d), lambda b, i: (b, 0, 0)),
                  pl.BlockSpec((1, d), lambda b, i: (0, 0))],
        out_specs=pl.BlockSpec((1, tm, d), lambda b, i: (b, i, 0)),
        out_shape=jax.ShapeDtypeStruct((nb, s, d), F32),
        compiler_params=_cparams(("arbitrary", "arbitrary")),
        name="out_proj",
    )(merged, w_out, x, ada3, g_post)


def _prep_weights(w_in, w_ukv):
    qcols = w_in[:, :Q_MLA_W].reshape(D_MODEL, MLA_HEADS // 2, 2, MLA_NOPE + MLA_ROPE)
    nope = qcols[..., :MLA_NOPE].reshape(D_MODEL, MLA_HEADS // 2, 2 * MLA_NOPE)
    rope = qcols[..., MLA_NOPE:].reshape(D_MODEL, MLA_HEADS // 2, 2 * MLA_ROPE)
    w_q = jnp.concatenate([nope, rope], axis=-1).reshape(D_MODEL, Q_MLA_W).astype(BF16)
    w_kvr = jnp.pad(w_in[:, Q_MLA_W:REST_OFF], ((0, 0), (0, KVR_PAD - KVR_W))).astype(BF16)
    w_rest = w_in[:, REST_OFF:].astype(BF16)
    ukv = w_ukv.reshape(KV_RANK, MLA_HEADS, MLA_NOPE + MLA_V)
    w_kn = ukv[..., :MLA_NOPE].reshape(KV_RANK, MLA_HEADS * MLA_NOPE).astype(BF16)
    w_v = ukv[..., MLA_NOPE:].reshape(KV_RANK, MLA_HEADS * MLA_V).astype(BF16)
    return w_q, w_kvr, w_rest, w_kn, w_v


def kernel(x, c, positions, w_ada, b_ada, g_pre, w_in, g_kv, w_ukv, lambda_q1, lambda_k1,
           lambda_q2, lambda_k2, g_subln, w_o_mla, w_o_diff, w_out, g_post):
    nb, s, d = x.shape
    depth = w_in.shape[0]
    half = MLA_ROPE // 2
    inv = ROPE_THETA ** (-jnp.arange(half, dtype=F32) / half)
    inv_tab = jnp.tile(inv, LANES // half).reshape(1, LANES)
    slopes = 2.0 ** (-8.0 * jnp.arange(1, DIFF_HEADS + 1, dtype=F32) / DIFF_HEADS)
    slopes = jnp.broadcast_to(slopes.reshape(DIFF_HEADS // 2, 2, 1), (DIFF_HEADS // 2, 2, LANES))
    pos_col = positions.reshape(nb * s, 1)
    pos_colf = pos_col.astype(F32)
    pos_row = positions.astype(F32).reshape(nb, s // ATTN_TQ, ATTN_TQ)

    for l in range(depth):
        w_q, w_kvr, w_rest, w_kn, w_v = _prep_weights(w_in[l], w_ukv[l])
        ada3 = _ada(c, w_ada[l], b_ada[l]).reshape(nb, 3, d)
        h, cos_tab, sin_tab = _norm(x, ada3, g_pre[l].reshape(1, d), pos_col, inv_tab)
        qm, kvr, rest = _proj(h, w_q, w_kvr, w_rest, cos_tab, sin_tab)
        kk, vv = _kv(kvr, g_kv[l].reshape(1, KV_RANK), w_kn, w_v, cos_tab, sin_tab)
        og_mla = _mla_attn(qm, kk, vv, rest, nb, s)
        lam_par = jnp.stack([lambda_q1[l], lambda_k1[l], lambda_q2[l], lambda_k2[l]]).astype(F32)
        og_diff = _diff_attn(rest, pos_colf, pos_row, slopes, lam_par,
                             g_subln[l].reshape(1, DIFF_V), nb, s)
        merged = _merge(og_mla, og_diff, w_o_mla[l].astype(BF16), w_o_diff[l].astype(BF16), rest)
        x = _out(merged, w_out[l].astype(BF16), x, ada3, g_post[l].reshape(1, d))
    return x
```

```python
import functools
import math

import jax
import jax.numpy as jnp
from jax import lax
from jax.experimental import pallas as pl
from jax.experimental.pallas import tpu as pltpu

F32 = jnp.float32
BF16 = jnp.bfloat16

D_MODEL = 2048
MLA_HEADS = 8
MLA_NOPE = 128
MLA_ROPE = 64
MLA_V = 128
KV_RANK = 512
ROPE_THETA = 10000.0
DIFF_HEADS = 8
DIFF_QK = 64
DIFF_V = 128
EPS = 1e-6
NEG = -1e30
LAMBDA_INIT = 0.8 - 0.6 * math.exp(-0.3 * 0)

LANES = 128
SUBLANES = 8
VMEM_LIMIT = 56 * 1024 * 1024
ATTN_TQ = 512
MLA_ROW_SPLIT = 2
LOG2E = math.log2(math.e)

Q_MLA_W = MLA_HEADS * (MLA_NOPE + MLA_ROPE)
KVR_W = KV_RANK + MLA_ROPE
KVR_PAD = 640
REST_OFF = Q_MLA_W + KVR_W
QD_OFF, KD_OFF, VD_OFF, GM_OFF, GD_OFF, MGM_OFF, MGD_OFF = 0, 1024, 2048, 3072, 4096, 5120, 7168
REST_W = 9216


def _cparams(sem):
    return pltpu.CompilerParams(dimension_semantics=sem, vmem_limit_bytes=VMEM_LIMIT)


def _ada_kernel(cb_ref, w_ref, b_ref, o_ref):
    k_dim, tn = w_ref.shape
    nb = cb_ref.shape[0]
    nchunk = tn // LANES

    def body(i, accs):
        k0 = pl.multiple_of(i * SUBLANES, SUBLANES)
        out = []
        for b in range(nb):
            cv = cb_ref[b, pl.ds(k0, SUBLANES), :]
            for j in range(nchunk):
                wv = w_ref[pl.ds(k0, SUBLANES), j * LANES:(j + 1) * LANES]
                out.append(accs[b * nchunk + j] + wv * cv)
        return tuple(out)

    init = tuple(jnp.zeros((SUBLANES, LANES), F32) for _ in range(nb * nchunk))
    accs = lax.fori_loop(0, k_dim // SUBLANES, body, init, unroll=8)
    for b in range(nb):
        row = jnp.concatenate(
            [jnp.sum(accs[b * nchunk + j], axis=0, keepdims=True) for j in range(nchunk)], axis=1)
        o_ref[b:b + 1, :] = row + b_ref[...]


def _ada(c, w, bias):
    nb, k_dim = c.shape
    n = w.shape[1]
    tn = 512
    cb = jnp.broadcast_to(c[:, :, None], (nb, k_dim, LANES))
    return pl.pallas_call(
        _ada_kernel,
        grid=(n // tn,),
        in_specs=[pl.BlockSpec((nb, k_dim, LANES), lambda j: (0, 0, 0)),
                  pl.BlockSpec((k_dim, tn), lambda j: (0, j)),
                  pl.BlockSpec((1, tn), lambda j: (0, j))],
        out_specs=pl.BlockSpec((nb, tn), lambda j: (0, j)),
        out_shape=jax.ShapeDtypeStruct((nb, n), F32),
        compiler_params=_cparams(("arbitrary",)),
        name="ada",
    )(cb, w, bias.reshape(1, n))


def _norm_kernel(x_ref, ada_ref, g_ref, pos_ref, inv_ref, h_ref, cos_ref, sin_ref):
    x = x_ref[0]
    ms = jnp.mean(x * x, axis=-1, keepdims=True)
    y = x * lax.rsqrt(ms + EPS) * g_ref[...]
    shift = ada_ref[0, 0:1, :]
    scale = ada_ref[0, 1:2, :]
    h_ref[...] = (y * (1.0 + scale) + shift).astype(BF16)
    ang = pos_ref[...].astype(F32) * inv_ref[...]
    lane = lax.broadcasted_iota(jnp.int32, ang.shape, 1)
    sign = jnp.where((lane % MLA_ROPE) < MLA_ROPE // 2, -1.0, 1.0).astype(F32)
    cos_ref[...] = jnp.cos(ang)
    sin_ref[...] = jnp.sin(ang) * sign


def _norm(x, ada3, g_pre, pos_col, inv_tab):
    nb, s, d = x.shape
    ts = 512
    ns = s // ts
    row = lambda b, i: (b * ns + i, 0)
    return pl.pallas_call(
        _norm_kernel,
        grid=(nb, ns),
        in_specs=[pl.BlockSpec((1, ts, d), lambda b, i: (b, i, 0)),
                  pl.BlockSpec((1, 3, d), lambda b, i: (b, 0, 0)),
                  pl.BlockSpec((1, d), lambda b, i: (0, 0)),
                  pl.BlockSpec((ts, 1), row),
                  pl.BlockSpec((1, LANES), lambda b, i: (0, 0))],
        out_specs=[pl.BlockSpec((ts, d), row),
                   pl.BlockSpec((ts, LANES), row),
                   pl.BlockSpec((ts, LANES), row)],
        out_shape=[jax.ShapeDtypeStruct((nb * s, d), BF16),
                   jax.ShapeDtypeStruct((nb * s, LANES), F32),
                   jax.ShapeDtypeStruct((nb * s, LANES), F32)],
        compiler_params=_cparams(("arbitrary", "arbitrary")),
        name="prenorm",
    )(x, ada3, g_pre, pos_col, inv_tab)


def _rope_cols(r, cos, sin_signed):
    lane = lax.broadcasted_iota(jnp.int32, r.shape, 1)
    half = MLA_ROPE // 2
    partner = jnp.where((lane % MLA_ROPE) < half,
                        pltpu.roll(r, LANES - half, 1), pltpu.roll(r, half, 1))
    return r * cos + partner * sin_signed


def _qmla_kernel(a_ref, w_ref, cos_ref, sin_ref, o_ref, *, scale):
    acc = jnp.dot(a_ref[...], w_ref[...], preferred_element_type=F32)
    rr = _rope_cols(acc[:, 2 * MLA_NOPE:], cos_ref[...], sin_ref[...])
    o_ref[:, :2 * MLA_NOPE] = (acc[:, :2 * MLA_NOPE] * scale).astype(BF16)
    o_ref[:, 2 * MLA_NOPE:] = (rr * scale).astype(BF16)


def _plain_mm_kernel(a_ref, w_ref, o_ref):
    o_ref[...] = jnp.dot(a_ref[...], w_ref[...], preferred_element_type=F32).astype(o_ref.dtype)


def _rest_kernel(a_ref, w_ref, o_ref, *, tn):
    j = pl.program_id(1)
    acc = jnp.dot(a_ref[...], w_ref[...], preferred_element_type=F32)

    @pl.when(j < KD_OFF // tn)
    def _():
        o_ref[...] = (acc * (DIFF_QK ** -0.5 * LOG2E)).astype(BF16)

    @pl.when(jnp.logical_and(j >= KD_OFF // tn, j < GM_OFF // tn))
    def _():
        o_ref[...] = acc.astype(BF16)

    @pl.when(jnp.logical_and(j >= GM_OFF // tn, j < MGM_OFF // tn))
    def _():
        o_ref[...] = (acc * jax.nn.sigmoid(acc)).astype(BF16)

    @pl.when(j >= MGM_OFF // tn)
    def _():
        o_ref[...] = jax.nn.sigmoid(acc).astype(BF16)


def _proj(h, w_q, w_kvr, w_rest, cos_tab, sin_tab):
    m, k_dim = h.shape
    tm = 1024
    pair_w = 2 * (MLA_NOPE + MLA_ROPE)
    q_scale = (MLA_NOPE + MLA_ROPE) ** -0.5 * LOG2E
    qm = pl.pallas_call(
        functools.partial(_qmla_kernel, scale=q_scale),
        grid=(m // tm, Q_MLA_W // pair_w),
        in_specs=[pl.BlockSpec((tm, k_dim), lambda i, j: (i, 0)),
                  pl.BlockSpec((k_dim, pair_w), lambda i, j: (0, j)),
                  pl.BlockSpec((tm, LANES), lambda i, j: (i, 0)),
                  pl.BlockSpec((tm, LANES), lambda i, j: (i, 0))],
        out_specs=pl.BlockSpec((tm, pair_w), lambda i, j: (i, j)),
        out_shape=jax.ShapeDtypeStruct((m, Q_MLA_W), BF16),
        compiler_params=_cparams(("arbitrary", "arbitrary")),
        name="proj_qmla",
    )(h, w_q, cos_tab, sin_tab)
    kvr = pl.pallas_call(
        _plain_mm_kernel,
        grid=(m // tm,),
        in_specs=[pl.BlockSpec((tm, k_dim), lambda i: (i, 0)),
                  pl.BlockSpec((k_dim, KVR_PAD), lambda i: (0, 0))],
        out_specs=pl.BlockSpec((tm, KVR_PAD), lambda i: (i, 0)),
        out_shape=jax.ShapeDtypeStruct((m, KVR_PAD), F32),
        compiler_params=_cparams(("arbitrary",)),
        name="proj_kvr",
    )(h, w_kvr)
    tn = 1024
    rest = pl.pallas_call(
        functools.partial(_rest_kernel, tn=tn),
        grid=(m // tm, REST_W // tn),
        in_specs=[pl.BlockSpec((tm, k_dim), lambda i, j: (i, 0)),
                  pl.BlockSpec((k_dim, tn), lambda i, j: (0, j))],
        out_specs=pl.BlockSpec((tm, tn), lambda i, j: (i, j)),
        out_shape=jax.ShapeDtypeStruct((m, REST_W), BF16),
        compiler_params=_cparams(("arbitrary", "arbitrary")),
        name="proj_rest",
    )(h, w_rest)
    return qm, kvr, rest


def _kv_kernel(p_ref, g_ref, wk_ref, wv_ref, cos_ref, sin_ref, k_ref, v_ref):
    p = p_ref[...]
    ckv = p[:, :KV_RANK]
    ms = jnp.mean(ckv * ckv, axis=-1, keepdims=True)
    n = (ckv * lax.rsqrt(ms + EPS) * g_ref[...]).astype(BF16)
    kn = jnp.dot(n, wk_ref[...], preferred_element_type=F32)
    vv = jnp.dot(n, wv_ref[...], preferred_element_type=F32)
    kr_even = _rope_cols(p[:, KV_RANK:], cos_ref[...], sin_ref[...])
    kr_odd = pltpu.roll(kr_even, MLA_ROPE, 1)
    lane = lax.broadcasted_iota(jnp.int32, kr_even.shape, 1)
    ones_col = jnp.where(lane == 0, 1.0, 0.0).astype(BF16)
    kw = 2 * MLA_NOPE
    for hd in range(MLA_HEADS):
        k_ref[:, hd * kw:hd * kw + MLA_NOPE] = kn[:, hd * MLA_NOPE:(hd + 1) * MLA_NOPE].astype(BF16)
        k_ref[:, hd * kw + MLA_NOPE:(hd + 1) * kw] = (kr_even if hd % 2 == 0 else kr_odd).astype(BF16)
        v_ref[:, hd * kw:hd * kw + MLA_V] = vv[:, hd * MLA_V:(hd + 1) * MLA_V].astype(BF16)
        v_ref[:, hd * kw + MLA_V:(hd + 1) * kw] = ones_col


def _kv(kvr, g_kv, w_kn, w_v, cos_tab, sin_tab):
    m = kvr.shape[0]
    tm = 512
    kw = MLA_HEADS * 2 * MLA_NOPE
    return pl.pallas_call(
        _kv_kernel,
        grid=(m // tm,),
        in_specs=[pl.BlockSpec((tm, KVR_PAD), lambda i: (i, 0)),
                  pl.BlockSpec((1, KV_RANK), lambda i: (0, 0)),
                  pl.BlockSpec((KV_RANK, MLA_HEADS * MLA_NOPE), lambda i: (0, 0)),
                  pl.BlockSpec((KV_RANK, MLA_HEADS * MLA_V), lambda i: (0, 0)),
                  pl.BlockSpec((tm, LANES), lambda i: (i, 0)),
                  pl.BlockSpec((tm, LANES), lambda i: (i, 0))],
        out_specs=[pl.BlockSpec((tm, kw), lambda i: (i, 0)),
                   pl.BlockSpec((tm, kw), lambda i: (i, 0))],
        out_shape=[jax.ShapeDtypeStruct((m, kw), BF16),
                   jax.ShapeDtypeStruct((m, kw), BF16)],
        compiler_params=_cparams(("arbitrary",)),
        name="kv_up",
    )(kvr, g_kv, w_kn, w_v, cos_tab, sin_tab)


def _flash_pipeline(n_chains, nq, score_fn, value_fn, finalize_fn, s_ref, acc_ref):
    rows = s_ref.shape[1]

    def qk_phase(qt, kc, m_run, diag):
        out = []
        for ci, s in enumerate(score_fn(qt, kc, diag)):
            s_ref[ci] = s
            out.append(jnp.maximum(m_run[ci], jnp.max(s, axis=-1, keepdims=True)))
        return tuple(out)

    def pv_phase(kc, m_acc, m_run):
        for ci in range(n_chains):
            p = jnp.exp2(s_ref[ci] - m_run[ci])
            alpha = jnp.exp2(m_acc[ci] - m_run[ci])
            pv = jnp.dot(p.astype(BF16), value_fn(ci, kc), preferred_element_type=F32)
            acc_ref[ci] = alpha * acc_ref[ci] + pv

    neg = tuple(jnp.full((rows, 1), NEG, F32) for _ in range(n_chains))
    acc_ref[...] = jnp.zeros_like(acc_ref)

    def tile(qt, m_run):
        def body(kc, carry):
            cur, m_acc, m_run = carry
            pv_phase(cur, m_acc, m_run)
            return kc, m_run, qk_phase(qt, kc, m_run, False)

        cur, m_acc, m_run = lax.fori_loop(0, qt, body, (qt, neg, m_run))
        pv_phase(cur, m_acc, m_run)
        finalize_fn(qt)
        acc_ref[...] = jnp.zeros_like(acc_ref)
        nxt = jnp.minimum(qt + 1, nq - 1)
        return qk_phase(nxt, nxt, neg, True)

    lax.fori_loop(0, nq, tile, qk_phase(0, 0, neg, True))


def _causal_mask(rows, cols, row0):
    row = lax.broadcasted_iota(jnp.int32, (rows, cols), 0) + row0
    col = lax.broadcasted_iota(jnp.int32, (rows, cols), 1)
    return col <= row


_NT = (((1,), (1,)), ((), ()))


def _mla_attn_kernel(q_ref, k_ref, v_ref, g_ref, o_ref, acc_ref, qs_ref, s_ref, *, tq, rs):
    kw = 2 * MLA_NOPE
    tr = tq // rs
    nq = q_ref.shape[0] // tq
    for hp in range(2):
        qs_ref[hp, :, :MLA_NOPE] = q_ref[:, hp * MLA_NOPE:(hp + 1) * MLA_NOPE]
        qs_ref[hp, :, MLA_NOPE:] = q_ref[:, 2 * MLA_NOPE:]
    chains = [(hp, r) for hp in range(2) for r in range(rs)]

    def scores(qt, kc, diag):
        k0 = pl.multiple_of(kc * tq, tq)
        out = []
        for hp, r in chains:
            q = qs_ref[hp, pl.ds(pl.multiple_of(qt * tq + r * tr, tr), tr), :]
            k = k_ref[pl.ds(k0, tq), hp * kw:(hp + 1) * kw]
            s = lax.dot_general(q, k, _NT, preferred_element_type=F32)
            if diag:
                s = jnp.where(_causal_mask(tr, tq, r * tr), s, NEG)
            out.append(s)
        return out

    def values(ci, kc):
        hp = chains[ci][0]
        return v_ref[pl.ds(pl.multiple_of(kc * tq, tq), tq), hp * kw:(hp + 1) * kw]

    def finalize(qt):
        for ci, (hp, r) in enumerate(chains):
            rows = pl.ds(pl.multiple_of(qt * tq + r * tr, tr), tr)
            acc = acc_ref[ci]
            o = acc[:, :MLA_V] / acc[:, MLA_V:MLA_V + 1]
            gate = g_ref[rows, hp * MLA_V:(hp + 1) * MLA_V].astype(F32)
            o_ref[rows, hp * MLA_V:(hp + 1) * MLA_V] = (o * gate).astype(BF16)

    _flash_pipeline(len(chains), nq, scores, values, finalize, s_ref, acc_ref)


def _mla_attn(qm, kk, vv, rest, nb, s):
    tq = ATTN_TQ
    pair_w = 2 * (MLA_NOPE + MLA_ROPE)
    kw = 4 * MLA_NOPE
    gate_blk = GM_OFF // (2 * MLA_V)
    rs = MLA_ROW_SPLIT
    return pl.pallas_call(
        functools.partial(_mla_attn_kernel, tq=tq, rs=rs),
        grid=(nb, MLA_HEADS // 2),
        in_specs=[pl.BlockSpec((s, pair_w), lambda b, hh: (b, hh)),
                  pl.BlockSpec((s, kw), lambda b, hh: (b, hh)),
                  pl.BlockSpec((s, kw), lambda b, hh: (b, hh)),
                  pl.BlockSpec((s, 2 * MLA_V), lambda b, hh: (b, gate_blk + hh))],
        out_specs=pl.BlockSpec((s, 2 * MLA_V), lambda b, hh: (b, hh)),
        out_shape=jax.ShapeDtypeStruct((nb * s, MLA_HEADS * MLA_V), BF16),
        scratch_shapes=[pltpu.VMEM((2 * rs, tq // rs, 2 * MLA_V), F32),
                        pltpu.VMEM((2, s, 2 * MLA_NOPE), BF16),
                        pltpu.VMEM((2 * rs, tq // rs, tq), F32)],
        compiler_params=_cparams(("arbitrary", "arbitrary")),
        name="mla_attn",
    )(qm, kk, vv, rest)


def _diff_attn_kernel(q_ref, k_ref, v_ref, g_ref, pq_ref, pk_ref, sl_ref, lam_ref, gs_ref,
                      o_ref, vaug_ref, acc_ref, qz_ref, s_ref, *, tq):
    seq = q_ref.shape[0]
    nq = seq // tq
    lane_v = lax.broadcasted_iota(jnp.int32, (seq, DIFF_V), 1)
    ones_col = jnp.where(lane_v == 0, 1.0, 0.0).astype(BF16)
    lane_q = lax.broadcasted_iota(jnp.int32, (seq, 2 * DIFF_QK), 1)
    slopes2 = []
    for hp in range(2):
        vaug_ref[hp, :, :DIFF_V] = v_ref[:, hp * DIFF_V:(hp + 1) * DIFF_V]
        vaug_ref[hp, :, DIFF_V:] = ones_col
        q = q_ref[:, hp * 2 * DIFF_QK:(hp + 1) * 2 * DIFF_QK]
        qz_ref[2 * hp] = jnp.where(lane_q < DIFF_QK, q, 0).astype(BF16)
        qz_ref[2 * hp + 1] = jnp.where(lane_q >= DIFF_QK, q, 0).astype(BF16)
        slopes2.append(sl_ref[0, hp:hp + 1, 0:1] * LOG2E)

    lq = lam_ref[...]
    lam = (jnp.exp(jnp.sum(lq[0:1] * lq[1:2], axis=-1, keepdims=True))
           - jnp.exp(jnp.sum(lq[2:3] * lq[3:4], axis=-1, keepdims=True)) + LAMBDA_INIT)

    def scores(qt, kc, diag):
        q0 = pl.multiple_of(qt * tq, tq)
        k0 = pl.multiple_of(kc * tq, tq)
        pq = pq_ref[pl.ds(q0, tq), :]
        pk = pk_ref[0, pl.ds(kc, 1), :]
        mask = _causal_mask(tq, tq, 0) if diag else None
        out = []
        for hp in range(2):
            k = k_ref[pl.ds(k0, tq), hp * 2 * DIFF_QK:(hp + 1) * 2 * DIFF_QK]
            bias = jnp.abs(slopes2[hp] * pq - slopes2[hp] * pk)
            for c in range(2):
                q = qz_ref[2 * hp + c, pl.ds(q0, tq), :]
                s = lax.dot_general(q, k, _NT, preferred_element_type=F32) - bias
                out.append(jnp.where(mask, s, NEG) if diag else s)
        return out

    def values(ci, kc):
        return vaug_ref[ci // 2, pl.ds(pl.multiple_of(kc * tq, tq), tq), :]

    def finalize(qt):
        rows = pl.ds(pl.multiple_of(qt * tq, tq), tq)
        for hp in range(2):
            a1 = acc_ref[2 * hp]
            a2 = acc_ref[2 * hp + 1]
            o = (a1[:, :DIFF_V] / a1[:, DIFF_V:DIFF_V + 1]
                 - lam * (a2[:, :DIFF_V] / a2[:, DIFF_V:DIFF_V + 1]))
            ms_o = jnp.mean(o * o, axis=-1, keepdims=True)
            o = o * lax.rsqrt(ms_o + EPS) * gs_ref[...] * (1.0 - LAMBDA_INIT)
            gate = g_ref[rows, hp * DIFF_V:(hp + 1) * DIFF_V].astype(F32)
            o_ref[rows, hp * DIFF_V:(hp + 1) * DIFF_V] = (o * gate).astype(BF16)

    _flash_pipeline(4, nq, scores, values, finalize, s_ref, acc_ref)


def _diff_attn(rest, pos_col, pos_row, slopes, lam_par, g_subln, nb, s):
    tq = ATTN_TQ
    nq = s // tq
    pw = 2 * DIFF_V
    return pl.pallas_call(
        functools.partial(_diff_attn_kernel, tq=tq),
        grid=(nb, DIFF_HEADS // 2),
        in_specs=[pl.BlockSpec((s, pw), lambda b, hh: (b, QD_OFF // pw + hh)),
                  pl.BlockSpec((s, pw), lambda b, hh: (b, KD_OFF // pw + hh)),
                  pl.BlockSpec((s, pw), lambda b, hh: (b, VD_OFF // pw + hh)),
                  pl.BlockSpec((s, pw), lambda b, hh: (b, GD_OFF // pw + hh)),
                  pl.BlockSpec((s, 1), lambda b, hh: (b, 0)),
                  pl.BlockSpec((1, nq, tq), lambda b, hh: (b, 0, 0)),
                  pl.BlockSpec((1, 2, LANES), lambda b, hh: (hh, 0, 0)),
                  pl.BlockSpec((4, DIFF_QK), lambda b, hh: (0, 0)),
                  pl.BlockSpec((1, DIFF_V), lambda b, hh: (0, 0))],
        out_specs=pl.BlockSpec((s, pw), lambda b, hh: (b, hh)),
        out_shape=jax.ShapeDtypeStruct((nb * s, DIFF_HEADS * DIFF_V), BF16),
        scratch_shapes=[pltpu.VMEM((2, s, 2 * DIFF_V), BF16),
                        pltpu.VMEM((4, tq, 2 * DIFF_V), F32),
                        pltpu.VMEM((4, s, 2 * DIFF_QK), BF16),
                        pltpu.VMEM((4, tq, tq), F32)],
        compiler_params=_cparams(("arbitrary", "arbitrary")),
        name="diff_attn",
    )(rest, rest, rest, rest, pos_col, pos_row, slopes, lam_par, g_subln)


def _merge_kernel(a1_ref, a2_ref, w1_ref, w2_ref, s1_ref, s2_ref, o_ref):
    y1 = jnp.dot(a1_ref[...], w1_ref[...], preferred_element_type=F32)
    y2 = jnp.dot(a2_ref[...], w2_ref[...], preferred_element_type=F32)
    o_ref[...] = (s1_ref[...].astype(F32) * y1 + s2_ref[...].astype(F32) * y2).astype(BF16)


def _merge(og_mla, og_diff, w1, w2, rest):
    m, k_dim = og_mla.shape
    n = w1.shape[1]
    tm, tn = 1024, 1024
    return pl.pallas_call(
        _merge_kernel,
        grid=(m // tm, n // tn),
        in_specs=[pl.BlockSpec((tm, k_dim), lambda i, j: (i, 0)),
                  pl.BlockSpec((tm, k_dim), lambda i, j: (i, 0)),
                  pl.BlockSpec((k_dim, tn), lambda i, j: (0, j)),
                  pl.BlockSpec((k_dim, tn), lambda i, j: (0, j)),
                  pl.BlockSpec((tm, tn), lambda i, j: (i, MGM_OFF // tn + j)),
                  pl.BlockSpec((tm, tn), lambda i, j: (i, MGD_OFF // tn + j))],
        out_specs=pl.BlockSpec((tm, tn), lambda i, j: (i, j)),
        out_shape=jax.ShapeDtypeStruct((m, n), BF16),
        compiler_params=_cparams(("arbitrary", "arbitrary")),
        name="merge",
    )(og_mla, og_diff, w1, w2, rest, rest)


def _out_kernel(a_ref, w_ref, x_ref, ada_ref, g_ref, o_ref):
    y = jnp.dot(a_ref[...], w_ref[...], preferred_element_type=F32)
    ms = jnp.mean(y * y, axis=-1, keepdims=True)
    yn = y * lax.rsqrt(ms + EPS) * g_ref[...]
    o_ref[0] = x_ref[0] + ada_ref[0, 2:3, :] * yn


def _out(merged, w_out, x, ada3, g_post):
    nb, s, d = x.shape
    tm = 512
    ns = s // tm
    return pl.pallas_call(
        _out_kernel,
        grid=(nb, ns),
        in_specs=[pl.BlockSpec((tm, d), lambda b, i: (b * ns + i, 0)),
                  pl.BlockSpec((d, d), lambda b, i: (0, 0)),
                  pl.BlockSpec((1, tm, d), lambda b, i: (b, i, 0)),
                  pl.BlockSpec((1, 3, d), lambda b, i: (b, 0, 0)),
                  pl.BlockSpec((1, d), lambda b, i: (0, 0))],
        out_specs=pl.BlockSpec((1, tm, d), lambda b, i: (b, i, 0)),
        out_shape=jax.ShapeDtypeStruct((nb, s, d), F32),
        compiler_params=_cparams(("arbitrary", "arbitrary")),
        name="out_proj",
    )(merged, w_out, x, ada3, g_post)


def _prep_weights(w_in, w_ukv):
    qcols = w_in[:, :Q_MLA_W].reshape(D_MODEL, MLA_HEADS // 2, 2, MLA_NOPE + MLA_ROPE)
    nope = qcols[..., :MLA_NOPE].reshape(D_MODEL, MLA_HEADS // 2, 2 * MLA_NOPE)
    rope = qcols[..., MLA_NOPE:].reshape(D_MODEL, MLA_HEADS // 2, 2 * MLA_ROPE)
    w_q = jnp.concatenate([nope, rope], axis=-1).reshape(D_MODEL, Q_MLA_W).astype(BF16)
    w_kvr = jnp.pad(w_in[:, Q_MLA_W:REST_OFF], ((0, 0), (0, KVR_PAD - KVR_W))).astype(BF16)
    w_rest = w_in[:, REST_OFF:].astype(BF16)
    ukv = w_ukv.reshape(KV_RANK, MLA_HEADS, MLA_NOPE + MLA_V)
    w_kn = ukv[..., :MLA_NOPE].reshape(KV_RANK, MLA_HEADS * MLA_NOPE).astype(BF16)
    w_v = ukv[..., MLA_NOPE:].reshape(KV_RANK, MLA_HEADS * MLA_V).astype(BF16)
    return w_q, w_kvr, w_rest, w_kn, w_v


def kernel(x, c, positions, w_ada, b_ada, g_pre, w_in, g_kv, w_ukv, lambda_q1, lambda_k1,
           lambda_q2, lambda_k2, g_subln, w_o_mla, w_o_diff, w_out, g_post):
    nb, s, d = x.shape
    depth = w_in.shape[0]
    half = MLA_ROPE // 2
    inv = ROPE_THETA ** (-jnp.arange(half, dtype=F32) / half)
    inv_tab = jnp.tile(inv, LANES // half).reshape(1, LANES)
    slopes = 2.0 ** (-8.0 * jnp.arange(1, DIFF_HEADS + 1, dtype=F32) / DIFF_HEADS)
    slopes = jnp.broadcast_to(slopes.reshape(DIFF_HEADS // 2, 2, 1), (DIFF_HEADS // 2, 2, LANES))
    pos_col = positions.reshape(nb * s, 1)
    pos_colf = pos_col.astype(F32)
    pos_row = positions.astype(F32).reshape(nb, s // ATTN_TQ, ATTN_TQ)

    for l in range(depth):
        w_q, w_kvr, w_rest, w_kn, w_v = _prep_weights(w_in[l], w_ukv[l])
        ada3 = _ada(c, w_ada[l], b_ada[l]).reshape(nb, 3, d)
        h, cos_tab, sin_tab = _norm(x, ada3, g_pre[l].reshape(1, d), pos_col, inv_tab)
        qm, kvr, rest = _proj(h, w_q, w_kvr, w_rest, cos_tab, sin_tab)
        kk, vv = _kv(kvr, g_kv[l].reshape(1, KV_RANK), w_kn, w_v, cos_tab, sin_tab)
        og_mla = _mla_attn(qm, kk, vv, rest, nb, s)
        lam_par = jnp.stack([lambda_q1[l], lambda_k1[l], lambda_q2[l], lambda_k2[l]]).astype(F32)
        og_diff = _diff_attn(rest, pos_colf, pos_row, slopes, lam_par,
                             g_subln[l].reshape(1, DIFF_V), nb, s)
        merged = _merge(og_mla, og_diff, w_o_mla[l].astype(BF16), w_o_diff[l].astype(BF16), rest)
        x = _out(merged, w_out[l].astype(BF16), x, ada3, g_post[l].reshape(1, d))
    return x
```

```python
import functools
import math

import jax
import jax.numpy as jnp
from jax import lax
from jax.experimental import pallas as pl
from jax.experimental.pallas import tpu as pltpu

F32 = jnp.float32
BF16 = jnp.bfloat16

D_MODEL = 2048
MLA_HEADS = 8
MLA_NOPE = 128
MLA_ROPE = 64
MLA_V = 128
KV_RANK = 512
ROPE_THETA = 10000.0
DIFF_HEADS = 8
DIFF_QK = 64
DIFF_V = 128
EPS = 1e-6
NEG = -1e30
LAMBDA_INIT = 0.8 - 0.6 * math.exp(-0.3 * 0)

LANES = 128
SUBLANES = 8
VMEM_LIMIT = 56 * 1024 * 1024
ATTN_TQ = 512
MLA_ROW_SPLIT = 2
LOG2E = math.log2(math.e)

Q_MLA_W = MLA_HEADS * (MLA_NOPE + MLA_ROPE)
KVR_W = KV_RANK + MLA_ROPE
KVR_PAD = 640
REST_OFF = Q_MLA_W + KVR_W
REST_SHIFT = REST_OFF % LANES
QD_OFF, KD_OFF, VD_OFF, GM_OFF, GD_OFF, MGM_OFF, MGD_OFF = 0, 1024, 2048, 3072, 4096, 5120, 7168
REST_W = 9216


def _cparams(sem):
    return pltpu.CompilerParams(dimension_semantics=sem, vmem_limit_bytes=VMEM_LIMIT)


def _ada_kernel(cb_ref, w_ref, b_ref, o_ref):
    k_dim, tn = w_ref.shape
    nb = cb_ref.shape[0]
    nchunk = tn // LANES

    def body(i, accs):
        k0 = pl.multiple_of(i * SUBLANES, SUBLANES)
        out = []
        for b in range(nb):
            cv = cb_ref[b, pl.ds(k0, SUBLANES), :]
            for j in range(nchunk):
                wv = w_ref[pl.ds(k0, SUBLANES), j * LANES:(j + 1) * LANES]
                out.append(accs[b * nchunk + j] + wv * cv)
        return tuple(out)

    init = tuple(jnp.zeros((SUBLANES, LANES), F32) for _ in range(nb * nchunk))
    accs = lax.fori_loop(0, k_dim // SUBLANES, body, init, unroll=8)
    for b in range(nb):
        row = jnp.concatenate(
            [jnp.sum(accs[b * nchunk + j], axis=0, keepdims=True) for j in range(nchunk)], axis=1)
        o_ref[b:b + 1, :] = row + b_ref[...]


def _ada(c, w, bias):
    nb, k_dim = c.shape
    n = w.shape[1]
    tn = 512
    cb = jnp.broadcast_to(c[:, :, None], (nb, k_dim, LANES))
    return pl.pallas_call(
        _ada_kernel,
        grid=(n // tn,),
        in_specs=[pl.BlockSpec((nb, k_dim, LANES), lambda j: (0, 0, 0)),
                  pl.BlockSpec((k_dim, tn), lambda j: (0, j)),
                  pl.BlockSpec((1, tn), lambda j: (0, j))],
        out_specs=pl.BlockSpec((nb, tn), lambda j: (0, j)),
        out_shape=jax.ShapeDtypeStruct((nb, n), F32),
        compiler_params=_cparams(("arbitrary",)),
        name="ada",
    )(cb, w, bias.reshape(1, n))


def _norm_kernel(x_ref, ada_ref, g_ref, pos_ref, inv_ref, h_ref, cos_ref, sin_ref):
    x = x_ref[0]
    ms = jnp.mean(x * x, axis=-1, keepdims=True)
    y = x * lax.rsqrt(ms + EPS) * g_ref[...]
    shift = ada_ref[0, 0:1, :]
    scale = ada_ref[0, 1:2, :]
    h_ref[...] = (y * (1.0 + scale) + shift).astype(BF16)
    ang = pos_ref[...].astype(F32) * inv_ref[...]
    lane = lax.broadcasted_iota(jnp.int32, ang.shape, 1)
    sign = jnp.where((lane % MLA_ROPE) < MLA_ROPE // 2, -1.0, 1.0).astype(F32)
    cos_ref[...] = jnp.cos(ang)
    sin_ref[...] = jnp.sin(ang) * sign


def _norm(x, ada3, g_pre, pos_col, inv_tab):
    nb, s, d = x.shape
    ts = 512
    ns = s // ts
    row = lambda b, i: (b * ns + i, 0)
    return pl.pallas_call(
        _norm_kernel,
        grid=(nb, ns),
        in_specs=[pl.BlockSpec((1, ts, d), lambda b, i: (b, i, 0)),
                  pl.BlockSpec((1, 3, d), lambda b, i: (b, 0, 0)),
                  pl.BlockSpec((1, d), lambda b, i: (0, 0)),
                  pl.BlockSpec((ts, 1), row),
                  pl.BlockSpec((1, LANES), lambda b, i: (0, 0))],
        out_specs=[pl.BlockSpec((ts, d), row),
                   pl.BlockSpec((ts, LANES), row),
                   pl.BlockSpec((ts, LANES), row)],
        out_shape=[jax.ShapeDtypeStruct((nb * s, d), BF16),
                   jax.ShapeDtypeStruct((nb * s, LANES), F32),
                   jax.ShapeDtypeStruct((nb * s, LANES), F32)],
        compiler_params=_cparams(("arbitrary", "arbitrary")),
        name="prenorm",
    )(x, ada3, g_pre, pos_col, inv_tab)


def _rope_cols(r, cos, sin_signed):
    lane = lax.broadcasted_iota(jnp.int32, r.shape, 1)
    half = MLA_ROPE // 2
    partner = jnp.where((lane % MLA_ROPE) < half,
                        pltpu.roll(r, LANES - half, 1), pltpu.roll(r, half, 1))
    return r * cos + partner * sin_signed


def _qmla_kernel(a_ref, w_ref, cos_ref, sin_ref, o_ref, wb_ref, *, scale):
    @pl.when(pl.program_id(1) == 0)
    def _():
        hd = MLA_NOPE + MLA_ROPE
        wb_ref[:, :MLA_NOPE] = w_ref[:, :MLA_NOPE].astype(BF16)
        wb_ref[:, MLA_NOPE:2 * MLA_NOPE] = w_ref[:, hd:hd + MLA_NOPE].astype(BF16)
        wb_ref[:, 2 * MLA_NOPE:2 * MLA_NOPE + MLA_ROPE] = w_ref[:, MLA_NOPE:hd].astype(BF16)
        wb_ref[:, 2 * MLA_NOPE + MLA_ROPE:] = w_ref[:, hd + MLA_NOPE:].astype(BF16)

    acc = jnp.dot(a_ref[...], wb_ref[...], preferred_element_type=F32)
    rr = _rope_cols(acc[:, 2 * MLA_NOPE:], cos_ref[...], sin_ref[...])
    o_ref[:, :2 * MLA_NOPE] = (acc[:, :2 * MLA_NOPE] * scale).astype(BF16)
    o_ref[:, 2 * MLA_NOPE:] = (rr * scale).astype(BF16)


def _kvr_kernel(a_ref, wc_ref, wr_ref, o_ref, wb_ref):
    @pl.when(pl.program_id(0) == 0)
    def _():
        wb_ref[:, :KV_RANK] = wc_ref[...].astype(BF16)
        lane = lax.broadcasted_iota(jnp.int32, wr_ref.shape, 1)
        wb_ref[:, KV_RANK:] = jnp.where(lane < MLA_ROPE, wr_ref[...], 0.0).astype(BF16)

    o_ref[...] = jnp.dot(a_ref[...], wb_ref[...], preferred_element_type=F32)


def _rest_kernel(a_ref, wa_ref, wn_ref, o_ref, wb_ref, *, tn):
    j = pl.program_id(0)

    @pl.when(pl.program_id(1) == 0)
    def _():
        wb_ref[:, :tn - REST_SHIFT] = wa_ref[:, REST_SHIFT:].astype(BF16)
        wb_ref[:, tn - REST_SHIFT:] = wn_ref[:, :REST_SHIFT].astype(BF16)

    acc = jnp.dot(a_ref[...], wb_ref[...], preferred_element_type=F32)

    @pl.when(j < KD_OFF // tn)
    def _():
        o_ref[...] = (acc * (DIFF_QK ** -0.5 * LOG2E)).astype(BF16)

    @pl.when(jnp.logical_and(j >= KD_OFF // tn, j < GM_OFF // tn))
    def _():
        o_ref[...] = acc.astype(BF16)

    @pl.when(jnp.logical_and(j >= GM_OFF // tn, j < MGM_OFF // tn))
    def _():
        o_ref[...] = (acc * jax.nn.sigmoid(acc)).astype(BF16)

    @pl.when(j >= MGM_OFF // tn)
    def _():
        o_ref[...] = jax.nn.sigmoid(acc).astype(BF16)


def _proj(h, w_in, cos_tab, sin_tab):
    m, k_dim = h.shape
    tm = 1024
    pair_w = 2 * (MLA_NOPE + MLA_ROPE)
    q_scale = (MLA_NOPE + MLA_ROPE) ** -0.5 * LOG2E
    qm = pl.pallas_call(
        functools.partial(_qmla_kernel, scale=q_scale),
        grid=(Q_MLA_W // pair_w, m // tm),
        in_specs=[pl.BlockSpec((tm, k_dim), lambda j, i: (i, 0)),
                  pl.BlockSpec((k_dim, pair_w), lambda j, i: (0, j)),
                  pl.BlockSpec((tm, LANES), lambda j, i: (i, 0)),
                  pl.BlockSpec((tm, LANES), lambda j, i: (i, 0))],
        out_specs=pl.BlockSpec((tm, pair_w), lambda j, i: (i, j)),
        out_shape=jax.ShapeDtypeStruct((m, Q_MLA_W), BF16),
        scratch_shapes=[pltpu.VMEM((k_dim, pair_w), BF16)],
        compiler_params=_cparams(("arbitrary", "arbitrary")),
        name="proj_qmla",
    )(h, w_in, cos_tab, sin_tab)
    kvr = pl.pallas_call(
        _kvr_kernel,
        grid=(m // tm,),
        in_specs=[pl.BlockSpec((tm, k_dim), lambda i: (i, 0)),
                  pl.BlockSpec((k_dim, KV_RANK), lambda i: (0, Q_MLA_W // KV_RANK)),
                  pl.BlockSpec((k_dim, LANES), lambda i: (0, (Q_MLA_W + KV_RANK) // LANES))],
        out_specs=pl.BlockSpec((tm, KVR_PAD), lambda i: (i, 0)),
        out_shape=jax.ShapeDtypeStruct((m, KVR_PAD), F32),
        scratch_shapes=[pltpu.VMEM((k_dim, KVR_PAD), BF16)],
        compiler_params=_cparams(("arbitrary",)),
        name="proj_kvr",
    )(h, w_in, w_in)
    tn = 1024
    a_blk = (REST_OFF - REST_SHIFT) // tn
    rest = pl.pallas_call(
        functools.partial(_rest_kernel, tn=tn),
        grid=(REST_W // tn, m // tm),
        in_specs=[pl.BlockSpec((tm, k_dim), lambda j, i: (i, 0)),
                  pl.BlockSpec((k_dim, tn), lambda j, i: (0, a_blk + j)),
                  pl.BlockSpec((k_dim, LANES), lambda j, i: (0, (a_blk + j + 1) * (tn // LANES)))],
        out_specs=pl.BlockSpec((tm, tn), lambda j, i: (i, j)),
        out_shape=jax.ShapeDtypeStruct((m, REST_W), BF16),
        scratch_shapes=[pltpu.VMEM((k_dim, tn), BF16)],
        compiler_params=_cparams(("arbitrary", "arbitrary")),
        name="proj_rest",
    )(h, w_in, w_in)
    return qm, kvr, rest


def _kv_kernel(p_ref, g_ref, w_ref, cos_ref, sin_ref, k_ref, v_ref, wb_ref):
    @pl.when(pl.program_id(0) == 0)
    def _():
        wb_ref[...] = w_ref[...].astype(BF16)

    p = p_ref[...]
    ckv = p[:, :KV_RANK]
    ms = jnp.mean(ckv * ckv, axis=-1, keepdims=True)
    n = (ckv * lax.rsqrt(ms + EPS) * g_ref[...]).astype(BF16)
    kv = jnp.dot(n, wb_ref[...], preferred_element_type=F32)
    kr_even = _rope_cols(p[:, KV_RANK:], cos_ref[...], sin_ref[...])
    kr_odd = pltpu.roll(kr_even, MLA_ROPE, 1)
    lane = lax.broadcasted_iota(jnp.int32, kr_even.shape, 1)
    ones_col = jnp.where(lane == 0, 1.0, 0.0).astype(BF16)
    kw = MLA_NOPE + MLA_V
    for hd in range(MLA_HEADS):
        k_ref[:, hd * kw:hd * kw + MLA_NOPE] = kv[:, hd * kw:hd * kw + MLA_NOPE].astype(BF16)
        k_ref[:, hd * kw + MLA_NOPE:(hd + 1) * kw] = (kr_even if hd % 2 == 0 else kr_odd).astype(BF16)
        v_ref[:, hd * kw:hd * kw + MLA_V] = kv[:, hd * kw + MLA_NOPE:(hd + 1) * kw].astype(BF16)
        v_ref[:, hd * kw + MLA_V:(hd + 1) * kw] = ones_col


def _kv(kvr, g_kv, w_ukv, cos_tab, sin_tab):
    m = kvr.shape[0]
    tm = 512
    kw = MLA_HEADS * (MLA_NOPE + MLA_V)
    return pl.pallas_call(
        _kv_kernel,
        grid=(m // tm,),
        in_specs=[pl.BlockSpec((tm, KVR_PAD), lambda i: (i, 0)),
                  pl.BlockSpec((1, KV_RANK), lambda i: (0, 0)),
                  pl.BlockSpec((KV_RANK, kw), lambda i: (0, 0)),
                  pl.BlockSpec((tm, LANES), lambda i: (i, 0)),
                  pl.BlockSpec((tm, LANES), lambda i: (i, 0))],
        out_specs=[pl.BlockSpec((tm, kw), lambda i: (i, 0)),
                   pl.BlockSpec((tm, kw), lambda i: (i, 0))],
        out_shape=[jax.ShapeDtypeStruct((m, kw), BF16),
                   jax.ShapeDtypeStruct((m, kw), BF16)],
        scratch_shapes=[pltpu.VMEM((KV_RANK, kw), BF16)],
        compiler_params=_cparams(("arbitrary",)),
        name="kv_up",
    )(kvr, g_kv, w_ukv, cos_tab, sin_tab)


def _flash_pipeline(n_chains, nq, score_fn, value_fn, finalize_fn, s_ref, acc_ref):
    rows = s_ref.shape[1]

    def qk_phase(qt, kc, m_run, diag):
        out = []
        for ci, s in enumerate(score_fn(qt, kc, diag)):
            s_ref[ci] = s
            out.append(jnp.maximum(m_run[ci], jnp.max(s, axis=-1, keepdims=True)))
        return tuple(out)

    def pv_phase(kc, m_acc, m_run):
        for ci in range(n_chains):
            p = jnp.exp2(s_ref[ci] - m_run[ci])
            alpha = jnp.exp2(m_acc[ci] - m_run[ci])
            pv = jnp.dot(p.astype(BF16), value_fn(ci, kc), preferred_element_type=F32)
            acc_ref[ci] = alpha * acc_ref[ci] + pv

    neg = tuple(jnp.full((rows, 1), NEG, F32) for _ in range(n_chains))
    acc_ref[...] = jnp.zeros_like(acc_ref)

    def tile(qt, m_run):
        def body(kc, carry):
            cur, m_acc, m_run = carry
            pv_phase(cur, m_acc, m_run)
            return kc, m_run, qk_phase(qt, kc, m_run, False)

        cur, m_acc, m_run = lax.fori_loop(0, qt, body, (qt, neg, m_run))
        pv_phase(cur, m_acc, m_run)
        finalize_fn(qt)
        acc_ref[...] = jnp.zeros_like(acc_ref)
        nxt = jnp.minimum(qt + 1, nq - 1)
        return qk_phase(nxt, nxt, neg, True)

    lax.fori_loop(0, nq, tile, qk_phase(0, 0, neg, True))


def _causal_mask(rows, cols, row0):
    row = lax.broadcasted_iota(jnp.int32, (rows, cols), 0) + row0
    col = lax.broadcasted_iota(jnp.int32, (rows, cols), 1)
    return col <= row


_NT = (((1,), (1,)), ((), ()))


def _mla_attn_kernel(q_ref, k_ref, v_ref, g_ref, o_ref, acc_ref, qs_ref, s_ref, *, tq, rs):
    kw = 2 * MLA_NOPE
    tr = tq // rs
    nq = q_ref.shape[0] // tq
    for hp in range(2):
        qs_ref[hp, :, :MLA_NOPE] = q_ref[:, hp * MLA_NOPE:(hp + 1) * MLA_NOPE]
        qs_ref[hp, :, MLA_NOPE:] = q_ref[:, 2 * MLA_NOPE:]
    chains = [(hp, r) for hp in range(2) for r in range(rs)]

    def scores(qt, kc, diag):
        k0 = pl.multiple_of(kc * tq, tq)
        out = []
        for hp, r in chains:
            q = qs_ref[hp, pl.ds(pl.multiple_of(qt * tq + r * tr, tr), tr), :]
            k = k_ref[pl.ds(k0, tq), hp * kw:(hp + 1) * kw]
            s = lax.dot_general(q, k, _NT, preferred_element_type=F32)
            if diag:
                s = jnp.where(_causal_mask(tr, tq, r * tr), s, NEG)
            out.append(s)
        return out

    def values(ci, kc):
        hp = chains[ci][0]
        return v_ref[pl.ds(pl.multiple_of(kc * tq, tq), tq), hp * kw:(hp + 1) * kw]

    def finalize(qt):
        for ci, (hp, r) in enumerate(chains):
            rows = pl.ds(pl.multiple_of(qt * tq + r * tr, tr), tr)
            acc = acc_ref[ci]
            o = acc[:, :MLA_V] / acc[:, MLA_V:MLA_V + 1]
            gate = g_ref[rows, hp * MLA_V:(hp + 1) * MLA_V].astype(F32)
            o_ref[rows, hp * MLA_V:(hp + 1) * MLA_V] = (o * gate).astype(BF16)

    _flash_pipeline(len(chains), nq, scores, values, finalize, s_ref, acc_ref)


def _mla_attn(qm, kk, vv, rest, nb, s):
    tq = ATTN_TQ
    pair_w = 2 * (MLA_NOPE + MLA_ROPE)
    kw = 4 * MLA_NOPE
    gate_blk = GM_OFF // (2 * MLA_V)
    rs = MLA_ROW_SPLIT
    return pl.pallas_call(
        functools.partial(_mla_attn_kernel, tq=tq, rs=rs),
        grid=(nb, MLA_HEADS // 2),
        in_specs=[pl.BlockSpec((s, pair_w), lambda b, hh: (b, hh)),
                  pl.BlockSpec((s, kw), lambda b, hh: (b, hh)),
                  pl.BlockSpec((s, kw), lambda b, hh: (b, hh)),
                  pl.BlockSpec((s, 2 * MLA_V), lambda b, hh: (b, gate_blk + hh))],
        out_specs=pl.BlockSpec((s, 2 * MLA_V), lambda b, hh: (b, hh)),
        out_shape=jax.ShapeDtypeStruct((nb * s, MLA_HEADS * MLA_V), BF16),
        scratch_shapes=[pltpu.VMEM((2 * rs, tq // rs, 2 * MLA_V), F32),
                        pltpu.VMEM((2, s, 2 * MLA_NOPE), BF16),
                        pltpu.VMEM((2 * rs, tq // rs, tq), F32)],
        compiler_params=_cparams(("arbitrary", "arbitrary")),
        name="mla_attn",
    )(qm, kk, vv, rest)


def _diff_attn_kernel(q_ref, k_ref, v_ref, g_ref, pq_ref, pk_ref, sl_ref, lam_ref, gs_ref,
                      o_ref, vaug_ref, acc_ref, qz_ref, s_ref, *, tq):
    seq = q_ref.shape[0]
    nq = seq // tq
    lane_v = lax.broadcasted_iota(jnp.int32, (seq, DIFF_V), 1)
    ones_col = jnp.where(lane_v == 0, 1.0, 0.0).astype(BF16)
    lane_q = lax.broadcasted_iota(jnp.int32, (seq, 2 * DIFF_QK), 1)
    slopes2 = []
    for hp in range(2):
        vaug_ref[hp, :, :DIFF_V] = v_ref[:, hp * DIFF_V:(hp + 1) * DIFF_V]
        vaug_ref[hp, :, DIFF_V:] = ones_col
        q = q_ref[:, hp * 2 * DIFF_QK:(hp + 1) * 2 * DIFF_QK]
        qz_ref[2 * hp] = jnp.where(lane_q < DIFF_QK, q, 0).astype(BF16)
        qz_ref[2 * hp + 1] = jnp.where(lane_q >= DIFF_QK, q, 0).astype(BF16)
        slopes2.append(sl_ref[0, hp:hp + 1, 0:1] * LOG2E)

    lq = lam_ref[...]
    lam = (jnp.exp(jnp.sum(lq[0:1] * lq[1:2], axis=-1, keepdims=True))
           - jnp.exp(jnp.sum(lq[2:3] * lq[3:4], axis=-1, keepdims=True)) + LAMBDA_INIT)

    def scores(qt, kc, diag):
        q0 = pl.multiple_of(qt * tq, tq)
        k0 = pl.multiple_of(kc * tq, tq)
        pq = pq_ref[pl.ds(q0, tq), :]
        pk = pk_ref[0, pl.ds(kc, 1), :]
        mask = _causal_mask(tq, tq, 0) if diag else None
        out = []
        for hp in range(2):
            k = k_ref[pl.ds(k0, tq), hp * 2 * DIFF_QK:(hp + 1) * 2 * DIFF_QK]
            bias = jnp.abs(slopes2[hp] * pq - slopes2[hp] * pk)
            for c in range(2):
                q = qz_ref[2 * hp + c, pl.ds(q0, tq), :]
                s = lax.dot_general(q, k, _NT, preferred_element_type=F32) - bias
                out.append(jnp.where(mask, s, NEG) if diag else s)
        return out

    def values(ci, kc):
        return vaug_ref[ci // 2, pl.ds(pl.multiple_of(kc * tq, tq), tq), :]

    def finalize(qt):
        rows = pl.ds(pl.multiple_of(qt * tq, tq), tq)
        for hp in range(2):
            a1 = acc_ref[2 * hp]
            a2 = acc_ref[2 * hp + 1]
            o = (a1[:, :DIFF_V] / a1[:, DIFF_V:DIFF_V + 1]
                 - lam * (a2[:, :DIFF_V] / a2[:, DIFF_V:DIFF_V + 1]))
            ms_o = jnp.mean(o * o, axis=-1, keepdims=True)
            o = o * lax.rsqrt(ms_o + EPS) * gs_ref[...] * (1.0 - LAMBDA_INIT)
            gate = g_ref[rows, hp * DIFF_V:(hp + 1) * DIFF_V].astype(F32)
            o_ref[rows, hp * DIFF_V:(hp + 1) * DIFF_V] = (o * gate).astype(BF16)

    _flash_pipeline(4, nq, scores, values, finalize, s_ref, acc_ref)


def _diff_attn(rest, pos_col, pos_row, slopes, lam_par, g_subln, nb, s):
    tq = ATTN_TQ
    nq = s // tq
    pw = 2 * DIFF_V
    return pl.pallas_call(
        functools.partial(_diff_attn_kernel, tq=tq),
        grid=(nb, DIFF_HEADS // 2),
        in_specs=[pl.BlockSpec((s, pw), lambda b, hh: (b, QD_OFF // pw + hh)),
                  pl.BlockSpec((s, pw), lambda b, hh: (b, KD_OFF // pw + hh)),
                  pl.BlockSpec((s, pw), lambda b, hh: (b, VD_OFF // pw + hh)),
                  pl.BlockSpec((s, pw), lambda b, hh: (b, GD_OFF // pw + hh)),
                  pl.BlockSpec((s, 1), lambda b, hh: (b, 0)),
                  pl.BlockSpec((1, nq, tq), lambda b, hh: (b, 0, 0)),
                  pl.BlockSpec((1, 2, LANES), lambda b, hh: (hh, 0, 0)),
                  pl.BlockSpec((4, DIFF_QK), lambda b, hh: (0, 0)),
                  pl.BlockSpec((1, DIFF_V), lambda b, hh: (0, 0))],
        out_specs=pl.BlockSpec((s, pw), lambda b, hh: (b, hh)),
        out_shape=jax.ShapeDtypeStruct((nb * s, DIFF_HEADS * DIFF_V), BF16),
        scratch_shapes=[pltpu.VMEM((2, s, 2 * DIFF_V), BF16),
                        pltpu.VMEM((4, tq, 2 * DIFF_V), F32),
                        pltpu.VMEM((4, s, 2 * DIFF_QK), BF16),
                        pltpu.VMEM((4, tq, tq), F32)],
        compiler_params=_cparams(("arbitrary", "arbitrary")),
        name="diff_attn",
    )(rest, rest, rest, rest, pos_col, pos_row, slopes, lam_par, g_subln)


def _merge_kernel(a1_ref, a2_ref, w1_ref, w2_ref, s1_ref, s2_ref, o_ref, wb_ref):
    @pl.when(pl.program_id(1) == 0)
    def _():
        wb_ref[0] = w1_ref[...].astype(BF16)
        wb_ref[1] = w2_ref[...].astype(BF16)

    y1 = jnp.dot(a1_ref[...], wb_ref[0], preferred_element_type=F32)
    y2 = jnp.dot(a2_ref[...], wb_ref[1], preferred_element_type=F32)
    o_ref[...] = (s1_ref[...].astype(F32) * y1 + s2_ref[...].astype(F32) * y2).astype(BF16)


def _merge(og_mla, og_diff, w1, w2, rest):
    m, k_dim = og_mla.shape
    n = w1.shape[1]
    tm, tn = 1024, 1024
    return pl.pallas_call(
        _merge_kernel,
        grid=(n // tn, m // tm),
        in_specs=[pl.BlockSpec((tm, k_dim), lambda j, i: (i, 0)),
                  pl.BlockSpec((tm, k_dim), lambda j, i: (i, 0)),
                  pl.BlockSpec((k_dim, tn), lambda j, i: (0, j)),
                  pl.BlockSpec((k_dim, tn), lambda j, i: (0, j)),
                  pl.BlockSpec((tm, tn), lambda j, i: (i, MGM_OFF // tn + j)),
                  pl.BlockSpec((tm, tn), lambda j, i: (i, MGD_OFF // tn + j))],
        out_specs=pl.BlockSpec((tm, tn), lambda j, i: (i, j)),
        out_shape=jax.ShapeDtypeStruct((m, n), BF16),
        scratch_shapes=[pltpu.VMEM((2, k_dim, tn), BF16)],
        compiler_params=_cparams(("arbitrary", "arbitrary")),
        name="merge",
    )(og_mla, og_diff, w1, w2, rest, rest)


def _out_kernel(a_ref, w_ref, x_ref, ada_ref, g_ref, o_ref, wb_ref):
    @pl.when(jnp.logical_and(pl.program_id(0) == 0, pl.program_id(1) == 0))
    def _():
        wb_ref[...] = w_ref[...].astype(BF16)

    y = jnp.dot(a_ref[...], wb_ref[...], preferred_element_type=F32)
    ms = jnp.mean(y * y, axis=-1, keepdims=True)
    yn = y * lax.rsqrt(ms + EPS) * g_ref[...]
    o_ref[0] = x_ref[0] + ada_ref[0, 2:3, :] * yn


def _out(merged, w_out, x, ada3, g_post):
    nb, s, d = x.shape
    tm = 512
    ns = s // tm
    return pl.pallas_call(
        _out_kernel,
        grid=(nb, ns),
        in_specs=[pl.BlockSpec((tm, d), lambda b, i: (b * ns + i, 0)),
                  pl.BlockSpec((d, d), lambda b, i: (0, 0), pipeline_mode=pl.Buffered(1)),
                  pl.BlockSpec((1, tm, d), lambda b, i: (b, i, 0)),
                  pl.BlockSpec((1, 3, d), lambda b, i: (b, 0, 0)),
                  pl.BlockSpec((1, d), lambda b, i: (0, 0))],
        out_specs=pl.BlockSpec((1, tm, d), lambda b, i: (b, i, 0)),
        out_shape=jax.ShapeDtypeStruct((nb, s, d), F32),
        scratch_shapes=[pltpu.VMEM((d, d), BF16)],
        compiler_params=_cparams(("arbitrary", "arbitrary")),
        name="out_proj",
    )(merged, w_out, x, ada3, g_post)


def kernel(x, c, positions, w_ada, b_ada, g_pre, w_in, g_kv, w_ukv, lambda_q1, lambda_k1,
           lambda_q2, lambda_k2, g_subln, w_o_mla, w_o_diff, w_out, g_post):
    nb, s, d = x.shape
    depth = w_in.shape[0]
    half = MLA_ROPE // 2
    inv = ROPE_THETA ** (-jnp.arange(half, dtype=F32) / half)
    inv_tab = jnp.tile(inv, LANES // half).reshape(1, LANES)
    slopes = 2.0 ** (-8.0 * jnp.arange(1, DIFF_HEADS + 1, dtype=F32) / DIFF_HEADS)
    slopes = jnp.broadcast_to(slopes.reshape(DIFF_HEADS // 2, 2, 1), (DIFF_HEADS // 2, 2, LANES))
    pos_col = positions.reshape(nb * s, 1)
    pos_colf = pos_col.astype(F32)
    pos_row = positions.astype(F32).reshape(nb, s // ATTN_TQ, ATTN_TQ)

    for l in range(depth):
        ada3 = _ada(c, w_ada[l], b_ada[l]).reshape(nb, 3, d)
        h, cos_tab, sin_tab = _norm(x, ada3, g_pre[l].reshape(1, d), pos_col, inv_tab)
        qm, kvr, rest = _proj(h, w_in[l], cos_tab, sin_tab)
        kk, vv = _kv(kvr, g_kv[l].reshape(1, KV_RANK), w_ukv[l], cos_tab, sin_tab)
        og_mla = _mla_attn(qm, kk, vv, rest, nb, s)
        lam_par = jnp.stack([lambda_q1[l], lambda_k1[l], lambda_q2[l], lambda_k2[l]]).astype(F32)
        og_diff = _diff_attn(rest, pos_colf, pos_row, slopes, lam_par,
                             g_subln[l].reshape(1, DIFF_V), nb, s)
        merged = _merge(og_mla, og_diff, w_o_mla[l], w_o_diff[l], rest)
        x = _out(merged, w_out[l], x, ada3, g_post[l].reshape(1, d))
    return x
```

```python
import functools
import math

import jax
import jax.numpy as jnp
from jax import lax
from jax.experimental import pallas as pl
from jax.experimental.pallas import tpu as pltpu

F32 = jnp.float32
BF16 = jnp.bfloat16

D_MODEL = 2048
MLA_HEADS = 8
MLA_NOPE = 128
MLA_ROPE = 64
MLA_V = 128
KV_RANK = 512
ROPE_THETA = 10000.0
DIFF_HEADS = 8
DIFF_QK = 64
DIFF_V = 128
EPS = 1e-6
NEG = -1e30
LAMBDA_INIT = 0.8 - 0.6 * math.exp(-0.3 * 0)

LANES = 128
SUBLANES = 8
VMEM_LIMIT = 56 * 1024 * 1024
ATTN_TQ = 512
MLA_ROW_SPLIT = 2
LOG2E = math.log2(math.e)

Q_MLA_W = MLA_HEADS * (MLA_NOPE + MLA_ROPE)
KVR_W = KV_RANK + MLA_ROPE
KVR_PAD = 640
REST_OFF = Q_MLA_W + KVR_W
REST_SHIFT = REST_OFF % LANES
QD_OFF, KD_OFF, VD_OFF, GM_OFF, GD_OFF, MGM_OFF, MGD_OFF = 0, 1024, 2048, 3072, 4096, 5120, 7168
REST_W = 9216


def _cparams(sem):
    return pltpu.CompilerParams(dimension_semantics=sem, vmem_limit_bytes=VMEM_LIMIT)


def _ada_kernel(cb_ref, w_ref, b_ref, o_ref):
    k_dim, tn = w_ref.shape
    nb = cb_ref.shape[0]
    nchunk = tn // LANES

    def body(i, accs):
        k0 = pl.multiple_of(i * SUBLANES, SUBLANES)
        out = []
        for b in range(nb):
            cv = cb_ref[b, pl.ds(k0, SUBLANES), :]
            for j in range(nchunk):
                wv = w_ref[pl.ds(k0, SUBLANES), j * LANES:(j + 1) * LANES]
                out.append(accs[b * nchunk + j] + wv * cv)
        return tuple(out)

    init = tuple(jnp.zeros((SUBLANES, LANES), F32) for _ in range(nb * nchunk))
    accs = lax.fori_loop(0, k_dim // SUBLANES, body, init, unroll=8)
    for b in range(nb):
        row = jnp.concatenate(
            [jnp.sum(accs[b * nchunk + j], axis=0, keepdims=True) for j in range(nchunk)], axis=1)
        o_ref[b:b + 1, :] = row + b_ref[...]


def _ada(c, w, bias):
    nb, k_dim = c.shape
    n = w.shape[1]
    tn = 512
    cb = jnp.broadcast_to(c[:, :, None], (nb, k_dim, LANES))
    return pl.pallas_call(
        _ada_kernel,
        grid=(n // tn,),
        in_specs=[pl.BlockSpec((nb, k_dim, LANES), lambda j: (0, 0, 0)),
                  pl.BlockSpec((k_dim, tn), lambda j: (0, j)),
                  pl.BlockSpec((1, tn), lambda j: (0, j))],
        out_specs=pl.BlockSpec((nb, tn), lambda j: (0, j)),
        out_shape=jax.ShapeDtypeStruct((nb, n), F32),
        compiler_params=_cparams(("arbitrary",)),
        name="ada",
    )(cb, w, bias.reshape(1, n))


def _norm_kernel(x_ref, ada_ref, g_ref, pos_ref, inv_ref, h_ref, cos_ref, sin_ref):
    x = x_ref[0]
    ms = jnp.mean(x * x, axis=-1, keepdims=True)
    y = x * lax.rsqrt(ms + EPS) * g_ref[...]
    shift = ada_ref[0, 0:1, :]
    scale = ada_ref[0, 1:2, :]
    h_ref[...] = (y * (1.0 + scale) + shift).astype(BF16)
    ang = pos_ref[...].astype(F32) * inv_ref[...]
    lane = lax.broadcasted_iota(jnp.int32, ang.shape, 1)
    sign = jnp.where((lane % MLA_ROPE) < MLA_ROPE // 2, -1.0, 1.0).astype(F32)
    cos_ref[...] = jnp.cos(ang)
    sin_ref[...] = jnp.sin(ang) * sign


def _norm(x, ada3, g_pre, pos_col, inv_tab):
    nb, s, d = x.shape
    ts = 512
    ns = s // ts
    row = lambda b, i: (b * ns + i, 0)
    return pl.pallas_call(
        _norm_kernel,
        grid=(nb, ns),
        in_specs=[pl.BlockSpec((1, ts, d), lambda b, i: (b, i, 0)),
                  pl.BlockSpec((1, 3, d), lambda b, i: (b, 0, 0)),
                  pl.BlockSpec((1, d), lambda b, i: (0, 0)),
                  pl.BlockSpec((ts, 1), row),
                  pl.BlockSpec((1, LANES), lambda b, i: (0, 0))],
        out_specs=[pl.BlockSpec((ts, d), row),
                   pl.BlockSpec((ts, LANES), row),
                   pl.BlockSpec((ts, LANES), row)],
        out_shape=[jax.ShapeDtypeStruct((nb * s, d), BF16),
                   jax.ShapeDtypeStruct((nb * s, LANES), F32),
                   jax.ShapeDtypeStruct((nb * s, LANES), F32)],
        compiler_params=_cparams(("arbitrary", "arbitrary")),
        name="prenorm",
    )(x, ada3, g_pre, pos_col, inv_tab)


def _rope_cols(r, cos, sin_signed):
    lane = lax.broadcasted_iota(jnp.int32, r.shape, 1)
    half = MLA_ROPE // 2
    partner = jnp.where((lane % MLA_ROPE) < half,
                        pltpu.roll(r, LANES - half, 1), pltpu.roll(r, half, 1))
    return r * cos + partner * sin_signed


def _qmla_kernel(a_ref, w_ref, cos_ref, sin_ref, o_ref, wb_ref, *, scale):
    @pl.when(pl.program_id(1) == 0)
    def _():
        hd = MLA_NOPE + MLA_ROPE
        wb_ref[:, :MLA_NOPE] = w_ref[:, :MLA_NOPE].astype(BF16)
        wb_ref[:, MLA_NOPE:2 * MLA_NOPE] = w_ref[:, hd:hd + MLA_NOPE].astype(BF16)
        wb_ref[:, 2 * MLA_NOPE:2 * MLA_NOPE + MLA_ROPE] = w_ref[:, MLA_NOPE:hd].astype(BF16)
        wb_ref[:, 2 * MLA_NOPE + MLA_ROPE:] = w_ref[:, hd + MLA_NOPE:].astype(BF16)

    acc = jnp.dot(a_ref[...], wb_ref[...], preferred_element_type=F32)
    rr = _rope_cols(acc[:, 2 * MLA_NOPE:], cos_ref[...], sin_ref[...])
    o_ref[:, :2 * MLA_NOPE] = (acc[:, :2 * MLA_NOPE] * scale).astype(BF16)
    o_ref[:, 2 * MLA_NOPE:] = (rr * scale).astype(BF16)


def _kvr_kernel(a_ref, wc_ref, wr_ref, o_ref, wb_ref):
    @pl.when(pl.program_id(0) == 0)
    def _():
        wb_ref[:, :KV_RANK] = wc_ref[...].astype(BF16)
        lane = lax.broadcasted_iota(jnp.int32, wr_ref.shape, 1)
        wb_ref[:, KV_RANK:] = jnp.where(lane < MLA_ROPE, wr_ref[...], 0.0).astype(BF16)

    o_ref[...] = jnp.dot(a_ref[...], wb_ref[...], preferred_element_type=F32)


def _rest_kernel(a_ref, wa_ref, wn_ref, o_ref, wb_ref, *, tn):
    j = pl.program_id(0)

    @pl.when(pl.program_id(1) == 0)
    def _():
        wb_ref[:, :tn - REST_SHIFT] = wa_ref[:, REST_SHIFT:].astype(BF16)
        wb_ref[:, tn - REST_SHIFT:] = wn_ref[:, :REST_SHIFT].astype(BF16)

    acc = jnp.dot(a_ref[...], wb_ref[...], preferred_element_type=F32)

    @pl.when(j < KD_OFF // tn)
    def _():
        o_ref[...] = (acc * (DIFF_QK ** -0.5 * LOG2E)).astype(BF16)

    @pl.when(jnp.logical_and(j >= KD_OFF // tn, j < GM_OFF // tn))
    def _():
        o_ref[...] = acc.astype(BF16)

    @pl.when(jnp.logical_and(j >= GM_OFF // tn, j < MGM_OFF // tn))
    def _():
        o_ref[...] = (acc * jax.nn.sigmoid(acc)).astype(BF16)

    @pl.when(j >= MGM_OFF // tn)
    def _():
        o_ref[...] = jax.nn.sigmoid(acc).astype(BF16)


def _proj(h, w_in, layer, cos_tab, sin_tab):
    m, k_dim = h.shape
    tm = 1024
    pair_w = 2 * (MLA_NOPE + MLA_ROPE)
    q_scale = (MLA_NOPE + MLA_ROPE) ** -0.5 * LOG2E
    qm = pl.pallas_call(
        functools.partial(_qmla_kernel, scale=q_scale),
        grid=(Q_MLA_W // pair_w, m // tm),
        in_specs=[pl.BlockSpec((tm, k_dim), lambda j, i: (i, 0)),
                  pl.BlockSpec((None, k_dim, pair_w), lambda j, i: (layer, 0, j)),
                  pl.BlockSpec((tm, LANES), lambda j, i: (i, 0)),
                  pl.BlockSpec((tm, LANES), lambda j, i: (i, 0))],
        out_specs=pl.BlockSpec((tm, pair_w), lambda j, i: (i, j)),
        out_shape=jax.ShapeDtypeStruct((m, Q_MLA_W), BF16),
        scratch_shapes=[pltpu.VMEM((k_dim, pair_w), BF16)],
        compiler_params=_cparams(("arbitrary", "arbitrary")),
        name="proj_qmla",
    )(h, w_in, cos_tab, sin_tab)
    kvr = pl.pallas_call(
        _kvr_kernel,
        grid=(m // tm,),
        in_specs=[pl.BlockSpec((tm, k_dim), lambda i: (i, 0)),
                  pl.BlockSpec((None, k_dim, KV_RANK), lambda i: (layer, 0, Q_MLA_W // KV_RANK)),
                  pl.BlockSpec((None, k_dim, LANES),
                               lambda i: (layer, 0, (Q_MLA_W + KV_RANK) // LANES))],
        out_specs=pl.BlockSpec((tm, KVR_PAD), lambda i: (i, 0)),
        out_shape=jax.ShapeDtypeStruct((m, KVR_PAD), F32),
        scratch_shapes=[pltpu.VMEM((k_dim, KVR_PAD), BF16)],
        compiler_params=_cparams(("arbitrary",)),
        name="proj_kvr",
    )(h, w_in, w_in)
    tn = 1024
    a_blk = (REST_OFF - REST_SHIFT) // tn
    rest = pl.pallas_call(
        functools.partial(_rest_kernel, tn=tn),
        grid=(REST_W // tn, m // tm),
        in_specs=[pl.BlockSpec((tm, k_dim), lambda j, i: (i, 0)),
                  pl.BlockSpec((None, k_dim, tn), lambda j, i: (layer, 0, a_blk + j)),
                  pl.BlockSpec((None, k_dim, LANES),
                               lambda j, i: (layer, 0, (a_blk + j + 1) * (tn // LANES)))],
        out_specs=pl.BlockSpec((tm, tn), lambda j, i: (i, j)),
        out_shape=jax.ShapeDtypeStruct((m, REST_W), BF16),
        scratch_shapes=[pltpu.VMEM((k_dim, tn), BF16)],
        compiler_params=_cparams(("arbitrary", "arbitrary")),
        name="proj_rest",
    )(h, w_in, w_in)
    return qm, kvr, rest


def _kv_kernel(p_ref, g_ref, w_ref, cos_ref, sin_ref, k_ref, v_ref, wb_ref):
    @pl.when(pl.program_id(0) == 0)
    def _():
        wb_ref[...] = w_ref[...].astype(BF16)

    p = p_ref[...]
    ckv = p[:, :KV_RANK]
    ms = jnp.mean(ckv * ckv, axis=-1, keepdims=True)
    n = (ckv * lax.rsqrt(ms + EPS) * g_ref[...]).astype(BF16)
    kv = jnp.dot(n, wb_ref[...], preferred_element_type=F32)
    kr_even = _rope_cols(p[:, KV_RANK:], cos_ref[...], sin_ref[...])
    kr_odd = pltpu.roll(kr_even, MLA_ROPE, 1)
    lane = lax.broadcasted_iota(jnp.int32, kr_even.shape, 1)
    ones_col = jnp.where(lane == 0, 1.0, 0.0).astype(BF16)
    kw = MLA_NOPE + MLA_V
    for hd in range(MLA_HEADS):
        k_ref[:, hd * kw:hd * kw + MLA_NOPE] = kv[:, hd * kw:hd * kw + MLA_NOPE].astype(BF16)
        k_ref[:, hd * kw + MLA_NOPE:(hd + 1) * kw] = (kr_even if hd % 2 == 0 else kr_odd).astype(BF16)
        v_ref[:, hd * kw:hd * kw + MLA_V] = kv[:, hd * kw + MLA_NOPE:(hd + 1) * kw].astype(BF16)
        v_ref[:, hd * kw + MLA_V:(hd + 1) * kw] = ones_col


def _kv(kvr, g_kv, w_ukv, cos_tab, sin_tab):
    m = kvr.shape[0]
    tm = 512
    kw = MLA_HEADS * (MLA_NOPE + MLA_V)
    return pl.pallas_call(
        _kv_kernel,
        grid=(m // tm,),
        in_specs=[pl.BlockSpec((tm, KVR_PAD), lambda i: (i, 0)),
                  pl.BlockSpec((1, KV_RANK), lambda i: (0, 0)),
                  pl.BlockSpec((KV_RANK, kw), lambda i: (0, 0)),
                  pl.BlockSpec((tm, LANES), lambda i: (i, 0)),
                  pl.BlockSpec((tm, LANES), lambda i: (i, 0))],
        out_specs=[pl.BlockSpec((tm, kw), lambda i: (i, 0)),
                   pl.BlockSpec((tm, kw), lambda i: (i, 0))],
        out_shape=[jax.ShapeDtypeStruct((m, kw), BF16),
                   jax.ShapeDtypeStruct((m, kw), BF16)],
        scratch_shapes=[pltpu.VMEM((KV_RANK, kw), BF16)],
        compiler_params=_cparams(("arbitrary",)),
        name="kv_up",
    )(kvr, g_kv, w_ukv, cos_tab, sin_tab)


def _flash_pipeline(n_chains, nq, diag_fn, loops, value_fn, finalize_fn,
                    s_ref, acc_ref, mpart_ref, macc_ref):
    chunk = s_ref.shape[2]

    def lane_tiles(x, n):
        return jnp.concatenate([x] * n, axis=1)

    def qk_phase(scores):
        for ci, s in enumerate(scores):
            s_ref[ci] = s
            part = s[:, :LANES]
            for j in range(1, chunk // LANES):
                part = jnp.maximum(part, s[:, j * LANES:(j + 1) * LANES])
            mpart_ref[ci] = part

    def pv_phase(kc):
        for ci in range(n_chains):
            m_acc = macc_ref[ci]
            m_run = jnp.maximum(m_acc, jnp.max(mpart_ref[ci], axis=-1, keepdims=True))
            macc_ref[ci] = m_run
            p = jnp.exp2(s_ref[ci] - lane_tiles(m_run, chunk // LANES))
            alpha = jnp.exp2(m_acc - m_run)
            pv = jnp.dot(p.astype(BF16), value_fn(ci, kc), preferred_element_type=F32)
            acc_ref[ci] = lane_tiles(alpha, acc_ref.shape[2] // LANES) * acc_ref[ci] + pv

    def reset():
        acc_ref[...] = jnp.zeros_like(acc_ref)
        macc_ref[...] = jnp.full(macc_ref.shape, NEG, F32)

    def tile(qt, _):
        cur = qt
        for trips_fn, score_fn in loops:
            def step(kc, cur, score_fn=score_fn):
                pv_phase(cur)
                qk_phase(score_fn(qt, kc))
                return kc

            cur = lax.fori_loop(0, trips_fn(qt), step, cur)
        pv_phase(cur)
        finalize_fn(qt)
        reset()
        nxt = jnp.minimum(qt + 1, nq - 1)
        qk_phase(diag_fn(nxt))
        return 0

    reset()
    qk_phase(diag_fn(0))
    lax.fori_loop(0, nq, tile, 0)


def _causal_mask(rows, cols, row0):
    row = lax.broadcasted_iota(jnp.int32, (rows, cols), 0) + row0
    col = lax.broadcasted_iota(jnp.int32, (rows, cols), 1)
    return col <= row


_NT = (((1,), (1,)), ((), ()))


def _mla_attn_kernel(q_ref, k_ref, v_ref, g_ref, o_ref, acc_ref, qs_ref, s_ref, mpart_ref, macc_ref,
                     *, tq, rs):
    kw = 2 * MLA_NOPE
    tr = tq // rs
    nq = q_ref.shape[0] // tq
    for hp in range(2):
        qs_ref[hp, :, :MLA_NOPE] = q_ref[:, hp * MLA_NOPE:(hp + 1) * MLA_NOPE]
        qs_ref[hp, :, MLA_NOPE:] = q_ref[:, 2 * MLA_NOPE:]
    chains = [(hp, r) for hp in range(2) for r in range(rs)]

    def scores(qt, kc, diag):
        k0 = pl.multiple_of(kc * tq, tq)
        out = []
        for hp, r in chains:
            q = qs_ref[hp, pl.ds(pl.multiple_of(qt * tq + r * tr, tr), tr), :]
            k = k_ref[pl.ds(k0, tq), hp * kw:(hp + 1) * kw]
            s = lax.dot_general(q, k, _NT, preferred_element_type=F32)
            if diag:
                s = jnp.where(_causal_mask(tr, tq, r * tr), s, NEG)
            out.append(s)
        return out

    def values(ci, kc):
        hp = chains[ci][0]
        return v_ref[pl.ds(pl.multiple_of(kc * tq, tq), tq), hp * kw:(hp + 1) * kw]

    def finalize(qt):
        for ci, (hp, r) in enumerate(chains):
            rows = pl.ds(pl.multiple_of(qt * tq + r * tr, tr), tr)
            acc = acc_ref[ci]
            o = acc[:, :MLA_V] / acc[:, MLA_V:MLA_V + 1]
            gate = g_ref[rows, hp * MLA_V:(hp + 1) * MLA_V].astype(F32)
            o_ref[rows, hp * MLA_V:(hp + 1) * MLA_V] = (o * gate).astype(BF16)

    _flash_pipeline(len(chains), nq, lambda qt: scores(qt, qt, True),
                    [(lambda qt: qt, lambda qt, kc: scores(qt, kc, False))],
                    values, finalize, s_ref, acc_ref, mpart_ref, macc_ref)


def _mla_attn(qm, kk, vv, rest, nb, s):
    tq = ATTN_TQ
    pair_w = 2 * (MLA_NOPE + MLA_ROPE)
    kw = 4 * MLA_NOPE
    gate_blk = GM_OFF // (2 * MLA_V)
    rs = MLA_ROW_SPLIT
    return pl.pallas_call(
        functools.partial(_mla_attn_kernel, tq=tq, rs=rs),
        grid=(nb, MLA_HEADS // 2),
        in_specs=[pl.BlockSpec((s, pair_w), lambda b, hh: (b, hh)),
                  pl.BlockSpec((s, kw), lambda b, hh: (b, hh)),
                  pl.BlockSpec((s, kw), lambda b, hh: (b, hh)),
                  pl.BlockSpec((s, 2 * MLA_V), lambda b, hh: (b, gate_blk + hh))],
        out_specs=pl.BlockSpec((s, 2 * MLA_V), lambda b, hh: (b, hh)),
        out_shape=jax.ShapeDtypeStruct((nb * s, MLA_HEADS * MLA_V), BF16),
        scratch_shapes=[pltpu.VMEM((2 * rs, tq // rs, 2 * MLA_V), F32),
                        pltpu.VMEM((2, s, 2 * MLA_NOPE), BF16),
                        pltpu.VMEM((2 * rs, tq // rs, tq), F32),
                        pltpu.VMEM((2 * rs, tq // rs, LANES), F32),
                        pltpu.VMEM((2 * rs, tq // rs, LANES), F32)],
        compiler_params=_cparams(("arbitrary", "arbitrary")),
        name="mla_attn",
    )(qm, kk, vv, rest)


def _bf16_pieces(x):
    p1 = x.astype(BF16)
    r = x - p1.astype(F32)
    p2 = r.astype(BF16)
    return p1.astype(F32), p2.astype(F32), r - p2.astype(F32)


def _diff_attn_kernel(ord_ref, q_ref, k_ref, v_ref, g_ref, pq_ref, pk_ref, sl_ref, lam_ref, gs_ref,
                      o_ref, vaug_ref, kaug_ref, acc_ref, qf_ref, s_ref, mpart_ref, macc_ref, *, tq):
    seq = q_ref.shape[0]
    nq = seq // tq
    lane = lax.broadcasted_iota(jnp.int32, (tq, LANES), 1)
    ones_col = jnp.where(lane == 0, 1.0, 0.0).astype(BF16)
    slopes2 = [sl_ref[0, hp:hp + 1, 0:1] * LOG2E for hp in range(2)]
    c_pieces = [_bf16_pieces(c) for c in slopes2]

    def pick(sel, x3):
        return jnp.where(sel == 0, x3[0], jnp.where(sel == 1, x3[1], x3[2]))

    def bias_lanes(hp, rows, key_side):
        pos = _bf16_pieces(pq_ref[rows, :])
        lane_b = lane - 9
        if key_side:
            lo, hi = pick(lane % 3, pos), pick(lane_b // 3, c_pieces[hp])
        else:
            lo, hi = pick(lane // 3, c_pieces[hp]), pick(lane_b % 3, [-p for p in pos])
        return jnp.where(lane < 9, lo, jnp.where(lane < 18, hi, 0.0)).astype(BF16)

    def stage_keys(kc, _):
        rows = pl.ds(pl.multiple_of(kc * tq, tq), tq)
        for hp in range(2):
            vaug_ref[hp, rows, :DIFF_V] = v_ref[rows, hp * DIFF_V:(hp + 1) * DIFF_V]
            vaug_ref[hp, rows, DIFF_V:] = ones_col
            kaug_ref[hp, rows, :2 * DIFF_QK] = k_ref[rows, hp * 2 * DIFF_QK:(hp + 1) * 2 * DIFF_QK]
            kaug_ref[hp, rows, 2 * DIFF_QK:] = bias_lanes(hp, rows, True)
        return 0

    lax.fori_loop(0, nq, stage_keys, 0)

    def stage_queries(qt):
        rows = pl.ds(pl.multiple_of(qt * tq, tq), tq)
        for hp in range(2):
            q = q_ref[rows, hp * 2 * DIFF_QK:(hp + 1) * 2 * DIFF_QK]
            q_side = bias_lanes(hp, rows, False)
            qf_ref[2 * hp, :, :2 * DIFF_QK] = jnp.where(lane < DIFF_QK, q, 0).astype(BF16)
            qf_ref[2 * hp + 1, :, :2 * DIFF_QK] = jnp.where(lane >= DIFF_QK, q, 0).astype(BF16)
            qf_ref[2 * hp, :, 2 * DIFF_QK:] = q_side
            qf_ref[2 * hp + 1, :, 2 * DIFF_QK:] = q_side

    lq = lam_ref[...]
    lam = (jnp.exp(jnp.sum(lq[0:1] * lq[1:2], axis=-1, keepdims=True))
           - jnp.exp(jnp.sum(lq[2:3] * lq[3:4], axis=-1, keepdims=True)) + LAMBDA_INIT)

    def scores(qt, kc, diag):
        q0 = pl.multiple_of(qt * tq, tq)
        k0 = pl.multiple_of(kc * tq, tq)
        pq = pq_ref[pl.ds(q0, tq), :]
        pk = pk_ref[0, pl.ds(kc, 1), :]
        mask = _causal_mask(tq, tq, 0) if diag else None
        out = []
        for hp in range(2):
            k = k_ref[pl.ds(k0, tq), hp * 2 * DIFF_QK:(hp + 1) * 2 * DIFF_QK]
            bias = jnp.abs(slopes2[hp] * pq - slopes2[hp] * pk)
            for c in range(2):
                q = qf_ref[2 * hp + c, :, :2 * DIFF_QK]
                s = lax.dot_general(q, k, _NT, preferred_element_type=F32) - bias
                out.append(jnp.where(mask, s, NEG) if diag else s)
        return out

    def scores_diag(qt):
        stage_queries(qt)
        return scores(qt, qt, True)

    def scores_ordered(qt, kc):
        k0 = pl.multiple_of(kc * tq, tq)
        out = []
        for hp in range(2):
            k = kaug_ref[hp, pl.ds(k0, tq), :]
            for c in range(2):
                out.append(lax.dot_general(qf_ref[2 * hp + c], k, _NT, preferred_element_type=F32))
        return out

    batch = pl.program_id(0)

    def trips_ordered(qt):
        return jnp.where(ord_ref[batch, qt] != 0, qt, 0)

    def values(ci, kc):
        return vaug_ref[ci // 2, pl.ds(pl.multiple_of(kc * tq, tq), tq), :]

    def finalize(qt):
        rows = pl.ds(pl.multiple_of(qt * tq, tq), tq)
        for hp in range(2):
            a1 = acc_ref[2 * hp]
            a2 = acc_ref[2 * hp + 1]
            o = (a1[:, :DIFF_V] / a1[:, DIFF_V:DIFF_V + 1]
                 - lam * (a2[:, :DIFF_V] / a2[:, DIFF_V:DIFF_V + 1]))
            ms_o = jnp.mean(o * o, axis=-1, keepdims=True)
            o = o * lax.rsqrt(ms_o + EPS) * gs_ref[...] * (1.0 - LAMBDA_INIT)
            gate = g_ref[rows, hp * DIFF_V:(hp + 1) * DIFF_V].astype(F32)
            o_ref[rows, hp * DIFF_V:(hp + 1) * DIFF_V] = (o * gate).astype(BF16)

    _flash_pipeline(4, nq, scores_diag,
                    [(trips_ordered, scores_ordered),
                     (lambda qt: qt - trips_ordered(qt), lambda qt, kc: scores(qt, kc, False))],
                    values, finalize, s_ref, acc_ref, mpart_ref, macc_ref)


def _diff_attn(rest, pos_col, pos_row, ordered, slopes, lam_par, g_subln, nb, s):
    tq = ATTN_TQ
    nq = s // tq
    pw = 2 * DIFF_V
    return pl.pallas_call(
        functools.partial(_diff_attn_kernel, tq=tq),
        grid=(nb, DIFF_HEADS // 2),
        in_specs=[pl.BlockSpec(memory_space=pltpu.SMEM),
                  pl.BlockSpec((s, pw), lambda b, hh: (b, QD_OFF // pw + hh)),
                  pl.BlockSpec((s, pw), lambda b, hh: (b, KD_OFF // pw + hh)),
                  pl.BlockSpec((s, pw), lambda b, hh: (b, VD_OFF // pw + hh)),
                  pl.BlockSpec((s, pw), lambda b, hh: (b, GD_OFF // pw + hh)),
                  pl.BlockSpec((s, 1), lambda b, hh: (b, 0)),
                  pl.BlockSpec((1, nq, tq), lambda b, hh: (b, 0, 0)),
                  pl.BlockSpec((1, 2, LANES), lambda b, hh: (hh, 0, 0)),
                  pl.BlockSpec((4, DIFF_QK), lambda b, hh: (0, 0)),
                  pl.BlockSpec((1, DIFF_V), lambda b, hh: (0, 0))],
        out_specs=pl.BlockSpec((s, pw), lambda b, hh: (b, hh)),
        out_shape=jax.ShapeDtypeStruct((nb * s, DIFF_HEADS * DIFF_V), BF16),
        scratch_shapes=[pltpu.VMEM((2, s, 2 * DIFF_V), BF16),
                        pltpu.VMEM((2, s, 4 * DIFF_QK), BF16),
                        pltpu.VMEM((4, tq, 2 * DIFF_V), F32),
                        pltpu.VMEM((4, tq, 4 * DIFF_QK), BF16),
                        pltpu.VMEM((4, tq, tq), F32),
                        pltpu.VMEM((4, tq, LANES), F32),
                        pltpu.VMEM((4, tq, LANES), F32)],
        compiler_params=_cparams(("arbitrary", "arbitrary")),
        name="diff_attn",
    )(ordered, rest, rest, rest, rest, pos_col, pos_row, slopes, lam_par, g_subln)


def _merge_kernel(a1_ref, a2_ref, w1_ref, w2_ref, s1_ref, s2_ref, o_ref, wb_ref):
    @pl.when(pl.program_id(1) == 0)
    def _():
        wb_ref[0] = w1_ref[...].astype(BF16)
        wb_ref[1] = w2_ref[...].astype(BF16)

    y1 = jnp.dot(a1_ref[...], wb_ref[0], preferred_element_type=F32)
    y2 = jnp.dot(a2_ref[...], wb_ref[1], preferred_element_type=F32)
    o_ref[...] = (s1_ref[...].astype(F32) * y1 + s2_ref[...].astype(F32) * y2).astype(BF16)


def _merge(og_mla, og_diff, w1, w2, rest):
    m, k_dim = og_mla.shape
    n = w1.shape[1]
    tm, tn = 1024, 1024
    return pl.pallas_call(
        _merge_kernel,
        grid=(n // tn, m // tm),
        in_specs=[pl.BlockSpec((tm, k_dim), lambda j, i: (i, 0)),
                  pl.BlockSpec((tm, k_dim), lambda j, i: (i, 0)),
                  pl.BlockSpec((k_dim, tn), lambda j, i: (0, j)),
                  pl.BlockSpec((k_dim, tn), lambda j, i: (0, j)),
                  pl.BlockSpec((tm, tn), lambda j, i: (i, MGM_OFF // tn + j)),
                  pl.BlockSpec((tm, tn), lambda j, i: (i, MGD_OFF // tn + j))],
        out_specs=pl.BlockSpec((tm, tn), lambda j, i: (i, j)),
        out_shape=jax.ShapeDtypeStruct((m, n), BF16),
        scratch_shapes=[pltpu.VMEM((2, k_dim, tn), BF16)],
        compiler_params=_cparams(("arbitrary", "arbitrary")),
        name="merge",
    )(og_mla, og_diff, w1, w2, rest, rest)


def _out_kernel(a_ref, w_ref, x_ref, ada_ref, g_ref, o_ref, wb_ref):
    @pl.when(jnp.logical_and(pl.program_id(0) == 0, pl.program_id(1) == 0))
    def _():
        wb_ref[...] = w_ref[...].astype(BF16)

    y = jnp.dot(a_ref[...], wb_ref[...], preferred_element_type=F32)
    ms = jnp.mean(y * y, axis=-1, keepdims=True)
    yn = y * lax.rsqrt(ms + EPS) * g_ref[...]
    o_ref[0] = x_ref[0] + ada_ref[0, 2:3, :] * yn


def _out(merged, w_out, x, ada3, g_post):
    nb, s, d = x.shape
    tm = 512
    ns = s // tm
    return pl.pallas_call(
        _out_kernel,
        grid=(nb, ns),
        in_specs=[pl.BlockSpec((tm, d), lambda b, i: (b * ns + i, 0)),
                  pl.BlockSpec((d, d), lambda b, i: (0, 0), pipeline_mode=pl.Buffered(1)),
                  pl.BlockSpec((1, tm, d), lambda b, i: (b, i, 0)),
                  pl.BlockSpec((1, 3, d), lambda b, i: (b, 0, 0)),
                  pl.BlockSpec((1, d), lambda b, i: (0, 0))],
        out_specs=pl.BlockSpec((1, tm, d), lambda b, i: (b, i, 0)),
        out_shape=jax.ShapeDtypeStruct((nb, s, d), F32),
        scratch_shapes=[pltpu.VMEM((d, d), BF16)],
        compiler_params=_cparams(("arbitrary", "arbitrary")),
        name="out_proj",
    )(merged, w_out, x, ada3, g_post)


def kernel(x, c, positions, w_ada, b_ada, g_pre, w_in, g_kv, w_ukv, lambda_q1, lambda_k1,
           lambda_q2, lambda_k2, g_subln, w_o_mla, w_o_diff, w_out, g_post):
    nb, s, d = x.shape
    depth = w_in.shape[0]
    half = MLA_ROPE // 2
    inv = ROPE_THETA ** (-jnp.arange(half, dtype=F32) / half)
    inv_tab = jnp.tile(inv, LANES // half).reshape(1, LANES)
    slopes = 2.0 ** (-8.0 * jnp.arange(1, DIFF_HEADS + 1, dtype=F32) / DIFF_HEADS)
    slopes = jnp.broadcast_to(slopes.reshape(DIFF_HEADS // 2, 2, 1), (DIFF_HEADS // 2, 2, LANES))
    pos_col = positions.reshape(nb * s, 1)
    pos_colf = pos_col.astype(F32)
    pos_chunks = positions.reshape(nb, s // ATTN_TQ, ATTN_TQ)
    pos_row = pos_chunks.astype(F32)
    run_max = lax.cummax(pos_chunks.max(axis=-1), axis=1)
    prev_max = jnp.concatenate(
        [jnp.full((nb, 1), jnp.iinfo(jnp.int32).min, jnp.int32), run_max[:, :-1]], axis=1)
    ordered = (prev_max <= pos_chunks.min(axis=-1)).astype(jnp.int32)

    for l in range(depth):
        ada3 = _ada(c, w_ada[l], b_ada[l]).reshape(nb, 3, d)
        h, cos_tab, sin_tab = _norm(x, ada3, g_pre[l].reshape(1, d), pos_col, inv_tab)
        qm, kvr, rest = _proj(h, w_in, l, cos_tab, sin_tab)
        kk, vv = _kv(kvr, g_kv[l].reshape(1, KV_RANK), w_ukv[l], cos_tab, sin_tab)
        og_mla = _mla_attn(qm, kk, vv, rest, nb, s)
        lam_par = jnp.stack([lambda_q1[l], lambda_k1[l], lambda_q2[l], lambda_k2[l]]).astype(F32)
        og_diff = _diff_attn(rest, pos_colf, pos_row, ordered, slopes, lam_par,
                             g_subln[l].reshape(1, DIFF_V), nb, s)
        merged = _merge(og_mla, og_diff, w_o_mla[l], w_o_diff[l], rest)
        x = _out(merged, w_out[l], x, ada3, g_post[l].reshape(1, d))
    return x
```

```python
import functools
import math

import jax
import jax.numpy as jnp
from jax import lax
from jax.experimental import pallas as pl
from jax.experimental.pallas import tpu as pltpu

F32 = jnp.float32
BF16 = jnp.bfloat16

D_MODEL = 2048
MLA_HEADS = 8
MLA_NOPE = 128
MLA_ROPE = 64
MLA_V = 128
KV_RANK = 512
ROPE_THETA = 10000.0
DIFF_HEADS = 8
DIFF_QK = 64
DIFF_V = 128
EPS = 1e-6
NEG = -1e30
LAMBDA_INIT = 0.8 - 0.6 * math.exp(-0.3 * 0)

LANES = 128
SUBLANES = 8
VMEM_LIMIT = 56 * 1024 * 1024
ATTN_TQ = 512
MLA_ROW_SPLIT = 2
LOG2E = math.log2(math.e)

Q_MLA_W = MLA_HEADS * (MLA_NOPE + MLA_ROPE)
KVR_W = KV_RANK + MLA_ROPE
KVR_PAD = 640
REST_OFF = Q_MLA_W + KVR_W
REST_SHIFT = REST_OFF % LANES
QD_OFF, KD_OFF, VD_OFF, GM_OFF, GD_OFF, MGM_OFF, MGD_OFF = 0, 1024, 2048, 3072, 4096, 5120, 7168
REST_W = 9216


def _cparams(sem):
    return pltpu.CompilerParams(dimension_semantics=sem, vmem_limit_bytes=VMEM_LIMIT)


def _ada_kernel(cb_ref, w_ref, b_ref, o_ref):
    k_dim, tn = w_ref.shape
    nb = cb_ref.shape[0]
    nchunk = tn // LANES

    def body(i, accs):
        k0 = pl.multiple_of(i * SUBLANES, SUBLANES)
        out = []
        for b in range(nb):
            cv = cb_ref[b, pl.ds(k0, SUBLANES), :]
            for j in range(nchunk):
                wv = w_ref[pl.ds(k0, SUBLANES), j * LANES:(j + 1) * LANES]
                out.append(accs[b * nchunk + j] + wv * cv)
        return tuple(out)

    init = tuple(jnp.zeros((SUBLANES, LANES), F32) for _ in range(nb * nchunk))
    accs = lax.fori_loop(0, k_dim // SUBLANES, body, init, unroll=8)
    for b in range(nb):
        row = jnp.concatenate(
            [jnp.sum(accs[b * nchunk + j], axis=0, keepdims=True) for j in range(nchunk)], axis=1)
        o_ref[b:b + 1, :] = row + b_ref[...]


def _ada(c, w, bias):
    nb, k_dim = c.shape
    n = w.shape[1]
    tn = 512
    cb = jnp.broadcast_to(c[:, :, None], (nb, k_dim, LANES))
    return pl.pallas_call(
        _ada_kernel,
        grid=(n // tn,),
        in_specs=[pl.BlockSpec((nb, k_dim, LANES), lambda j: (0, 0, 0)),
                  pl.BlockSpec((k_dim, tn), lambda j: (0, j)),
                  pl.BlockSpec((1, tn), lambda j: (0, j))],
        out_specs=pl.BlockSpec((nb, tn), lambda j: (0, j)),
        out_shape=jax.ShapeDtypeStruct((nb, n), F32),
        compiler_params=_cparams(("arbitrary",)),
        name="ada",
    )(cb, w, bias.reshape(1, n))


def _norm_kernel(x_ref, ada_ref, g_ref, pos_ref, inv_ref, h_ref, cos_ref, sin_ref):
    x = x_ref[0]
    ms = jnp.mean(x * x, axis=-1, keepdims=True)
    y = x * lax.rsqrt(ms + EPS) * g_ref[...]
    shift = ada_ref[0, 0:1, :]
    scale = ada_ref[0, 1:2, :]
    h_ref[...] = (y * (1.0 + scale) + shift).astype(BF16)
    ang = pos_ref[...].astype(F32) * inv_ref[...]
    lane = lax.broadcasted_iota(jnp.int32, ang.shape, 1)
    sign = jnp.where((lane % MLA_ROPE) < MLA_ROPE // 2, -1.0, 1.0).astype(F32)
    cos_ref[...] = jnp.cos(ang)
    sin_ref[...] = jnp.sin(ang) * sign


def _norm(x, ada3, g_pre, pos_col, inv_tab):
    nb, s, d = x.shape
    ts = 512
    ns = s // ts
    row = lambda b, i: (b * ns + i, 0)
    return pl.pallas_call(
        _norm_kernel,
        grid=(nb, ns),
        in_specs=[pl.BlockSpec((1, ts, d), lambda b, i: (b, i, 0)),
                  pl.BlockSpec((1, 3, d), lambda b, i: (b, 0, 0)),
                  pl.BlockSpec((1, d), lambda b, i: (0, 0)),
                  pl.BlockSpec((ts, 1), row),
                  pl.BlockSpec((1, LANES), lambda b, i: (0, 0))],
        out_specs=[pl.BlockSpec((ts, d), row),
                   pl.BlockSpec((ts, LANES), row),
                   pl.BlockSpec((ts, LANES), row)],
        out_shape=[jax.ShapeDtypeStruct((nb * s, d), BF16),
                   jax.ShapeDtypeStruct((nb * s, LANES), F32),
                   jax.ShapeDtypeStruct((nb * s, LANES), F32)],
        compiler_params=_cparams(("arbitrary", "arbitrary")),
        name="prenorm",
    )(x, ada3, g_pre, pos_col, inv_tab)


def _rope_cols(r, cos, sin_signed):
    lane = lax.broadcasted_iota(jnp.int32, r.shape, 1)
    half = MLA_ROPE // 2
    partner = jnp.where((lane % MLA_ROPE) < half,
                        pltpu.roll(r, LANES - half, 1), pltpu.roll(r, half, 1))
    return r * cos + partner * sin_signed


def _qmla_kernel(a_ref, w_ref, cos_ref, sin_ref, o_ref, wb_ref, *, scale):
    @pl.when(pl.program_id(1) == 0)
    def _():
        hd = MLA_NOPE + MLA_ROPE
        wb_ref[:MLA_NOPE] = w_ref[:MLA_NOPE].astype(BF16)
        wb_ref[MLA_NOPE:2 * MLA_NOPE] = w_ref[hd:hd + MLA_NOPE].astype(BF16)
        wb_ref[2 * MLA_NOPE:2 * MLA_NOPE + MLA_ROPE] = w_ref[MLA_NOPE:hd].astype(BF16)
        wb_ref[2 * MLA_NOPE + MLA_ROPE:] = w_ref[hd + MLA_NOPE:].astype(BF16)

    acc = lax.dot_general(a_ref[...], wb_ref[...], _NT, preferred_element_type=F32)
    rr = _rope_cols(acc[:, 2 * MLA_NOPE:], cos_ref[...], sin_ref[...])
    o_ref[:, :2 * MLA_NOPE] = (acc[:, :2 * MLA_NOPE] * scale).astype(BF16)
    o_ref[:, 2 * MLA_NOPE:] = (rr * scale).astype(BF16)


def _kvr_kernel(a_ref, w_ref, o_ref, wb_ref):
    @pl.when(pl.program_id(0) == 0)
    def _():
        wb_ref[:KVR_W] = w_ref[...].astype(BF16)
        wb_ref[KVR_W:] = jnp.zeros((KVR_PAD - KVR_W, wb_ref.shape[1]), BF16)

    o_ref[...] = lax.dot_general(a_ref[...], wb_ref[...], _NT, preferred_element_type=F32)


def _rest_kernel(a_ref, w_ref, o_ref, wb_ref, *, tn):
    j = pl.program_id(0)

    @pl.when(pl.program_id(1) == 0)
    def _():
        wb_ref[...] = w_ref[...].astype(BF16)

    acc = lax.dot_general(a_ref[...], wb_ref[...], _NT, preferred_element_type=F32)

    @pl.when(j < KD_OFF // tn)
    def _():
        o_ref[...] = (acc * (DIFF_QK ** -0.5 * LOG2E)).astype(BF16)

    @pl.when(jnp.logical_and(j >= KD_OFF // tn, j < GM_OFF // tn))
    def _():
        o_ref[...] = acc.astype(BF16)

    @pl.when(jnp.logical_and(j >= GM_OFF // tn, j < MGM_OFF // tn))
    def _():
        o_ref[...] = (acc * jax.nn.sigmoid(acc)).astype(BF16)

    @pl.when(j >= MGM_OFF // tn)
    def _():
        o_ref[...] = jax.nn.sigmoid(acc).astype(BF16)


def _proj(h, w_t, cos_tab, sin_tab):
    m, k_dim = h.shape
    tm = 1024
    pair_w = 2 * (MLA_NOPE + MLA_ROPE)
    q_scale = (MLA_NOPE + MLA_ROPE) ** -0.5 * LOG2E

    def row_window(rows, offset_fn):
        return pl.BlockSpec((pl.Element(rows), pl.Element(k_dim)),
                            lambda *g: (pl.multiple_of(offset_fn(*g), SUBLANES), 0))

    qm = pl.pallas_call(
        functools.partial(_qmla_kernel, scale=q_scale),
        grid=(Q_MLA_W // pair_w, m // tm),
        in_specs=[pl.BlockSpec((tm, k_dim), lambda j, i: (i, 0)),
                  pl.BlockSpec((pair_w, k_dim), lambda j, i: (j, 0)),
                  pl.BlockSpec((tm, LANES), lambda j, i: (i, 0)),
                  pl.BlockSpec((tm, LANES), lambda j, i: (i, 0))],
        out_specs=pl.BlockSpec((tm, pair_w), lambda j, i: (i, j)),
        out_shape=jax.ShapeDtypeStruct((m, Q_MLA_W), BF16),
        scratch_shapes=[pltpu.VMEM((pair_w, k_dim), BF16)],
        compiler_params=_cparams(("arbitrary", "arbitrary")),
        name="proj_qmla",
    )(h, w_t, cos_tab, sin_tab)
    kvr = pl.pallas_call(
        _kvr_kernel,
        grid=(m // tm,),
        in_specs=[pl.BlockSpec((tm, k_dim), lambda i: (i, 0)),
                  row_window(KVR_W, lambda i: Q_MLA_W + 0 * i)],
        out_specs=pl.BlockSpec((tm, KVR_PAD), lambda i: (i, 0)),
        out_shape=jax.ShapeDtypeStruct((m, KVR_PAD), F32),
        scratch_shapes=[pltpu.VMEM((KVR_PAD, k_dim), BF16)],
        compiler_params=_cparams(("arbitrary",)),
        name="proj_kvr",
    )(h, w_t)
    tn = 1024
    rest = pl.pallas_call(
        functools.partial(_rest_kernel, tn=tn),
        grid=(REST_W // tn, m // tm),
        in_specs=[pl.BlockSpec((tm, k_dim), lambda j, i: (i, 0)),
                  row_window(tn, lambda j, i: REST_OFF + tn * j)],
        out_specs=pl.BlockSpec((tm, tn), lambda j, i: (i, j)),
        out_shape=jax.ShapeDtypeStruct((m, REST_W), BF16),
        scratch_shapes=[pltpu.VMEM((tn, k_dim), BF16)],
        compiler_params=_cparams(("arbitrary", "arbitrary")),
        name="proj_rest",
    )(h, w_t)
    return qm, kvr, rest


def _kv_kernel(p_ref, g_ref, w_ref, cos_ref, sin_ref, k_ref, v_ref, wb_ref):
    @pl.when(pl.program_id(0) == 0)
    def _():
        wb_ref[...] = w_ref[...].astype(BF16)

    p = p_ref[...]
    ckv = p[:, :KV_RANK]
    ms = jnp.mean(ckv * ckv, axis=-1, keepdims=True)
    n = (ckv * lax.rsqrt(ms + EPS) * g_ref[...]).astype(BF16)
    kv = jnp.dot(n, wb_ref[...], preferred_element_type=F32)
    kr_even = _rope_cols(p[:, KV_RANK:], cos_ref[...], sin_ref[...])
    kr_odd = pltpu.roll(kr_even, MLA_ROPE, 1)
    ones_col = jnp.ones(kr_even.shape, BF16)
    kw = MLA_NOPE + MLA_V
    for hd in range(MLA_HEADS):
        k_ref[:, hd * kw:hd * kw + MLA_NOPE] = kv[:, hd * kw:hd * kw + MLA_NOPE].astype(BF16)
        k_ref[:, hd * kw + MLA_NOPE:(hd + 1) * kw] = (kr_even if hd % 2 == 0 else kr_odd).astype(BF16)
        v_ref[:, hd * kw:hd * kw + MLA_V] = kv[:, hd * kw + MLA_NOPE:(hd + 1) * kw].astype(BF16)
        v_ref[:, hd * kw + MLA_V:(hd + 1) * kw] = ones_col


def _kv(kvr, g_kv, w_ukv, cos_tab, sin_tab):
    m = kvr.shape[0]
    tm = 512
    kw = MLA_HEADS * (MLA_NOPE + MLA_V)
    return pl.pallas_call(
        _kv_kernel,
        grid=(m // tm,),
        in_specs=[pl.BlockSpec((tm, KVR_PAD), lambda i: (i, 0)),
                  pl.BlockSpec((1, KV_RANK), lambda i: (0, 0)),
                  pl.BlockSpec((KV_RANK, kw), lambda i: (0, 0)),
                  pl.BlockSpec((tm, LANES), lambda i: (i, 0)),
                  pl.BlockSpec((tm, LANES), lambda i: (i, 0))],
        out_specs=[pl.BlockSpec((tm, kw), lambda i: (i, 0)),
                   pl.BlockSpec((tm, kw), lambda i: (i, 0))],
        out_shape=[jax.ShapeDtypeStruct((m, kw), BF16),
                   jax.ShapeDtypeStruct((m, kw), BF16)],
        scratch_shapes=[pltpu.VMEM((KV_RANK, kw), BF16)],
        compiler_params=_cparams(("arbitrary",)),
        name="kv_up",
    )(kvr, g_kv, w_ukv, cos_tab, sin_tab)


def _flash_pipeline(n_chains, nq, diag_fn, loops, value_fn, finalize_fn,
                    s_ref, acc_ref, mpart_ref, macc_ref):
    chunk = s_ref.shape[2]

    def lane_tiles(x, n):
        return jnp.concatenate([x] * n, axis=1)

    def qk_phase(scores):
        for ci, s in enumerate(scores):
            s_ref[ci] = s
            part = s[:, :LANES]
            for j in range(1, chunk // LANES):
                part = jnp.maximum(part, s[:, j * LANES:(j + 1) * LANES])
            mpart_ref[ci] = part

    def pv_phase(kc):
        for ci in range(n_chains):
            m_acc = macc_ref[ci]
            m_run = jnp.maximum(m_acc, jnp.max(mpart_ref[ci], axis=-1, keepdims=True))
            macc_ref[ci] = m_run
            p = jnp.exp2(s_ref[ci] - lane_tiles(m_run, chunk // LANES))
            alpha = jnp.exp2(m_acc - m_run)
            pv = jnp.dot(p.astype(BF16), value_fn(ci, kc), preferred_element_type=F32)
            acc_ref[ci] = lane_tiles(alpha, acc_ref.shape[2] // LANES) * acc_ref[ci] + pv

    def reset():
        acc_ref[...] = jnp.zeros_like(acc_ref)
        macc_ref[...] = jnp.full(macc_ref.shape, NEG, F32)

    def tile(qt, _):
        cur = qt
        for trips_fn, score_fn in loops:
            def step(kc, cur, score_fn=score_fn):
                pv_phase(cur)
                qk_phase(score_fn(qt, kc))
                return kc

            cur = lax.fori_loop(0, trips_fn(qt), step, cur)
        pv_phase(cur)
        finalize_fn(qt)
        reset()
        nxt = jnp.minimum(qt + 1, nq - 1)
        qk_phase(diag_fn(nxt))
        return 0

    reset()
    qk_phase(diag_fn(0))
    lax.fori_loop(0, nq, tile, 0)


def _causal_mask(rows, cols, row0):
    row = lax.broadcasted_iota(jnp.int32, (rows, cols), 0) + row0
    col = lax.broadcasted_iota(jnp.int32, (rows, cols), 1)
    return col <= row


_NT = (((1,), (1,)), ((), ()))


def _mla_attn_kernel(q_ref, k_ref, v_ref, g_ref, o_ref, acc_ref, qs_ref, s_ref, mpart_ref, macc_ref,
                     *, tq, rs):
    kw = 2 * MLA_NOPE
    tr = tq // rs
    nq = q_ref.shape[0] // tq
    for hp in range(2):
        qs_ref[hp, :, :MLA_NOPE] = q_ref[:, hp * MLA_NOPE:(hp + 1) * MLA_NOPE]
        qs_ref[hp, :, MLA_NOPE:] = q_ref[:, 2 * MLA_NOPE:]
    chains = [(hp, r) for hp in range(2) for r in range(rs)]

    def scores(qt, kc, diag):
        k0 = pl.multiple_of(kc * tq, tq)
        out = []
        for hp, r in chains:
            q = qs_ref[hp, pl.ds(pl.multiple_of(qt * tq + r * tr, tr), tr), :]
            k = k_ref[pl.ds(k0, tq), hp * kw:(hp + 1) * kw]
            s = lax.dot_general(q, k, _NT, preferred_element_type=F32)
            if diag:
                s = jnp.where(_causal_mask(tr, tq, r * tr), s, NEG)
            out.append(s)
        return out

    def values(ci, kc):
        hp = chains[ci][0]
        return v_ref[pl.ds(pl.multiple_of(kc * tq, tq), tq), hp * kw:(hp + 1) * kw]

    def finalize(qt):
        for ci, (hp, r) in enumerate(chains):
            rows = pl.ds(pl.multiple_of(qt * tq + r * tr, tr), tr)
            acc = acc_ref[ci]
            o = acc[:, :MLA_V] / acc[:, MLA_V:]
            gate = g_ref[rows, hp * MLA_V:(hp + 1) * MLA_V].astype(F32)
            o_ref[rows, hp * MLA_V:(hp + 1) * MLA_V] = (o * gate).astype(BF16)

    _flash_pipeline(len(chains), nq, lambda qt: scores(qt, qt, True),
                    [(lambda qt: qt, lambda qt, kc: scores(qt, kc, False))],
                    values, finalize, s_ref, acc_ref, mpart_ref, macc_ref)


def _mla_attn(qm, kk, vv, rest, nb, s):
    tq = ATTN_TQ
    pair_w = 2 * (MLA_NOPE + MLA_ROPE)
    kw = 4 * MLA_NOPE
    gate_blk = GM_OFF // (2 * MLA_V)
    rs = MLA_ROW_SPLIT
    return pl.pallas_call(
        functools.partial(_mla_attn_kernel, tq=tq, rs=rs),
        grid=(nb, MLA_HEADS // 2),
        in_specs=[pl.BlockSpec((s, pair_w), lambda b, hh: (b, hh)),
                  pl.BlockSpec((s, kw), lambda b, hh: (b, hh)),
                  pl.BlockSpec((s, kw), lambda b, hh: (b, hh)),
                  pl.BlockSpec((s, 2 * MLA_V), lambda b, hh: (b, gate_blk + hh))],
        out_specs=pl.BlockSpec((s, 2 * MLA_V), lambda b, hh: (b, hh)),
        out_shape=jax.ShapeDtypeStruct((nb * s, MLA_HEADS * MLA_V), BF16),
        scratch_shapes=[pltpu.VMEM((2 * rs, tq // rs, 2 * MLA_V), F32),
                        pltpu.VMEM((2, s, 2 * MLA_NOPE), BF16),
                        pltpu.VMEM((2 * rs, tq // rs, tq), F32),
                        pltpu.VMEM((2 * rs, tq // rs, LANES), F32),
                        pltpu.VMEM((2 * rs, tq // rs, LANES), F32)],
        compiler_params=_cparams(("arbitrary", "arbitrary")),
        name="mla_attn",
    )(qm, kk, vv, rest)


def _bf16_pieces(x):
    p1 = x.astype(BF16)
    r = x - p1.astype(F32)
    p2 = r.astype(BF16)
    return p1.astype(F32), p2.astype(F32), r - p2.astype(F32)


def _diff_attn_kernel(ord_ref, q_ref, k_ref, v_ref, g_ref, pq_ref, pk_ref, sl_ref, lam_ref, gs_ref,
                      o_ref, vaug_ref, kaug_ref, acc_ref, qf_ref, s_ref, mpart_ref, macc_ref, *, tq):
    seq = q_ref.shape[0]
    nq = seq // tq
    lane = lax.broadcasted_iota(jnp.int32, (tq, LANES), 1)
    ones_col = jnp.ones((tq, LANES), BF16)
    slopes2 = [sl_ref[0, hp:hp + 1, 0:1] * LOG2E for hp in range(2)]
    c_pieces = [_bf16_pieces(c) for c in slopes2]

    def pick(sel, x3):
        return jnp.where(sel == 0, x3[0], jnp.where(sel == 1, x3[1], x3[2]))

    def bias_lanes(hp, rows, key_side):
        pos = _bf16_pieces(pq_ref[rows, :])
        lane_b = lane - 9
        if key_side:
            lo, hi = pick(lane % 3, pos), pick(lane_b // 3, c_pieces[hp])
        else:
            lo, hi = pick(lane // 3, c_pieces[hp]), pick(lane_b % 3, [-p for p in pos])
        return jnp.where(lane < 9, lo, jnp.where(lane < 18, hi, 0.0)).astype(BF16)

    def stage_keys(kc, _):
        rows = pl.ds(pl.multiple_of(kc * tq, tq), tq)
        for hp in range(2):
            vaug_ref[hp, rows, :DIFF_V] = v_ref[rows, hp * DIFF_V:(hp + 1) * DIFF_V]
            vaug_ref[hp, rows, DIFF_V:] = ones_col
            kaug_ref[hp, rows, :2 * DIFF_QK] = k_ref[rows, hp * 2 * DIFF_QK:(hp + 1) * 2 * DIFF_QK]
            kaug_ref[hp, rows, 2 * DIFF_QK:] = bias_lanes(hp, rows, True)
        return 0

    lax.fori_loop(0, nq, stage_keys, 0)

    def stage_queries(qt):
        rows = pl.ds(pl.multiple_of(qt * tq, tq), tq)
        for hp in range(2):
            q = q_ref[rows, hp * 2 * DIFF_QK:(hp + 1) * 2 * DIFF_QK]
            q_side = bias_lanes(hp, rows, False)
            qf_ref[2 * hp, :, :2 * DIFF_QK] = jnp.where(lane < DIFF_QK, q, 0).astype(BF16)
            qf_ref[2 * hp + 1, :, :2 * DIFF_QK] = jnp.where(lane >= DIFF_QK, q, 0).astype(BF16)
            qf_ref[2 * hp, :, 2 * DIFF_QK:] = q_side
            qf_ref[2 * hp + 1, :, 2 * DIFF_QK:] = q_side

    lq = lam_ref[...]
    lam = (jnp.exp(jnp.sum(lq[0:1] * lq[1:2], axis=-1, keepdims=True))
           - jnp.exp(jnp.sum(lq[2:3] * lq[3:4], axis=-1, keepdims=True)) + LAMBDA_INIT)

    def scores(qt, kc, diag):
        q0 = pl.multiple_of(qt * tq, tq)
        k0 = pl.multiple_of(kc * tq, tq)
        pq = pq_ref[pl.ds(q0, tq), :]
        pk = pk_ref[0, pl.ds(kc, 1), :]
        mask = _causal_mask(tq, tq, 0) if diag else None
        out = []
        for hp in range(2):
            k = k_ref[pl.ds(k0, tq), hp * 2 * DIFF_QK:(hp + 1) * 2 * DIFF_QK]
            bias = jnp.abs(slopes2[hp] * pq - slopes2[hp] * pk)
            for c in range(2):
                q = qf_ref[2 * hp + c, :, :2 * DIFF_QK]
                s = lax.dot_general(q, k, _NT, preferred_element_type=F32) - bias
                out.append(jnp.where(mask, s, NEG) if diag else s)
        return out

    def scores_diag(qt):
        stage_queries(qt)
        return scores(qt, qt, True)

    def scores_ordered(qt, kc):
        k0 = pl.multiple_of(kc * tq, tq)
        out = []
        for hp in range(2):
            k = kaug_ref[hp, pl.ds(k0, tq), :]
            for c in range(2):
                out.append(lax.dot_general(qf_ref[2 * hp + c], k, _NT, preferred_element_type=F32))
        return out

    batch = pl.program_id(0)

    def trips_ordered(qt):
        return jnp.where(ord_ref[batch, qt] != 0, qt, 0)

    def values(ci, kc):
        return vaug_ref[ci // 2, pl.ds(pl.multiple_of(kc * tq, tq), tq), :]

    def finalize(qt):
        rows = pl.ds(pl.multiple_of(qt * tq, tq), tq)
        for hp in range(2):
            a1 = acc_ref[2 * hp]
            a2 = acc_ref[2 * hp + 1]
            o = a1[:, :DIFF_V] / a1[:, DIFF_V:] - lam * (a2[:, :DIFF_V] / a2[:, DIFF_V:])
            ms_o = jnp.mean(o * o, axis=-1, keepdims=True)
            o = o * lax.rsqrt(ms_o + EPS) * gs_ref[...] * (1.0 - LAMBDA_INIT)
            gate = g_ref[rows, hp * DIFF_V:(hp + 1) * DIFF_V].astype(F32)
            o_ref[rows, hp * DIFF_V:(hp + 1) * DIFF_V] = (o * gate).astype(BF16)

    _flash_pipeline(4, nq, scores_diag,
                    [(trips_ordered, scores_ordered),
                     (lambda qt: qt - trips_ordered(qt), lambda qt, kc: scores(qt, kc, False))],
                    values, finalize, s_ref, acc_ref, mpart_ref, macc_ref)


def _diff_attn(rest, pos_col, pos_row, ordered, slopes, lam_par, g_subln, nb, s):
    tq = ATTN_TQ
    nq = s // tq
    pw = 2 * DIFF_V
    return pl.pallas_call(
        functools.partial(_diff_attn_kernel, tq=tq),
        grid=(nb, DIFF_HEADS // 2),
        in_specs=[pl.BlockSpec(memory_space=pltpu.SMEM),
                  pl.BlockSpec((s, pw), lambda b, hh: (b, QD_OFF // pw + hh)),
                  pl.BlockSpec((s, pw), lambda b, hh: (b, KD_OFF // pw + hh)),
                  pl.BlockSpec((s, pw), lambda b, hh: (b, VD_OFF // pw + hh)),
                  pl.BlockSpec((s, pw), lambda b, hh: (b, GD_OFF // pw + hh)),
                  pl.BlockSpec((s, 1), lambda b, hh: (b, 0)),
                  pl.BlockSpec((1, nq, tq), lambda b, hh: (b, 0, 0)),
                  pl.BlockSpec((1, 2, LANES), lambda b, hh: (hh, 0, 0)),
                  pl.BlockSpec((4, DIFF_QK), lambda b, hh: (0, 0)),
                  pl.BlockSpec((1, DIFF_V), lambda b, hh: (0, 0))],
        out_specs=pl.BlockSpec((s, pw), lambda b, hh: (b, hh)),
        out_shape=jax.ShapeDtypeStruct((nb * s, DIFF_HEADS * DIFF_V), BF16),
        scratch_shapes=[pltpu.VMEM((2, s, 2 * DIFF_V), BF16),
                        pltpu.VMEM((2, s, 4 * DIFF_QK), BF16),
                        pltpu.VMEM((4, tq, 2 * DIFF_V), F32),
                        pltpu.VMEM((4, tq, 4 * DIFF_QK), BF16),
                        pltpu.VMEM((4, tq, tq), F32),
                        pltpu.VMEM((4, tq, LANES), F32),
                        pltpu.VMEM((4, tq, LANES), F32)],
        compiler_params=_cparams(("arbitrary", "arbitrary")),
        name="diff_attn",
    )(ordered, rest, rest, rest, rest, pos_col, pos_row, slopes, lam_par, g_subln)


def _merge_kernel(a1_ref, a2_ref, w1_ref, w2_ref, s1_ref, s2_ref, o_ref, wb_ref):
    @pl.when(pl.program_id(1) == 0)
    def _():
        wb_ref[0] = w1_ref[...].astype(BF16)
        wb_ref[1] = w2_ref[...].astype(BF16)

    y1 = jnp.dot(a1_ref[...], wb_ref[0], preferred_element_type=F32)
    y2 = jnp.dot(a2_ref[...], wb_ref[1], preferred_element_type=F32)
    o_ref[...] = (s1_ref[...].astype(F32) * y1 + s2_ref[...].astype(F32) * y2).astype(BF16)


def _merge(og_mla, og_diff, w1, w2, rest):
    m, k_dim = og_mla.shape
    n = w1.shape[1]
    tm, tn = 1024, 1024
    return pl.pallas_call(
        _merge_kernel,
        grid=(n // tn, m // tm),
        in_specs=[pl.BlockSpec((tm, k_dim), lambda j, i: (i, 0)),
                  pl.BlockSpec((tm, k_dim), lambda j, i: (i, 0)),
                  pl.BlockSpec((k_dim, tn), lambda j, i: (0, j)),
                  pl.BlockSpec((k_dim, tn), lambda j, i: (0, j)),
                  pl.BlockSpec((tm, tn), lambda j, i: (i, MGM_OFF // tn + j)),
                  pl.BlockSpec((tm, tn), lambda j, i: (i, MGD_OFF // tn + j))],
        out_specs=pl.BlockSpec((tm, tn), lambda j, i: (i, j)),
        out_shape=jax.ShapeDtypeStruct((m, n), BF16),
        scratch_shapes=[pltpu.VMEM((2, k_dim, tn), BF16)],
        compiler_params=_cparams(("arbitrary", "arbitrary")),
        name="merge",
    )(og_mla, og_diff, w1, w2, rest, rest)


def _out_kernel(a_ref, w_ref, x_ref, ada_ref, g_ref, o_ref, wb_ref):
    @pl.when(jnp.logical_and(pl.program_id(0) == 0, pl.program_id(1) == 0))
    def _():
        wb_ref[...] = w_ref[...].astype(BF16)

    y = jnp.dot(a_ref[...], wb_ref[...], preferred_element_type=F32)
    ms = jnp.mean(y * y, axis=-1, keepdims=True)
    yn = y * lax.rsqrt(ms + EPS) * g_ref[...]
    o_ref[0] = x_ref[0] + ada_ref[0, 2:3, :] * yn


def _out(merged, w_out, x, ada3, g_post):
    nb, s, d = x.shape
    tm = 512
    ns = s // tm
    return pl.pallas_call(
        _out_kernel,
        grid=(nb, ns),
        in_specs=[pl.BlockSpec((tm, d), lambda b, i: (b * ns + i, 0)),
                  pl.BlockSpec((d, d), lambda b, i: (0, 0), pipeline_mode=pl.Buffered(1)),
                  pl.BlockSpec((1, tm, d), lambda b, i: (b, i, 0)),
                  pl.BlockSpec((1, 3, d), lambda b, i: (b, 0, 0)),
                  pl.BlockSpec((1, d), lambda b, i: (0, 0))],
        out_specs=pl.BlockSpec((1, tm, d), lambda b, i: (b, i, 0)),
        out_shape=jax.ShapeDtypeStruct((nb, s, d), F32),
        scratch_shapes=[pltpu.VMEM((d, d), BF16)],
        compiler_params=_cparams(("arbitrary", "arbitrary")),
        name="out_proj",
    )(merged, w_out, x, ada3, g_post)


def kernel(x, c, positions, w_ada, b_ada, g_pre, w_in, g_kv, w_ukv, lambda_q1, lambda_k1,
           lambda_q2, lambda_k2, g_subln, w_o_mla, w_o_diff, w_out, g_post):
    nb, s, d = x.shape
    depth = w_in.shape[0]
    half = MLA_ROPE // 2
    inv = ROPE_THETA ** (-jnp.arange(half, dtype=F32) / half)
    inv_tab = jnp.tile(inv, LANES // half).reshape(1, LANES)
    slopes = 2.0 ** (-8.0 * jnp.arange(1, DIFF_HEADS + 1, dtype=F32) / DIFF_HEADS)
    slopes = jnp.broadcast_to(slopes.reshape(DIFF_HEADS // 2, 2, 1), (DIFF_HEADS // 2, 2, LANES))
    pos_col = positions.reshape(nb * s, 1)
    pos_colf = pos_col.astype(F32)
    pos_chunks = positions.reshape(nb, s // ATTN_TQ, ATTN_TQ)
    pos_row = pos_chunks.astype(F32)
    run_max = lax.cummax(pos_chunks.max(axis=-1), axis=1)
    prev_max = jnp.concatenate(
        [jnp.full((nb, 1), jnp.iinfo(jnp.int32).min, jnp.int32), run_max[:, :-1]], axis=1)
    ordered = (prev_max <= pos_chunks.min(axis=-1)).astype(jnp.int32)

    for l in range(depth):
        ada3 = _ada(c, w_ada[l], b_ada[l]).reshape(nb, 3, d)
        h, cos_tab, sin_tab = _norm(x, ada3, g_pre[l].reshape(1, d), pos_col, inv_tab)
        qm, kvr, rest = _proj(h, jnp.swapaxes(w_in[l], 0, 1), cos_tab, sin_tab)
        kk, vv = _kv(kvr, g_kv[l].reshape(1, KV_RANK), w_ukv[l], cos_tab, sin_tab)
        og_mla = _mla_attn(qm, kk, vv, rest, nb, s)
        lam_par = jnp.stack([lambda_q1[l], lambda_k1[l], lambda_q2[l], lambda_k2[l]]).astype(F32)
        og_diff = _diff_attn(rest, pos_colf, pos_row, ordered, slopes, lam_par,
                             g_subln[l].reshape(1, DIFF_V), nb, s)
        merged = _merge(og_mla, og_diff, w_o_mla[l], w_o_diff[l], rest)
        x = _out(merged, w_out[l], x, ada3, g_post[l].reshape(1, d))
    return x
```

```python
import functools
import math

import jax
import jax.numpy as jnp
from jax import lax
from jax.experimental import pallas as pl
from jax.experimental.pallas import tpu as pltpu

F32 = jnp.float32
BF16 = jnp.bfloat16

D_MODEL = 2048
MLA_HEADS = 8
MLA_NOPE = 128
MLA_ROPE = 64
MLA_V = 128
KV_RANK = 512
ROPE_THETA = 10000.0
DIFF_HEADS = 8
DIFF_QK = 64
DIFF_V = 128
EPS = 1e-6
NEG = -1e30
LAMBDA_INIT = 0.8 - 0.6 * math.exp(-0.3 * 0)

LANES = 128
SUBLANES = 8
VMEM_LIMIT = 56 * 1024 * 1024
ATTN_TQ = 512
MLA_TQ = 1024
MLA_ROW_SPLIT = 2
LOG2E = math.log2(math.e)

Q_MLA_W = MLA_HEADS * (MLA_NOPE + MLA_ROPE)
KVR_W = KV_RANK + MLA_ROPE
KVR_PAD = 640
REST_OFF = Q_MLA_W + KVR_W
REST_SHIFT = REST_OFF % LANES
QD_OFF, KD_OFF, VD_OFF, GM_OFF, GD_OFF, MGM_OFF, MGD_OFF = 0, 1024, 2048, 3072, 4096, 5120, 7168
REST_W = 9216


def _cparams(sem):
    return pltpu.CompilerParams(dimension_semantics=sem, vmem_limit_bytes=VMEM_LIMIT)


def _ada_kernel(cb_ref, w_ref, b_ref, o_ref):
    k_dim, tn = w_ref.shape
    nb = cb_ref.shape[0]
    nchunk = tn // LANES

    def body(i, accs):
        k0 = pl.multiple_of(i * SUBLANES, SUBLANES)
        out = []
        for b in range(nb):
            cv = cb_ref[b, pl.ds(k0, SUBLANES), :]
            for j in range(nchunk):
                wv = w_ref[pl.ds(k0, SUBLANES), j * LANES:(j + 1) * LANES]
                out.append(accs[b * nchunk + j] + wv * cv)
        return tuple(out)

    init = tuple(jnp.zeros((SUBLANES, LANES), F32) for _ in range(nb * nchunk))
    accs = lax.fori_loop(0, k_dim // SUBLANES, body, init, unroll=8)
    for b in range(nb):
        row = jnp.concatenate(
            [jnp.sum(accs[b * nchunk + j], axis=0, keepdims=True) for j in range(nchunk)], axis=1)
        o_ref[b:b + 1, :] = row + b_ref[...]


def _ada(c, w, bias):
    nb, k_dim = c.shape
    n = w.shape[1]
    tn = 512
    cb = jnp.broadcast_to(c[:, :, None], (nb, k_dim, LANES))
    return pl.pallas_call(
        _ada_kernel,
        grid=(n // tn,),
        in_specs=[pl.BlockSpec((nb, k_dim, LANES), lambda j: (0, 0, 0)),
                  pl.BlockSpec((k_dim, tn), lambda j: (0, j)),
                  pl.BlockSpec((1, tn), lambda j: (0, j))],
        out_specs=pl.BlockSpec((nb, tn), lambda j: (0, j)),
        out_shape=jax.ShapeDtypeStruct((nb, n), F32),
        compiler_params=_cparams(("arbitrary",)),
        name="ada",
    )(cb, w, bias.reshape(1, n))


def _norm_kernel(x_ref, ada_ref, g_ref, pos_ref, inv_ref, h_ref, cos_ref, sin_ref):
    x = x_ref[0]
    ms = jnp.mean(x * x, axis=-1, keepdims=True)
    y = x * lax.rsqrt(ms + EPS) * g_ref[...]
    shift = ada_ref[0, 0:1, :]
    scale = ada_ref[0, 1:2, :]
    h_ref[...] = (y * (1.0 + scale) + shift).astype(BF16)
    ang = pos_ref[...].astype(F32) * inv_ref[...]
    lane = lax.broadcasted_iota(jnp.int32, ang.shape, 1)
    sign = jnp.where((lane % MLA_ROPE) < MLA_ROPE // 2, -1.0, 1.0).astype(F32)
    cos_ref[...] = jnp.cos(ang)
    sin_ref[...] = jnp.sin(ang) * sign


def _norm(x, ada3, g_pre, pos_col, inv_tab):
    nb, s, d = x.shape
    ts = 512
    ns = s // ts
    row = lambda b, i: (b * ns + i, 0)
    return pl.pallas_call(
        _norm_kernel,
        grid=(nb, ns),
        in_specs=[pl.BlockSpec((1, ts, d), lambda b, i: (b, i, 0)),
                  pl.BlockSpec((1, 3, d), lambda b, i: (b, 0, 0)),
                  pl.BlockSpec((1, d), lambda b, i: (0, 0)),
                  pl.BlockSpec((ts, 1), row),
                  pl.BlockSpec((1, LANES), lambda b, i: (0, 0))],
        out_specs=[pl.BlockSpec((ts, d), row),
                   pl.BlockSpec((ts, LANES), row),
                   pl.BlockSpec((ts, LANES), row)],
        out_shape=[jax.ShapeDtypeStruct((nb * s, d), BF16),
                   jax.ShapeDtypeStruct((nb * s, LANES), F32),
                   jax.ShapeDtypeStruct((nb * s, LANES), F32)],
        compiler_params=_cparams(("arbitrary", "arbitrary")),
        name="prenorm",
    )(x, ada3, g_pre, pos_col, inv_tab)


def _rope_cols(r, cos, sin_signed):
    lane = lax.broadcasted_iota(jnp.int32, r.shape, 1)
    half = MLA_ROPE // 2
    partner = jnp.where((lane % MLA_ROPE) < half,
                        pltpu.roll(r, LANES - half, 1), pltpu.roll(r, half, 1))
    return r * cos + partner * sin_signed


def _qmla_kernel(a_ref, w_ref, cos_ref, sin_ref, o_ref, wb_ref, *, scale):
    @pl.when(pl.program_id(1) == 0)
    def _():
        hd = MLA_NOPE + MLA_ROPE
        wb_ref[:MLA_NOPE] = w_ref[:MLA_NOPE].astype(BF16)
        wb_ref[MLA_NOPE:2 * MLA_NOPE] = w_ref[hd:hd + MLA_NOPE].astype(BF16)
        wb_ref[2 * MLA_NOPE:2 * MLA_NOPE + MLA_ROPE] = w_ref[MLA_NOPE:hd].astype(BF16)
        wb_ref[2 * MLA_NOPE + MLA_ROPE:] = w_ref[hd + MLA_NOPE:].astype(BF16)

    acc = lax.dot_general(a_ref[...], wb_ref[...], _NT, preferred_element_type=F32)
    rr = _rope_cols(acc[:, 2 * MLA_NOPE:], cos_ref[...], sin_ref[...])
    o_ref[:, :2 * MLA_NOPE] = (acc[:, :2 * MLA_NOPE] * scale).astype(BF16)
    o_ref[:, 2 * MLA_NOPE:] = (rr * scale).astype(BF16)


def _kvr_kernel(a_ref, w_ref, o_ref, wb_ref):
    @pl.when(pl.program_id(0) == 0)
    def _():
        wb_ref[:KVR_W] = w_ref[...].astype(BF16)
        wb_ref[KVR_W:] = jnp.zeros((KVR_PAD - KVR_W, wb_ref.shape[1]), BF16)

    o_ref[...] = lax.dot_general(a_ref[...], wb_ref[...], _NT, preferred_element_type=F32)


def _rest_kernel(a_ref, w_ref, o_ref, wb_ref, *, tn):
    j = pl.program_id(0)

    @pl.when(pl.program_id(1) == 0)
    def _():
        wb_ref[...] = w_ref[...].astype(BF16)

    acc = lax.dot_general(a_ref[...], wb_ref[...], _NT, preferred_element_type=F32)

    @pl.when(j < KD_OFF // tn)
    def _():
        o_ref[...] = (acc * (DIFF_QK ** -0.5 * LOG2E)).astype(BF16)

    @pl.when(jnp.logical_and(j >= KD_OFF // tn, j < GM_OFF // tn))
    def _():
        o_ref[...] = acc.astype(BF16)

    @pl.when(jnp.logical_and(j >= GM_OFF // tn, j < MGM_OFF // tn))
    def _():
        o_ref[...] = (acc * jax.nn.sigmoid(acc)).astype(BF16)

    @pl.when(j >= MGM_OFF // tn)
    def _():
        o_ref[...] = jax.nn.sigmoid(acc).astype(BF16)


def _proj(h, w_t, cos_tab, sin_tab):
    m, k_dim = h.shape
    tm = 1024
    pair_w = 2 * (MLA_NOPE + MLA_ROPE)
    q_scale = (MLA_NOPE + MLA_ROPE) ** -0.5 * LOG2E

    def row_window(rows, offset_fn):
        return pl.BlockSpec((pl.Element(rows), pl.Element(k_dim)),
                            lambda *g: (pl.multiple_of(offset_fn(*g), SUBLANES), 0))

    qm = pl.pallas_call(
        functools.partial(_qmla_kernel, scale=q_scale),
        grid=(Q_MLA_W // pair_w, m // tm),
        in_specs=[pl.BlockSpec((tm, k_dim), lambda j, i: (i, 0)),
                  pl.BlockSpec((pair_w, k_dim), lambda j, i: (j, 0)),
                  pl.BlockSpec((tm, LANES), lambda j, i: (i, 0)),
                  pl.BlockSpec((tm, LANES), lambda j, i: (i, 0))],
        out_specs=pl.BlockSpec((tm, pair_w), lambda j, i: (i, j)),
        out_shape=jax.ShapeDtypeStruct((m, Q_MLA_W), BF16),
        scratch_shapes=[pltpu.VMEM((pair_w, k_dim), BF16)],
        compiler_params=_cparams(("arbitrary", "arbitrary")),
        name="proj_qmla",
    )(h, w_t, cos_tab, sin_tab)
    kvr = pl.pallas_call(
        _kvr_kernel,
        grid=(m // tm,),
        in_specs=[pl.BlockSpec((tm, k_dim), lambda i: (i, 0)),
                  row_window(KVR_W, lambda i: Q_MLA_W + 0 * i)],
        out_specs=pl.BlockSpec((tm, KVR_PAD), lambda i: (i, 0)),
        out_shape=jax.ShapeDtypeStruct((m, KVR_PAD), F32),
        scratch_shapes=[pltpu.VMEM((KVR_PAD, k_dim), BF16)],
        compiler_params=_cparams(("arbitrary",)),
        name="proj_kvr",
    )(h, w_t)
    tn = 1024
    rest = pl.pallas_call(
        functools.partial(_rest_kernel, tn=tn),
        grid=(REST_W // tn, m // tm),
        in_specs=[pl.BlockSpec((tm, k_dim), lambda j, i: (i, 0)),
                  row_window(tn, lambda j, i: REST_OFF + tn * j)],
        out_specs=pl.BlockSpec((tm, tn), lambda j, i: (i, j)),
        out_shape=jax.ShapeDtypeStruct((m, REST_W), BF16),
        scratch_shapes=[pltpu.VMEM((tn, k_dim), BF16)],
        compiler_params=_cparams(("arbitrary", "arbitrary")),
        name="proj_rest",
    )(h, w_t)
    return qm, kvr, rest


def _kv_kernel(p_ref, g_ref, w_ref, cos_ref, sin_ref, k_ref, v_ref, wb_ref):
    @pl.when(pl.program_id(0) == 0)
    def _():
        wb_ref[...] = w_ref[...].astype(BF16)

    p = p_ref[...]
    ckv = p[:, :KV_RANK]
    ms = jnp.mean(ckv * ckv, axis=-1, keepdims=True)
    n = (ckv * lax.rsqrt(ms + EPS) * g_ref[...]).astype(BF16)
    kv = jnp.dot(n, wb_ref[...], preferred_element_type=F32)
    kr_even = _rope_cols(p[:, KV_RANK:], cos_ref[...], sin_ref[...])
    kr_odd = pltpu.roll(kr_even, MLA_ROPE, 1)
    ones_col = jnp.ones(kr_even.shape, BF16)
    kw = MLA_NOPE + MLA_V
    for hd in range(MLA_HEADS):
        k_ref[:, hd * kw:hd * kw + MLA_NOPE] = kv[:, hd * kw:hd * kw + MLA_NOPE].astype(BF16)
        k_ref[:, hd * kw + MLA_NOPE:(hd + 1) * kw] = (kr_even if hd % 2 == 0 else kr_odd).astype(BF16)
        v_ref[:, hd * kw:hd * kw + MLA_V] = kv[:, hd * kw + MLA_NOPE:(hd + 1) * kw].astype(BF16)
        v_ref[:, hd * kw + MLA_V:(hd + 1) * kw] = ones_col


def _kv(kvr, g_kv, w_ukv, cos_tab, sin_tab):
    m = kvr.shape[0]
    tm = 512
    kw = MLA_HEADS * (MLA_NOPE + MLA_V)
    return pl.pallas_call(
        _kv_kernel,
        grid=(m // tm,),
        in_specs=[pl.BlockSpec((tm, KVR_PAD), lambda i: (i, 0)),
                  pl.BlockSpec((1, KV_RANK), lambda i: (0, 0)),
                  pl.BlockSpec((KV_RANK, kw), lambda i: (0, 0)),
                  pl.BlockSpec((tm, LANES), lambda i: (i, 0)),
                  pl.BlockSpec((tm, LANES), lambda i: (i, 0))],
        out_specs=[pl.BlockSpec((tm, kw), lambda i: (i, 0)),
                   pl.BlockSpec((tm, kw), lambda i: (i, 0))],
        out_shape=[jax.ShapeDtypeStruct((m, kw), BF16),
                   jax.ShapeDtypeStruct((m, kw), BF16)],
        scratch_shapes=[pltpu.VMEM((KV_RANK, kw), BF16)],
        compiler_params=_cparams(("arbitrary",)),
        name="kv_up",
    )(kvr, g_kv, w_ukv, cos_tab, sin_tab)


def _flash_pipeline(n_chains, nq, diag_fn, loops, value_fn, finalize_fn,
                    s_ref, acc_ref, mpart_ref, macc_ref, sd_ref=None, mpd_ref=None):
    chunk = s_ref.shape[2]

    def lane_tiles(x, n):
        return jnp.concatenate([x] * n, axis=1)

    def qk_phase(scores, s_dst=s_ref, m_dst=mpart_ref):
        for ci, s in enumerate(scores):
            s_dst[ci] = s
            part = s[:, :LANES]
            for j in range(1, chunk // LANES):
                part = jnp.maximum(part, s[:, j * LANES:(j + 1) * LANES])
            m_dst[ci] = part

    def pv_phase(kc):
        for ci in range(n_chains):
            m_acc = macc_ref[ci]
            m_run = jnp.maximum(m_acc, jnp.max(mpart_ref[ci], axis=-1, keepdims=True))
            macc_ref[ci] = m_run
            p = jnp.exp2(s_ref[ci] - lane_tiles(m_run, chunk // LANES))
            alpha = jnp.exp2(m_acc - m_run)
            pv = jnp.dot(p.astype(BF16), value_fn(ci, kc), preferred_element_type=F32)
            acc_ref[ci] = lane_tiles(alpha, acc_ref.shape[2] // LANES) * acc_ref[ci] + pv

    def reset():
        acc_ref[...] = jnp.zeros_like(acc_ref)
        macc_ref[...] = jnp.full(macc_ref.shape, NEG, F32)

    def tile(qt, _):
        cur = qt
        for trips_fn, score_fn in loops:
            def step(kc, cur, score_fn=score_fn):
                pv_phase(cur)
                qk_phase(score_fn(qt, kc))
                return kc

            cur = lax.fori_loop(0, trips_fn(qt), step, cur)
        nxt = jnp.minimum(qt + 1, nq - 1)
        if sd_ref is None:
            pv_phase(cur)
            finalize_fn(qt)
            reset()
            qk_phase(diag_fn(nxt))
        else:
            qk_phase(diag_fn(nxt), sd_ref, mpd_ref)
            pv_phase(cur)
            finalize_fn(qt)
            reset()
            s_ref[...] = sd_ref[...]
            mpart_ref[...] = mpd_ref[...]
        return 0

    reset()
    qk_phase(diag_fn(0))
    lax.fori_loop(0, nq, tile, 0)


def _causal_mask(rows, cols, row0):
    row = lax.broadcasted_iota(jnp.int32, (rows, cols), 0) + row0
    col = lax.broadcasted_iota(jnp.int32, (rows, cols), 1)
    return col <= row


_NT = (((1,), (1,)), ((), ()))


def _mla_attn_kernel(q_ref, k_ref, v_ref, g_ref, o_ref, acc_ref, qs_ref, s_ref, mpart_ref, macc_ref,
                     *, tq, rs):
    kw = 2 * MLA_NOPE
    tr = tq // rs
    nq = q_ref.shape[0] // tq
    for hp in range(2):
        qs_ref[hp, :, :MLA_NOPE] = q_ref[:, hp * MLA_NOPE:(hp + 1) * MLA_NOPE]
        qs_ref[hp, :, MLA_NOPE:] = q_ref[:, 2 * MLA_NOPE:]
    chains = [(hp, r) for hp in range(2) for r in range(rs)]

    def scores(qt, kc, diag):
        k0 = pl.multiple_of(kc * tq, tq)
        out = []
        for hp, r in chains:
            q = qs_ref[hp, pl.ds(pl.multiple_of(qt * tq + r * tr, tr), tr), :]
            k = k_ref[pl.ds(k0, tq), hp * kw:(hp + 1) * kw]
            s = lax.dot_general(q, k, _NT, preferred_element_type=F32)
            if diag:
                s = jnp.where(_causal_mask(tr, tq, r * tr), s, NEG)
            out.append(s)
        return out

    def values(ci, kc):
        hp = chains[ci][0]
        return v_ref[pl.ds(pl.multiple_of(kc * tq, tq), tq), hp * kw:(hp + 1) * kw]

    def finalize(qt):
        for ci, (hp, r) in enumerate(chains):
            rows = pl.ds(pl.multiple_of(qt * tq + r * tr, tr), tr)
            acc = acc_ref[ci]
            o = acc[:, :MLA_V] / acc[:, MLA_V:]
            gate = g_ref[rows, hp * MLA_V:(hp + 1) * MLA_V].astype(F32)
            o_ref[rows, hp * MLA_V:(hp + 1) * MLA_V] = (o * gate).astype(BF16)

    _flash_pipeline(len(chains), nq, lambda qt: scores(qt, qt, True),
                    [(lambda qt: qt, lambda qt, kc: scores(qt, kc, False))],
                    values, finalize, s_ref, acc_ref, mpart_ref, macc_ref)


def _mla_attn(qm, kk, vv, rest, nb, s):
    tq = MLA_TQ
    pair_w = 2 * (MLA_NOPE + MLA_ROPE)
    kw = 4 * MLA_NOPE
    gate_blk = GM_OFF // (2 * MLA_V)
    rs = MLA_ROW_SPLIT
    return pl.pallas_call(
        functools.partial(_mla_attn_kernel, tq=tq, rs=rs),
        grid=(nb, MLA_HEADS // 2),
        in_specs=[pl.BlockSpec((s, pair_w), lambda b, hh: (b, hh)),
                  pl.BlockSpec((s, kw), lambda b, hh: (b, hh)),
                  pl.BlockSpec((s, kw), lambda b, hh: (b, hh)),
                  pl.BlockSpec((s, 2 * MLA_V), lambda b, hh: (b, gate_blk + hh))],
        out_specs=pl.BlockSpec((s, 2 * MLA_V), lambda b, hh: (b, hh)),
        out_shape=jax.ShapeDtypeStruct((nb * s, MLA_HEADS * MLA_V), BF16),
        scratch_shapes=[pltpu.VMEM((2 * rs, tq // rs, 2 * MLA_V), F32),
                        pltpu.VMEM((2, s, 2 * MLA_NOPE), BF16),
                        pltpu.VMEM((2 * rs, tq // rs, tq), F32),
                        pltpu.VMEM((2 * rs, tq // rs, LANES), F32),
                        pltpu.VMEM((2 * rs, tq // rs, LANES), F32)],
        compiler_params=_cparams(("arbitrary", "arbitrary")),
        name="mla_attn",
    )(qm, kk, vv, rest)


def _bf16_pieces(x):
    p1 = x.astype(BF16)
    r = x - p1.astype(F32)
    p2 = r.astype(BF16)
    return p1.astype(F32), p2.astype(F32), r - p2.astype(F32)


def _diff_attn_kernel(ord_ref, q_ref, k_ref, v_ref, g_ref, pq_ref, pk_ref, sl_ref, lam_ref, gs_ref,
                      o_ref, vaug_ref, kaug_ref, acc_ref, qf_ref, s_ref, mpart_ref, macc_ref,
                      sd_ref, mpd_ref, *, tq):
    seq = q_ref.shape[0]
    nq = seq // tq
    lane = lax.broadcasted_iota(jnp.int32, (tq, LANES), 1)
    ones_col = jnp.ones((tq, LANES), BF16)
    slopes2 = [sl_ref[0, hp:hp + 1, 0:1] * LOG2E for hp in range(2)]
    c_pieces = [_bf16_pieces(c) for c in slopes2]

    def pick(sel, x3):
        return jnp.where(sel == 0, x3[0], jnp.where(sel == 1, x3[1], x3[2]))

    def bias_lanes(hp, rows, key_side):
        pos = _bf16_pieces(pq_ref[rows, :])
        lane_b = lane - 9
        if key_side:
            lo, hi = pick(lane % 3, pos), pick(lane_b // 3, c_pieces[hp])
        else:
            lo, hi = pick(lane // 3, c_pieces[hp]), pick(lane_b % 3, [-p for p in pos])
        return jnp.where(lane < 9, lo, jnp.where(lane < 18, hi, 0.0)).astype(BF16)

    def stage_keys(kc, _):
        rows = pl.ds(pl.multiple_of(kc * tq, tq), tq)
        for hp in range(2):
            vaug_ref[hp, rows, :DIFF_V] = v_ref[rows, hp * DIFF_V:(hp + 1) * DIFF_V]
            vaug_ref[hp, rows, DIFF_V:] = ones_col
            kaug_ref[hp, rows, :2 * DIFF_QK] = k_ref[rows, hp * 2 * DIFF_QK:(hp + 1) * 2 * DIFF_QK]
            kaug_ref[hp, rows, 2 * DIFF_QK:] = bias_lanes(hp, rows, True)
        return 0

    lax.fori_loop(0, nq, stage_keys, 0)

    def stage_queries(qt):
        rows = pl.ds(pl.multiple_of(qt * tq, tq), tq)
        for hp in range(2):
            q = q_ref[rows, hp * 2 * DIFF_QK:(hp + 1) * 2 * DIFF_QK]
            q_side = bias_lanes(hp, rows, False)
            qf_ref[2 * hp, :, :2 * DIFF_QK] = jnp.where(lane < DIFF_QK, q, 0).astype(BF16)
            qf_ref[2 * hp + 1, :, :2 * DIFF_QK] = jnp.where(lane >= DIFF_QK, q, 0).astype(BF16)
            qf_ref[2 * hp, :, 2 * DIFF_QK:] = q_side
            qf_ref[2 * hp + 1, :, 2 * DIFF_QK:] = q_side

    lq = lam_ref[...]
    lam = (jnp.exp(jnp.sum(lq[0:1] * lq[1:2], axis=-1, keepdims=True))
           - jnp.exp(jnp.sum(lq[2:3] * lq[3:4], axis=-1, keepdims=True)) + LAMBDA_INIT)

    def scores(qt, kc, diag):
        q0 = pl.multiple_of(qt * tq, tq)
        k0 = pl.multiple_of(kc * tq, tq)
        pq = pq_ref[pl.ds(q0, tq), :]
        pk = pk_ref[0, pl.ds(kc, 1), :]
        mask = _causal_mask(tq, tq, 0) if diag else None
        out = []
        for hp in range(2):
            k = k_ref[pl.ds(k0, tq), hp * 2 * DIFF_QK:(hp + 1) * 2 * DIFF_QK]
            bias = jnp.abs(slopes2[hp] * pq - slopes2[hp] * pk)
            for c in range(2):
                q = qf_ref[2 * hp + c, :, :2 * DIFF_QK]
                s = lax.dot_general(q, k, _NT, preferred_element_type=F32) - bias
                out.append(jnp.where(mask, s, NEG) if diag else s)
        return out

    def scores_diag(qt):
        stage_queries(qt)
        return scores(qt, qt, True)

    def scores_ordered(qt, kc):
        k0 = pl.multiple_of(kc * tq, tq)
        out = []
        for hp in range(2):
            k = kaug_ref[hp, pl.ds(k0, tq), :]
            for c in range(2):
                out.append(lax.dot_general(qf_ref[2 * hp + c], k, _NT, preferred_element_type=F32))
        return out

    batch = pl.program_id(0)

    def trips_ordered(qt):
        return jnp.where(ord_ref[batch, qt] != 0, qt, 0)

    def values(ci, kc):
        return vaug_ref[ci // 2, pl.ds(pl.multiple_of(kc * tq, tq), tq), :]

    def finalize(qt):
        rows = pl.ds(pl.multiple_of(qt * tq, tq), tq)
        for hp in range(2):
            a1 = acc_ref[2 * hp]
            a2 = acc_ref[2 * hp + 1]
            o = a1[:, :DIFF_V] / a1[:, DIFF_V:] - lam * (a2[:, :DIFF_V] / a2[:, DIFF_V:])
            ms_o = jnp.mean(o * o, axis=-1, keepdims=True)
            o = o * lax.rsqrt(ms_o + EPS) * gs_ref[...] * (1.0 - LAMBDA_INIT)
            gate = g_ref[rows, hp * DIFF_V:(hp + 1) * DIFF_V].astype(F32)
            o_ref[rows, hp * DIFF_V:(hp + 1) * DIFF_V] = (o * gate).astype(BF16)

    _flash_pipeline(4, nq, scores_diag,
                    [(trips_ordered, scores_ordered),
                     (lambda qt: qt - trips_ordered(qt), lambda qt, kc: scores(qt, kc, False))],
                    values, finalize, s_ref, acc_ref, mpart_ref, macc_ref, sd_ref, mpd_ref)


def _diff_attn(rest, pos_col, pos_row, ordered, slopes, lam_par, g_subln, nb, s):
    tq = ATTN_TQ
    nq = s // tq
    pw = 2 * DIFF_V
    return pl.pallas_call(
        functools.partial(_diff_attn_kernel, tq=tq),
        grid=(nb, DIFF_HEADS // 2),
        in_specs=[pl.BlockSpec(memory_space=pltpu.SMEM),
                  pl.BlockSpec((s, pw), lambda b, hh: (b, QD_OFF // pw + hh)),
                  pl.BlockSpec((s, pw), lambda b, hh: (b, KD_OFF // pw + hh)),
                  pl.BlockSpec((s, pw), lambda b, hh: (b, VD_OFF // pw + hh)),
                  pl.BlockSpec((s, pw), lambda b, hh: (b, GD_OFF // pw + hh)),
                  pl.BlockSpec((s, 1), lambda b, hh: (b, 0)),
                  pl.BlockSpec((1, nq, tq), lambda b, hh: (b, 0, 0)),
                  pl.BlockSpec((1, 2, LANES), lambda b, hh: (hh, 0, 0)),
                  pl.BlockSpec((4, DIFF_QK), lambda b, hh: (0, 0)),
                  pl.BlockSpec((1, DIFF_V), lambda b, hh: (0, 0))],
        out_specs=pl.BlockSpec((s, pw), lambda b, hh: (b, hh)),
        out_shape=jax.ShapeDtypeStruct((nb * s, DIFF_HEADS * DIFF_V), BF16),
        scratch_shapes=[pltpu.VMEM((2, s, 2 * DIFF_V), BF16),
                        pltpu.VMEM((2, s, 4 * DIFF_QK), BF16),
                        pltpu.VMEM((4, tq, 2 * DIFF_V), F32),
                        pltpu.VMEM((4, tq, 4 * DIFF_QK), BF16),
                        pltpu.VMEM((4, tq, tq), F32),
                        pltpu.VMEM((4, tq, LANES), F32),
                        pltpu.VMEM((4, tq, LANES), F32),
                        pltpu.VMEM((4, tq, tq), F32),
                        pltpu.VMEM((4, tq, LANES), F32)],
        compiler_params=_cparams(("arbitrary", "arbitrary")),
        name="diff_attn",
    )(ordered, rest, rest, rest, rest, pos_col, pos_row, slopes, lam_par, g_subln)


def _merge_kernel(a1_ref, a2_ref, w1_ref, w2_ref, s1_ref, s2_ref, o_ref, wb_ref):
    @pl.when(pl.program_id(1) == 0)
    def _():
        wb_ref[0] = w1_ref[...].astype(BF16)
        wb_ref[1] = w2_ref[...].astype(BF16)

    y1 = jnp.dot(a1_ref[...], wb_ref[0], preferred_element_type=F32)
    y2 = jnp.dot(a2_ref[...], wb_ref[1], preferred_element_type=F32)
    o_ref[...] = (s1_ref[...].astype(F32) * y1 + s2_ref[...].astype(F32) * y2).astype(BF16)


def _merge(og_mla, og_diff, w1, w2, rest):
    m, k_dim = og_mla.shape
    n = w1.shape[1]
    tm, tn = 1024, 1024
    return pl.pallas_call(
        _merge_kernel,
        grid=(n // tn, m // tm),
        in_specs=[pl.BlockSpec((tm, k_dim), lambda j, i: (i, 0)),
                  pl.BlockSpec((tm, k_dim), lambda j, i: (i, 0)),
                  pl.BlockSpec((k_dim, tn), lambda j, i: (0, j)),
                  pl.BlockSpec((k_dim, tn), lambda j, i: (0, j)),
                  pl.BlockSpec((tm, tn), lambda j, i: (i, MGM_OFF // tn + j)),
                  pl.BlockSpec((tm, tn), lambda j, i: (i, MGD_OFF // tn + j))],
        out_specs=pl.BlockSpec((tm, tn), lambda j, i: (i, j)),
        out_shape=jax.ShapeDtypeStruct((m, n), BF16),
        scratch_shapes=[pltpu.VMEM((2, k_dim, tn), BF16)],
        compiler_params=_cparams(("arbitrary", "arbitrary")),
        name="merge",
    )(og_mla, og_diff, w1, w2, rest, rest)


def _out_kernel(a_ref, w_ref, x_ref, ada_ref, g_ref, o_ref, wb_ref):
    @pl.when(jnp.logical_and(pl.program_id(0) == 0, pl.program_id(1) == 0))
    def _():
        wb_ref[...] = w_ref[...].astype(BF16)

    y = jnp.dot(a_ref[...], wb_ref[...], preferred_element_type=F32)
    ms = jnp.mean(y * y, axis=-1, keepdims=True)
    yn = y * lax.rsqrt(ms + EPS) * g_ref[...]
    o_ref[0] = x_ref[0] + ada_ref[0, 2:3, :] * yn


def _out(merged, w_out, x, ada3, g_post):
    nb, s, d = x.shape
    tm = 512
    ns = s // tm
    return pl.pallas_call(
        _out_kernel,
        grid=(nb, ns),
        in_specs=[pl.BlockSpec((tm, d), lambda b, i: (b * ns + i, 0)),
                  pl.BlockSpec((d, d), lambda b, i: (0, 0), pipeline_mode=pl.Buffered(1)),
                  pl.BlockSpec((1, tm, d), lambda b, i: (b, i, 0)),
                  pl.BlockSpec((1, 3, d), lambda b, i: (b, 0, 0)),
                  pl.BlockSpec((1, d), lambda b, i: (0, 0))],
        out_specs=pl.BlockSpec((1, tm, d), lambda b, i: (b, i, 0)),
        out_shape=jax.ShapeDtypeStruct((nb, s, d), F32),
        scratch_shapes=[pltpu.VMEM((d, d), BF16)],
        compiler_params=_cparams(("arbitrary", "arbitrary")),
        name="out_proj",
    )(merged, w_out, x, ada3, g_post)


def kernel(x, c, positions, w_ada, b_ada, g_pre, w_in, g_kv, w_ukv, lambda_q1, lambda_k1,
           lambda_q2, lambda_k2, g_subln, w_o_mla, w_o_diff, w_out, g_post):
    nb, s, d = x.shape
    depth = w_in.shape[0]
    half = MLA_ROPE // 2
    inv = ROPE_THETA ** (-jnp.arange(half, dtype=F32) / half)
    inv_tab = jnp.tile(inv, LANES // half).reshape(1, LANES)
    slopes = 2.0 ** (-8.0 * jnp.arange(1, DIFF_HEADS + 1, dtype=F32) / DIFF_HEADS)
    slopes = jnp.broadcast_to(slopes.reshape(DIFF_HEADS // 2, 2, 1), (DIFF_HEADS // 2, 2, LANES))
    pos_col = positions.reshape(nb * s, 1)
    pos_colf = pos_col.astype(F32)
    pos_chunks = positions.reshape(nb, s // ATTN_TQ, ATTN_TQ)
    pos_row = pos_chunks.astype(F32)
    run_max = lax.cummax(pos_chunks.max(axis=-1), axis=1)
    prev_max = jnp.concatenate(
        [jnp.full((nb, 1), jnp.iinfo(jnp.int32).min, jnp.int32), run_max[:, :-1]], axis=1)
    ordered = (prev_max <= pos_chunks.min(axis=-1)).astype(jnp.int32)

    for l in range(depth):
        ada3 = _ada(c, w_ada[l], b_ada[l]).reshape(nb, 3, d)
        h, cos_tab, sin_tab = _norm(x, ada3, g_pre[l].reshape(1, d), pos_col, inv_tab)
        qm, kvr, rest = _proj(h, jnp.swapaxes(w_in[l], 0, 1), cos_tab, sin_tab)
        kk, vv = _kv(kvr, g_kv[l].reshape(1, KV_RANK), w_ukv[l], cos_tab, sin_tab)
        og_mla = _mla_attn(qm, kk, vv, rest, nb, s)
        lam_par = jnp.stack([lambda_q1[l], lambda_k1[l], lambda_q2[l], lambda_k2[l]]).astype(F32)
        og_diff = _diff_attn(rest, pos_colf, pos_row, ordered, slopes, lam_par,
                             g_subln[l].reshape(1, DIFF_V), nb, s)
        merged = _merge(og_mla, og_diff, w_o_mla[l], w_o_diff[l], rest)
        x = _out(merged, w_out[l], x, ada3, g_post[l].reshape(1, d))
    return x
```

```python
import functools
import math

import jax
import jax.numpy as jnp
from jax import lax
from jax.experimental import pallas as pl
from jax.experimental.pallas import tpu as pltpu

F32 = jnp.float32
BF16 = jnp.bfloat16

D_MODEL = 2048
MLA_HEADS = 8
MLA_NOPE = 128
MLA_ROPE = 64
MLA_V = 128
KV_RANK = 512
ROPE_THETA = 10000.0
DIFF_HEADS = 8
DIFF_QK = 64
DIFF_V = 128
EPS = 1e-6
NEG = -1e30
LAMBDA_INIT = 0.8 - 0.6 * math.exp(-0.3 * 0)

LANES = 128
SUBLANES = 8
VMEM_LIMIT = 56 * 1024 * 1024
ATTN_TQ = 1024
DIFF_ROW_SPLIT = 2
MLA_TQ = 1024
MLA_ROW_SPLIT = 2
LOG2E = math.log2(math.e)

Q_MLA_W = MLA_HEADS * (MLA_NOPE + MLA_ROPE)
KVR_W = KV_RANK + MLA_ROPE
KVR_PAD = 640
REST_OFF = Q_MLA_W + KVR_W
REST_SHIFT = REST_OFF % LANES
QD_OFF, KD_OFF, VD_OFF, GM_OFF, GD_OFF, MGM_OFF, MGD_OFF = 0, 1024, 2048, 3072, 4096, 5120, 7168
REST_W = 9216


def _cparams(sem):
    return pltpu.CompilerParams(dimension_semantics=sem, vmem_limit_bytes=VMEM_LIMIT)


def _ada_kernel(cb_ref, w_ref, b_ref, o_ref):
    k_dim, tn = w_ref.shape
    nb = cb_ref.shape[0]
    nchunk = tn // LANES

    def body(i, accs):
        k0 = pl.multiple_of(i * SUBLANES, SUBLANES)
        out = []
        for b in range(nb):
            cv = cb_ref[b, pl.ds(k0, SUBLANES), :]
            for j in range(nchunk):
                wv = w_ref[pl.ds(k0, SUBLANES), j * LANES:(j + 1) * LANES]
                out.append(accs[b * nchunk + j] + wv * cv)
        return tuple(out)

    init = tuple(jnp.zeros((SUBLANES, LANES), F32) for _ in range(nb * nchunk))
    accs = lax.fori_loop(0, k_dim // SUBLANES, body, init, unroll=8)
    for b in range(nb):
        row = jnp.concatenate(
            [jnp.sum(accs[b * nchunk + j], axis=0, keepdims=True) for j in range(nchunk)], axis=1)
        o_ref[b:b + 1, :] = row + b_ref[...]


def _ada(c, w, bias):
    nb, k_dim = c.shape
    n = w.shape[1]
    tn = 512
    cb = jnp.broadcast_to(c[:, :, None], (nb, k_dim, LANES))
    return pl.pallas_call(
        _ada_kernel,
        grid=(n // tn,),
        in_specs=[pl.BlockSpec((nb, k_dim, LANES), lambda j: (0, 0, 0)),
                  pl.BlockSpec((k_dim, tn), lambda j: (0, j)),
                  pl.BlockSpec((1, tn), lambda j: (0, j))],
        out_specs=pl.BlockSpec((nb, tn), lambda j: (0, j)),
        out_shape=jax.ShapeDtypeStruct((nb, n), F32),
        compiler_params=_cparams(("arbitrary",)),
        name="ada",
    )(cb, w, bias.reshape(1, n))


def _norm_kernel(x_ref, ada_ref, g_ref, pos_ref, inv_ref, h_ref, cos_ref, sin_ref):
    x = x_ref[0]
    ms = jnp.mean(x * x, axis=-1, keepdims=True)
    y = x * lax.rsqrt(ms + EPS) * g_ref[...]
    shift = ada_ref[0, 0:1, :]
    scale = ada_ref[0, 1:2, :]
    h_ref[...] = (y * (1.0 + scale) + shift).astype(BF16)
    ang = pos_ref[...].astype(F32) * inv_ref[...]
    lane = lax.broadcasted_iota(jnp.int32, ang.shape, 1)
    sign = jnp.where((lane % MLA_ROPE) < MLA_ROPE // 2, -1.0, 1.0).astype(F32)
    cos_ref[...] = jnp.cos(ang)
    sin_ref[...] = jnp.sin(ang) * sign


def _norm(x, ada3, g_pre, pos_col, inv_tab):
    nb, s, d = x.shape
    ts = 512
    ns = s // ts
    row = lambda b, i: (b * ns + i, 0)
    return pl.pallas_call(
        _norm_kernel,
        grid=(nb, ns),
        in_specs=[pl.BlockSpec((1, ts, d), lambda b, i: (b, i, 0)),
                  pl.BlockSpec((1, 3, d), lambda b, i: (b, 0, 0)),
                  pl.BlockSpec((1, d), lambda b, i: (0, 0)),
                  pl.BlockSpec((ts, 1), row),
                  pl.BlockSpec((1, LANES), lambda b, i: (0, 0))],
        out_specs=[pl.BlockSpec((ts, d), row),
                   pl.BlockSpec((ts, LANES), row),
                   pl.BlockSpec((ts, LANES), row)],
        out_shape=[jax.ShapeDtypeStruct((nb * s, d), BF16),
                   jax.ShapeDtypeStruct((nb * s, LANES), F32),
                   jax.ShapeDtypeStruct((nb * s, LANES), F32)],
        compiler_params=_cparams(("arbitrary", "arbitrary")),
        name="prenorm",
    )(x, ada3, g_pre, pos_col, inv_tab)


def _rope_cols(r, cos, sin_signed):
    lane = lax.broadcasted_iota(jnp.int32, r.shape, 1)
    half = MLA_ROPE // 2
    partner = jnp.where((lane % MLA_ROPE) < half,
                        pltpu.roll(r, LANES - half, 1), pltpu.roll(r, half, 1))
    return r * cos + partner * sin_signed


def _qmla_kernel(a_ref, w_ref, cos_ref, sin_ref, o_ref, wb_ref, *, scale):
    @pl.when(pl.program_id(1) == 0)
    def _():
        hd = MLA_NOPE + MLA_ROPE
        wb_ref[:MLA_NOPE] = w_ref[:MLA_NOPE].astype(BF16)
        wb_ref[MLA_NOPE:2 * MLA_NOPE] = w_ref[hd:hd + MLA_NOPE].astype(BF16)
        wb_ref[2 * MLA_NOPE:2 * MLA_NOPE + MLA_ROPE] = w_ref[MLA_NOPE:hd].astype(BF16)
        wb_ref[2 * MLA_NOPE + MLA_ROPE:] = w_ref[hd + MLA_NOPE:].astype(BF16)

    acc = lax.dot_general(a_ref[...], wb_ref[...], _NT, preferred_element_type=F32)
    rr = _rope_cols(acc[:, 2 * MLA_NOPE:], cos_ref[...], sin_ref[...])
    o_ref[:, :2 * MLA_NOPE] = (acc[:, :2 * MLA_NOPE] * scale).astype(BF16)
    o_ref[:, 2 * MLA_NOPE:] = (rr * scale).astype(BF16)


def _kvr_kernel(a_ref, w_ref, o_ref, wb_ref):
    @pl.when(pl.program_id(0) == 0)
    def _():
        wb_ref[:KVR_W] = w_ref[...].astype(BF16)
        wb_ref[KVR_W:] = jnp.zeros((KVR_PAD - KVR_W, wb_ref.shape[1]), BF16)

    o_ref[...] = lax.dot_general(a_ref[...], wb_ref[...], _NT, preferred_element_type=F32)


def _rest_kernel(a_ref, w_ref, o_ref, wb_ref, *, tn):
    j = pl.program_id(0)

    @pl.when(pl.program_id(1) == 0)
    def _():
        wb_ref[...] = w_ref[...].astype(BF16)

    acc = lax.dot_general(a_ref[...], wb_ref[...], _NT, preferred_element_type=F32)

    @pl.when(j < KD_OFF // tn)
    def _():
        o_ref[...] = (acc * (DIFF_QK ** -0.5 * LOG2E)).astype(BF16)

    @pl.when(jnp.logical_and(j >= KD_OFF // tn, j < GM_OFF // tn))
    def _():
        o_ref[...] = acc.astype(BF16)

    @pl.when(jnp.logical_and(j >= GM_OFF // tn, j < MGM_OFF // tn))
    def _():
        o_ref[...] = (acc * jax.nn.sigmoid(acc)).astype(BF16)

    @pl.when(j >= MGM_OFF // tn)
    def _():
        o_ref[...] = jax.nn.sigmoid(acc).astype(BF16)


def _proj(h, w_t, cos_tab, sin_tab):
    m, k_dim = h.shape
    tm = 1024
    pair_w = 2 * (MLA_NOPE + MLA_ROPE)
    q_scale = (MLA_NOPE + MLA_ROPE) ** -0.5 * LOG2E

    def row_window(rows, offset_fn):
        return pl.BlockSpec((pl.Element(rows), pl.Element(k_dim)),
                            lambda *g: (pl.multiple_of(offset_fn(*g), SUBLANES), 0))

    qm = pl.pallas_call(
        functools.partial(_qmla_kernel, scale=q_scale),
        grid=(Q_MLA_W // pair_w, m // tm),
        in_specs=[pl.BlockSpec((tm, k_dim), lambda j, i: (i, 0)),
                  pl.BlockSpec((pair_w, k_dim), lambda j, i: (j, 0)),
                  pl.BlockSpec((tm, LANES), lambda j, i: (i, 0)),
                  pl.BlockSpec((tm, LANES), lambda j, i: (i, 0))],
        out_specs=pl.BlockSpec((tm, pair_w), lambda j, i: (i, j)),
        out_shape=jax.ShapeDtypeStruct((m, Q_MLA_W), BF16),
        scratch_shapes=[pltpu.VMEM((pair_w, k_dim), BF16)],
        compiler_params=_cparams(("arbitrary", "arbitrary")),
        name="proj_qmla",
    )(h, w_t, cos_tab, sin_tab)
    kvr = pl.pallas_call(
        _kvr_kernel,
        grid=(m // tm,),
        in_specs=[pl.BlockSpec((tm, k_dim), lambda i: (i, 0)),
                  row_window(KVR_W, lambda i: Q_MLA_W + 0 * i)],
        out_specs=pl.BlockSpec((tm, KVR_PAD), lambda i: (i, 0)),
        out_shape=jax.ShapeDtypeStruct((m, KVR_PAD), F32),
        scratch_shapes=[pltpu.VMEM((KVR_PAD, k_dim), BF16)],
        compiler_params=_cparams(("arbitrary",)),
        name="proj_kvr",
    )(h, w_t)
    tn = 1024
    rest = pl.pallas_call(
        functools.partial(_rest_kernel, tn=tn),
        grid=(REST_W // tn, m // tm),
        in_specs=[pl.BlockSpec((tm, k_dim), lambda j, i: (i, 0)),
                  row_window(tn, lambda j, i: REST_OFF + tn * j)],
        out_specs=pl.BlockSpec((tm, tn), lambda j, i: (i, j)),
        out_shape=jax.ShapeDtypeStruct((m, REST_W), BF16),
        scratch_shapes=[pltpu.VMEM((tn, k_dim), BF16)],
        compiler_params=_cparams(("arbitrary", "arbitrary")),
        name="proj_rest",
    )(h, w_t)
    return qm, kvr, rest


def _kv_kernel(p_ref, g_ref, w_ref, cos_ref, sin_ref, k_ref, v_ref, wb_ref):
    @pl.when(pl.program_id(0) == 0)
    def _():
        wb_ref[...] = w_ref[...].astype(BF16)

    p = p_ref[...]
    ckv = p[:, :KV_RANK]
    ms = jnp.mean(ckv * ckv, axis=-1, keepdims=True)
    n = (ckv * lax.rsqrt(ms + EPS) * g_ref[...]).astype(BF16)
    kv = jnp.dot(n, wb_ref[...], preferred_element_type=F32)
    kr_even = _rope_cols(p[:, KV_RANK:], cos_ref[...], sin_ref[...])
    kr_odd = pltpu.roll(kr_even, MLA_ROPE, 1)
    ones_col = jnp.ones(kr_even.shape, BF16)
    kw = MLA_NOPE + MLA_V
    for hd in range(MLA_HEADS):
        k_ref[:, hd * kw:hd * kw + MLA_NOPE] = kv[:, hd * kw:hd * kw + MLA_NOPE].astype(BF16)
        k_ref[:, hd * kw + MLA_NOPE:(hd + 1) * kw] = (kr_even if hd % 2 == 0 else kr_odd).astype(BF16)
        v_ref[:, hd * kw:hd * kw + MLA_V] = kv[:, hd * kw + MLA_NOPE:(hd + 1) * kw].astype(BF16)
        v_ref[:, hd * kw + MLA_V:(hd + 1) * kw] = ones_col


def _kv(kvr, g_kv, w_ukv, cos_tab, sin_tab):
    m = kvr.shape[0]
    tm = 512
    kw = MLA_HEADS * (MLA_NOPE + MLA_V)
    return pl.pallas_call(
        _kv_kernel,
        grid=(m // tm,),
        in_specs=[pl.BlockSpec((tm, KVR_PAD), lambda i: (i, 0)),
                  pl.BlockSpec((1, KV_RANK), lambda i: (0, 0)),
                  pl.BlockSpec((KV_RANK, kw), lambda i: (0, 0)),
                  pl.BlockSpec((tm, LANES), lambda i: (i, 0)),
                  pl.BlockSpec((tm, LANES), lambda i: (i, 0))],
        out_specs=[pl.BlockSpec((tm, kw), lambda i: (i, 0)),
                   pl.BlockSpec((tm, kw), lambda i: (i, 0))],
        out_shape=[jax.ShapeDtypeStruct((m, kw), BF16),
                   jax.ShapeDtypeStruct((m, kw), BF16)],
        scratch_shapes=[pltpu.VMEM((KV_RANK, kw), BF16)],
        compiler_params=_cparams(("arbitrary",)),
        name="kv_up",
    )(kvr, g_kv, w_ukv, cos_tab, sin_tab)


def _flash_pipeline(n_chains, nq, diags, loops, value_fn, finalize_fn,
                    s_ref, acc_ref, mpart_ref, macc_ref, sd_ref=None, mpd_ref=None):
    chunk = s_ref.shape[2]

    def lane_tiles(x, n):
        return jnp.concatenate([x] * n, axis=1)

    def qk_phase(scores, stage):
        s_dst, m_dst = stage
        for ci, s in enumerate(scores):
            s_dst[ci] = s
            part = s[:, :LANES]
            for j in range(1, chunk // LANES):
                part = jnp.maximum(part, s[:, j * LANES:(j + 1) * LANES])
            m_dst[ci] = part

    def pv_phase(kc, stage):
        s_src, m_src = stage
        for ci in range(n_chains):
            m_acc = macc_ref[ci]
            m_run = jnp.maximum(m_acc, jnp.max(m_src[ci], axis=-1, keepdims=True))
            macc_ref[ci] = m_run
            p = jnp.exp2(s_src[ci] - lane_tiles(m_run, chunk // LANES))
            alpha = jnp.exp2(m_acc - m_run)
            pv = jnp.dot(p.astype(BF16), value_fn(ci, kc), preferred_element_type=F32)
            acc_ref[ci] = lane_tiles(alpha, acc_ref.shape[2] // LANES) * acc_ref[ci] + pv

    def reset():
        acc_ref[...] = jnp.zeros_like(acc_ref)
        macc_ref[...] = jnp.full(macc_ref.shape, NEG, F32)

    def tile(qt, stage, next_stage):
        cur = qt
        for trips_fn, score_fn in loops:
            def step(kc, cur, score_fn=score_fn):
                pv_phase(cur, stage)
                qk_phase(score_fn(qt, kc), stage)
                return kc

            cur = lax.fori_loop(0, trips_fn(qt), step, cur)
        nxt = jnp.minimum(qt + 1, nq - 1)

        def transition(_, cur, fn):
            if next_stage is stage:
                pv_phase(cur, stage)
                finalize_fn(qt)
                reset()
                qk_phase(fn(nxt), stage)
            else:
                qk_phase(fn(nxt), next_stage)
                pv_phase(cur, stage)
                finalize_fn(qt)
                reset()
            return cur

        for select_fn, fn in diags:
            if select_fn is None:
                transition(0, cur, fn)
            else:
                lax.fori_loop(0, select_fn(nxt), functools.partial(transition, fn=fn), cur)

    stage_a = (s_ref, mpart_ref)
    reset()
    qk_phase(diags[-1][1](0), stage_a)
    if sd_ref is None:
        lax.fori_loop(0, nq, lambda qt, c: tile(qt, stage_a, stage_a) or c, 0)
    else:
        stage_b = (sd_ref, mpd_ref)

        def tile_pair(j, c):
            tile(2 * j, stage_a, stage_b)
            tile(2 * j + 1, stage_b, stage_a)
            return c

        lax.fori_loop(0, nq // 2, tile_pair, 0)


def _causal_mask(rows, cols, row0):
    row = lax.broadcasted_iota(jnp.int32, (rows, cols), 0) + row0
    col = lax.broadcasted_iota(jnp.int32, (rows, cols), 1)
    return col <= row


_NT = (((1,), (1,)), ((), ()))


def _mla_attn_kernel(q_ref, k_ref, v_ref, g_ref, o_ref, acc_ref, qs_ref, s_ref, mpart_ref, macc_ref,
                     *, tq, rs):
    kw = 2 * MLA_NOPE
    tr = tq // rs
    nq = q_ref.shape[0] // tq
    for hp in range(2):
        qs_ref[hp, :, :MLA_NOPE] = q_ref[:, hp * MLA_NOPE:(hp + 1) * MLA_NOPE]
        qs_ref[hp, :, MLA_NOPE:] = q_ref[:, 2 * MLA_NOPE:]
    chains = [(hp, r) for hp in range(2) for r in range(rs)]

    def scores(qt, kc, diag):
        k0 = pl.multiple_of(kc * tq, tq)
        out = []
        for hp, r in chains:
            q = qs_ref[hp, pl.ds(pl.multiple_of(qt * tq + r * tr, tr), tr), :]
            k = k_ref[pl.ds(k0, tq), hp * kw:(hp + 1) * kw]
            s = lax.dot_general(q, k, _NT, preferred_element_type=F32)
            if diag:
                s = jnp.where(_causal_mask(tr, tq, r * tr), s, NEG)
            out.append(s)
        return out

    def values(ci, kc):
        hp = chains[ci][0]
        return v_ref[pl.ds(pl.multiple_of(kc * tq, tq), tq), hp * kw:(hp + 1) * kw]

    def finalize(qt):
        for ci, (hp, r) in enumerate(chains):
            rows = pl.ds(pl.multiple_of(qt * tq + r * tr, tr), tr)
            acc = acc_ref[ci]
            o = acc[:, :MLA_V] / acc[:, MLA_V:]
            gate = g_ref[rows, hp * MLA_V:(hp + 1) * MLA_V].astype(F32)
            o_ref[rows, hp * MLA_V:(hp + 1) * MLA_V] = (o * gate).astype(BF16)

    _flash_pipeline(len(chains), nq, [(None, lambda qt: scores(qt, qt, True))],
                    [(lambda qt: qt, lambda qt, kc: scores(qt, kc, False))],
                    values, finalize, s_ref, acc_ref, mpart_ref, macc_ref)


def _mla_attn(qm, kk, vv, rest, nb, s):
    tq = MLA_TQ
    pair_w = 2 * (MLA_NOPE + MLA_ROPE)
    kw = 4 * MLA_NOPE
    gate_blk = GM_OFF // (2 * MLA_V)
    rs = MLA_ROW_SPLIT
    return pl.pallas_call(
        functools.partial(_mla_attn_kernel, tq=tq, rs=rs),
        grid=(nb, MLA_HEADS // 2),
        in_specs=[pl.BlockSpec((s, pair_w), lambda b, hh: (b, hh)),
                  pl.BlockSpec((s, kw), lambda b, hh: (b, hh)),
                  pl.BlockSpec((s, kw), lambda b, hh: (b, hh)),
                  pl.BlockSpec((s, 2 * MLA_V), lambda b, hh: (b, gate_blk + hh))],
        out_specs=pl.BlockSpec((s, 2 * MLA_V), lambda b, hh: (b, hh)),
        out_shape=jax.ShapeDtypeStruct((nb * s, MLA_HEADS * MLA_V), BF16),
        scratch_shapes=[pltpu.VMEM((2 * rs, tq // rs, 2 * MLA_V), F32),
                        pltpu.VMEM((2, s, 2 * MLA_NOPE), BF16),
                        pltpu.VMEM((2 * rs, tq // rs, tq), F32),
                        pltpu.VMEM((2 * rs, tq // rs, LANES), F32),
                        pltpu.VMEM((2 * rs, tq // rs, LANES), F32)],
        compiler_params=_cparams(("arbitrary", "arbitrary")),
        name="mla_attn",
    )(qm, kk, vv, rest)


def _bf16_pieces(x):
    p1 = x.astype(BF16)
    r = x - p1.astype(F32)
    p2 = r.astype(BF16)
    return p1.astype(F32), p2.astype(F32), r - p2.astype(F32)


def _diff_attn_kernel(ord_ref, q_ref, k_ref, v_ref, g_ref, pq_ref, pk_ref, sl_ref, lam_ref, gs_ref,
                      o_ref, vaug_ref, kaug_ref, acc_ref, qf_ref, s_ref, mpart_ref, macc_ref,
                      sd_ref, mpd_ref, *, tq, rs):
    seq = q_ref.shape[0]
    nq = seq // tq
    tr = tq // rs
    chains = [(c, r) for c in range(2) for r in range(rs)]
    lane = lax.broadcasted_iota(jnp.int32, (tq, LANES), 1)
    ones_col = jnp.ones((tq, LANES), BF16)
    slope2 = sl_ref[0, :, 0:1] * LOG2E
    c_pieces = _bf16_pieces(slope2)

    def pick(sel, x3):
        return jnp.where(sel == 0, x3[0], jnp.where(sel == 1, x3[1], x3[2]))

    def bias_lanes(rows, key_side):
        pos = _bf16_pieces(pq_ref[rows, :])
        lane_b = lane - 9
        if key_side:
            lo, hi = pick(lane % 3, pos), pick(lane_b // 3, c_pieces)
        else:
            lo, hi = pick(lane // 3, c_pieces), pick(lane_b % 3, [-p for p in pos])
        return jnp.where(lane < 9, lo, jnp.where(lane < 18, hi, 0.0)).astype(BF16)

    def stage_keys(kc, _):
        rows = pl.ds(pl.multiple_of(kc * tq, tq), tq)
        vaug_ref[rows, :DIFF_V] = v_ref[rows, :]
        vaug_ref[rows, DIFF_V:] = ones_col
        kaug_ref[rows, :2 * DIFF_QK] = k_ref[rows, :]
        kaug_ref[rows, 2 * DIFF_QK:] = bias_lanes(rows, True)
        return 0

    lax.fori_loop(0, nq, stage_keys, 0)

    def stage_queries(qt):
        rows = pl.ds(pl.multiple_of(qt * tq, tq), tq)
        q = q_ref[rows, :]
        q_side = bias_lanes(rows, False)
        qf_ref[0, :, :2 * DIFF_QK] = jnp.where(lane < DIFF_QK, q, 0).astype(BF16)
        qf_ref[1, :, :2 * DIFF_QK] = jnp.where(lane >= DIFF_QK, q, 0).astype(BF16)
        qf_ref[0, :, 2 * DIFF_QK:] = q_side
        qf_ref[1, :, 2 * DIFF_QK:] = q_side

    lq = lam_ref[...]
    lam = (jnp.exp(jnp.sum(lq[0:1] * lq[1:2], axis=-1, keepdims=True))
           - jnp.exp(jnp.sum(lq[2:3] * lq[3:4], axis=-1, keepdims=True)) + LAMBDA_INIT)

    def scores(qt, kc, diag):
        k0 = pl.multiple_of(kc * tq, tq)
        k = k_ref[pl.ds(k0, tq), :]
        pk = slope2 * pk_ref[0, pl.ds(kc, 1), :]
        out = [None] * len(chains)
        for r in range(rs):
            pq = pq_ref[pl.ds(pl.multiple_of(qt * tq + r * tr, tr), tr), :]
            bias = jnp.abs(slope2 * pq - pk)
            mask = _causal_mask(tr, tq, r * tr) if diag else None
            for c in range(2):
                q = qf_ref[c, r * tr:(r + 1) * tr, :2 * DIFF_QK]
                s = lax.dot_general(q, k, _NT, preferred_element_type=F32) - bias
                out[chains.index((c, r))] = jnp.where(mask, s, NEG) if diag else s
        return out

    def scores_diag(qt):
        stage_queries(qt)
        return scores(qt, qt, True)

    def scores_ordered(qt, kc):
        k = kaug_ref[pl.ds(pl.multiple_of(kc * tq, tq), tq), :]
        return [lax.dot_general(qf_ref[c, r * tr:(r + 1) * tr, :], k, _NT, preferred_element_type=F32)
                for c, r in chains]

    def scores_diag_sorted(qt):
        stage_queries(qt)
        return [jnp.where(_causal_mask(tr, tq, r * tr), s, NEG)
                for (c, r), s in zip(chains, scores_ordered(qt, qt))]

    batch = pl.program_id(0)

    def trips_ordered(qt):
        return jnp.where((ord_ref[batch, qt] & 1) != 0, qt, 0)

    def tile_sorted(qt):
        return (ord_ref[batch, qt] >> 1) & 1

    def values(ci, kc):
        return vaug_ref[pl.ds(pl.multiple_of(kc * tq, tq), tq), :]

    def finalize(qt):
        for r in range(rs):
            rows = pl.ds(pl.multiple_of(qt * tq + r * tr, tr), tr)
            a1 = acc_ref[chains.index((0, r))]
            a2 = acc_ref[chains.index((1, r))]
            o = a1[:, :DIFF_V] / a1[:, DIFF_V:] - lam * (a2[:, :DIFF_V] / a2[:, DIFF_V:])
            ms_o = jnp.mean(o * o, axis=-1, keepdims=True)
            o = o * lax.rsqrt(ms_o + EPS) * gs_ref[...] * (1.0 - LAMBDA_INIT)
            o_ref[rows, :] = (o * g_ref[rows, :].astype(F32)).astype(BF16)

    _flash_pipeline(len(chains), nq,
                    [(tile_sorted, scores_diag_sorted), (lambda qt: 1 - tile_sorted(qt), scores_diag)],
                    [(trips_ordered, scores_ordered),
                     (lambda qt: qt - trips_ordered(qt), lambda qt, kc: scores(qt, kc, False))],
                    values, finalize, s_ref, acc_ref, mpart_ref, macc_ref, sd_ref, mpd_ref)


def _diff_attn(rest, pos_col, pos_row, ordered, slopes, lam_par, g_subln, nb, s):
    tq = ATTN_TQ
    nq = s // tq
    rs = DIFF_ROW_SPLIT
    tr = tq // rs
    hw = DIFF_V
    return pl.pallas_call(
        functools.partial(_diff_attn_kernel, tq=tq, rs=rs),
        grid=(nb, DIFF_HEADS),
        in_specs=[pl.BlockSpec(memory_space=pltpu.SMEM),
                  pl.BlockSpec((s, hw), lambda b, hd: (b, QD_OFF // hw + hd)),
                  pl.BlockSpec((s, hw), lambda b, hd: (b, KD_OFF // hw + hd)),
                  pl.BlockSpec((s, hw), lambda b, hd: (b, VD_OFF // hw + hd)),
                  pl.BlockSpec((s, hw), lambda b, hd: (b, GD_OFF // hw + hd)),
                  pl.BlockSpec((s, 1), lambda b, hd: (b, 0)),
                  pl.BlockSpec((1, nq, tq), lambda b, hd: (b, 0, 0)),
                  pl.BlockSpec((1, 1, LANES), lambda b, hd: (hd, 0, 0)),
                  pl.BlockSpec((4, DIFF_QK), lambda b, hd: (0, 0)),
                  pl.BlockSpec((1, DIFF_V), lambda b, hd: (0, 0))],
        out_specs=pl.BlockSpec((s, hw), lambda b, hd: (b, hd)),
        out_shape=jax.ShapeDtypeStruct((nb * s, DIFF_HEADS * DIFF_V), BF16),
        scratch_shapes=[pltpu.VMEM((s, 2 * DIFF_V), BF16),
                        pltpu.VMEM((s, 4 * DIFF_QK), BF16),
                        pltpu.VMEM((2 * rs, tr, 2 * DIFF_V), F32),
                        pltpu.VMEM((2, tq, 4 * DIFF_QK), BF16),
                        pltpu.VMEM((2 * rs, tr, tq), F32),
                        pltpu.VMEM((2 * rs, tr, LANES), F32),
                        pltpu.VMEM((2 * rs, tr, LANES), F32),
                        pltpu.VMEM((2 * rs, tr, tq), F32),
                        pltpu.VMEM((2 * rs, tr, LANES), F32)],
        compiler_params=_cparams(("arbitrary", "arbitrary")),
        name="diff_attn",
    )(ordered, rest, rest, rest, rest, pos_col, pos_row, slopes, lam_par, g_subln)


def _merge_kernel(a1_ref, a2_ref, w1_ref, w2_ref, s1_ref, s2_ref, o_ref, wb_ref):
    @pl.when(pl.program_id(1) == 0)
    def _():
        wb_ref[0] = w1_ref[...].astype(BF16)
        wb_ref[1] = w2_ref[...].astype(BF16)

    y1 = jnp.dot(a1_ref[...], wb_ref[0], preferred_element_type=F32)
    y2 = jnp.dot(a2_ref[...], wb_ref[1], preferred_element_type=F32)
    o_ref[...] = (s1_ref[...].astype(F32) * y1 + s2_ref[...].astype(F32) * y2).astype(BF16)


def _merge(og_mla, og_diff, w1, w2, rest):
    m, k_dim = og_mla.shape
    n = w1.shape[1]
    tm, tn = 1024, 1024
    return pl.pallas_call(
        _merge_kernel,
        grid=(n // tn, m // tm),
        in_specs=[pl.BlockSpec((tm, k_dim), lambda j, i: (i, 0)),
                  pl.BlockSpec((tm, k_dim), lambda j, i: (i, 0)),
                  pl.BlockSpec((k_dim, tn), lambda j, i: (0, j)),
                  pl.BlockSpec((k_dim, tn), lambda j, i: (0, j)),
                  pl.BlockSpec((tm, tn), lambda j, i: (i, MGM_OFF // tn + j)),
                  pl.BlockSpec((tm, tn), lambda j, i: (i, MGD_OFF // tn + j))],
        out_specs=pl.BlockSpec((tm, tn), lambda j, i: (i, j)),
        out_shape=jax.ShapeDtypeStruct((m, n), BF16),
        scratch_shapes=[pltpu.VMEM((2, k_dim, tn), BF16)],
        compiler_params=_cparams(("arbitrary", "arbitrary")),
        name="merge",
    )(og_mla, og_diff, w1, w2, rest, rest)


def _out_kernel(a_ref, w_ref, x_ref, ada_ref, g_ref, o_ref, wb_ref):
    @pl.when(jnp.logical_and(pl.program_id(0) == 0, pl.program_id(1) == 0))
    def _():
        wb_ref[...] = w_ref[...].astype(BF16)

    y = jnp.dot(a_ref[...], wb_ref[...], preferred_element_type=F32)
    ms = jnp.mean(y * y, axis=-1, keepdims=True)
    yn = y * lax.rsqrt(ms + EPS) * g_ref[...]
    o_ref[0] = x_ref[0] + ada_ref[0, 2:3, :] * yn


def _out(merged, w_out, x, ada3, g_post):
    nb, s, d = x.shape
    tm = 512
    ns = s // tm
    return pl.pallas_call(
        _out_kernel,
        grid=(nb, ns),
        in_specs=[pl.BlockSpec((tm, d), lambda b, i: (b * ns + i, 0)),
                  pl.BlockSpec((d, d), lambda b, i: (0, 0), pipeline_mode=pl.Buffered(1)),
                  pl.BlockSpec((1, tm, d), lambda b, i: (b, i, 0)),
                  pl.BlockSpec((1, 3, d), lambda b, i: (b, 0, 0)),
                  pl.BlockSpec((1, d), lambda b, i: (0, 0))],
        out_specs=pl.BlockSpec((1, tm, d), lambda b, i: (b, i, 0)),
        out_shape=jax.ShapeDtypeStruct((nb, s, d), F32),
        scratch_shapes=[pltpu.VMEM((d, d), BF16)],
        compiler_params=_cparams(("arbitrary", "arbitrary")),
        name="out_proj",
    )(merged, w_out, x, ada3, g_post)


def kernel(x, c, positions, w_ada, b_ada, g_pre, w_in, g_kv, w_ukv, lambda_q1, lambda_k1,
           lambda_q2, lambda_k2, g_subln, w_o_mla, w_o_diff, w_out, g_post):
    nb, s, d = x.shape
    depth = w_in.shape[0]
    half = MLA_ROPE // 2
    inv = ROPE_THETA ** (-jnp.arange(half, dtype=F32) / half)
    inv_tab = jnp.tile(inv, LANES // half).reshape(1, LANES)
    slopes = 2.0 ** (-8.0 * jnp.arange(1, DIFF_HEADS + 1, dtype=F32) / DIFF_HEADS)
    slopes = jnp.broadcast_to(slopes.reshape(DIFF_HEADS, 1, 1), (DIFF_HEADS, 1, LANES))
    pos_col = positions.reshape(nb * s, 1)
    pos_colf = pos_col.astype(F32)
    pos_chunks = positions.reshape(nb, s // ATTN_TQ, ATTN_TQ)
    pos_row = pos_chunks.astype(F32)
    run_max = lax.cummax(pos_chunks.max(axis=-1), axis=1)
    prev_max = jnp.concatenate(
        [jnp.full((nb, 1), jnp.iinfo(jnp.int32).min, jnp.int32), run_max[:, :-1]], axis=1)
    tile_sorted = jnp.all(pos_chunks[..., 1:] >= pos_chunks[..., :-1], axis=-1)
    ordered = ((prev_max <= pos_chunks.min(axis=-1)).astype(jnp.int32)
               + 2 * tile_sorted.astype(jnp.int32))

    for l in range(depth):
        ada3 = _ada(c, w_ada[l], b_ada[l]).reshape(nb, 3, d)
        h, cos_tab, sin_tab = _norm(x, ada3, g_pre[l].reshape(1, d), pos_col, inv_tab)
        qm, kvr, rest = _proj(h, jnp.swapaxes(w_in[l], 0, 1), cos_tab, sin_tab)
        kk, vv = _kv(kvr, g_kv[l].reshape(1, KV_RANK), w_ukv[l], cos_tab, sin_tab)
        og_mla = _mla_attn(qm, kk, vv, rest, nb, s)
        lam_par = jnp.stack([lambda_q1[l], lambda_k1[l], lambda_q2[l], lambda_k2[l]]).astype(F32)
        og_diff = _diff_attn(rest, pos_colf, pos_row, ordered, slopes, lam_par,
                             g_subln[l].reshape(1, DIFF_V), nb, s)
        merged = _merge(og_mla, og_diff, w_o_mla[l], w_o_diff[l], rest)
        x = _out(merged, w_out[l], x, ada3, g_post[l].reshape(1, d))
    return x
```

```python
import functools
import math

import jax
import jax.numpy as jnp
from jax import lax
from jax.experimental import pallas as pl
from jax.experimental.pallas import tpu as pltpu

F32 = jnp.float32
BF16 = jnp.bfloat16

D_MODEL = 2048
MLA_HEADS = 8
MLA_NOPE = 128
MLA_ROPE = 64
MLA_V = 128
KV_RANK = 512
ROPE_THETA = 10000.0
DIFF_HEADS = 8
DIFF_QK = 64
DIFF_V = 128
EPS = 1e-6
NEG = -1e30
LAMBDA_INIT = 0.8 - 0.6 * math.exp(-0.3 * 0)

LANES = 128
SUBLANES = 8
VMEM_LIMIT = 56 * 1024 * 1024
ATTN_TQ = 1024
DIFF_ROW_SPLIT = 2
MLA_TQ = 1024
MLA_ROW_SPLIT = 2
PROJ_TM = 1024
REST_TN = 1024
REST_COL_SPLIT = 1
QMLA_PAIRS_PER_TILE = 2
KV_TM = 1024
MERGE_TM, MERGE_TN = 1024, 1024
OUT_TM = 512
LOG2E = math.log2(math.e)

Q_MLA_W = MLA_HEADS * (MLA_NOPE + MLA_ROPE)
KVR_W = KV_RANK + MLA_ROPE
KVR_PAD = 640
REST_OFF = Q_MLA_W + KVR_W
REST_SHIFT = REST_OFF % LANES
QD_OFF, KD_OFF, VD_OFF, GM_OFF, GD_OFF, MGM_OFF, MGD_OFF = 0, 1024, 2048, 3072, 4096, 5120, 7168
REST_W = 9216


def _cparams(sem):
    return pltpu.CompilerParams(dimension_semantics=sem, vmem_limit_bytes=VMEM_LIMIT)


def _ada_kernel(cb_ref, w_ref, b_ref, o_ref):
    k_dim, tn = w_ref.shape
    nb = cb_ref.shape[0]
    nchunk = tn // LANES

    def body(i, accs):
        k0 = pl.multiple_of(i * SUBLANES, SUBLANES)
        out = []
        for b in range(nb):
            cv = cb_ref[b, pl.ds(k0, SUBLANES), :]
            for j in range(nchunk):
                wv = w_ref[pl.ds(k0, SUBLANES), j * LANES:(j + 1) * LANES]
                out.append(accs[b * nchunk + j] + wv * cv)
        return tuple(out)

    init = tuple(jnp.zeros((SUBLANES, LANES), F32) for _ in range(nb * nchunk))
    accs = lax.fori_loop(0, k_dim // SUBLANES, body, init, unroll=8)
    for b in range(nb):
        row = jnp.concatenate(
            [jnp.sum(accs[b * nchunk + j], axis=0, keepdims=True) for j in range(nchunk)], axis=1)
        o_ref[b:b + 1, :] = row + b_ref[...]


def _ada(c, w, bias):
    nb, k_dim = c.shape
    n = w.shape[1]
    tn = 512
    cb = jnp.broadcast_to(c[:, :, None], (nb, k_dim, LANES))
    return pl.pallas_call(
        _ada_kernel,
        grid=(n // tn,),
        in_specs=[pl.BlockSpec((nb, k_dim, LANES), lambda j: (0, 0, 0)),
                  pl.BlockSpec((k_dim, tn), lambda j: (0, j)),
                  pl.BlockSpec((1, tn), lambda j: (0, j))],
        out_specs=pl.BlockSpec((nb, tn), lambda j: (0, j)),
        out_shape=jax.ShapeDtypeStruct((nb, n), F32),
        compiler_params=_cparams(("arbitrary",)),
        name="ada",
    )(cb, w, bias.reshape(1, n))


def _norm_kernel(x_ref, ada_ref, g_ref, pos_ref, inv_ref, h_ref, cos_ref, sin_ref):
    x = x_ref[0]
    ms = jnp.mean(x * x, axis=-1, keepdims=True)
    y = x * lax.rsqrt(ms + EPS) * g_ref[...]
    shift = ada_ref[0, 0:1, :]
    scale = ada_ref[0, 1:2, :]
    h_ref[...] = (y * (1.0 + scale) + shift).astype(BF16)
    ang = pos_ref[...].astype(F32) * inv_ref[...]
    lane = lax.broadcasted_iota(jnp.int32, ang.shape, 1)
    sign = jnp.where((lane % MLA_ROPE) < MLA_ROPE // 2, -1.0, 1.0).astype(F32)
    cos_ref[...] = jnp.cos(ang)
    sin_ref[...] = jnp.sin(ang) * sign


def _norm(x, ada3, g_pre, pos_col, inv_tab):
    nb, s, d = x.shape
    ts = 512
    ns = s // ts
    row = lambda b, i: (b * ns + i, 0)
    return pl.pallas_call(
        _norm_kernel,
        grid=(nb, ns),
        in_specs=[pl.BlockSpec((1, ts, d), lambda b, i: (b, i, 0)),
                  pl.BlockSpec((1, 3, d), lambda b, i: (b, 0, 0)),
                  pl.BlockSpec((1, d), lambda b, i: (0, 0)),
                  pl.BlockSpec((ts, 1), row),
                  pl.BlockSpec((1, LANES), lambda b, i: (0, 0))],
        out_specs=[pl.BlockSpec((ts, d), row),
                   pl.BlockSpec((ts, LANES), row),
                   pl.BlockSpec((ts, LANES), row)],
        out_shape=[jax.ShapeDtypeStruct((nb * s, d), BF16),
                   jax.ShapeDtypeStruct((nb * s, LANES), F32),
                   jax.ShapeDtypeStruct((nb * s, LANES), F32)],
        compiler_params=_cparams(("arbitrary", "arbitrary")),
        name="prenorm",
    )(x, ada3, g_pre, pos_col, inv_tab)


def _rope_cols(r, cos, sin_signed):
    lane = lax.broadcasted_iota(jnp.int32, r.shape, 1)
    half = MLA_ROPE // 2
    partner = jnp.where((lane % MLA_ROPE) < half,
                        pltpu.roll(r, LANES - half, 1), pltpu.roll(r, half, 1))
    return r * cos + partner * sin_signed


def _qmla_kernel(a_ref, w_ref, cos_ref, sin_ref, o_ref, wb_ref, *, scale):
    hd = MLA_NOPE + MLA_ROPE
    pair_w = 2 * hd
    n_pairs = wb_ref.shape[0] // pair_w

    @pl.when(pl.program_id(1) == 0)
    def _():
        for p in range(n_pairs):
            src, dst = w_ref.at[p * pair_w:(p + 1) * pair_w], wb_ref.at[p * pair_w:(p + 1) * pair_w]
            dst[:MLA_NOPE] = src[:MLA_NOPE].astype(BF16)
            dst[MLA_NOPE:2 * MLA_NOPE] = src[hd:hd + MLA_NOPE].astype(BF16)
            dst[2 * MLA_NOPE:2 * MLA_NOPE + MLA_ROPE] = src[MLA_NOPE:hd].astype(BF16)
            dst[2 * MLA_NOPE + MLA_ROPE:] = src[hd + MLA_NOPE:].astype(BF16)

    acc = lax.dot_general(a_ref[...], wb_ref[...], _NT, preferred_element_type=F32)
    for p in range(n_pairs):
        c0 = p * pair_w
        rr = _rope_cols(acc[:, c0 + 2 * MLA_NOPE:c0 + pair_w], cos_ref[...], sin_ref[...])
        o_ref[:, c0:c0 + 2 * MLA_NOPE] = (acc[:, c0:c0 + 2 * MLA_NOPE] * scale).astype(BF16)
        o_ref[:, c0 + 2 * MLA_NOPE:c0 + pair_w] = (rr * scale).astype(BF16)


def _rest_kernel(a_ref, w_ref, o_ref, wb_ref, *, tn):
    j = pl.program_id(0)

    @pl.when(pl.program_id(1) == 0)
    def _():
        wb_ref[...] = w_ref[...].astype(BF16)

    is_q = j < KD_OFF // tn
    is_silu = jnp.logical_and(j >= GM_OFF // tn, j < MGM_OFF // tn)
    is_sig = j >= MGM_OFF // tn
    a = jnp.where(is_q, DIFF_QK ** -0.5 * LOG2E, jnp.where(j < GM_OFF // tn, 1.0, 0.0)).astype(F32)
    b = jnp.where(is_sig, 1.0, 0.0).astype(F32)
    c = jnp.where(is_silu, 1.0, 0.0).astype(F32)
    sub = tn // REST_COL_SPLIT
    for t in range(REST_COL_SPLIT):
        acc = lax.dot_general(a_ref[...], wb_ref[t * sub:(t + 1) * sub, :], _NT,
                              preferred_element_type=F32)
        sig = jax.nn.sigmoid(acc)
        o_ref[:, t * sub:(t + 1) * sub] = (acc * (a + c * sig) + b * sig).astype(BF16)


def _proj(h, w_t, cos_tab, sin_tab):
    m, k_dim = h.shape
    tm = PROJ_TM
    pair_w = 2 * (MLA_NOPE + MLA_ROPE)
    q_scale = (MLA_NOPE + MLA_ROPE) ** -0.5 * LOG2E

    def row_window(rows, offset_fn):
        return pl.BlockSpec((pl.Element(rows), pl.Element(k_dim)),
                            lambda *g: (pl.multiple_of(offset_fn(*g), SUBLANES), 0))

    qtn = QMLA_PAIRS_PER_TILE * pair_w
    qm = pl.pallas_call(
        functools.partial(_qmla_kernel, scale=q_scale),
        grid=(Q_MLA_W // qtn, m // tm),
        in_specs=[pl.BlockSpec((tm, k_dim), lambda j, i: (i, 0)),
                  pl.BlockSpec((qtn, k_dim), lambda j, i: (j, 0)),
                  pl.BlockSpec((tm, LANES), lambda j, i: (i, 0)),
                  pl.BlockSpec((tm, LANES), lambda j, i: (i, 0))],
        out_specs=pl.BlockSpec((tm, qtn), lambda j, i: (i, j)),
        out_shape=jax.ShapeDtypeStruct((m, Q_MLA_W), BF16),
        scratch_shapes=[pltpu.VMEM((qtn, k_dim), BF16)],
        compiler_params=_cparams(("arbitrary", "arbitrary")),
        name="proj_qmla",
    )(h, w_t, cos_tab, sin_tab)
    tn = REST_TN
    rest = pl.pallas_call(
        functools.partial(_rest_kernel, tn=tn),
        grid=(REST_W // tn, m // tm),
        in_specs=[pl.BlockSpec((tm, k_dim), lambda j, i: (i, 0)),
                  row_window(tn, lambda j, i: REST_OFF + tn * j)],
        out_specs=pl.BlockSpec((tm, tn), lambda j, i: (i, j)),
        out_shape=jax.ShapeDtypeStruct((m, REST_W), BF16),
        scratch_shapes=[pltpu.VMEM((tn, k_dim), BF16)],
        compiler_params=_cparams(("arbitrary", "arbitrary")),
        name="proj_rest",
    )(h, w_t)
    return qm, rest


def _kv_kernel(a_ref, wp_ref, g_ref, w_ref, cos_ref, sin_ref, k_ref, v_ref, wpb_ref, wb_ref):
    @pl.when(pl.program_id(0) == 0)
    def _():
        wpb_ref[:KVR_W] = wp_ref[...].astype(BF16)
        wpb_ref[KVR_W:] = jnp.zeros((KVR_PAD - KVR_W, wpb_ref.shape[1]), BF16)
        wb_ref[...] = w_ref[...].astype(BF16)

    p = lax.dot_general(a_ref[...], wpb_ref[...], _NT, preferred_element_type=F32)
    ckv = p[:, :KV_RANK]
    ms = jnp.mean(ckv * ckv, axis=-1, keepdims=True)
    n = (ckv * lax.rsqrt(ms + EPS) * g_ref[...]).astype(BF16)
    kv = jnp.dot(n, wb_ref[...], preferred_element_type=F32)
    kr_even = _rope_cols(p[:, KV_RANK:], cos_ref[...], sin_ref[...])
    kr_odd = pltpu.roll(kr_even, MLA_ROPE, 1)
    ones_col = jnp.ones(kr_even.shape, BF16)
    kw = MLA_NOPE + MLA_V
    for hd in range(MLA_HEADS):
        k_ref[:, hd * kw:hd * kw + MLA_NOPE] = kv[:, hd * kw:hd * kw + MLA_NOPE].astype(BF16)
        k_ref[:, hd * kw + MLA_NOPE:(hd + 1) * kw] = (kr_even if hd % 2 == 0 else kr_odd).astype(BF16)
        v_ref[:, hd * kw:hd * kw + MLA_V] = kv[:, hd * kw + MLA_NOPE:(hd + 1) * kw].astype(BF16)
        v_ref[:, hd * kw + MLA_V:(hd + 1) * kw] = ones_col


def _kv(h, w_t, g_kv, w_ukv, cos_tab, sin_tab):
    m, k_dim = h.shape
    tm = KV_TM
    kw = MLA_HEADS * (MLA_NOPE + MLA_V)
    return pl.pallas_call(
        _kv_kernel,
        grid=(m // tm,),
        in_specs=[pl.BlockSpec((tm, k_dim), lambda i: (i, 0)),
                  pl.BlockSpec((pl.Element(KVR_W), pl.Element(k_dim)),
                               lambda i: (pl.multiple_of(Q_MLA_W + 0 * i, SUBLANES), 0)),
                  pl.BlockSpec((1, KV_RANK), lambda i: (0, 0)),
                  pl.BlockSpec((KV_RANK, kw), lambda i: (0, 0)),
                  pl.BlockSpec((tm, LANES), lambda i: (i, 0)),
                  pl.BlockSpec((tm, LANES), lambda i: (i, 0))],
        out_specs=[pl.BlockSpec((tm, kw), lambda i: (i, 0)),
                   pl.BlockSpec((tm, kw), lambda i: (i, 0))],
        out_shape=[jax.ShapeDtypeStruct((m, kw), BF16),
                   jax.ShapeDtypeStruct((m, kw), BF16)],
        scratch_shapes=[pltpu.VMEM((KVR_PAD, k_dim), BF16), pltpu.VMEM((KV_RANK, kw), BF16)],
        compiler_params=_cparams(("arbitrary",)),
        name="kv_up",
    )(h, w_t, g_kv, w_ukv, cos_tab, sin_tab)


def _flash_pipeline(n_chains, nq, diags, loops, value_fn, finalize_fn,
                    s_ref, acc_ref, mpart_ref, macc_ref, sd_ref=None, mpd_ref=None):
    chunk = s_ref.shape[2]

    def lane_tiles(x, n):
        return jnp.concatenate([x] * n, axis=1)

    def qk_phase(scores, stage):
        s_dst, m_dst = stage
        for ci, s in enumerate(scores):
            s_dst[ci] = s
            part = s[:, :LANES]
            for j in range(1, chunk // LANES):
                part = jnp.maximum(part, s[:, j * LANES:(j + 1) * LANES])
            m_dst[ci] = part

    def pv_phase(kc, stage):
        s_src, m_src = stage
        for ci in range(n_chains):
            m_acc = macc_ref[ci]
            m_run = jnp.maximum(m_acc, jnp.max(m_src[ci], axis=-1, keepdims=True))
            macc_ref[ci] = m_run
            p = jnp.exp2(s_src[ci] - lane_tiles(m_run, chunk // LANES))
            alpha = jnp.exp2(m_acc - m_run)
            pv = jnp.dot(p.astype(BF16), value_fn(ci, kc), preferred_element_type=F32)
            acc_ref[ci] = lane_tiles(alpha, acc_ref.shape[2] // LANES) * acc_ref[ci] + pv

    def reset():
        acc_ref[...] = jnp.zeros_like(acc_ref)
        macc_ref[...] = jnp.full(macc_ref.shape, NEG, F32)

    def tile(qt, stage, next_stage):
        cur = qt
        for trips_fn, score_fn in loops:
            def step(kc, cur, score_fn=score_fn):
                pv_phase(cur, stage)
                qk_phase(score_fn(qt, kc), stage)
                return kc

            cur = lax.fori_loop(0, trips_fn(qt), step, cur)
        nxt = jnp.minimum(qt + 1, nq - 1)

        def transition(_, cur, fn):
            if next_stage is stage:
                pv_phase(cur, stage)
                finalize_fn(qt)
                reset()
                qk_phase(fn(nxt), stage)
            else:
                qk_phase(fn(nxt), next_stage)
                pv_phase(cur, stage)
                finalize_fn(qt)
                reset()
            return cur

        for select_fn, fn in diags:
            if select_fn is None:
                transition(0, cur, fn)
            else:
                lax.fori_loop(0, select_fn(nxt), functools.partial(transition, fn=fn), cur)

    stage_a = (s_ref, mpart_ref)
    reset()
    qk_phase(diags[-1][1](0), stage_a)
    if sd_ref is None:
        lax.fori_loop(0, nq, lambda qt, c: tile(qt, stage_a, stage_a) or c, 0)
    else:
        stage_b = (sd_ref, mpd_ref)

        def tile_pair(j, c):
            tile(2 * j, stage_a, stage_b)
            tile(2 * j + 1, stage_b, stage_a)
            return c

        lax.fori_loop(0, nq // 2, tile_pair, 0)


def _causal_mask(rows, cols, row0):
    row = lax.broadcasted_iota(jnp.int32, (rows, cols), 0) + row0
    col = lax.broadcasted_iota(jnp.int32, (rows, cols), 1)
    return col <= row


_NT = (((1,), (1,)), ((), ()))


def _mla_attn_kernel(q_ref, k_ref, v_ref, g_ref, o_ref, acc_ref, qs_ref, s_ref, mpart_ref, macc_ref,
                     *, tq, rs):
    kw = 2 * MLA_NOPE
    tr = tq // rs
    nq = q_ref.shape[0] // tq
    for hp in range(2):
        qs_ref[hp, :, :MLA_NOPE] = q_ref[:, hp * MLA_NOPE:(hp + 1) * MLA_NOPE]
        qs_ref[hp, :, MLA_NOPE:] = q_ref[:, 2 * MLA_NOPE:]
    chains = [(hp, r) for hp in range(2) for r in range(rs)]

    def scores(qt, kc, diag):
        k0 = pl.multiple_of(kc * tq, tq)
        out = []
        for hp, r in chains:
            q = qs_ref[hp, pl.ds(pl.multiple_of(qt * tq + r * tr, tr), tr), :]
            k = k_ref[pl.ds(k0, tq), hp * kw:(hp + 1) * kw]
            s = lax.dot_general(q, k, _NT, preferred_element_type=F32)
            if diag:
                s = jnp.where(_causal_mask(tr, tq, r * tr), s, NEG)
            out.append(s)
        return out

    def values(ci, kc):
        hp = chains[ci][0]
        return v_ref[pl.ds(pl.multiple_of(kc * tq, tq), tq), hp * kw:(hp + 1) * kw]

    def finalize(qt):
        for ci, (hp, r) in enumerate(chains):
            rows = pl.ds(pl.multiple_of(qt * tq + r * tr, tr), tr)
            acc = acc_ref[ci]
            o = acc[:, :MLA_V] / acc[:, MLA_V:]
            gate = g_ref[rows, hp * MLA_V:(hp + 1) * MLA_V].astype(F32)
            o_ref[rows, hp * MLA_V:(hp + 1) * MLA_V] = (o * gate).astype(BF16)

    _flash_pipeline(len(chains), nq, [(None, lambda qt: scores(qt, qt, True))],
                    [(lambda qt: qt, lambda qt, kc: scores(qt, kc, False))],
                    values, finalize, s_ref, acc_ref, mpart_ref, macc_ref)


def _mla_attn(qm, kk, vv, rest, nb, s):
    tq = MLA_TQ
    pair_w = 2 * (MLA_NOPE + MLA_ROPE)
    kw = 4 * MLA_NOPE
    gate_blk = GM_OFF // (2 * MLA_V)
    rs = MLA_ROW_SPLIT
    return pl.pallas_call(
        functools.partial(_mla_attn_kernel, tq=tq, rs=rs),
        grid=(nb, MLA_HEADS // 2),
        in_specs=[pl.BlockSpec((s, pair_w), lambda b, hh: (b, hh)),
                  pl.BlockSpec((s, kw), lambda b, hh: (b, hh)),
                  pl.BlockSpec((s, kw), lambda b, hh: (b, hh)),
                  pl.BlockSpec((s, 2 * MLA_V), lambda b, hh: (b, gate_blk + hh))],
        out_specs=pl.BlockSpec((s, 2 * MLA_V), lambda b, hh: (b, hh)),
        out_shape=jax.ShapeDtypeStruct((nb * s, MLA_HEADS * MLA_V), BF16),
        scratch_shapes=[pltpu.VMEM((2 * rs, tq // rs, 2 * MLA_V), F32),
                        pltpu.VMEM((2, s, 2 * MLA_NOPE), BF16),
                        pltpu.VMEM((2 * rs, tq // rs, tq), F32),
                        pltpu.VMEM((2 * rs, tq // rs, LANES), F32),
                        pltpu.VMEM((2 * rs, tq // rs, LANES), F32)],
        compiler_params=_cparams(("arbitrary", "arbitrary")),
        name="mla_attn",
    )(qm, kk, vv, rest)


def _bf16_pieces(x):
    p1 = x.astype(BF16)
    r = x - p1.astype(F32)
    p2 = r.astype(BF16)
    return p1.astype(F32), p2.astype(F32), r - p2.astype(F32)


def _diff_attn_kernel(ord_ref, q_ref, k_ref, v_ref, g_ref, pq_ref, pk_ref, sl_ref, lam_ref, gs_ref,
                      o_ref, vaug_ref, kaug_ref, acc_ref, qf_ref, s_ref, mpart_ref, macc_ref,
                      sd_ref, mpd_ref, *, tq, rs):
    seq = q_ref.shape[0]
    nq = seq // tq
    tr = tq // rs
    chains = [(c, r) for c in range(2) for r in range(rs)]
    lane = lax.broadcasted_iota(jnp.int32, (tq, LANES), 1)
    ones_col = jnp.ones((tq, LANES), BF16)
    slope2 = sl_ref[0, :, 0:1] * LOG2E
    c_pieces = _bf16_pieces(slope2)

    def pick(sel, x3):
        return jnp.where(sel == 0, x3[0], jnp.where(sel == 1, x3[1], x3[2]))

    def bias_lanes(rows, key_side):
        pos = _bf16_pieces(pq_ref[rows, :])
        lane_b = lane - 9
        if key_side:
            lo, hi = pick(lane % 3, pos), pick(lane_b // 3, c_pieces)
        else:
            lo, hi = pick(lane // 3, c_pieces), pick(lane_b % 3, [-p for p in pos])
        return jnp.where(lane < 9, lo, jnp.where(lane < 18, hi, 0.0)).astype(BF16)

    def stage_keys(kc, _):
        rows = pl.ds(pl.multiple_of(kc * tq, tq), tq)
        vaug_ref[rows, :DIFF_V] = v_ref[rows, :]
        vaug_ref[rows, DIFF_V:] = ones_col
        kaug_ref[rows, :2 * DIFF_QK] = k_ref[rows, :]
        kaug_ref[rows, 2 * DIFF_QK:] = bias_lanes(rows, True)
        return 0

    lax.fori_loop(0, nq, stage_keys, 0)

    def stage_queries(qt):
        rows = pl.ds(pl.multiple_of(qt * tq, tq), tq)
        q = q_ref[rows, :]
        q_side = bias_lanes(rows, False)
        qf_ref[0, :, :2 * DIFF_QK] = jnp.where(lane < DIFF_QK, q, 0).astype(BF16)
        qf_ref[1, :, :2 * DIFF_QK] = jnp.where(lane >= DIFF_QK, q, 0).astype(BF16)
        qf_ref[0, :, 2 * DIFF_QK:] = q_side
        qf_ref[1, :, 2 * DIFF_QK:] = q_side

    lq = lam_ref[...]
    lam = (jnp.exp(jnp.sum(lq[0:1] * lq[1:2], axis=-1, keepdims=True))
           - jnp.exp(jnp.sum(lq[2:3] * lq[3:4], axis=-1, keepdims=True)) + LAMBDA_INIT)

    def scores(qt, kc, diag):
        k0 = pl.multiple_of(kc * tq, tq)
        k = k_ref[pl.ds(k0, tq), :]
        pk = slope2 * pk_ref[0, pl.ds(kc, 1), :]
        out = [None] * len(chains)
        for r in range(rs):
            pq = pq_ref[pl.ds(pl.multiple_of(qt * tq + r * tr, tr), tr), :]
            bias = jnp.abs(slope2 * pq - pk)
            mask = _causal_mask(tr, tq, r * tr) if diag else None
            for c in range(2):
                q = qf_ref[c, r * tr:(r + 1) * tr, :2 * DIFF_QK]
                s = lax.dot_general(q, k, _NT, preferred_element_type=F32) - bias
                out[chains.index((c, r))] = jnp.where(mask, s, NEG) if diag else s
        return out

    def scores_diag(qt):
        stage_queries(qt)
        return scores(qt, qt, True)

    def scores_ordered(qt, kc):
        k = kaug_ref[pl.ds(pl.multiple_of(kc * tq, tq), tq), :]
        return [lax.dot_general(qf_ref[c, r * tr:(r + 1) * tr, :], k, _NT, preferred_element_type=F32)
                for c, r in chains]

    def scores_diag_sorted(qt):
        stage_queries(qt)
        return [jnp.where(_causal_mask(tr, tq, r * tr), s, NEG)
                for (c, r), s in zip(chains, scores_ordered(qt, qt))]

    batch = pl.program_id(0)

    def trips_ordered(qt):
        return jnp.where((ord_ref[batch, qt] & 1) != 0, qt, 0)

    def tile_sorted(qt):
        return (ord_ref[batch, qt] >> 1) & 1

    def values(ci, kc):
        return vaug_ref[pl.ds(pl.multiple_of(kc * tq, tq), tq), :]

    def finalize(qt):
        for r in range(rs):
            rows = pl.ds(pl.multiple_of(qt * tq + r * tr, tr), tr)
            a1 = acc_ref[chains.index((0, r))]
            a2 = acc_ref[chains.index((1, r))]
            o = a1[:, :DIFF_V] / a1[:, DIFF_V:] - lam * (a2[:, :DIFF_V] / a2[:, DIFF_V:])
            ms_o = jnp.mean(o * o, axis=-1, keepdims=True)
            o = o * lax.rsqrt(ms_o + EPS) * gs_ref[...] * (1.0 - LAMBDA_INIT)
            o_ref[rows, :] = (o * g_ref[rows, :].astype(F32)).astype(BF16)

    _flash_pipeline(len(chains), nq,
                    [(tile_sorted, scores_diag_sorted), (lambda qt: 1 - tile_sorted(qt), scores_diag)],
                    [(trips_ordered, scores_ordered),
                     (lambda qt: qt - trips_ordered(qt), lambda qt, kc: scores(qt, kc, False))],
                    values, finalize, s_ref, acc_ref, mpart_ref, macc_ref, sd_ref, mpd_ref)


def _diff_attn(rest, pos_col, pos_row, ordered, slopes, lam_par, g_subln, nb, s):
    tq = ATTN_TQ
    nq = s // tq
    rs = DIFF_ROW_SPLIT
    tr = tq // rs
    hw = DIFF_V
    return pl.pallas_call(
        functools.partial(_diff_attn_kernel, tq=tq, rs=rs),
        grid=(nb, DIFF_HEADS),
        in_specs=[pl.BlockSpec(memory_space=pltpu.SMEM),
                  pl.BlockSpec((s, hw), lambda b, hd: (b, QD_OFF // hw + hd)),
                  pl.BlockSpec((s, hw), lambda b, hd: (b, KD_OFF // hw + hd)),
                  pl.BlockSpec((s, hw), lambda b, hd: (b, VD_OFF // hw + hd)),
                  pl.BlockSpec((s, hw), lambda b, hd: (b, GD_OFF // hw + hd)),
                  pl.BlockSpec((s, 1), lambda b, hd: (b, 0)),
                  pl.BlockSpec((1, nq, tq), lambda b, hd: (b, 0, 0)),
                  pl.BlockSpec((1, 1, LANES), lambda b, hd: (hd, 0, 0)),
                  pl.BlockSpec((4, DIFF_QK), lambda b, hd: (0, 0)),
                  pl.BlockSpec((1, DIFF_V), lambda b, hd: (0, 0))],
        out_specs=pl.BlockSpec((s, hw), lambda b, hd: (b, hd)),
        out_shape=jax.ShapeDtypeStruct((nb * s, DIFF_HEADS * DIFF_V), BF16),
        scratch_shapes=[pltpu.VMEM((s, 2 * DIFF_V), BF16),
                        pltpu.VMEM((s, 4 * DIFF_QK), BF16),
                        pltpu.VMEM((2 * rs, tr, 2 * DIFF_V), F32),
                        pltpu.VMEM((2, tq, 4 * DIFF_QK), BF16),
                        pltpu.VMEM((2 * rs, tr, tq), F32),
                        pltpu.VMEM((2 * rs, tr, LANES), F32),
                        pltpu.VMEM((2 * rs, tr, LANES), F32),
                        pltpu.VMEM((2 * rs, tr, tq), F32),
                        pltpu.VMEM((2 * rs, tr, LANES), F32)],
        compiler_params=_cparams(("arbitrary", "arbitrary")),
        name="diff_attn",
    )(ordered, rest, rest, rest, rest, pos_col, pos_row, slopes, lam_par, g_subln)


def _merge_kernel(a1_ref, a2_ref, w1_ref, w2_ref, s1_ref, s2_ref, o_ref, wb_ref):
    @pl.when(pl.program_id(1) == 0)
    def _():
        wb_ref[0] = w1_ref[...].astype(BF16)
        wb_ref[1] = w2_ref[...].astype(BF16)

    y1 = jnp.dot(a1_ref[...], wb_ref[0], preferred_element_type=F32)
    y2 = jnp.dot(a2_ref[...], wb_ref[1], preferred_element_type=F32)
    o_ref[...] = (s1_ref[...].astype(F32) * y1 + s2_ref[...].astype(F32) * y2).astype(BF16)


def _merge(og_mla, og_diff, w1, w2, rest):
    m, k_dim = og_mla.shape
    n = w1.shape[1]
    tm, tn = MERGE_TM, MERGE_TN
    return pl.pallas_call(
        _merge_kernel,
        grid=(n // tn, m // tm),
        in_specs=[pl.BlockSpec((tm, k_dim), lambda j, i: (i, 0)),
                  pl.BlockSpec((tm, k_dim), lambda j, i: (i, 0)),
                  pl.BlockSpec((k_dim, tn), lambda j, i: (0, j)),
                  pl.BlockSpec((k_dim, tn), lambda j, i: (0, j)),
                  pl.BlockSpec((tm, tn), lambda j, i: (i, MGM_OFF // tn + j)),
                  pl.BlockSpec((tm, tn), lambda j, i: (i, MGD_OFF // tn + j))],
        out_specs=pl.BlockSpec((tm, tn), lambda j, i: (i, j)),
        out_shape=jax.ShapeDtypeStruct((m, n), BF16),
        scratch_shapes=[pltpu.VMEM((2, k_dim, tn), BF16)],
        compiler_params=_cparams(("arbitrary", "arbitrary")),
        name="merge",
    )(og_mla, og_diff, w1, w2, rest, rest)


def _out_kernel(a_ref, w_ref, x_ref, ada_ref, g_ref, o_ref, wb_ref):
    @pl.when(jnp.logical_and(pl.program_id(0) == 0, pl.program_id(1) == 0))
    def _():
        wb_ref[...] = w_ref[...].astype(BF16)

    y = jnp.dot(a_ref[...], wb_ref[...], preferred_element_type=F32)
    ms = jnp.mean(y * y, axis=-1, keepdims=True)
    yn = y * lax.rsqrt(ms + EPS) * g_ref[...]
    o_ref[0] = x_ref[0] + ada_ref[0, 2:3, :] * yn


def _out(merged, w_out, x, ada3, g_post):
    nb, s, d = x.shape
    tm = OUT_TM
    ns = s // tm
    return pl.pallas_call(
        _out_kernel,
        grid=(nb, ns),
        in_specs=[pl.BlockSpec((tm, d), lambda b, i: (b * ns + i, 0)),
                  pl.BlockSpec((d, d), lambda b, i: (0, 0), pipeline_mode=pl.Buffered(1)),
                  pl.BlockSpec((1, tm, d), lambda b, i: (b, i, 0)),
                  pl.BlockSpec((1, 3, d), lambda b, i: (b, 0, 0)),
                  pl.BlockSpec((1, d), lambda b, i: (0, 0))],
        out_specs=pl.BlockSpec((1, tm, d), lambda b, i: (b, i, 0)),
        out_shape=jax.ShapeDtypeStruct((nb, s, d), F32),
        scratch_shapes=[pltpu.VMEM((d, d), BF16)],
        compiler_params=_cparams(("arbitrary", "arbitrary")),
        name="out_proj",
    )(merged, w_out, x, ada3, g_post)


def kernel(x, c, positions, w_ada, b_ada, g_pre, w_in, g_kv, w_ukv, lambda_q1, lambda_k1,
           lambda_q2, lambda_k2, g_subln, w_o_mla, w_o_diff, w_out, g_post):
    nb, s, d = x.shape
    depth = w_in.shape[0]
    half = MLA_ROPE // 2
    inv = ROPE_THETA ** (-jnp.arange(half, dtype=F32) / half)
    inv_tab = jnp.tile(inv, LANES // half).reshape(1, LANES)
    slopes = 2.0 ** (-8.0 * jnp.arange(1, DIFF_HEADS + 1, dtype=F32) / DIFF_HEADS)
    slopes = jnp.broadcast_to(slopes.reshape(DIFF_HEADS, 1, 1), (DIFF_HEADS, 1, LANES))
    pos_col = positions.reshape(nb * s, 1)
    pos_colf = pos_col.astype(F32)
    pos_chunks = positions.reshape(nb, s // ATTN_TQ, ATTN_TQ)
    pos_row = pos_chunks.astype(F32)
    run_max = lax.cummax(pos_chunks.max(axis=-1), axis=1)
    prev_max = jnp.concatenate(
        [jnp.full((nb, 1), jnp.iinfo(jnp.int32).min, jnp.int32), run_max[:, :-1]], axis=1)
    tile_sorted = jnp.all(pos_chunks[..., 1:] >= pos_chunks[..., :-1], axis=-1)
    ordered = ((prev_max <= pos_chunks.min(axis=-1)).astype(jnp.int32)
               + 2 * tile_sorted.astype(jnp.int32))

    for l in range(depth):
        ada3 = _ada(c, w_ada[l], b_ada[l]).reshape(nb, 3, d)
        h, cos_tab, sin_tab = _norm(x, ada3, g_pre[l].reshape(1, d), pos_col, inv_tab)
        w_t = jnp.swapaxes(w_in[l], 0, 1)
        qm, rest = _proj(h, w_t, cos_tab, sin_tab)
        kk, vv = _kv(h, w_t, g_kv[l].reshape(1, KV_RANK), w_ukv[l], cos_tab, sin_tab)
        og_mla = _mla_attn(qm, kk, vv, rest, nb, s)
        lam_par = jnp.stack([lambda_q1[l], lambda_k1[l], lambda_q2[l], lambda_k2[l]]).astype(F32)
        og_diff = _diff_attn(rest, pos_colf, pos_row, ordered, slopes, lam_par,
                             g_subln[l].reshape(1, DIFF_V), nb, s)
        merged = _merge(og_mla, og_diff, w_o_mla[l], w_o_diff[l], rest)
        x = _out(merged, w_out[l], x, ada3, g_post[l].reshape(1, d))
    return x
```

```python
import functools
import math

import jax
import jax.numpy as jnp
from jax import lax
from jax.experimental import pallas as pl
from jax.experimental.pallas import tpu as pltpu

F32 = jnp.float32
BF16 = jnp.bfloat16

D_MODEL = 2048
MLA_HEADS = 8
MLA_NOPE = 128
MLA_ROPE = 64
MLA_V = 128
KV_RANK = 512
ROPE_THETA = 10000.0
DIFF_HEADS = 8
DIFF_QK = 64
DIFF_V = 128
EPS = 1e-6
NEG = -1e30
LAMBDA_INIT = 0.8 - 0.6 * math.exp(-0.3 * 0)

LANES = 128
SUBLANES = 8
VMEM_LIMIT = 56 * 1024 * 1024
ATTN_TQ = 1024
DIFF_ROW_SPLIT = 2
MLA_TQ = 1024
MLA_ROW_SPLIT = 2
PROJ_TM = 1024
REST_TN = 1024
QMLA_PAIRS_PER_TILE = 2
KV_TM = 1024
MERGE_TM, MERGE_TN = 1024, 1024
OUT_TM = 512
LOG2E = math.log2(math.e)

Q_MLA_W = MLA_HEADS * (MLA_NOPE + MLA_ROPE)
KVR_W = KV_RANK + MLA_ROPE
KVR_PAD = 640
REST_OFF = Q_MLA_W + KVR_W
REST_SHIFT = REST_OFF % LANES
QD_OFF, KD_OFF, VD_OFF, GM_OFF, GD_OFF, MGM_OFF, MGD_OFF = 0, 1024, 2048, 3072, 4096, 5120, 7168
REST_W = 9216


def _cparams(sem):
    return pltpu.CompilerParams(dimension_semantics=sem, vmem_limit_bytes=VMEM_LIMIT)


def _ada_kernel(cb_ref, w_ref, b_ref, o_ref):
    k_dim, tn = w_ref.shape
    nb = cb_ref.shape[0]
    nchunk = tn // LANES

    def body(i, accs):
        k0 = pl.multiple_of(i * SUBLANES, SUBLANES)
        out = []
        for b in range(nb):
            cv = cb_ref[b, pl.ds(k0, SUBLANES), :]
            for j in range(nchunk):
                wv = w_ref[pl.ds(k0, SUBLANES), j * LANES:(j + 1) * LANES]
                out.append(accs[b * nchunk + j] + wv * cv)
        return tuple(out)

    init = tuple(jnp.zeros((SUBLANES, LANES), F32) for _ in range(nb * nchunk))
    accs = lax.fori_loop(0, k_dim // SUBLANES, body, init, unroll=8)
    for b in range(nb):
        row = jnp.concatenate(
            [jnp.sum(accs[b * nchunk + j], axis=0, keepdims=True) for j in range(nchunk)], axis=1)
        o_ref[b:b + 1, :] = row + b_ref[...]


def _ada(c, w, bias):
    nb, k_dim = c.shape
    n = w.shape[1]
    tn = 512
    cb = jnp.broadcast_to(c[:, :, None], (nb, k_dim, LANES))
    return pl.pallas_call(
        _ada_kernel,
        grid=(n // tn,),
        in_specs=[pl.BlockSpec((nb, k_dim, LANES), lambda j: (0, 0, 0)),
                  pl.BlockSpec((k_dim, tn), lambda j: (0, j)),
                  pl.BlockSpec((1, tn), lambda j: (0, j))],
        out_specs=pl.BlockSpec((nb, tn), lambda j: (0, j)),
        out_shape=jax.ShapeDtypeStruct((nb, n), F32),
        compiler_params=_cparams(("arbitrary",)),
        name="ada",
    )(cb, w, bias.reshape(1, n))


def _norm_kernel(x_ref, ada_ref, g_ref, pos_ref, inv_ref, h_ref, cos_ref, sin_ref):
    x = x_ref[0]
    ms = jnp.mean(x * x, axis=-1, keepdims=True)
    y = x * lax.rsqrt(ms + EPS) * g_ref[...]
    shift = ada_ref[0, 0:1, :]
    scale = ada_ref[0, 1:2, :]
    h_ref[...] = (y * (1.0 + scale) + shift).astype(BF16)
    ang = pos_ref[...].astype(F32) * inv_ref[...]
    lane = lax.broadcasted_iota(jnp.int32, ang.shape, 1)
    sign = jnp.where((lane % MLA_ROPE) < MLA_ROPE // 2, -1.0, 1.0).astype(F32)
    cos_ref[...] = jnp.cos(ang)
    sin_ref[...] = jnp.sin(ang) * sign


def _norm(x, ada3, g_pre, pos_col, inv_tab):
    nb, s, d = x.shape
    ts = 512
    ns = s // ts
    row = lambda b, i: (b * ns + i, 0)
    return pl.pallas_call(
        _norm_kernel,
        grid=(nb, ns),
        in_specs=[pl.BlockSpec((1, ts, d), lambda b, i: (b, i, 0)),
                  pl.BlockSpec((1, 3, d), lambda b, i: (b, 0, 0)),
                  pl.BlockSpec((1, d), lambda b, i: (0, 0)),
                  pl.BlockSpec((ts, 1), row),
                  pl.BlockSpec((1, LANES), lambda b, i: (0, 0))],
        out_specs=[pl.BlockSpec((ts, d), row),
                   pl.BlockSpec((ts, LANES), row),
                   pl.BlockSpec((ts, LANES), row)],
        out_shape=[jax.ShapeDtypeStruct((nb * s, d), BF16),
                   jax.ShapeDtypeStruct((nb * s, LANES), F32),
                   jax.ShapeDtypeStruct((nb * s, LANES), F32)],
        compiler_params=_cparams(("arbitrary", "arbitrary")),
        name="prenorm",
    )(x, ada3, g_pre, pos_col, inv_tab)


def _rope_cols(r, cos, sin_signed):
    lane = lax.broadcasted_iota(jnp.int32, r.shape, 1)
    half = MLA_ROPE // 2
    partner = jnp.where((lane % MLA_ROPE) < half,
                        pltpu.roll(r, LANES - half, 1), pltpu.roll(r, half, 1))
    return r * cos + partner * sin_signed


def _qmla_kernel(a_ref, w_ref, cos_ref, sin_ref, o_ref, wb_ref, *, scale):
    hd = MLA_NOPE + MLA_ROPE
    pair_w = 2 * hd
    n_pairs = wb_ref.shape[0] // pair_w

    @pl.when(pl.program_id(1) == 0)
    def _():
        for p in range(n_pairs):
            src, dst = w_ref.at[p * pair_w:(p + 1) * pair_w], wb_ref.at[p * pair_w:(p + 1) * pair_w]
            dst[:MLA_NOPE] = src[:MLA_NOPE].astype(BF16)
            dst[MLA_NOPE:2 * MLA_NOPE] = src[hd:hd + MLA_NOPE].astype(BF16)
            dst[2 * MLA_NOPE:2 * MLA_NOPE + MLA_ROPE] = src[MLA_NOPE:hd].astype(BF16)
            dst[2 * MLA_NOPE + MLA_ROPE:] = src[hd + MLA_NOPE:].astype(BF16)

    acc = lax.dot_general(a_ref[...], wb_ref[...], _NT, preferred_element_type=F32)
    for p in range(n_pairs):
        c0 = p * pair_w
        rr = _rope_cols(acc[:, c0 + 2 * MLA_NOPE:c0 + pair_w], cos_ref[...], sin_ref[...])
        o_ref[:, c0:c0 + 2 * MLA_NOPE] = (acc[:, c0:c0 + 2 * MLA_NOPE] * scale).astype(BF16)
        o_ref[:, c0 + 2 * MLA_NOPE:c0 + pair_w] = (rr * scale).astype(BF16)


def _rest_kernel(a_ref, w_ref, o_ref, wb_ref, *, tn):
    j = pl.program_id(0)

    @pl.when(pl.program_id(1) == 0)
    def _():
        wb_ref[...] = w_ref[...].astype(BF16)

    def tile(epilogue):
        acc = lax.dot_general(a_ref[...], wb_ref[...], _NT, preferred_element_type=F32)
        o_ref[...] = epilogue(acc).astype(BF16)

    @pl.when(j < KD_OFF // tn)
    def _():
        tile(lambda acc: acc * (DIFF_QK ** -0.5 * LOG2E))

    @pl.when(jnp.logical_and(j >= KD_OFF // tn, j < GM_OFF // tn))
    def _():
        tile(lambda acc: acc)

    @pl.when(jnp.logical_and(j >= GM_OFF // tn, j < MGM_OFF // tn))
    def _():
        tile(lambda acc: acc * jax.nn.sigmoid(acc))

    @pl.when(j >= MGM_OFF // tn)
    def _():
        tile(jax.nn.sigmoid)


def _proj(h, w_t, cos_tab, sin_tab):
    m, k_dim = h.shape
    tm = PROJ_TM
    pair_w = 2 * (MLA_NOPE + MLA_ROPE)
    q_scale = (MLA_NOPE + MLA_ROPE) ** -0.5 * LOG2E

    def row_window(rows, offset_fn):
        return pl.BlockSpec((pl.Element(rows), pl.Element(k_dim)),
                            lambda *g: (pl.multiple_of(offset_fn(*g), SUBLANES), 0))

    qtn = QMLA_PAIRS_PER_TILE * pair_w
    qm = pl.pallas_call(
        functools.partial(_qmla_kernel, scale=q_scale),
        grid=(Q_MLA_W // qtn, m // tm),
        in_specs=[pl.BlockSpec((tm, k_dim), lambda j, i: (i, 0)),
                  pl.BlockSpec((qtn, k_dim), lambda j, i: (j, 0)),
                  pl.BlockSpec((tm, LANES), lambda j, i: (i, 0)),
                  pl.BlockSpec((tm, LANES), lambda j, i: (i, 0))],
        out_specs=pl.BlockSpec((tm, qtn), lambda j, i: (i, j)),
        out_shape=jax.ShapeDtypeStruct((m, Q_MLA_W), BF16),
        scratch_shapes=[pltpu.VMEM((qtn, k_dim), BF16)],
        compiler_params=_cparams(("arbitrary", "arbitrary")),
        name="proj_qmla",
    )(h, w_t, cos_tab, sin_tab)
    tn = REST_TN
    rest = pl.pallas_call(
        functools.partial(_rest_kernel, tn=tn),
        grid=(REST_W // tn, m // tm),
        in_specs=[pl.BlockSpec((tm, k_dim), lambda j, i: (i, 0)),
                  row_window(tn, lambda j, i: REST_OFF + tn * j)],
        out_specs=pl.BlockSpec((tm, tn), lambda j, i: (i, j)),
        out_shape=jax.ShapeDtypeStruct((m, REST_W), BF16),
        scratch_shapes=[pltpu.VMEM((tn, k_dim), BF16)],
        compiler_params=_cparams(("arbitrary", "arbitrary")),
        name="proj_rest",
    )(h, w_t)
    return qm, rest


def _kv_kernel(a_ref, wp_ref, g_ref, w_ref, cos_ref, sin_ref, k_ref, v_ref, wpb_ref, wb_ref):
    @pl.when(pl.program_id(0) == 0)
    def _():
        wpb_ref[:KVR_W] = wp_ref[...].astype(BF16)
        wpb_ref[KVR_W:] = jnp.zeros((KVR_PAD - KVR_W, wpb_ref.shape[1]), BF16)
        wb_ref[...] = w_ref[...].astype(BF16)

    p = lax.dot_general(a_ref[...], wpb_ref[...], _NT, preferred_element_type=F32)
    ckv = p[:, :KV_RANK]
    ms = jnp.mean(ckv * ckv, axis=-1, keepdims=True)
    n = (ckv * lax.rsqrt(ms + EPS) * g_ref[...]).astype(BF16)
    kv = jnp.dot(n, wb_ref[...], preferred_element_type=F32)
    kr_even = _rope_cols(p[:, KV_RANK:], cos_ref[...], sin_ref[...])
    kr_odd = pltpu.roll(kr_even, MLA_ROPE, 1)
    ones_col = jnp.ones(kr_even.shape, BF16)
    kw = MLA_NOPE + MLA_V
    for hd in range(MLA_HEADS):
        k_ref[:, hd * kw:hd * kw + MLA_NOPE] = kv[:, hd * kw:hd * kw + MLA_NOPE].astype(BF16)
        k_ref[:, hd * kw + MLA_NOPE:(hd + 1) * kw] = (kr_even if hd % 2 == 0 else kr_odd).astype(BF16)
        v_ref[:, hd * kw:hd * kw + MLA_V] = kv[:, hd * kw + MLA_NOPE:(hd + 1) * kw].astype(BF16)
        v_ref[:, hd * kw + MLA_V:(hd + 1) * kw] = ones_col


def _kv(h, w_t, g_kv, w_ukv, cos_tab, sin_tab):
    m, k_dim = h.shape
    tm = KV_TM
    kw = MLA_HEADS * (MLA_NOPE + MLA_V)
    return pl.pallas_call(
        _kv_kernel,
        grid=(m // tm,),
        in_specs=[pl.BlockSpec((tm, k_dim), lambda i: (i, 0)),
                  pl.BlockSpec((pl.Element(KVR_W), pl.Element(k_dim)),
                               lambda i: (pl.multiple_of(Q_MLA_W + 0 * i, SUBLANES), 0)),
                  pl.BlockSpec((1, KV_RANK), lambda i: (0, 0)),
                  pl.BlockSpec((KV_RANK, kw), lambda i: (0, 0)),
                  pl.BlockSpec((tm, LANES), lambda i: (i, 0)),
                  pl.BlockSpec((tm, LANES), lambda i: (i, 0))],
        out_specs=[pl.BlockSpec((tm, kw), lambda i: (i, 0)),
                   pl.BlockSpec((tm, kw), lambda i: (i, 0))],
        out_shape=[jax.ShapeDtypeStruct((m, kw), BF16),
                   jax.ShapeDtypeStruct((m, kw), BF16)],
        scratch_shapes=[pltpu.VMEM((KVR_PAD, k_dim), BF16), pltpu.VMEM((KV_RANK, kw), BF16)],
        compiler_params=_cparams(("arbitrary",)),
        name="kv_up",
    )(h, w_t, g_kv, w_ukv, cos_tab, sin_tab)


def _flash_pipeline(n_chains, nq, diags, loops, value_fn, finalize_fn,
                    s_ref, acc_ref, mpart_ref, macc_ref, sd_ref=None, mpd_ref=None):
    chunk = s_ref.shape[2]

    def lane_tiles(x, n):
        return jnp.concatenate([x] * n, axis=1)

    def qk_phase(scores, stage):
        s_dst, m_dst = stage
        for ci, s in enumerate(scores):
            s_dst[ci] = s
            part = s[:, :LANES]
            for j in range(1, chunk // LANES):
                part = jnp.maximum(part, s[:, j * LANES:(j + 1) * LANES])
            m_dst[ci] = part

    def pv_phase(kc, stage):
        s_src, m_src = stage
        for ci in range(n_chains):
            m_acc = macc_ref[ci]
            m_run = jnp.maximum(m_acc, jnp.max(m_src[ci], axis=-1, keepdims=True))
            macc_ref[ci] = m_run
            p = jnp.exp2(s_src[ci] - lane_tiles(m_run, chunk // LANES))
            alpha = jnp.exp2(m_acc - m_run)
            pv = jnp.dot(p.astype(BF16), value_fn(ci, kc), preferred_element_type=F32)
            acc_ref[ci] = lane_tiles(alpha, acc_ref.shape[2] // LANES) * acc_ref[ci] + pv

    def reset():
        acc_ref[...] = jnp.zeros_like(acc_ref)
        macc_ref[...] = jnp.full(macc_ref.shape, NEG, F32)

    def tile(qt, stage, next_stage):
        cur = qt
        for trips_fn, score_fn in loops:
            def step(kc, cur, score_fn=score_fn):
                pv_phase(cur, stage)
                qk_phase(score_fn(qt, kc), stage)
                return kc

            cur = lax.fori_loop(0, trips_fn(qt), step, cur)
        nxt = jnp.minimum(qt + 1, nq - 1)

        def transition(_, cur, fn):
            if next_stage is stage:
                pv_phase(cur, stage)
                finalize_fn(qt)
                reset()
                qk_phase(fn(nxt), stage)
            else:
                qk_phase(fn(nxt), next_stage)
                pv_phase(cur, stage)
                finalize_fn(qt)
                reset()
            return cur

        for select_fn, fn in diags:
            if select_fn is None:
                transition(0, cur, fn)
            else:
                lax.fori_loop(0, select_fn(nxt), functools.partial(transition, fn=fn), cur)

    stage_a = (s_ref, mpart_ref)
    reset()
    qk_phase(diags[-1][1](0), stage_a)
    if sd_ref is None:
        lax.fori_loop(0, nq, lambda qt, c: tile(qt, stage_a, stage_a) or c, 0)
    else:
        stage_b = (sd_ref, mpd_ref)

        def tile_pair(j, c):
            tile(2 * j, stage_a, stage_b)
            tile(2 * j + 1, stage_b, stage_a)
            return c

        lax.fori_loop(0, nq // 2, tile_pair, 0)


def _causal_mask(rows, cols, row0):
    row = lax.broadcasted_iota(jnp.int32, (rows, cols), 0) + row0
    col = lax.broadcasted_iota(jnp.int32, (rows, cols), 1)
    return col <= row


_NT = (((1,), (1,)), ((), ()))


def _mla_attn_kernel(q_ref, k_ref, v_ref, g_ref, o_ref, acc_ref, qs_ref, s_ref, mpart_ref, macc_ref,
                     *, tq, rs):
    kw = 2 * MLA_NOPE
    tr = tq // rs
    nq = q_ref.shape[0] // tq
    for hp in range(2):
        qs_ref[hp, :, :MLA_NOPE] = q_ref[:, hp * MLA_NOPE:(hp + 1) * MLA_NOPE]
        qs_ref[hp, :, MLA_NOPE:] = q_ref[:, 2 * MLA_NOPE:]
    chains = [(hp, r) for hp in range(2) for r in range(rs)]

    def scores(qt, kc, diag):
        k0 = pl.multiple_of(kc * tq, tq)
        out = []
        for hp, r in chains:
            q = qs_ref[hp, pl.ds(pl.multiple_of(qt * tq + r * tr, tr), tr), :]
            k = k_ref[pl.ds(k0, tq), hp * kw:(hp + 1) * kw]
            s = lax.dot_general(q, k, _NT, preferred_element_type=F32)
            if diag:
                s = jnp.where(_causal_mask(tr, tq, r * tr), s, NEG)
            out.append(s)
        return out

    def values(ci, kc):
        hp = chains[ci][0]
        return v_ref[pl.ds(pl.multiple_of(kc * tq, tq), tq), hp * kw:(hp + 1) * kw]

    def finalize(qt):
        for ci, (hp, r) in enumerate(chains):
            rows = pl.ds(pl.multiple_of(qt * tq + r * tr, tr), tr)
            acc = acc_ref[ci]
            o = acc[:, :MLA_V] / acc[:, MLA_V:]
            gate = g_ref[rows, hp * MLA_V:(hp + 1) * MLA_V].astype(F32)
            o_ref[rows, hp * MLA_V:(hp + 1) * MLA_V] = (o * gate).astype(BF16)

    _flash_pipeline(len(chains), nq, [(None, lambda qt: scores(qt, qt, True))],
                    [(lambda qt: qt, lambda qt, kc: scores(qt, kc, False))],
                    values, finalize, s_ref, acc_ref, mpart_ref, macc_ref)


def _mla_attn(qm, kk, vv, rest, nb, s):
    tq = MLA_TQ
    pair_w = 2 * (MLA_NOPE + MLA_ROPE)
    kw = 4 * MLA_NOPE
    gate_blk = GM_OFF // (2 * MLA_V)
    rs = MLA_ROW_SPLIT
    return pl.pallas_call(
        functools.partial(_mla_attn_kernel, tq=tq, rs=rs),
        grid=(nb, MLA_HEADS // 2),
        in_specs=[pl.BlockSpec((s, pair_w), lambda b, hh: (b, hh)),
                  pl.BlockSpec((s, kw), lambda b, hh: (b, hh)),
                  pl.BlockSpec((s, kw), lambda b, hh: (b, hh)),
                  pl.BlockSpec((s, 2 * MLA_V), lambda b, hh: (b, gate_blk + hh))],
        out_specs=pl.BlockSpec((s, 2 * MLA_V), lambda b, hh: (b, hh)),
        out_shape=jax.ShapeDtypeStruct((nb * s, MLA_HEADS * MLA_V), BF16),
        scratch_shapes=[pltpu.VMEM((2 * rs, tq // rs, 2 * MLA_V), F32),
                        pltpu.VMEM((2, s, 2 * MLA_NOPE), BF16),
                        pltpu.VMEM((2 * rs, tq // rs, tq), F32),
                        pltpu.VMEM((2 * rs, tq // rs, LANES), F32),
                        pltpu.VMEM((2 * rs, tq // rs, LANES), F32)],
        compiler_params=_cparams(("arbitrary", "arbitrary")),
        name="mla_attn",
    )(qm, kk, vv, rest)


def _bf16_pieces(x):
    p1 = x.astype(BF16)
    r = x - p1.astype(F32)
    p2 = r.astype(BF16)
    return p1.astype(F32), p2.astype(F32), r - p2.astype(F32)


def _diff_attn_kernel(ord_ref, q_ref, k_ref, v_ref, g_ref, pq_ref, pk_ref, sl_ref, lam_ref, gs_ref,
                      o_ref, vaug_ref, kaug_ref, acc_ref, qf_ref, s_ref, mpart_ref, macc_ref,
                      sd_ref, mpd_ref, *, tq, rs):
    seq = q_ref.shape[0]
    nq = seq // tq
    tr = tq // rs
    chains = [(c, r) for c in range(2) for r in range(rs)]
    lane = lax.broadcasted_iota(jnp.int32, (tq, LANES), 1)
    ones_col = jnp.ones((tq, LANES), BF16)
    slope2 = sl_ref[0, :, 0:1] * LOG2E
    c_pieces = _bf16_pieces(slope2)

    def pick(sel, x3):
        return jnp.where(sel == 0, x3[0], jnp.where(sel == 1, x3[1], x3[2]))

    def bias_lanes(rows, key_side):
        pos = _bf16_pieces(pq_ref[rows, :])
        lane_b = lane - 9
        if key_side:
            lo, hi = pick(lane % 3, pos), pick(lane_b // 3, c_pieces)
        else:
            lo, hi = pick(lane // 3, c_pieces), pick(lane_b % 3, [-p for p in pos])
        return jnp.where(lane < 9, lo, jnp.where(lane < 18, hi, 0.0)).astype(BF16)

    def stage_keys(kc, _):
        rows = pl.ds(pl.multiple_of(kc * tq, tq), tq)
        vaug_ref[rows, :DIFF_V] = v_ref[rows, :]
        vaug_ref[rows, DIFF_V:] = ones_col
        kaug_ref[rows, :2 * DIFF_QK] = k_ref[rows, :]
        kaug_ref[rows, 2 * DIFF_QK:] = bias_lanes(rows, True)
        return 0

    lax.fori_loop(0, nq, stage_keys, 0)

    def stage_queries(qt):
        rows = pl.ds(pl.multiple_of(qt * tq, tq), tq)
        q = q_ref[rows, :]
        q_side = bias_lanes(rows, False)
        qf_ref[0, :, :2 * DIFF_QK] = jnp.where(lane < DIFF_QK, q, 0).astype(BF16)
        qf_ref[1, :, :2 * DIFF_QK] = jnp.where(lane >= DIFF_QK, q, 0).astype(BF16)
        qf_ref[0, :, 2 * DIFF_QK:] = q_side
        qf_ref[1, :, 2 * DIFF_QK:] = q_side

    lq = lam_ref[...]
    lam = (jnp.exp(jnp.sum(lq[0:1] * lq[1:2], axis=-1, keepdims=True))
           - jnp.exp(jnp.sum(lq[2:3] * lq[3:4], axis=-1, keepdims=True)) + LAMBDA_INIT)

    def scores(qt, kc, diag):
        k0 = pl.multiple_of(kc * tq, tq)
        k = k_ref[pl.ds(k0, tq), :]
        pk = slope2 * pk_ref[0, pl.ds(kc, 1), :]
        out = [None] * len(chains)
        for r in range(rs):
            pq = pq_ref[pl.ds(pl.multiple_of(qt * tq + r * tr, tr), tr), :]
            bias = jnp.abs(slope2 * pq - pk)
            mask = _causal_mask(tr, tq, r * tr) if diag else None
            for c in range(2):
                q = qf_ref[c, r * tr:(r + 1) * tr, :2 * DIFF_QK]
                s = lax.dot_general(q, k, _NT, preferred_element_type=F32) - bias
                out[chains.index((c, r))] = jnp.where(mask, s, NEG) if diag else s
        return out

    def scores_diag(qt):
        stage_queries(qt)
        return scores(qt, qt, True)

    def scores_ordered(qt, kc):
        k = kaug_ref[pl.ds(pl.multiple_of(kc * tq, tq), tq), :]
        return [lax.dot_general(qf_ref[c, r * tr:(r + 1) * tr, :], k, _NT, preferred_element_type=F32)
                for c, r in chains]

    def scores_diag_sorted(qt):
        stage_queries(qt)
        return [jnp.where(_causal_mask(tr, tq, r * tr), s, NEG)
                for (c, r), s in zip(chains, scores_ordered(qt, qt))]

    batch = pl.program_id(0)

    def trips_ordered(qt):
        return jnp.where((ord_ref[batch, qt] & 1) != 0, qt, 0)

    def tile_sorted(qt):
        return (ord_ref[batch, qt] >> 1) & 1

    def values(ci, kc):
        return vaug_ref[pl.ds(pl.multiple_of(kc * tq, tq), tq), :]

    def finalize(qt):
        for r in range(rs):
            rows = pl.ds(pl.multiple_of(qt * tq + r * tr, tr), tr)
            a1 = acc_ref[chains.index((0, r))]
            a2 = acc_ref[chains.index((1, r))]
            o = a1[:, :DIFF_V] / a1[:, DIFF_V:] - lam * (a2[:, :DIFF_V] / a2[:, DIFF_V:])
            ms_o = jnp.mean(o * o, axis=-1, keepdims=True)
            o = o * lax.rsqrt(ms_o + EPS) * gs_ref[...] * (1.0 - LAMBDA_INIT)
            o_ref[rows, :] = (o * g_ref[rows, :].astype(F32)).astype(BF16)

    _flash_pipeline(len(chains), nq,
                    [(tile_sorted, scores_diag_sorted), (lambda qt: 1 - tile_sorted(qt), scores_diag)],
                    [(trips_ordered, scores_ordered),
                     (lambda qt: qt - trips_ordered(qt), lambda qt, kc: scores(qt, kc, False))],
                    values, finalize, s_ref, acc_ref, mpart_ref, macc_ref, sd_ref, mpd_ref)


def _diff_attn(rest, pos_col, pos_row, ordered, slopes, lam_par, g_subln, nb, s):
    tq = ATTN_TQ
    nq = s // tq
    rs = DIFF_ROW_SPLIT
    tr = tq // rs
    hw = DIFF_V
    return pl.pallas_call(
        functools.partial(_diff_attn_kernel, tq=tq, rs=rs),
        grid=(nb, DIFF_HEADS),
        in_specs=[pl.BlockSpec(memory_space=pltpu.SMEM),
                  pl.BlockSpec((s, hw), lambda b, hd: (b, QD_OFF // hw + hd)),
                  pl.BlockSpec((s, hw), lambda b, hd: (b, KD_OFF // hw + hd)),
                  pl.BlockSpec((s, hw), lambda b, hd: (b, VD_OFF // hw + hd)),
                  pl.BlockSpec((s, hw), lambda b, hd: (b, GD_OFF // hw + hd)),
                  pl.BlockSpec((s, 1), lambda b, hd: (b, 0)),
                  pl.BlockSpec((1, nq, tq), lambda b, hd: (b, 0, 0)),
                  pl.BlockSpec((1, 1, LANES), lambda b, hd: (hd, 0, 0)),
                  pl.BlockSpec((4, DIFF_QK), lambda b, hd: (0, 0)),
                  pl.BlockSpec((1, DIFF_V), lambda b, hd: (0, 0))],
        out_specs=pl.BlockSpec((s, hw), lambda b, hd: (b, hd)),
        out_shape=jax.ShapeDtypeStruct((nb * s, DIFF_HEADS * DIFF_V), BF16),
        scratch_shapes=[pltpu.VMEM((s, 2 * DIFF_V), BF16),
                        pltpu.VMEM((s, 4 * DIFF_QK), BF16),
                        pltpu.VMEM((2 * rs, tr, 2 * DIFF_V), F32),
                        pltpu.VMEM((2, tq, 4 * DIFF_QK), BF16),
                        pltpu.VMEM((2 * rs, tr, tq), F32),
                        pltpu.VMEM((2 * rs, tr, LANES), F32),
                        pltpu.VMEM((2 * rs, tr, LANES), F32),
                        pltpu.VMEM((2 * rs, tr, tq), F32),
                        pltpu.VMEM((2 * rs, tr, LANES), F32)],
        compiler_params=_cparams(("arbitrary", "arbitrary")),
        name="diff_attn",
    )(ordered, rest, rest, rest, rest, pos_col, pos_row, slopes, lam_par, g_subln)


def _merge_kernel(a1_ref, a2_ref, w1_ref, w2_ref, s1_ref, s2_ref, o_ref, wb_ref):
    @pl.when(pl.program_id(1) == 0)
    def _():
        wb_ref[0] = w1_ref[...].astype(BF16)
        wb_ref[1] = w2_ref[...].astype(BF16)

    y1 = jnp.dot(a1_ref[...], wb_ref[0], preferred_element_type=F32)
    y2 = jnp.dot(a2_ref[...], wb_ref[1], preferred_element_type=F32)
    o_ref[...] = (s1_ref[...].astype(F32) * y1 + s2_ref[...].astype(F32) * y2).astype(BF16)


def _merge(og_mla, og_diff, w1, w2, rest):
    m, k_dim = og_mla.shape
    n = w1.shape[1]
    tm, tn = MERGE_TM, MERGE_TN
    return pl.pallas_call(
        _merge_kernel,
        grid=(n // tn, m // tm),
        in_specs=[pl.BlockSpec((tm, k_dim), lambda j, i: (i, 0)),
                  pl.BlockSpec((tm, k_dim), lambda j, i: (i, 0)),
                  pl.BlockSpec((k_dim, tn), lambda j, i: (0, j)),
                  pl.BlockSpec((k_dim, tn), lambda j, i: (0, j)),
                  pl.BlockSpec((tm, tn), lambda j, i: (i, MGM_OFF // tn + j)),
                  pl.BlockSpec((tm, tn), lambda j, i: (i, MGD_OFF // tn + j))],
        out_specs=pl.BlockSpec((tm, tn), lambda j, i: (i, j)),
        out_shape=jax.ShapeDtypeStruct((m, n), BF16),
        scratch_shapes=[pltpu.VMEM((2, k_dim, tn), BF16)],
        compiler_params=_cparams(("arbitrary", "arbitrary")),
        name="merge",
    )(og_mla, og_diff, w1, w2, rest, rest)


def _out_kernel(a_ref, w_ref, x_ref, ada_ref, g_ref, o_ref, wb_ref):
    @pl.when(jnp.logical_and(pl.program_id(0) == 0, pl.program_id(1) == 0))
    def _():
        wb_ref[...] = w_ref[...].astype(BF16)

    y = jnp.dot(a_ref[...], wb_ref[...], preferred_element_type=F32)
    ms = jnp.mean(y * y, axis=-1, keepdims=True)
    yn = y * lax.rsqrt(ms + EPS) * g_ref[...]
    o_ref[0] = x_ref[0] + ada_ref[0, 2:3, :] * yn


def _out(merged, w_out, x, ada3, g_post):
    nb, s, d = x.shape
    tm = OUT_TM
    ns = s // tm
    return pl.pallas_call(
        _out_kernel,
        grid=(nb, ns),
        in_specs=[pl.BlockSpec((tm, d), lambda b, i: (b * ns + i, 0)),
                  pl.BlockSpec((d, d), lambda b, i: (0, 0), pipeline_mode=pl.Buffered(1)),
                  pl.BlockSpec((1, tm, d), lambda b, i: (b, i, 0)),
                  pl.BlockSpec((1, 3, d), lambda b, i: (b, 0, 0)),
                  pl.BlockSpec((1, d), lambda b, i: (0, 0))],
        out_specs=pl.BlockSpec((1, tm, d), lambda b, i: (b, i, 0)),
        out_shape=jax.ShapeDtypeStruct((nb, s, d), F32),
        scratch_shapes=[pltpu.VMEM((d, d), BF16)],
        compiler_params=_cparams(("arbitrary", "arbitrary")),
        name="out_proj",
    )(merged, w_out, x, ada3, g_post)


def kernel(x, c, positions, w_ada, b_ada, g_pre, w_in, g_kv, w_ukv, lambda_q1, lambda_k1,
           lambda_q2, lambda_k2, g_subln, w_o_mla, w_o_diff, w_out, g_post):
    nb, s, d = x.shape
    depth = w_in.shape[0]
    half = MLA_ROPE // 2
    inv = ROPE_THETA ** (-jnp.arange(half, dtype=F32) / half)
    inv_tab = jnp.tile(inv, LANES // half).reshape(1, LANES)
    slopes = 2.0 ** (-8.0 * jnp.arange(1, DIFF_HEADS + 1, dtype=F32) / DIFF_HEADS)
    slopes = jnp.broadcast_to(slopes.reshape(DIFF_HEADS, 1, 1), (DIFF_HEADS, 1, LANES))
    pos_col = positions.reshape(nb * s, 1)
    pos_colf = pos_col.astype(F32)
    pos_chunks = positions.reshape(nb, s // ATTN_TQ, ATTN_TQ)
    pos_row = pos_chunks.astype(F32)
    run_max = lax.cummax(pos_chunks.max(axis=-1), axis=1)
    prev_max = jnp.concatenate(
        [jnp.full((nb, 1), jnp.iinfo(jnp.int32).min, jnp.int32), run_max[:, :-1]], axis=1)
    tile_sorted = jnp.all(pos_chunks[..., 1:] >= pos_chunks[..., :-1], axis=-1)
    ordered = ((prev_max <= pos_chunks.min(axis=-1)).astype(jnp.int32)
               + 2 * tile_sorted.astype(jnp.int32))

    for l in range(depth):
        ada3 = _ada(c, w_ada[l], b_ada[l]).reshape(nb, 3, d)
        h, cos_tab, sin_tab = _norm(x, ada3, g_pre[l].reshape(1, d), pos_col, inv_tab)
        w_t = jnp.swapaxes(w_in[l], 0, 1)
        qm, rest = _proj(h, w_t, cos_tab, sin_tab)
        kk, vv = _kv(h, w_t, g_kv[l].reshape(1, KV_RANK), w_ukv[l], cos_tab, sin_tab)
        og_mla = _mla_attn(qm, kk, vv, rest, nb, s)
        lam_par = jnp.stack([lambda_q1[l], lambda_k1[l], lambda_q2[l], lambda_k2[l]]).astype(F32)
        og_diff = _diff_attn(rest, pos_colf, pos_row, ordered, slopes, lam_par,
                             g_subln[l].reshape(1, DIFF_V), nb, s)
        merged = _merge(og_mla, og_diff, w_o_mla[l], w_o_diff[l], rest)
        x = _out(merged, w_out[l], x, ada3, g_post[l].reshape(1, d))
    return x
```

```python
import functools
import math

import jax
import jax.numpy as jnp
from jax import lax
from jax.experimental import pallas as pl
from jax.experimental.pallas import tpu as pltpu

F32 = jnp.float32
BF16 = jnp.bfloat16

D_MODEL = 2048
MLA_HEADS = 8
MLA_NOPE = 128
MLA_ROPE = 64
MLA_V = 128
KV_RANK = 512
ROPE_THETA = 10000.0
DIFF_HEADS = 8
DIFF_QK = 64
DIFF_V = 128
EPS = 1e-6
NEG = -1e30
LAMBDA_INIT = 0.8 - 0.6 * math.exp(-0.3 * 0)

LANES = 128
SUBLANES = 8
VMEM_LIMIT = 56 * 1024 * 1024
ATTN_TQ = 1024
DIFF_ROW_SPLIT = 2
MLA_TQ = 1024
MLA_ROW_SPLIT = 2
PROJ_TM = 1024
REST_TN = 1024
QMLA_PAIRS_PER_TILE = 2
KV_TM = 1024
MERGE_TM, MERGE_TN = 1024, 1024
OUT_TM = 512
EPILOGUE_ROW_SPLIT = 2
LOG2E = math.log2(math.e)

Q_MLA_W = MLA_HEADS * (MLA_NOPE + MLA_ROPE)
KVR_W = KV_RANK + MLA_ROPE
KVR_PAD = 640
REST_OFF = Q_MLA_W + KVR_W
REST_SHIFT = REST_OFF % LANES
QD_OFF, KD_OFF, VD_OFF, GM_OFF, GD_OFF, MGM_OFF, MGD_OFF = 0, 1024, 2048, 3072, 4096, 5120, 7168
REST_W = 9216


def _cparams(sem):
    return pltpu.CompilerParams(dimension_semantics=sem, vmem_limit_bytes=VMEM_LIMIT)


def _ada_kernel(cb_ref, w_ref, b_ref, o_ref):
    k_dim, tn = w_ref.shape
    nb = cb_ref.shape[0]
    nchunk = tn // LANES

    def body(i, accs):
        k0 = pl.multiple_of(i * SUBLANES, SUBLANES)
        out = []
        for b in range(nb):
            cv = cb_ref[b, pl.ds(k0, SUBLANES), :]
            for j in range(nchunk):
                wv = w_ref[pl.ds(k0, SUBLANES), j * LANES:(j + 1) * LANES]
                out.append(accs[b * nchunk + j] + wv * cv)
        return tuple(out)

    init = tuple(jnp.zeros((SUBLANES, LANES), F32) for _ in range(nb * nchunk))
    accs = lax.fori_loop(0, k_dim // SUBLANES, body, init, unroll=8)
    for b in range(nb):
        row = jnp.concatenate(
            [jnp.sum(accs[b * nchunk + j], axis=0, keepdims=True) for j in range(nchunk)], axis=1)
        o_ref[b:b + 1, :] = row + b_ref[...]


def _ada(c, w, bias):
    nb, k_dim = c.shape
    n = w.shape[1]
    tn = 512
    cb = jnp.broadcast_to(c[:, :, None], (nb, k_dim, LANES))
    return pl.pallas_call(
        _ada_kernel,
        grid=(n // tn,),
        in_specs=[pl.BlockSpec((nb, k_dim, LANES), lambda j: (0, 0, 0)),
                  pl.BlockSpec((k_dim, tn), lambda j: (0, j)),
                  pl.BlockSpec((1, tn), lambda j: (0, j))],
        out_specs=pl.BlockSpec((nb, tn), lambda j: (0, j)),
        out_shape=jax.ShapeDtypeStruct((nb, n), F32),
        compiler_params=_cparams(("arbitrary",)),
        name="ada",
    )(cb, w, bias.reshape(1, n))


def _norm_kernel(x_ref, ada_ref, g_ref, pos_ref, inv_ref, h_ref, cos_ref, sin_ref):
    x = x_ref[0]
    ms = jnp.mean(x * x, axis=-1, keepdims=True)
    y = x * lax.rsqrt(ms + EPS) * g_ref[...]
    shift = ada_ref[0, 0:1, :]
    scale = ada_ref[0, 1:2, :]
    h_ref[...] = (y * (1.0 + scale) + shift).astype(BF16)
    ang = pos_ref[...].astype(F32) * inv_ref[...]
    lane = lax.broadcasted_iota(jnp.int32, ang.shape, 1)
    sign = jnp.where((lane % MLA_ROPE) < MLA_ROPE // 2, -1.0, 1.0).astype(F32)
    cos_ref[...] = jnp.cos(ang)
    sin_ref[...] = jnp.sin(ang) * sign


def _norm(x, ada3, g_pre, pos_col, inv_tab):
    nb, s, d = x.shape
    ts = 512
    ns = s // ts
    row = lambda b, i: (b * ns + i, 0)
    return pl.pallas_call(
        _norm_kernel,
        grid=(nb, ns),
        in_specs=[pl.BlockSpec((1, ts, d), lambda b, i: (b, i, 0)),
                  pl.BlockSpec((1, 3, d), lambda b, i: (b, 0, 0)),
                  pl.BlockSpec((1, d), lambda b, i: (0, 0)),
                  pl.BlockSpec((ts, 1), row),
                  pl.BlockSpec((1, LANES), lambda b, i: (0, 0))],
        out_specs=[pl.BlockSpec((ts, d), row),
                   pl.BlockSpec((ts, LANES), row),
                   pl.BlockSpec((ts, LANES), row)],
        out_shape=[jax.ShapeDtypeStruct((nb * s, d), BF16),
                   jax.ShapeDtypeStruct((nb * s, LANES), F32),
                   jax.ShapeDtypeStruct((nb * s, LANES), F32)],
        compiler_params=_cparams(("arbitrary", "arbitrary")),
        name="prenorm",
    )(x, ada3, g_pre, pos_col, inv_tab)


def _rope_cols(r, cos, sin_signed):
    lane = lax.broadcasted_iota(jnp.int32, r.shape, 1)
    half = MLA_ROPE // 2
    partner = jnp.where((lane % MLA_ROPE) < half,
                        pltpu.roll(r, LANES - half, 1), pltpu.roll(r, half, 1))
    return r * cos + partner * sin_signed


def _qmla_kernel(a_ref, w_ref, cos_ref, sin_ref, o_ref, wb_ref, *, scale):
    hd = MLA_NOPE + MLA_ROPE
    pair_w = 2 * hd
    n_pairs = wb_ref.shape[0] // pair_w

    @pl.when(pl.program_id(1) == 0)
    def _():
        for p in range(n_pairs):
            src, dst = w_ref.at[p * pair_w:(p + 1) * pair_w], wb_ref.at[p * pair_w:(p + 1) * pair_w]
            dst[:MLA_NOPE] = src[:MLA_NOPE].astype(BF16)
            dst[MLA_NOPE:2 * MLA_NOPE] = src[hd:hd + MLA_NOPE].astype(BF16)
            dst[2 * MLA_NOPE:2 * MLA_NOPE + MLA_ROPE] = src[MLA_NOPE:hd].astype(BF16)
            dst[2 * MLA_NOPE + MLA_ROPE:] = src[hd + MLA_NOPE:].astype(BF16)

    acc = lax.dot_general(a_ref[...], wb_ref[...], _NT, preferred_element_type=F32)
    for p in range(n_pairs):
        c0 = p * pair_w
        rr = _rope_cols(acc[:, c0 + 2 * MLA_NOPE:c0 + pair_w], cos_ref[...], sin_ref[...])
        o_ref[:, c0:c0 + 2 * MLA_NOPE] = (acc[:, c0:c0 + 2 * MLA_NOPE] * scale).astype(BF16)
        o_ref[:, c0 + 2 * MLA_NOPE:c0 + pair_w] = (rr * scale).astype(BF16)


def _rest_kernel(a_ref, w_ref, o_ref, wb_ref, *, tn):
    j = pl.program_id(0)

    @pl.when(pl.program_id(1) == 0)
    def _():
        wb_ref[...] = w_ref[...].astype(BF16)

    def tile(epilogue):
        acc = lax.dot_general(a_ref[...], wb_ref[...], _NT, preferred_element_type=F32)
        o_ref[...] = epilogue(acc).astype(BF16)

    @pl.when(j < KD_OFF // tn)
    def _():
        tile(lambda acc: acc * (DIFF_QK ** -0.5 * LOG2E))

    @pl.when(jnp.logical_and(j >= KD_OFF // tn, j < GM_OFF // tn))
    def _():
        tile(lambda acc: acc)

    @pl.when(jnp.logical_and(j >= GM_OFF // tn, j < MGM_OFF // tn))
    def _():
        tile(lambda acc: acc * jax.nn.sigmoid(acc))

    @pl.when(j >= MGM_OFF // tn)
    def _():
        tile(jax.nn.sigmoid)


def _proj(h, w_t, cos_tab, sin_tab):
    m, k_dim = h.shape
    tm = PROJ_TM
    pair_w = 2 * (MLA_NOPE + MLA_ROPE)
    q_scale = (MLA_NOPE + MLA_ROPE) ** -0.5 * LOG2E

    def row_window(rows, offset_fn):
        return pl.BlockSpec((pl.Element(rows), pl.Element(k_dim)),
                            lambda *g: (pl.multiple_of(offset_fn(*g), SUBLANES), 0))

    qtn = QMLA_PAIRS_PER_TILE * pair_w
    qm = pl.pallas_call(
        functools.partial(_qmla_kernel, scale=q_scale),
        grid=(Q_MLA_W // qtn, m // tm),
        in_specs=[pl.BlockSpec((tm, k_dim), lambda j, i: (i, 0)),
                  pl.BlockSpec((qtn, k_dim), lambda j, i: (j, 0)),
                  pl.BlockSpec((tm, LANES), lambda j, i: (i, 0)),
                  pl.BlockSpec((tm, LANES), lambda j, i: (i, 0))],
        out_specs=pl.BlockSpec((tm, qtn), lambda j, i: (i, j)),
        out_shape=jax.ShapeDtypeStruct((m, Q_MLA_W), BF16),
        scratch_shapes=[pltpu.VMEM((qtn, k_dim), BF16)],
        compiler_params=_cparams(("arbitrary", "arbitrary")),
        name="proj_qmla",
    )(h, w_t, cos_tab, sin_tab)
    tn = REST_TN
    rest = pl.pallas_call(
        functools.partial(_rest_kernel, tn=tn),
        grid=(REST_W // tn, m // tm),
        in_specs=[pl.BlockSpec((tm, k_dim), lambda j, i: (i, 0)),
                  row_window(tn, lambda j, i: REST_OFF + tn * j)],
        out_specs=pl.BlockSpec((tm, tn), lambda j, i: (i, j)),
        out_shape=jax.ShapeDtypeStruct((m, REST_W), BF16),
        scratch_shapes=[pltpu.VMEM((tn, k_dim), BF16)],
        compiler_params=_cparams(("arbitrary", "arbitrary")),
        name="proj_rest",
    )(h, w_t)
    return qm, rest


def _kv_kernel(a_ref, wp_ref, g_ref, w_ref, cos_ref, sin_ref, k_ref, v_ref, wpb_ref, wb_ref):
    @pl.when(pl.program_id(0) == 0)
    def _():
        wpb_ref[:KVR_W] = wp_ref[...].astype(BF16)
        wpb_ref[KVR_W:] = jnp.zeros((KVR_PAD - KVR_W, wpb_ref.shape[1]), BF16)
        wb_ref[...] = w_ref[...].astype(BF16)

    kw = MLA_NOPE + MLA_V
    rb = a_ref.shape[0] // EPILOGUE_ROW_SPLIT
    for t in range(EPILOGUE_ROW_SPLIT):
        rows = slice(t * rb, (t + 1) * rb)
        p = lax.dot_general(a_ref[rows, :], wpb_ref[...], _NT, preferred_element_type=F32)
        ckv = p[:, :KV_RANK]
        ms = jnp.mean(ckv * ckv, axis=-1, keepdims=True)
        n = (ckv * lax.rsqrt(ms + EPS) * g_ref[...]).astype(BF16)
        kv = jnp.dot(n, wb_ref[...], preferred_element_type=F32)
        kr_even = _rope_cols(p[:, KV_RANK:], cos_ref[rows, :], sin_ref[rows, :])
        kr_odd = pltpu.roll(kr_even, MLA_ROPE, 1)
        ones_col = jnp.ones(kr_even.shape, BF16)
        for hd in range(MLA_HEADS):
            k_ref[rows, hd * kw:hd * kw + MLA_NOPE] = kv[:, hd * kw:hd * kw + MLA_NOPE].astype(BF16)
            k_ref[rows, hd * kw + MLA_NOPE:(hd + 1) * kw] = (
                kr_even if hd % 2 == 0 else kr_odd).astype(BF16)
            v_ref[rows, hd * kw:hd * kw + MLA_V] = kv[:, hd * kw + MLA_NOPE:(hd + 1) * kw].astype(BF16)
            v_ref[rows, hd * kw + MLA_V:(hd + 1) * kw] = ones_col


def _kv(h, w_t, g_kv, w_ukv, cos_tab, sin_tab):
    m, k_dim = h.shape
    tm = KV_TM
    kw = MLA_HEADS * (MLA_NOPE + MLA_V)
    return pl.pallas_call(
        _kv_kernel,
        grid=(m // tm,),
        in_specs=[pl.BlockSpec((tm, k_dim), lambda i: (i, 0)),
                  pl.BlockSpec((pl.Element(KVR_W), pl.Element(k_dim)),
                               lambda i: (pl.multiple_of(Q_MLA_W + 0 * i, SUBLANES), 0)),
                  pl.BlockSpec((1, KV_RANK), lambda i: (0, 0)),
                  pl.BlockSpec((KV_RANK, kw), lambda i: (0, 0)),
                  pl.BlockSpec((tm, LANES), lambda i: (i, 0)),
                  pl.BlockSpec((tm, LANES), lambda i: (i, 0))],
        out_specs=[pl.BlockSpec((tm, kw), lambda i: (i, 0)),
                   pl.BlockSpec((tm, kw), lambda i: (i, 0))],
        out_shape=[jax.ShapeDtypeStruct((m, kw), BF16),
                   jax.ShapeDtypeStruct((m, kw), BF16)],
        scratch_shapes=[pltpu.VMEM((KVR_PAD, k_dim), BF16), pltpu.VMEM((KV_RANK, kw), BF16)],
        compiler_params=_cparams(("arbitrary",)),
        name="kv_up",
    )(h, w_t, g_kv, w_ukv, cos_tab, sin_tab)


def _flash_pipeline(n_chains, nq, diags, loops, value_fn, finalize_fn,
                    s_ref, acc_ref, mpart_ref, macc_ref, sd_ref=None, mpd_ref=None):
    chunk = s_ref.shape[2]

    def lane_tiles(x, n):
        return jnp.concatenate([x] * n, axis=1)

    def qk_phase(scores, stage):
        s_dst, m_dst = stage
        for ci, s in enumerate(scores):
            s_dst[ci] = s
            part = s[:, :LANES]
            for j in range(1, chunk // LANES):
                part = jnp.maximum(part, s[:, j * LANES:(j + 1) * LANES])
            m_dst[ci] = part

    def pv_phase(kc, stage):
        s_src, m_src = stage
        for ci in range(n_chains):
            m_acc = macc_ref[ci]
            m_run = jnp.maximum(m_acc, jnp.max(m_src[ci], axis=-1, keepdims=True))
            macc_ref[ci] = m_run
            p = jnp.exp2(s_src[ci] - lane_tiles(m_run, chunk // LANES))
            alpha = jnp.exp2(m_acc - m_run)
            pv = jnp.dot(p.astype(BF16), value_fn(ci, kc), preferred_element_type=F32)
            acc_ref[ci] = lane_tiles(alpha, acc_ref.shape[2] // LANES) * acc_ref[ci] + pv

    def reset():
        acc_ref[...] = jnp.zeros_like(acc_ref)
        macc_ref[...] = jnp.full(macc_ref.shape, NEG, F32)

    def tile(qt, stage, next_stage):
        cur = qt
        for trips_fn, score_fn in loops:
            def step(kc, cur, score_fn=score_fn):
                pv_phase(cur, stage)
                qk_phase(score_fn(qt, kc), stage)
                return kc

            cur = lax.fori_loop(0, trips_fn(qt), step, cur)
        nxt = jnp.minimum(qt + 1, nq - 1)

        def transition(_, cur, fn):
            if next_stage is stage:
                pv_phase(cur, stage)
                finalize_fn(qt)
                reset()
                qk_phase(fn(nxt), stage)
            else:
                qk_phase(fn(nxt), next_stage)
                pv_phase(cur, stage)
                finalize_fn(qt)
                reset()
            return cur

        for select_fn, fn in diags:
            if select_fn is None:
                transition(0, cur, fn)
            else:
                lax.fori_loop(0, select_fn(nxt), functools.partial(transition, fn=fn), cur)

    stage_a = (s_ref, mpart_ref)
    reset()
    for select_fn, fn in diags:
        if select_fn is None:
            qk_phase(fn(0), stage_a)
        else:
            lax.fori_loop(0, select_fn(0), lambda _, c, fn=fn: qk_phase(fn(0), stage_a) or c, 0)
    if sd_ref is None:
        lax.fori_loop(0, nq, lambda qt, c: tile(qt, stage_a, stage_a) or c, 0)
    else:
        stage_b = (sd_ref, mpd_ref)

        def tile_pair(j, c):
            tile(2 * j, stage_a, stage_b)
            tile(2 * j + 1, stage_b, stage_a)
            return c

        lax.fori_loop(0, nq // 2, tile_pair, 0)


def _causal_mask(rows, cols, row0):
    row = lax.broadcasted_iota(jnp.int32, (rows, cols), 0) + row0
    col = lax.broadcasted_iota(jnp.int32, (rows, cols), 1)
    return col <= row


_NT = (((1,), (1,)), ((), ()))


def _mla_attn_kernel(q_ref, k_ref, v_ref, g_ref, o_ref, acc_ref, qs_ref, s_ref, mpart_ref, macc_ref,
                     *, tq, rs):
    kw = 2 * MLA_NOPE
    tr = tq // rs
    nq = q_ref.shape[0] // tq
    chains = [(hp, r) for hp in range(2) for r in range(rs)]

    def stage_queries(qt):
        rows = pl.ds(pl.multiple_of(qt * tq, tq), tq)
        for hp in range(2):
            qs_ref[hp, :, :MLA_NOPE] = q_ref[rows, hp * MLA_NOPE:(hp + 1) * MLA_NOPE]
            qs_ref[hp, :, MLA_NOPE:] = q_ref[rows, 2 * MLA_NOPE:]

    def scores(qt, kc, diag):
        if diag:
            stage_queries(qt)
        k0 = pl.multiple_of(kc * tq, tq)
        out = []
        for hp, r in chains:
            q = qs_ref[hp, r * tr:(r + 1) * tr, :]
            k = k_ref[pl.ds(k0, tq), hp * kw:(hp + 1) * kw]
            s = lax.dot_general(q, k, _NT, preferred_element_type=F32)
            if diag:
                s = jnp.where(_causal_mask(tr, tq, r * tr), s, NEG)
            out.append(s)
        return out

    def values(ci, kc):
        hp = chains[ci][0]
        return v_ref[pl.ds(pl.multiple_of(kc * tq, tq), tq), hp * kw:(hp + 1) * kw]

    def finalize(qt):
        for ci, (hp, r) in enumerate(chains):
            rows = pl.ds(pl.multiple_of(qt * tq + r * tr, tr), tr)
            acc = acc_ref[ci]
            o = acc[:, :MLA_V] / acc[:, MLA_V:]
            gate = g_ref[rows, hp * MLA_V:(hp + 1) * MLA_V].astype(F32)
            o_ref[rows, hp * MLA_V:(hp + 1) * MLA_V] = (o * gate).astype(BF16)

    _flash_pipeline(len(chains), nq, [(None, lambda qt: scores(qt, qt, True))],
                    [(lambda qt: qt, lambda qt, kc: scores(qt, kc, False))],
                    values, finalize, s_ref, acc_ref, mpart_ref, macc_ref)


def _mla_attn(qm, kk, vv, rest, nb, s):
    tq = MLA_TQ
    pair_w = 2 * (MLA_NOPE + MLA_ROPE)
    kw = 4 * MLA_NOPE
    gate_blk = GM_OFF // (2 * MLA_V)
    rs = MLA_ROW_SPLIT
    return pl.pallas_call(
        functools.partial(_mla_attn_kernel, tq=tq, rs=rs),
        grid=(nb, MLA_HEADS // 2),
        in_specs=[pl.BlockSpec((s, pair_w), lambda b, hh: (b, hh)),
                  pl.BlockSpec((s, kw), lambda b, hh: (b, hh)),
                  pl.BlockSpec((s, kw), lambda b, hh: (b, hh)),
                  pl.BlockSpec((s, 2 * MLA_V), lambda b, hh: (b, gate_blk + hh))],
        out_specs=pl.BlockSpec((s, 2 * MLA_V), lambda b, hh: (b, hh)),
        out_shape=jax.ShapeDtypeStruct((nb * s, MLA_HEADS * MLA_V), BF16),
        scratch_shapes=[pltpu.VMEM((2 * rs, tq // rs, 2 * MLA_V), F32),
                        pltpu.VMEM((2, tq, 2 * MLA_NOPE), BF16),
                        pltpu.VMEM((2 * rs, tq // rs, tq), F32),
                        pltpu.VMEM((2 * rs, tq // rs, LANES), F32),
                        pltpu.VMEM((2 * rs, tq // rs, LANES), F32)],
        compiler_params=_cparams(("arbitrary", "arbitrary")),
        name="mla_attn",
    )(qm, kk, vv, rest)


def _bf16_pieces(x):
    p1 = x.astype(BF16)
    r = x - p1.astype(F32)
    p2 = r.astype(BF16)
    return p1.astype(F32), p2.astype(F32), r - p2.astype(F32)


def _diff_attn_kernel(ord_ref, q_ref, k_ref, v_ref, g_ref, pq_ref, pk_ref, sl_ref, lam_ref, gs_ref,
                      o_ref, vaug_ref, kaug_ref, acc_ref, qf_ref, s_ref, mpart_ref, macc_ref,
                      sd_ref, mpd_ref, *, tq, rs):
    seq = q_ref.shape[0]
    nq = seq // tq
    tr = tq // rs
    chains = [(c, r) for c in range(2) for r in range(rs)]
    lane = lax.broadcasted_iota(jnp.int32, (tq, LANES), 1)
    ones_col = jnp.ones((tq, LANES), BF16)
    slope2 = sl_ref[0, :, 0:1] * LOG2E
    c_pieces = _bf16_pieces(slope2)

    def pick(sel, x3):
        return jnp.where(sel == 0, x3[0], jnp.where(sel == 1, x3[1], x3[2]))

    def bias_lanes(rows, key_side):
        pos = _bf16_pieces(pq_ref[rows, :])
        lane_b = lane - 9
        if key_side:
            lo, hi = pick(lane % 3, pos), pick(lane_b // 3, c_pieces)
        else:
            lo, hi = pick(lane // 3, c_pieces), pick(lane_b % 3, [-p for p in pos])
        return jnp.where(lane < 9, lo, jnp.where(lane < 18, hi, 0.0)).astype(BF16)

    def stage_keys(kc, _):
        rows = pl.ds(pl.multiple_of(kc * tq, tq), tq)
        vaug_ref[rows, :DIFF_V] = v_ref[rows, :]
        vaug_ref[rows, DIFF_V:] = ones_col
        kaug_ref[rows, :2 * DIFF_QK] = k_ref[rows, :]
        kaug_ref[rows, 2 * DIFF_QK:] = bias_lanes(rows, True)
        return 0

    lax.fori_loop(0, nq, stage_keys, 0)

    def stage_queries(qt):
        rows = pl.ds(pl.multiple_of(qt * tq, tq), tq)
        q = q_ref[rows, :]
        q_side = bias_lanes(rows, False)
        qf_ref[0, :, :2 * DIFF_QK] = jnp.where(lane < DIFF_QK, q, 0).astype(BF16)
        qf_ref[1, :, :2 * DIFF_QK] = jnp.where(lane >= DIFF_QK, q, 0).astype(BF16)
        qf_ref[0, :, 2 * DIFF_QK:] = q_side
        qf_ref[1, :, 2 * DIFF_QK:] = q_side

    lq = lam_ref[...]
    lam = (jnp.exp(jnp.sum(lq[0:1] * lq[1:2], axis=-1, keepdims=True))
           - jnp.exp(jnp.sum(lq[2:3] * lq[3:4], axis=-1, keepdims=True)) + LAMBDA_INIT)

    def scores(qt, kc, diag):
        k0 = pl.multiple_of(kc * tq, tq)
        k = k_ref[pl.ds(k0, tq), :]
        pk = slope2 * pk_ref[0, pl.ds(kc, 1), :]
        out = [None] * len(chains)
        for r in range(rs):
            pq = pq_ref[pl.ds(pl.multiple_of(qt * tq + r * tr, tr), tr), :]
            bias = jnp.abs(slope2 * pq - pk)
            mask = _causal_mask(tr, tq, r * tr) if diag else None
            for c in range(2):
                q = qf_ref[c, r * tr:(r + 1) * tr, :2 * DIFF_QK]
                s = lax.dot_general(q, k, _NT, preferred_element_type=F32) - bias
                out[chains.index((c, r))] = jnp.where(mask, s, NEG) if diag else s
        return out

    def scores_diag(qt):
        stage_queries(qt)
        return scores(qt, qt, True)

    def scores_ordered(qt, kc):
        k = kaug_ref[pl.ds(pl.multiple_of(kc * tq, tq), tq), :]
        return [lax.dot_general(qf_ref[c, r * tr:(r + 1) * tr, :], k, _NT, preferred_element_type=F32)
                for c, r in chains]

    def scores_diag_sorted(qt):
        stage_queries(qt)
        return [jnp.where(_causal_mask(tr, tq, r * tr), s, NEG)
                for (c, r), s in zip(chains, scores_ordered(qt, qt))]

    batch = pl.program_id(0)

    def trips_ordered(qt):
        return jnp.where((ord_ref[batch, qt] & 1) != 0, qt, 0)

    def tile_sorted(qt):
        return (ord_ref[batch, qt] >> 1) & 1

    def values(ci, kc):
        return vaug_ref[pl.ds(pl.multiple_of(kc * tq, tq), tq), :]

    def finalize(qt):
        for r in range(rs):
            rows = pl.ds(pl.multiple_of(qt * tq + r * tr, tr), tr)
            a1 = acc_ref[chains.index((0, r))]
            a2 = acc_ref[chains.index((1, r))]
            o = a1[:, :DIFF_V] / a1[:, DIFF_V:] - lam * (a2[:, :DIFF_V] / a2[:, DIFF_V:])
            ms_o = jnp.mean(o * o, axis=-1, keepdims=True)
            o = o * lax.rsqrt(ms_o + EPS) * gs_ref[...] * (1.0 - LAMBDA_INIT)
            o_ref[rows, :] = (o * g_ref[rows, :].astype(F32)).astype(BF16)

    _flash_pipeline(len(chains), nq,
                    [(tile_sorted, scores_diag_sorted), (lambda qt: 1 - tile_sorted(qt), scores_diag)],
                    [(trips_ordered, scores_ordered),
                     (lambda qt: qt - trips_ordered(qt), lambda qt, kc: scores(qt, kc, False))],
                    values, finalize, s_ref, acc_ref, mpart_ref, macc_ref, sd_ref, mpd_ref)


def _diff_attn(rest, pos_col, pos_row, ordered, slopes, lam_par, g_subln, nb, s):
    tq = ATTN_TQ
    nq = s // tq
    rs = DIFF_ROW_SPLIT
    tr = tq // rs
    hw = DIFF_V
    return pl.pallas_call(
        functools.partial(_diff_attn_kernel, tq=tq, rs=rs),
        grid=(nb, DIFF_HEADS),
        in_specs=[pl.BlockSpec(memory_space=pltpu.SMEM),
                  pl.BlockSpec((s, hw), lambda b, hd: (b, QD_OFF // hw + hd)),
                  pl.BlockSpec((s, hw), lambda b, hd: (b, KD_OFF // hw + hd)),
                  pl.BlockSpec((s, hw), lambda b, hd: (b, VD_OFF // hw + hd)),
                  pl.BlockSpec((s, hw), lambda b, hd: (b, GD_OFF // hw + hd)),
                  pl.BlockSpec((s, 1), lambda b, hd: (b, 0)),
                  pl.BlockSpec((1, nq, tq), lambda b, hd: (b, 0, 0)),
                  pl.BlockSpec((1, 1, LANES), lambda b, hd: (hd, 0, 0)),
                  pl.BlockSpec((4, DIFF_QK), lambda b, hd: (0, 0)),
                  pl.BlockSpec((1, DIFF_V), lambda b, hd: (0, 0))],
        out_specs=pl.BlockSpec((s, hw), lambda b, hd: (b, hd)),
        out_shape=jax.ShapeDtypeStruct((nb * s, DIFF_HEADS * DIFF_V), BF16),
        scratch_shapes=[pltpu.VMEM((s, 2 * DIFF_V), BF16),
                        pltpu.VMEM((s, 4 * DIFF_QK), BF16),
                        pltpu.VMEM((2 * rs, tr, 2 * DIFF_V), F32),
                        pltpu.VMEM((2, tq, 4 * DIFF_QK), BF16),
                        pltpu.VMEM((2 * rs, tr, tq), F32),
                        pltpu.VMEM((2 * rs, tr, LANES), F32),
                        pltpu.VMEM((2 * rs, tr, LANES), F32),
                        pltpu.VMEM((2 * rs, tr, tq), F32),
                        pltpu.VMEM((2 * rs, tr, LANES), F32)],
        compiler_params=_cparams(("arbitrary", "arbitrary")),
        name="diff_attn",
    )(ordered, rest, rest, rest, rest, pos_col, pos_row, slopes, lam_par, g_subln)


def _merge_kernel(a1_ref, a2_ref, w1_ref, w2_ref, s1_ref, s2_ref, o_ref, wb_ref):
    @pl.when(pl.program_id(1) == 0)
    def _():
        wb_ref[0] = w1_ref[...].astype(BF16)
        wb_ref[1] = w2_ref[...].astype(BF16)

    rb = a1_ref.shape[0] // EPILOGUE_ROW_SPLIT
    for t in range(EPILOGUE_ROW_SPLIT):
        rows = slice(t * rb, (t + 1) * rb)
        y1 = jnp.dot(a1_ref[rows, :], wb_ref[0], preferred_element_type=F32)
        y2 = jnp.dot(a2_ref[rows, :], wb_ref[1], preferred_element_type=F32)
        o_ref[rows, :] = (s1_ref[rows, :].astype(F32) * y1
                          + s2_ref[rows, :].astype(F32) * y2).astype(BF16)


def _merge(og_mla, og_diff, w1, w2, rest):
    m, k_dim = og_mla.shape
    n = w1.shape[1]
    tm, tn = MERGE_TM, MERGE_TN
    return pl.pallas_call(
        _merge_kernel,
        grid=(n // tn, m // tm),
        in_specs=[pl.BlockSpec((tm, k_dim), lambda j, i: (i, 0)),
                  pl.BlockSpec((tm, k_dim), lambda j, i: (i, 0)),
                  pl.BlockSpec((k_dim, tn), lambda j, i: (0, j)),
                  pl.BlockSpec((k_dim, tn), lambda j, i: (0, j)),
                  pl.BlockSpec((tm, tn), lambda j, i: (i, MGM_OFF // tn + j)),
                  pl.BlockSpec((tm, tn), lambda j, i: (i, MGD_OFF // tn + j))],
        out_specs=pl.BlockSpec((tm, tn), lambda j, i: (i, j)),
        out_shape=jax.ShapeDtypeStruct((m, n), BF16),
        scratch_shapes=[pltpu.VMEM((2, k_dim, tn), BF16)],
        compiler_params=_cparams(("arbitrary", "arbitrary")),
        name="merge",
    )(og_mla, og_diff, w1, w2, rest, rest)


def _out_kernel(a_ref, w_ref, x_ref, ada_ref, g_ref, o_ref, wb_ref):
    @pl.when(jnp.logical_and(pl.program_id(0) == 0, pl.program_id(1) == 0))
    def _():
        wb_ref[...] = w_ref[...].astype(BF16)

    rb = a_ref.shape[0] // EPILOGUE_ROW_SPLIT
    for t in range(EPILOGUE_ROW_SPLIT):
        rows = slice(t * rb, (t + 1) * rb)
        y = jnp.dot(a_ref[rows, :], wb_ref[...], preferred_element_type=F32)
        ms = jnp.mean(y * y, axis=-1, keepdims=True)
        yn = y * lax.rsqrt(ms + EPS) * g_ref[...]
        o_ref[0, rows, :] = x_ref[0, rows, :] + ada_ref[0, 2:3, :] * yn


def _out(merged, w_out, x, ada3, g_post):
    nb, s, d = x.shape
    tm = OUT_TM
    ns = s // tm
    return pl.pallas_call(
        _out_kernel,
        grid=(nb, ns),
        in_specs=[pl.BlockSpec((tm, d), lambda b, i: (b * ns + i, 0)),
                  pl.BlockSpec((d, d), lambda b, i: (0, 0), pipeline_mode=pl.Buffered(1)),
                  pl.BlockSpec((1, tm, d), lambda b, i: (b, i, 0)),
                  pl.BlockSpec((1, 3, d), lambda b, i: (b, 0, 0)),
                  pl.BlockSpec((1, d), lambda b, i: (0, 0))],
        out_specs=pl.BlockSpec((1, tm, d), lambda b, i: (b, i, 0)),
        out_shape=jax.ShapeDtypeStruct((nb, s, d), F32),
        scratch_shapes=[pltpu.VMEM((d, d), BF16)],
        compiler_params=_cparams(("arbitrary", "arbitrary")),
        name="out_proj",
    )(merged, w_out, x, ada3, g_post)


def kernel(x, c, positions, w_ada, b_ada, g_pre, w_in, g_kv, w_ukv, lambda_q1, lambda_k1,
           lambda_q2, lambda_k2, g_subln, w_o_mla, w_o_diff, w_out, g_post):
    nb, s, d = x.shape
    depth = w_in.shape[0]
    half = MLA_ROPE // 2
    inv = ROPE_THETA ** (-jnp.arange(half, dtype=F32) / half)
    inv_tab = jnp.tile(inv, LANES // half).reshape(1, LANES)
    slopes = 2.0 ** (-8.0 * jnp.arange(1, DIFF_HEADS + 1, dtype=F32) / DIFF_HEADS)
    slopes = jnp.broadcast_to(slopes.reshape(DIFF_HEADS, 1, 1), (DIFF_HEADS, 1, LANES))
    pos_col = positions.reshape(nb * s, 1)
    pos_colf = pos_col.astype(F32)
    pos_chunks = positions.reshape(nb, s // ATTN_TQ, ATTN_TQ)
    pos_row = pos_chunks.astype(F32)
    run_max = lax.cummax(pos_chunks.max(axis=-1), axis=1)
    prev_max = jnp.concatenate(
        [jnp.full((nb, 1), jnp.iinfo(jnp.int32).min, jnp.int32), run_max[:, :-1]], axis=1)
    tile_sorted = jnp.all(pos_chunks[..., 1:] >= pos_chunks[..., :-1], axis=-1)
    ordered = ((prev_max <= pos_chunks.min(axis=-1)).astype(jnp.int32)
               + 2 * tile_sorted.astype(jnp.int32))

    for l in range(depth):
        ada3 = _ada(c, w_ada[l], b_ada[l]).reshape(nb, 3, d)
        h, cos_tab, sin_tab = _norm(x, ada3, g_pre[l].reshape(1, d), pos_col, inv_tab)
        w_t = jnp.swapaxes(w_in[l], 0, 1)
        qm, rest = _proj(h, w_t, cos_tab, sin_tab)
        kk, vv = _kv(h, w_t, g_kv[l].reshape(1, KV_RANK), w_ukv[l], cos_tab, sin_tab)
        og_mla = _mla_attn(qm, kk, vv, rest, nb, s)
        lam_par = jnp.stack([lambda_q1[l], lambda_k1[l], lambda_q2[l], lambda_k2[l]]).astype(F32)
        og_diff = _diff_attn(rest, pos_colf, pos_row, ordered, slopes, lam_par,
                             g_subln[l].reshape(1, DIFF_V), nb, s)
        merged = _merge(og_mla, og_diff, w_o_mla[l], w_o_diff[l], rest)
        x = _out(merged, w_out[l], x, ada3, g_post[l].reshape(1, d))
    return x
```

```python
import functools
import math

import jax
import jax.numpy as jnp
from jax import lax
from jax.experimental import pallas as pl
from jax.experimental.pallas import tpu as pltpu

F32 = jnp.float32
BF16 = jnp.bfloat16

D_MODEL = 2048
MLA_HEADS = 8
MLA_NOPE = 128
MLA_ROPE = 64
MLA_V = 128
KV_RANK = 512
ROPE_THETA = 10000.0
DIFF_HEADS = 8
DIFF_QK = 64
DIFF_V = 128
EPS = 1e-6
NEG = -1e30
LAMBDA_INIT = 0.8 - 0.6 * math.exp(-0.3 * 0)

LANES = 128
SUBLANES = 8
VMEM_LIMIT = 56 * 1024 * 1024
ATTN_TQ = 1024
DIFF_ROW_SPLIT = 2
MLA_TQ = 1024
MLA_ROW_SPLIT = 2
PROJ_TM = 1024
REST_TN = 1024
QMLA_PAIRS_PER_TILE = 2
KV_TM = 1024
MERGE_TM, MERGE_TN = 1024, 1024
OUT_TM = 512
EPILOGUE_ROW_SPLIT = 2
LOG2E = math.log2(math.e)

Q_MLA_W = MLA_HEADS * (MLA_NOPE + MLA_ROPE)
KVR_W = KV_RANK + MLA_ROPE
KVR_PAD = 640
REST_OFF = Q_MLA_W + KVR_W
REST_SHIFT = REST_OFF % LANES
QD_OFF, KD_OFF, VD_OFF, GM_OFF, GD_OFF, MGM_OFF, MGD_OFF = 0, 1024, 2048, 3072, 4096, 5120, 7168
REST_W = 9216


def _cparams(sem):
    return pltpu.CompilerParams(dimension_semantics=sem, vmem_limit_bytes=VMEM_LIMIT)


def _ada_kernel(cb_ref, w_ref, b_ref, o_ref):
    k_dim, tn = w_ref.shape
    nb = cb_ref.shape[0]
    nchunk = tn // LANES

    def body(i, accs):
        k0 = pl.multiple_of(i * SUBLANES, SUBLANES)
        out = []
        for b in range(nb):
            cv = cb_ref[b, pl.ds(k0, SUBLANES), :]
            for j in range(nchunk):
                wv = w_ref[pl.ds(k0, SUBLANES), j * LANES:(j + 1) * LANES]
                out.append(accs[b * nchunk + j] + wv * cv)
        return tuple(out)

    init = tuple(jnp.zeros((SUBLANES, LANES), F32) for _ in range(nb * nchunk))
    accs = lax.fori_loop(0, k_dim // SUBLANES, body, init, unroll=8)
    for b in range(nb):
        row = jnp.concatenate(
            [jnp.sum(accs[b * nchunk + j], axis=0, keepdims=True) for j in range(nchunk)], axis=1)
        o_ref[b:b + 1, :] = row + b_ref[...]


def _ada(c, w, bias):
    nb, k_dim = c.shape
    n = w.shape[1]
    tn = 512
    cb = jnp.broadcast_to(c[:, :, None], (nb, k_dim, LANES))
    return pl.pallas_call(
        _ada_kernel,
        grid=(n // tn,),
        in_specs=[pl.BlockSpec((nb, k_dim, LANES), lambda j: (0, 0, 0)),
                  pl.BlockSpec((k_dim, tn), lambda j: (0, j)),
                  pl.BlockSpec((1, tn), lambda j: (0, j))],
        out_specs=pl.BlockSpec((nb, tn), lambda j: (0, j)),
        out_shape=jax.ShapeDtypeStruct((nb, n), F32),
        compiler_params=_cparams(("arbitrary",)),
        name="ada",
    )(cb, w, bias.reshape(1, n))


def _norm_kernel(x_ref, ada_ref, g_ref, pos_ref, inv_ref, h_ref, cos_ref, sin_ref):
    x = x_ref[0]
    ms = jnp.mean(x * x, axis=-1, keepdims=True)
    y = x * lax.rsqrt(ms + EPS) * g_ref[...]
    shift = ada_ref[0, 0:1, :]
    scale = ada_ref[0, 1:2, :]
    h_ref[...] = (y * (1.0 + scale) + shift).astype(BF16)
    ang = pos_ref[...].astype(F32) * inv_ref[...]
    lane = lax.broadcasted_iota(jnp.int32, ang.shape, 1)
    sign = jnp.where((lane % MLA_ROPE) < MLA_ROPE // 2, -1.0, 1.0).astype(F32)
    cos_ref[...] = jnp.cos(ang)
    sin_ref[...] = jnp.sin(ang) * sign


def _norm(x, ada3, g_pre, pos_col, inv_tab):
    nb, s, d = x.shape
    ts = 512
    ns = s // ts
    row = lambda b, i: (b * ns + i, 0)
    return pl.pallas_call(
        _norm_kernel,
        grid=(nb, ns),
        in_specs=[pl.BlockSpec((1, ts, d), lambda b, i: (b, i, 0)),
                  pl.BlockSpec((1, 3, d), lambda b, i: (b, 0, 0)),
                  pl.BlockSpec((1, d), lambda b, i: (0, 0)),
                  pl.BlockSpec((ts, 1), row),
                  pl.BlockSpec((1, LANES), lambda b, i: (0, 0))],
        out_specs=[pl.BlockSpec((ts, d), row),
                   pl.BlockSpec((ts, LANES), row),
                   pl.BlockSpec((ts, LANES), row)],
        out_shape=[jax.ShapeDtypeStruct((nb * s, d), BF16),
                   jax.ShapeDtypeStruct((nb * s, LANES), F32),
                   jax.ShapeDtypeStruct((nb * s, LANES), F32)],
        compiler_params=_cparams(("arbitrary", "arbitrary")),
        name="prenorm",
    )(x, ada3, g_pre, pos_col, inv_tab)


def _rope_cols(r, cos, sin_signed):
    lane = lax.broadcasted_iota(jnp.int32, r.shape, 1)
    half = MLA_ROPE // 2
    partner = jnp.where((lane % MLA_ROPE) < half,
                        pltpu.roll(r, LANES - half, 1), pltpu.roll(r, half, 1))
    return r * cos + partner * sin_signed


def _qmla_kernel(a_ref, w_ref, cos_ref, sin_ref, o_ref, wb_ref, *, scale):
    hd = MLA_NOPE + MLA_ROPE
    pair_w = 2 * hd
    n_pairs = wb_ref.shape[0] // pair_w

    @pl.when(pl.program_id(1) == 0)
    def _():
        for p in range(n_pairs):
            src, dst = w_ref.at[p * pair_w:(p + 1) * pair_w], wb_ref.at[p * pair_w:(p + 1) * pair_w]
            dst[:MLA_NOPE] = src[:MLA_NOPE].astype(BF16)
            dst[MLA_NOPE:2 * MLA_NOPE] = src[hd:hd + MLA_NOPE].astype(BF16)
            dst[2 * MLA_NOPE:2 * MLA_NOPE + MLA_ROPE] = src[MLA_NOPE:hd].astype(BF16)
            dst[2 * MLA_NOPE + MLA_ROPE:] = src[hd + MLA_NOPE:].astype(BF16)

    acc = lax.dot_general(a_ref[...], wb_ref[...], _NT, preferred_element_type=F32)
    for p in range(n_pairs):
        c0 = p * pair_w
        rr = _rope_cols(acc[:, c0 + 2 * MLA_NOPE:c0 + pair_w], cos_ref[...], sin_ref[...])
        o_ref[:, c0:c0 + 2 * MLA_NOPE] = (acc[:, c0:c0 + 2 * MLA_NOPE] * scale).astype(BF16)
        o_ref[:, c0 + 2 * MLA_NOPE:c0 + pair_w] = (rr * scale).astype(BF16)


def _rest_kernel(a_ref, w_ref, o_ref, wb_ref, *, tn):
    j = pl.program_id(0)

    @pl.when(pl.program_id(1) == 0)
    def _():
        wb_ref[...] = w_ref[...].astype(BF16)

    def tile(epilogue):
        acc = lax.dot_general(a_ref[...], wb_ref[...], _NT, preferred_element_type=F32)
        o_ref[...] = epilogue(acc).astype(BF16)

    @pl.when(j < KD_OFF // tn)
    def _():
        tile(lambda acc: acc * (DIFF_QK ** -0.5 * LOG2E))

    @pl.when(jnp.logical_and(j >= KD_OFF // tn, j < GM_OFF // tn))
    def _():
        tile(lambda acc: acc)

    @pl.when(jnp.logical_and(j >= GM_OFF // tn, j < MGM_OFF // tn))
    def _():
        tile(lambda acc: acc * jax.nn.sigmoid(acc))

    @pl.when(j >= MGM_OFF // tn)
    def _():
        tile(jax.nn.sigmoid)


def _proj(h, w_t, cos_tab, sin_tab):
    m, k_dim = h.shape
    tm = PROJ_TM
    pair_w = 2 * (MLA_NOPE + MLA_ROPE)
    q_scale = (MLA_NOPE + MLA_ROPE) ** -0.5 * LOG2E

    def row_window(rows, offset_fn):
        return pl.BlockSpec((pl.Element(rows), pl.Element(k_dim)),
                            lambda *g: (pl.multiple_of(offset_fn(*g), SUBLANES), 0))

    qtn = QMLA_PAIRS_PER_TILE * pair_w
    qm = pl.pallas_call(
        functools.partial(_qmla_kernel, scale=q_scale),
        grid=(Q_MLA_W // qtn, m // tm),
        in_specs=[pl.BlockSpec((tm, k_dim), lambda j, i: (i, 0)),
                  pl.BlockSpec((qtn, k_dim), lambda j, i: (j, 0)),
                  pl.BlockSpec((tm, LANES), lambda j, i: (i, 0)),
                  pl.BlockSpec((tm, LANES), lambda j, i: (i, 0))],
        out_specs=pl.BlockSpec((tm, qtn), lambda j, i: (i, j)),
        out_shape=jax.ShapeDtypeStruct((m, Q_MLA_W), BF16),
        scratch_shapes=[pltpu.VMEM((qtn, k_dim), BF16)],
        compiler_params=_cparams(("arbitrary", "arbitrary")),
        name="proj_qmla",
    )(h, w_t, cos_tab, sin_tab)
    tn = REST_TN
    rest = pl.pallas_call(
        functools.partial(_rest_kernel, tn=tn),
        grid=(REST_W // tn, m // tm),
        in_specs=[pl.BlockSpec((tm, k_dim), lambda j, i: (i, 0)),
                  row_window(tn, lambda j, i: REST_OFF + tn * j)],
        out_specs=pl.BlockSpec((tm, tn), lambda j, i: (i, j)),
        out_shape=jax.ShapeDtypeStruct((m, REST_W), BF16),
        scratch_shapes=[pltpu.VMEM((tn, k_dim), BF16)],
        compiler_params=_cparams(("arbitrary", "arbitrary")),
        name="proj_rest",
    )(h, w_t)
    return qm, rest


def _kv_kernel(a_ref, wp_ref, g_ref, w_ref, cos_ref, sin_ref, k_ref, v_ref, wpb_ref, wb_ref):
    @pl.when(pl.program_id(0) == 0)
    def _():
        wpb_ref[:KVR_W] = wp_ref[...].astype(BF16)
        wpb_ref[KVR_W:] = jnp.zeros((KVR_PAD - KVR_W, wpb_ref.shape[1]), BF16)
        wb_ref[...] = w_ref[...].astype(BF16)

    kw = MLA_NOPE + MLA_V
    rb = a_ref.shape[0] // EPILOGUE_ROW_SPLIT
    for t in range(EPILOGUE_ROW_SPLIT):
        rows = slice(t * rb, (t + 1) * rb)
        p = lax.dot_general(a_ref[rows, :], wpb_ref[...], _NT, preferred_element_type=F32)
        ckv = p[:, :KV_RANK]
        ms = jnp.mean(ckv * ckv, axis=-1, keepdims=True)
        n = (ckv * lax.rsqrt(ms + EPS) * g_ref[...]).astype(BF16)
        kv = jnp.dot(n, wb_ref[...], preferred_element_type=F32)
        kr_even = _rope_cols(p[:, KV_RANK:], cos_ref[rows, :], sin_ref[rows, :])
        kr_odd = pltpu.roll(kr_even, MLA_ROPE, 1)
        ones_col = jnp.ones(kr_even.shape, BF16)
        for hd in range(MLA_HEADS):
            k_ref[rows, hd * kw:hd * kw + MLA_NOPE] = kv[:, hd * kw:hd * kw + MLA_NOPE].astype(BF16)
            k_ref[rows, hd * kw + MLA_NOPE:(hd + 1) * kw] = (
                kr_even if hd % 2 == 0 else kr_odd).astype(BF16)
            v_ref[rows, hd * kw:hd * kw + MLA_V] = kv[:, hd * kw + MLA_NOPE:(hd + 1) * kw].astype(BF16)
            v_ref[rows, hd * kw + MLA_V:(hd + 1) * kw] = ones_col


def _kv(h, w_t, g_kv, w_ukv, cos_tab, sin_tab):
    m, k_dim = h.shape
    tm = KV_TM
    kw = MLA_HEADS * (MLA_NOPE + MLA_V)
    return pl.pallas_call(
        _kv_kernel,
        grid=(m // tm,),
        in_specs=[pl.BlockSpec((tm, k_dim), lambda i: (i, 0)),
                  pl.BlockSpec((pl.Element(KVR_W), pl.Element(k_dim)),
                               lambda i: (pl.multiple_of(Q_MLA_W + 0 * i, SUBLANES), 0)),
                  pl.BlockSpec((1, KV_RANK), lambda i: (0, 0)),
                  pl.BlockSpec((KV_RANK, kw), lambda i: (0, 0)),
                  pl.BlockSpec((tm, LANES), lambda i: (i, 0)),
                  pl.BlockSpec((tm, LANES), lambda i: (i, 0))],
        out_specs=[pl.BlockSpec((tm, kw), lambda i: (i, 0)),
                   pl.BlockSpec((tm, kw), lambda i: (i, 0))],
        out_shape=[jax.ShapeDtypeStruct((m, kw), BF16),
                   jax.ShapeDtypeStruct((m, kw), BF16)],
        scratch_shapes=[pltpu.VMEM((KVR_PAD, k_dim), BF16), pltpu.VMEM((KV_RANK, kw), BF16)],
        compiler_params=_cparams(("arbitrary",)),
        name="kv_up",
    )(h, w_t, g_kv, w_ukv, cos_tab, sin_tab)


def _flash_pipeline(n_chains, nq, diags, loops, value_fn, finalize_fn,
                    s_ref, acc_ref, mpart_ref, macc_ref, sd_ref=None, mpd_ref=None):
    chunk = s_ref.shape[2]

    def lane_tiles(x, n):
        return jnp.concatenate([x] * n, axis=1)

    def qk_phase(scores, stage):
        s_dst, m_dst = stage
        for ci, s in enumerate(scores):
            s_dst[ci] = s
            part = s[:, :LANES]
            for j in range(1, chunk // LANES):
                part = jnp.maximum(part, s[:, j * LANES:(j + 1) * LANES])
            m_dst[ci] = part

    def pv_phase(kc, stage):
        s_src, m_src = stage
        for ci in range(n_chains):
            m_acc = macc_ref[ci]
            m_run = jnp.maximum(m_acc, jnp.max(m_src[ci], axis=-1, keepdims=True))
            macc_ref[ci] = m_run
            p = jnp.exp2(s_src[ci] - lane_tiles(m_run, chunk // LANES))
            alpha = jnp.exp2(m_acc - m_run)
            pv = jnp.dot(p.astype(BF16), value_fn(ci, kc), preferred_element_type=F32)
            acc_ref[ci] = lane_tiles(alpha, acc_ref.shape[2] // LANES) * acc_ref[ci] + pv

    def reset():
        acc_ref[...] = jnp.zeros_like(acc_ref)
        macc_ref[...] = jnp.full(macc_ref.shape, NEG, F32)

    def tile(qt, stage, next_stage):
        cur = qt
        for trips_fn, score_fn in loops:
            def step(kc, cur, score_fn=score_fn):
                pv_phase(cur, stage)
                qk_phase(score_fn(qt, kc), stage)
                return kc

            cur = lax.fori_loop(0, trips_fn(qt), step, cur)
        nxt = jnp.minimum(qt + 1, nq - 1)

        def transition(_, cur, fn):
            if next_stage is stage:
                pv_phase(cur, stage)
                finalize_fn(qt)
                reset()
                qk_phase(fn(nxt), stage)
            else:
                qk_phase(fn(nxt), next_stage)
                pv_phase(cur, stage)
                finalize_fn(qt)
                reset()
            return cur

        for select_fn, fn in diags:
            if select_fn is None:
                transition(0, cur, fn)
            else:
                lax.fori_loop(0, select_fn(nxt), functools.partial(transition, fn=fn), cur)

    stage_a = (s_ref, mpart_ref)
    reset()
    for select_fn, fn in diags:
        if select_fn is None:
            qk_phase(fn(0), stage_a)
        else:
            lax.fori_loop(0, select_fn(0), lambda _, c, fn=fn: qk_phase(fn(0), stage_a) or c, 0)
    if sd_ref is None:
        lax.fori_loop(0, nq, lambda qt, c: tile(qt, stage_a, stage_a) or c, 0)
    else:
        stage_b = (sd_ref, mpd_ref)

        def tile_pair(j, c):
            tile(2 * j, stage_a, stage_b)
            tile(2 * j + 1, stage_b, stage_a)
            return c

        lax.fori_loop(0, nq // 2, tile_pair, 0)


def _causal_mask(rows, cols, row0):
    row = lax.broadcasted_iota(jnp.int32, (rows, cols), 0) + row0
    col = lax.broadcasted_iota(jnp.int32, (rows, cols), 1)
    return col <= row


def _pad_masked(s, cols):
    if s.shape[1] == cols:
        return s
    return jnp.concatenate([s, jnp.full((s.shape[0], cols - s.shape[1]), NEG, s.dtype)], axis=1)


_NT = (((1,), (1,)), ((), ()))


def _mla_attn_kernel(q_ref, k_ref, v_ref, g_ref, o_ref, acc_ref, qs_ref, s_ref, mpart_ref, macc_ref,
                     *, tq, rs):
    kw = 2 * MLA_NOPE
    tr = tq // rs
    nq = q_ref.shape[0] // tq
    chains = [(hp, r) for hp in range(2) for r in range(rs)]

    def stage_queries(qt):
        rows = pl.ds(pl.multiple_of(qt * tq, tq), tq)
        for hp in range(2):
            qs_ref[hp, :, :MLA_NOPE] = q_ref[rows, hp * MLA_NOPE:(hp + 1) * MLA_NOPE]
            qs_ref[hp, :, MLA_NOPE:] = q_ref[rows, 2 * MLA_NOPE:]

    def scores(qt, kc, diag):
        if diag:
            stage_queries(qt)
        k0 = pl.multiple_of(kc * tq, tq)
        out = []
        for hp, r in chains:
            nk = (r + 1) * tr if diag else tq
            q = qs_ref[hp, r * tr:(r + 1) * tr, :]
            k = k_ref[pl.ds(k0, nk), hp * kw:(hp + 1) * kw]
            s = lax.dot_general(q, k, _NT, preferred_element_type=F32)
            if diag:
                s = _pad_masked(jnp.where(_causal_mask(tr, nk, r * tr), s, NEG), tq)
            out.append(s)
        return out

    def values(ci, kc):
        hp = chains[ci][0]
        return v_ref[pl.ds(pl.multiple_of(kc * tq, tq), tq), hp * kw:(hp + 1) * kw]

    def finalize(qt):
        for ci, (hp, r) in enumerate(chains):
            rows = pl.ds(pl.multiple_of(qt * tq + r * tr, tr), tr)
            acc = acc_ref[ci]
            o = acc[:, :MLA_V] / acc[:, MLA_V:]
            gate = g_ref[rows, hp * MLA_V:(hp + 1) * MLA_V].astype(F32)
            o_ref[rows, hp * MLA_V:(hp + 1) * MLA_V] = (o * gate).astype(BF16)

    _flash_pipeline(len(chains), nq, [(None, lambda qt: scores(qt, qt, True))],
                    [(lambda qt: qt, lambda qt, kc: scores(qt, kc, False))],
                    values, finalize, s_ref, acc_ref, mpart_ref, macc_ref)


def _mla_attn(qm, kk, vv, rest, nb, s):
    tq = MLA_TQ
    pair_w = 2 * (MLA_NOPE + MLA_ROPE)
    kw = 4 * MLA_NOPE
    gate_blk = GM_OFF // (2 * MLA_V)
    rs = MLA_ROW_SPLIT
    return pl.pallas_call(
        functools.partial(_mla_attn_kernel, tq=tq, rs=rs),
        grid=(nb, MLA_HEADS // 2),
        in_specs=[pl.BlockSpec((s, pair_w), lambda b, hh: (b, hh)),
                  pl.BlockSpec((s, kw), lambda b, hh: (b, hh)),
                  pl.BlockSpec((s, kw), lambda b, hh: (b, hh)),
                  pl.BlockSpec((s, 2 * MLA_V), lambda b, hh: (b, gate_blk + hh))],
        out_specs=pl.BlockSpec((s, 2 * MLA_V), lambda b, hh: (b, hh)),
        out_shape=jax.ShapeDtypeStruct((nb * s, MLA_HEADS * MLA_V), BF16),
        scratch_shapes=[pltpu.VMEM((2 * rs, tq // rs, 2 * MLA_V), F32),
                        pltpu.VMEM((2, tq, 2 * MLA_NOPE), BF16),
                        pltpu.VMEM((2 * rs, tq // rs, tq), F32),
                        pltpu.VMEM((2 * rs, tq // rs, LANES), F32),
                        pltpu.VMEM((2 * rs, tq // rs, LANES), F32)],
        compiler_params=_cparams(("arbitrary", "arbitrary")),
        name="mla_attn",
    )(qm, kk, vv, rest)


def _bf16_pieces(x):
    p1 = x.astype(BF16)
    r = x - p1.astype(F32)
    p2 = r.astype(BF16)
    return p1.astype(F32), p2.astype(F32), r - p2.astype(F32)


def _diff_attn_kernel(ord_ref, q_ref, k_ref, v_ref, g_ref, pq_ref, pk_ref, sl_ref, lam_ref, gs_ref,
                      o_ref, vaug_ref, kaug_ref, acc_ref, qf_ref, s_ref, mpart_ref, macc_ref,
                      sd_ref, mpd_ref, *, tq, rs):
    seq = q_ref.shape[0]
    nq = seq // tq
    tr = tq // rs
    chains = [(c, r) for c in range(2) for r in range(rs)]
    lane = lax.broadcasted_iota(jnp.int32, (tq, LANES), 1)
    ones_col = jnp.ones((tq, LANES), BF16)
    slope2 = sl_ref[0, :, 0:1] * LOG2E
    c_pieces = _bf16_pieces(slope2)

    def pick(sel, x3):
        return jnp.where(sel == 0, x3[0], jnp.where(sel == 1, x3[1], x3[2]))

    def bias_lanes(rows, key_side):
        pos = _bf16_pieces(pq_ref[rows, :])
        lane_b = lane - 9
        if key_side:
            lo, hi = pick(lane % 3, pos), pick(lane_b // 3, c_pieces)
        else:
            lo, hi = pick(lane // 3, c_pieces), pick(lane_b % 3, [-p for p in pos])
        return jnp.where(lane < 9, lo, jnp.where(lane < 18, hi, 0.0)).astype(BF16)

    def stage_keys(kc, _):
        rows = pl.ds(pl.multiple_of(kc * tq, tq), tq)
        vaug_ref[rows, :DIFF_V] = v_ref[rows, :]
        vaug_ref[rows, DIFF_V:] = ones_col
        kaug_ref[rows, :2 * DIFF_QK] = k_ref[rows, :]
        kaug_ref[rows, 2 * DIFF_QK:] = bias_lanes(rows, True)
        return 0

    lax.fori_loop(0, nq, stage_keys, 0)

    def stage_queries(qt):
        rows = pl.ds(pl.multiple_of(qt * tq, tq), tq)
        q = q_ref[rows, :]
        q_side = bias_lanes(rows, False)
        qf_ref[0, :, :2 * DIFF_QK] = jnp.where(lane < DIFF_QK, q, 0).astype(BF16)
        qf_ref[1, :, :2 * DIFF_QK] = jnp.where(lane >= DIFF_QK, q, 0).astype(BF16)
        qf_ref[0, :, 2 * DIFF_QK:] = q_side
        qf_ref[1, :, 2 * DIFF_QK:] = q_side

    lq = lam_ref[...]
    lam = (jnp.exp(jnp.sum(lq[0:1] * lq[1:2], axis=-1, keepdims=True))
           - jnp.exp(jnp.sum(lq[2:3] * lq[3:4], axis=-1, keepdims=True)) + LAMBDA_INIT)

    def scores(qt, kc, diag):
        k0 = pl.multiple_of(kc * tq, tq)
        k = k_ref[pl.ds(k0, tq), :]
        pk = slope2 * pk_ref[0, pl.ds(kc, 1), :]
        out = [None] * len(chains)
        for r in range(rs):
            pq = pq_ref[pl.ds(pl.multiple_of(qt * tq + r * tr, tr), tr), :]
            bias = jnp.abs(slope2 * pq - pk)
            mask = _causal_mask(tr, tq, r * tr) if diag else None
            for c in range(2):
                q = qf_ref[c, r * tr:(r + 1) * tr, :2 * DIFF_QK]
                s = lax.dot_general(q, k, _NT, preferred_element_type=F32) - bias
                out[chains.index((c, r))] = jnp.where(mask, s, NEG) if diag else s
        return out

    def scores_diag(qt):
        stage_queries(qt)
        return scores(qt, qt, True)

    def scores_ordered(qt, kc):
        k = kaug_ref[pl.ds(pl.multiple_of(kc * tq, tq), tq), :]
        return [lax.dot_general(qf_ref[c, r * tr:(r + 1) * tr, :], k, _NT, preferred_element_type=F32)
                for c, r in chains]

    def scores_diag_sorted(qt):
        stage_queries(qt)
        k0 = pl.multiple_of(qt * tq, tq)
        out = []
        for c, r in chains:
            nk = (r + 1) * tr
            s = lax.dot_general(qf_ref[c, r * tr:(r + 1) * tr, :], kaug_ref[pl.ds(k0, nk), :], _NT,
                                preferred_element_type=F32)
            out.append(_pad_masked(jnp.where(_causal_mask(tr, nk, r * tr), s, NEG), tq))
        return out

    batch = pl.program_id(0)

    def trips_ordered(qt):
        return jnp.where((ord_ref[batch, qt] & 1) != 0, qt, 0)

    def tile_sorted(qt):
        return (ord_ref[batch, qt] >> 1) & 1

    def values(ci, kc):
        return vaug_ref[pl.ds(pl.multiple_of(kc * tq, tq), tq), :]

    def finalize(qt):
        for r in range(rs):
            rows = pl.ds(pl.multiple_of(qt * tq + r * tr, tr), tr)
            a1 = acc_ref[chains.index((0, r))]
            a2 = acc_ref[chains.index((1, r))]
            o = a1[:, :DIFF_V] / a1[:, DIFF_V:] - lam * (a2[:, :DIFF_V] / a2[:, DIFF_V:])
            ms_o = jnp.mean(o * o, axis=-1, keepdims=True)
            o = o * lax.rsqrt(ms_o + EPS) * gs_ref[...] * (1.0 - LAMBDA_INIT)
            o_ref[rows, :] = (o * g_ref[rows, :].astype(F32)).astype(BF16)

    _flash_pipeline(len(chains), nq,
                    [(tile_sorted, scores_diag_sorted), (lambda qt: 1 - tile_sorted(qt), scores_diag)],
                    [(trips_ordered, scores_ordered),
                     (lambda qt: qt - trips_ordered(qt), lambda qt, kc: scores(qt, kc, False))],
                    values, finalize, s_ref, acc_ref, mpart_ref, macc_ref, sd_ref, mpd_ref)


def _diff_attn(rest, pos_col, pos_row, ordered, slopes, lam_par, g_subln, nb, s):
    tq = ATTN_TQ
    nq = s // tq
    rs = DIFF_ROW_SPLIT
    tr = tq // rs
    hw = DIFF_V
    return pl.pallas_call(
        functools.partial(_diff_attn_kernel, tq=tq, rs=rs),
        grid=(nb, DIFF_HEADS),
        in_specs=[pl.BlockSpec(memory_space=pltpu.SMEM),
                  pl.BlockSpec((s, hw), lambda b, hd: (b, QD_OFF // hw + hd)),
                  pl.BlockSpec((s, hw), lambda b, hd: (b, KD_OFF // hw + hd)),
                  pl.BlockSpec((s, hw), lambda b, hd: (b, VD_OFF // hw + hd)),
                  pl.BlockSpec((s, hw), lambda b, hd: (b, GD_OFF // hw + hd)),
                  pl.BlockSpec((s, 1), lambda b, hd: (b, 0)),
                  pl.BlockSpec((1, nq, tq), lambda b, hd: (b, 0, 0)),
                  pl.BlockSpec((1, 1, LANES), lambda b, hd: (hd, 0, 0)),
                  pl.BlockSpec((4, DIFF_QK), lambda b, hd: (0, 0)),
                  pl.BlockSpec((1, DIFF_V), lambda b, hd: (0, 0))],
        out_specs=pl.BlockSpec((s, hw), lambda b, hd: (b, hd)),
        out_shape=jax.ShapeDtypeStruct((nb * s, DIFF_HEADS * DIFF_V), BF16),
        scratch_shapes=[pltpu.VMEM((s, 2 * DIFF_V), BF16),
                        pltpu.VMEM((s, 4 * DIFF_QK), BF16),
                        pltpu.VMEM((2 * rs, tr, 2 * DIFF_V), F32),
                        pltpu.VMEM((2, tq, 4 * DIFF_QK), BF16),
                        pltpu.VMEM((2 * rs, tr, tq), F32),
                        pltpu.VMEM((2 * rs, tr, LANES), F32),
                        pltpu.VMEM((2 * rs, tr, LANES), F32),
                        pltpu.VMEM((2 * rs, tr, tq), F32),
                        pltpu.VMEM((2 * rs, tr, LANES), F32)],
        compiler_params=_cparams(("arbitrary", "arbitrary")),
        name="diff_attn",
    )(ordered, rest, rest, rest, rest, pos_col, pos_row, slopes, lam_par, g_subln)


def _merge_kernel(a1_ref, a2_ref, w1_ref, w2_ref, s1_ref, s2_ref, o_ref, wb_ref):
    @pl.when(pl.program_id(1) == 0)
    def _():
        wb_ref[0] = w1_ref[...].astype(BF16)
        wb_ref[1] = w2_ref[...].astype(BF16)

    rb = a1_ref.shape[0] // EPILOGUE_ROW_SPLIT
    for t in range(EPILOGUE_ROW_SPLIT):
        rows = slice(t * rb, (t + 1) * rb)
        y1 = jnp.dot(a1_ref[rows, :], wb_ref[0], preferred_element_type=F32)
        y2 = jnp.dot(a2_ref[rows, :], wb_ref[1], preferred_element_type=F32)
        o_ref[rows, :] = (s1_ref[rows, :].astype(F32) * y1
                          + s2_ref[rows, :].astype(F32) * y2).astype(BF16)


def _merge(og_mla, og_diff, w1, w2, rest):
    m, k_dim = og_mla.shape
    n = w1.shape[1]
    tm, tn = MERGE_TM, MERGE_TN
    return pl.pallas_call(
        _merge_kernel,
        grid=(n // tn, m // tm),
        in_specs=[pl.BlockSpec((tm, k_dim), lambda j, i: (i, 0)),
                  pl.BlockSpec((tm, k_dim), lambda j, i: (i, 0)),
                  pl.BlockSpec((k_dim, tn), lambda j, i: (0, j)),
                  pl.BlockSpec((k_dim, tn), lambda j, i: (0, j)),
                  pl.BlockSpec((tm, tn), lambda j, i: (i, MGM_OFF // tn + j)),
                  pl.BlockSpec((tm, tn), lambda j, i: (i, MGD_OFF // tn + j))],
        out_specs=pl.BlockSpec((tm, tn), lambda j, i: (i, j)),
        out_shape=jax.ShapeDtypeStruct((m, n), BF16),
        scratch_shapes=[pltpu.VMEM((2, k_dim, tn), BF16)],
        compiler_params=_cparams(("arbitrary", "arbitrary")),
        name="merge",
    )(og_mla, og_diff, w1, w2, rest, rest)


def _out_kernel(a_ref, w_ref, x_ref, ada_ref, g_ref, o_ref, wb_ref):
    @pl.when(jnp.logical_and(pl.program_id(0) == 0, pl.program_id(1) == 0))
    def _():
        wb_ref[...] = w_ref[...].astype(BF16)

    rb = a_ref.shape[0] // EPILOGUE_ROW_SPLIT
    for t in range(EPILOGUE_ROW_SPLIT):
        rows = slice(t * rb, (t + 1) * rb)
        y = jnp.dot(a_ref[rows, :], wb_ref[...], preferred_element_type=F32)
        ms = jnp.mean(y * y, axis=-1, keepdims=True)
        yn = y * lax.rsqrt(ms + EPS) * g_ref[...]
        o_ref[0, rows, :] = x_ref[0, rows, :] + ada_ref[0, 2:3, :] * yn


def _out(merged, w_out, x, ada3, g_post):
    nb, s, d = x.shape
    tm = OUT_TM
    ns = s // tm
    return pl.pallas_call(
        _out_kernel,
        grid=(nb, ns),
        in_specs=[pl.BlockSpec((tm, d), lambda b, i: (b * ns + i, 0)),
                  pl.BlockSpec((d, d), lambda b, i: (0, 0), pipeline_mode=pl.Buffered(1)),
                  pl.BlockSpec((1, tm, d), lambda b, i: (b, i, 0)),
                  pl.BlockSpec((1, 3, d), lambda b, i: (b, 0, 0)),
                  pl.BlockSpec((1, d), lambda b, i: (0, 0))],
        out_specs=pl.BlockSpec((1, tm, d), lambda b, i: (b, i, 0)),
        out_shape=jax.ShapeDtypeStruct((nb, s, d), F32),
        scratch_shapes=[pltpu.VMEM((d, d), BF16)],
        compiler_params=_cparams(("arbitrary", "arbitrary")),
        name="out_proj",
    )(merged, w_out, x, ada3, g_post)


def kernel(x, c, positions, w_ada, b_ada, g_pre, w_in, g_kv, w_ukv, lambda_q1, lambda_k1,
           lambda_q2, lambda_k2, g_subln, w_o_mla, w_o_diff, w_out, g_post):
    nb, s, d = x.shape
    depth = w_in.shape[0]
    half = MLA_ROPE // 2
    inv = ROPE_THETA ** (-jnp.arange(half, dtype=F32) / half)
    inv_tab = jnp.tile(inv, LANES // half).reshape(1, LANES)
    slopes = 2.0 ** (-8.0 * jnp.arange(1, DIFF_HEADS + 1, dtype=F32) / DIFF_HEADS)
    slopes = jnp.broadcast_to(slopes.reshape(DIFF_HEADS, 1, 1), (DIFF_HEADS, 1, LANES))
    pos_col = positions.reshape(nb * s, 1)
    pos_colf = pos_col.astype(F32)
    pos_chunks = positions.reshape(nb, s // ATTN_TQ, ATTN_TQ)
    pos_row = pos_chunks.astype(F32)
    run_max = lax.cummax(pos_chunks.max(axis=-1), axis=1)
    prev_max = jnp.concatenate(
        [jnp.full((nb, 1), jnp.iinfo(jnp.int32).min, jnp.int32), run_max[:, :-1]], axis=1)
    tile_sorted = jnp.all(pos_chunks[..., 1:] >= pos_chunks[..., :-1], axis=-1)
    ordered = ((prev_max <= pos_chunks.min(axis=-1)).astype(jnp.int32)
               + 2 * tile_sorted.astype(jnp.int32))

    for l in range(depth):
        ada3 = _ada(c, w_ada[l], b_ada[l]).reshape(nb, 3, d)
        h, cos_tab, sin_tab = _norm(x, ada3, g_pre[l].reshape(1, d), pos_col, inv_tab)
        w_t = jnp.swapaxes(w_in[l], 0, 1)
        qm, rest = _proj(h, w_t, cos_tab, sin_tab)
        kk, vv = _kv(h, w_t, g_kv[l].reshape(1, KV_RANK), w_ukv[l], cos_tab, sin_tab)
        og_mla = _mla_attn(qm, kk, vv, rest, nb, s)
        lam_par = jnp.stack([lambda_q1[l], lambda_k1[l], lambda_q2[l], lambda_k2[l]]).astype(F32)
        og_diff = _diff_attn(rest, pos_colf, pos_row, ordered, slopes, lam_par,
                             g_subln[l].reshape(1, DIFF_V), nb, s)
        merged = _merge(og_mla, og_diff, w_o_mla[l], w_o_diff[l], rest)
        x = _out(merged, w_out[l], x, ada3, g_post[l].reshape(1, d))
    return x
```

```python
import functools
import math

import jax
import jax.numpy as jnp
from jax import lax
from jax.experimental import pallas as pl
from jax.experimental.pallas import tpu as pltpu

F32 = jnp.float32
BF16 = jnp.bfloat16

D_MODEL = 2048
MLA_HEADS = 8
MLA_NOPE = 128
MLA_ROPE = 64
MLA_V = 128
KV_RANK = 512
ROPE_THETA = 10000.0
DIFF_HEADS = 8
DIFF_QK = 64
DIFF_V = 128
EPS = 1e-6
NEG = -1e30
LAMBDA_INIT = 0.8 - 0.6 * math.exp(-0.3 * 0)

LANES = 128
SUBLANES = 8
VMEM_LIMIT = 56 * 1024 * 1024
ATTN_TQ = 1024
DIFF_ROW_SPLIT = 2
MLA_TQ = 1024
MLA_ROW_SPLIT = 2
PROJ_TM = 1024
REST_TN = 1024
QMLA_PAIRS_PER_TILE = 2
KV_TM = 1024
MERGE_TM, MERGE_TN = 1024, 1024
OUT_TM = 512
EPILOGUE_ROW_SPLIT = 2
LOG2E = math.log2(math.e)

Q_MLA_W = MLA_HEADS * (MLA_NOPE + MLA_ROPE)
KVR_W = KV_RANK + MLA_ROPE
KVR_PAD = 640
REST_OFF = Q_MLA_W + KVR_W
REST_SHIFT = REST_OFF % LANES
QD_OFF, KD_OFF, VD_OFF, GM_OFF, GD_OFF, MGM_OFF, MGD_OFF = 0, 1024, 2048, 3072, 4096, 5120, 7168
REST_W = 9216


def _cparams(sem):
    return pltpu.CompilerParams(dimension_semantics=sem, vmem_limit_bytes=VMEM_LIMIT)


def _ada_kernel(cb_ref, w_ref, b_ref, o_ref):
    k_dim, tn = w_ref.shape
    nb = cb_ref.shape[0]
    nchunk = tn // LANES

    def body(i, accs):
        k0 = pl.multiple_of(i * SUBLANES, SUBLANES)
        out = []
        for b in range(nb):
            cv = cb_ref[b, pl.ds(k0, SUBLANES), :]
            for j in range(nchunk):
                wv = w_ref[pl.ds(k0, SUBLANES), j * LANES:(j + 1) * LANES]
                out.append(accs[b * nchunk + j] + wv * cv)
        return tuple(out)

    init = tuple(jnp.zeros((SUBLANES, LANES), F32) for _ in range(nb * nchunk))
    accs = lax.fori_loop(0, k_dim // SUBLANES, body, init, unroll=8)
    for b in range(nb):
        row = jnp.concatenate(
            [jnp.sum(accs[b * nchunk + j], axis=0, keepdims=True) for j in range(nchunk)], axis=1)
        o_ref[b:b + 1, :] = row + b_ref[...]


def _ada(c, w, bias):
    nb, k_dim = c.shape
    n = w.shape[1]
    tn = 512
    cb = jnp.broadcast_to(c[:, :, None], (nb, k_dim, LANES))
    return pl.pallas_call(
        _ada_kernel,
        grid=(n // tn,),
        in_specs=[pl.BlockSpec((nb, k_dim, LANES), lambda j: (0, 0, 0)),
                  pl.BlockSpec((k_dim, tn), lambda j: (0, j)),
                  pl.BlockSpec((1, tn), lambda j: (0, j))],
        out_specs=pl.BlockSpec((nb, tn), lambda j: (0, j)),
        out_shape=jax.ShapeDtypeStruct((nb, n), F32),
        compiler_params=_cparams(("arbitrary",)),
        name="ada",
    )(cb, w, bias.reshape(1, n))


def _bf16_pieces(x):
    p1 = x.astype(BF16)
    r = x - p1.astype(F32)
    p2 = r.astype(BF16)
    return p1.astype(F32), p2.astype(F32), r - p2.astype(F32)


def _pick3(sel, x3):
    return jnp.where(sel == 0, x3[0], jnp.where(sel == 1, x3[1], x3[2]))


def _norm_kernel(x_ref, ada_ref, g_ref, pos_ref, inv_ref, h_ref, cos_ref, sin_ref, plane_ref):
    x = x_ref[0]
    ms = jnp.mean(x * x, axis=-1, keepdims=True)
    y = x * lax.rsqrt(ms + EPS) * g_ref[...]
    shift = ada_ref[0, 0:1, :]
    scale = ada_ref[0, 1:2, :]
    h_ref[...] = (y * (1.0 + scale) + shift).astype(BF16)
    ang = pos_ref[...].astype(F32) * inv_ref[...]
    lane = lax.broadcasted_iota(jnp.int32, ang.shape, 1)
    sign = jnp.where((lane % MLA_ROPE) < MLA_ROPE // 2, -1.0, 1.0).astype(F32)
    cos_ref[...] = jnp.cos(ang)
    sin_ref[...] = jnp.sin(ang) * sign
    pos3 = _bf16_pieces(pos_ref[...].astype(F32))
    plane_ref[...] = jnp.where(lane < 9, _pick3(lane % 3, pos3),
                               jnp.where(lane < 18, -_pick3((lane - 9) % 3, pos3), 0.0)).astype(BF16)


def _norm(x, ada3, g_pre, pos_col, inv_tab):
    nb, s, d = x.shape
    ts = 512
    ns = s // ts
    row = lambda b, i: (b * ns + i, 0)
    return pl.pallas_call(
        _norm_kernel,
        grid=(nb, ns),
        in_specs=[pl.BlockSpec((1, ts, d), lambda b, i: (b, i, 0)),
                  pl.BlockSpec((1, 3, d), lambda b, i: (b, 0, 0)),
                  pl.BlockSpec((1, d), lambda b, i: (0, 0)),
                  pl.BlockSpec((ts, 1), row),
                  pl.BlockSpec((1, LANES), lambda b, i: (0, 0))],
        out_specs=[pl.BlockSpec((ts, d), row),
                   pl.BlockSpec((ts, LANES), row),
                   pl.BlockSpec((ts, LANES), row),
                   pl.BlockSpec((ts, LANES), row)],
        out_shape=[jax.ShapeDtypeStruct((nb * s, d), BF16),
                   jax.ShapeDtypeStruct((nb * s, LANES), F32),
                   jax.ShapeDtypeStruct((nb * s, LANES), F32),
                   jax.ShapeDtypeStruct((nb * s, LANES), BF16)],
        compiler_params=_cparams(("arbitrary", "arbitrary")),
        name="prenorm",
    )(x, ada3, g_pre, pos_col, inv_tab)


def _rope_cols(r, cos, sin_signed):
    lane = lax.broadcasted_iota(jnp.int32, r.shape, 1)
    half = MLA_ROPE // 2
    partner = jnp.where((lane % MLA_ROPE) < half,
                        pltpu.roll(r, LANES - half, 1), pltpu.roll(r, half, 1))
    return r * cos + partner * sin_signed


def _qmla_kernel(a_ref, w_ref, cos_ref, sin_ref, o_ref, wb_ref, *, scale):
    hd = MLA_NOPE + MLA_ROPE
    pair_w = 2 * hd
    n_pairs = wb_ref.shape[0] // pair_w

    @pl.when(pl.program_id(1) == 0)
    def _():
        for p in range(n_pairs):
            src, dst = w_ref.at[p * pair_w:(p + 1) * pair_w], wb_ref.at[p * pair_w:(p + 1) * pair_w]
            dst[:MLA_NOPE] = src[:MLA_NOPE].astype(BF16)
            dst[MLA_NOPE:2 * MLA_NOPE] = src[hd:hd + MLA_NOPE].astype(BF16)
            dst[2 * MLA_NOPE:2 * MLA_NOPE + MLA_ROPE] = src[MLA_NOPE:hd].astype(BF16)
            dst[2 * MLA_NOPE + MLA_ROPE:] = src[hd + MLA_NOPE:].astype(BF16)

    acc = lax.dot_general(a_ref[...], wb_ref[...], _NT, preferred_element_type=F32)
    for p in range(n_pairs):
        c0 = p * pair_w
        rr = _rope_cols(acc[:, c0 + 2 * MLA_NOPE:c0 + pair_w], cos_ref[...], sin_ref[...])
        o_ref[:, c0:c0 + 2 * MLA_NOPE] = (acc[:, c0:c0 + 2 * MLA_NOPE] * scale).astype(BF16)
        o_ref[:, c0 + 2 * MLA_NOPE:c0 + pair_w] = (rr * scale).astype(BF16)


def _rest_kernel(a_ref, w_ref, o_ref, wb_ref, *, tn):
    j = pl.program_id(0)

    @pl.when(pl.program_id(1) == 0)
    def _():
        wb_ref[...] = w_ref[...].astype(BF16)

    def tile(epilogue):
        acc = lax.dot_general(a_ref[...], wb_ref[...], _NT, preferred_element_type=F32)
        o_ref[...] = epilogue(acc).astype(BF16)

    @pl.when(j < KD_OFF // tn)
    def _():
        tile(lambda acc: acc * (DIFF_QK ** -0.5 * LOG2E))

    @pl.when(jnp.logical_and(j >= KD_OFF // tn, j < GM_OFF // tn))
    def _():
        tile(lambda acc: acc)

    @pl.when(jnp.logical_and(j >= GM_OFF // tn, j < MGM_OFF // tn))
    def _():
        tile(lambda acc: acc * jax.nn.sigmoid(acc))

    @pl.when(j >= MGM_OFF // tn)
    def _():
        tile(jax.nn.sigmoid)


def _proj(h, w_t, cos_tab, sin_tab):
    m, k_dim = h.shape
    tm = PROJ_TM
    pair_w = 2 * (MLA_NOPE + MLA_ROPE)
    q_scale = (MLA_NOPE + MLA_ROPE) ** -0.5 * LOG2E

    def row_window(rows, offset_fn):
        return pl.BlockSpec((pl.Element(rows), pl.Element(k_dim)),
                            lambda *g: (pl.multiple_of(offset_fn(*g), SUBLANES), 0))

    qtn = QMLA_PAIRS_PER_TILE * pair_w
    qm = pl.pallas_call(
        functools.partial(_qmla_kernel, scale=q_scale),
        grid=(Q_MLA_W // qtn, m // tm),
        in_specs=[pl.BlockSpec((tm, k_dim), lambda j, i: (i, 0)),
                  pl.BlockSpec((qtn, k_dim), lambda j, i: (j, 0)),
                  pl.BlockSpec((tm, LANES), lambda j, i: (i, 0)),
                  pl.BlockSpec((tm, LANES), lambda j, i: (i, 0))],
        out_specs=pl.BlockSpec((tm, qtn), lambda j, i: (i, j)),
        out_shape=jax.ShapeDtypeStruct((m, Q_MLA_W), BF16),
        scratch_shapes=[pltpu.VMEM((qtn, k_dim), BF16)],
        compiler_params=_cparams(("arbitrary", "arbitrary")),
        name="proj_qmla",
    )(h, w_t, cos_tab, sin_tab)
    tn = REST_TN
    rest = pl.pallas_call(
        functools.partial(_rest_kernel, tn=tn),
        grid=(REST_W // tn, m // tm),
        in_specs=[pl.BlockSpec((tm, k_dim), lambda j, i: (i, 0)),
                  row_window(tn, lambda j, i: REST_OFF + tn * j)],
        out_specs=pl.BlockSpec((tm, tn), lambda j, i: (i, j)),
        out_shape=jax.ShapeDtypeStruct((m, REST_W), BF16),
        scratch_shapes=[pltpu.VMEM((tn, k_dim), BF16)],
        compiler_params=_cparams(("arbitrary", "arbitrary")),
        name="proj_rest",
    )(h, w_t)
    return qm, rest


def _kv_kernel(a_ref, wp_ref, g_ref, w_ref, cos_ref, sin_ref, k_ref, v_ref, wpb_ref, wb_ref):
    @pl.when(pl.program_id(0) == 0)
    def _():
        wpb_ref[:KVR_W] = wp_ref[...].astype(BF16)
        wpb_ref[KVR_W:] = jnp.zeros((KVR_PAD - KVR_W, wpb_ref.shape[1]), BF16)
        wb_ref[...] = w_ref[...].astype(BF16)

    kw = MLA_NOPE + MLA_V
    rb = a_ref.shape[0] // EPILOGUE_ROW_SPLIT
    for t in range(EPILOGUE_ROW_SPLIT):
        rows = slice(t * rb, (t + 1) * rb)
        p = lax.dot_general(a_ref[rows, :], wpb_ref[...], _NT, preferred_element_type=F32)
        ckv = p[:, :KV_RANK]
        ms = jnp.mean(ckv * ckv, axis=-1, keepdims=True)
        n = (ckv * lax.rsqrt(ms + EPS) * g_ref[...]).astype(BF16)
        kv = jnp.dot(n, wb_ref[...], preferred_element_type=F32)
        kr_even = _rope_cols(p[:, KV_RANK:], cos_ref[rows, :], sin_ref[rows, :])
        kr_odd = pltpu.roll(kr_even, MLA_ROPE, 1)
        ones_col = jnp.ones(kr_even.shape, BF16)
        for hd in range(MLA_HEADS):
            k_ref[rows, hd * kw:hd * kw + MLA_NOPE] = kv[:, hd * kw:hd * kw + MLA_NOPE].astype(BF16)
            k_ref[rows, hd * kw + MLA_NOPE:(hd + 1) * kw] = (
                kr_even if hd % 2 == 0 else kr_odd).astype(BF16)
            v_ref[rows, hd * kw:hd * kw + MLA_V] = kv[:, hd * kw + MLA_NOPE:(hd + 1) * kw].astype(BF16)
            v_ref[rows, hd * kw + MLA_V:(hd + 1) * kw] = ones_col


def _kv(h, w_t, g_kv, w_ukv, cos_tab, sin_tab):
    m, k_dim = h.shape
    tm = KV_TM
    kw = MLA_HEADS * (MLA_NOPE + MLA_V)
    return pl.pallas_call(
        _kv_kernel,
        grid=(m // tm,),
        in_specs=[pl.BlockSpec((tm, k_dim), lambda i: (i, 0)),
                  pl.BlockSpec((pl.Element(KVR_W), pl.Element(k_dim)),
                               lambda i: (pl.multiple_of(Q_MLA_W + 0 * i, SUBLANES), 0)),
                  pl.BlockSpec((1, KV_RANK), lambda i: (0, 0)),
                  pl.BlockSpec((KV_RANK, kw), lambda i: (0, 0)),
                  pl.BlockSpec((tm, LANES), lambda i: (i, 0)),
                  pl.BlockSpec((tm, LANES), lambda i: (i, 0))],
        out_specs=[pl.BlockSpec((tm, kw), lambda i: (i, 0)),
                   pl.BlockSpec((tm, kw), lambda i: (i, 0))],
        out_shape=[jax.ShapeDtypeStruct((m, kw), BF16),
                   jax.ShapeDtypeStruct((m, kw), BF16)],
        scratch_shapes=[pltpu.VMEM((KVR_PAD, k_dim), BF16), pltpu.VMEM((KV_RANK, kw), BF16)],
        compiler_params=_cparams(("arbitrary",)),
        name="kv_up",
    )(h, w_t, g_kv, w_ukv, cos_tab, sin_tab)


def _flash_pipeline(n_chains, nq, diags, loops, value_fn, finalize_fn,
                    s_ref, acc_ref, mpart_ref, macc_ref, sd_ref=None, mpd_ref=None):
    chunk = s_ref.shape[2]

    def lane_tiles(x, n):
        return jnp.concatenate([x] * n, axis=1)

    def qk_phase(scores, stage):
        s_dst, m_dst = stage
        for ci, s in enumerate(scores):
            s_dst[ci] = s
            part = s[:, :LANES]
            for j in range(1, chunk // LANES):
                part = jnp.maximum(part, s[:, j * LANES:(j + 1) * LANES])
            m_dst[ci] = part

    def pv_phase(kc, stage):
        s_src, m_src = stage
        for ci in range(n_chains):
            m_acc = macc_ref[ci]
            m_run = jnp.maximum(m_acc, jnp.max(m_src[ci], axis=-1, keepdims=True))
            macc_ref[ci] = m_run
            p = jnp.exp2(s_src[ci] - lane_tiles(m_run, chunk // LANES))
            alpha = jnp.exp2(m_acc - m_run)
            pv = jnp.dot(p.astype(BF16), value_fn(ci, kc), preferred_element_type=F32)
            acc_ref[ci] = lane_tiles(alpha, acc_ref.shape[2] // LANES) * acc_ref[ci] + pv

    def reset():
        acc_ref[...] = jnp.zeros_like(acc_ref)
        macc_ref[...] = jnp.full(macc_ref.shape, NEG, F32)

    def tile(qt, stage, next_stage):
        cur = qt
        for trips_fn, score_fn in loops:
            def step(kc, cur, score_fn=score_fn):
                pv_phase(cur, stage)
                qk_phase(score_fn(qt, kc), stage)
                return kc

            cur = lax.fori_loop(0, trips_fn(qt), step, cur)
        nxt = jnp.minimum(qt + 1, nq - 1)

        def transition(_, cur, fn):
            if next_stage is stage:
                pv_phase(cur, stage)
                finalize_fn(qt)
                reset()
                qk_phase(fn(nxt), stage)
            else:
                qk_phase(fn(nxt), next_stage)
                pv_phase(cur, stage)
                finalize_fn(qt)
                reset()
            return cur

        for select_fn, fn in diags:
            if select_fn is None:
                transition(0, cur, fn)
            else:
                lax.fori_loop(0, select_fn(nxt), functools.partial(transition, fn=fn), cur)

    stage_a = (s_ref, mpart_ref)
    reset()
    for select_fn, fn in diags:
        if select_fn is None:
            qk_phase(fn(0), stage_a)
        else:
            lax.fori_loop(0, select_fn(0), lambda _, c, fn=fn: qk_phase(fn(0), stage_a) or c, 0)
    if sd_ref is None:
        lax.fori_loop(0, nq, lambda qt, c: tile(qt, stage_a, stage_a) or c, 0)
    else:
        stage_b = (sd_ref, mpd_ref)

        def tile_pair(j, c):
            tile(2 * j, stage_a, stage_b)
            tile(2 * j + 1, stage_b, stage_a)
            return c

        lax.fori_loop(0, nq // 2, tile_pair, 0)


def _causal_mask(rows, cols, row0):
    row = lax.broadcasted_iota(jnp.int32, (rows, cols), 0) + row0
    col = lax.broadcasted_iota(jnp.int32, (rows, cols), 1)
    return col <= row


def _pad_masked(s, cols):
    if s.shape[1] == cols:
        return s
    return jnp.concatenate([s, jnp.full((s.shape[0], cols - s.shape[1]), NEG, s.dtype)], axis=1)


_NT = (((1,), (1,)), ((), ()))


def _mla_attn_kernel(q_ref, k_ref, v_ref, g_ref, o_ref, acc_ref, qs_ref, s_ref, mpart_ref, macc_ref,
                     *, tq, rs):
    kw = 2 * MLA_NOPE
    tr = tq // rs
    nq = q_ref.shape[0] // tq
    chains = [(hp, r) for hp in range(2) for r in range(rs)]

    def stage_queries(qt):
        rows = pl.ds(pl.multiple_of(qt * tq, tq), tq)
        for hp in range(2):
            qs_ref[hp, :, :MLA_NOPE] = q_ref[rows, hp * MLA_NOPE:(hp + 1) * MLA_NOPE]
            qs_ref[hp, :, MLA_NOPE:] = q_ref[rows, 2 * MLA_NOPE:]

    def scores(qt, kc, diag):
        if diag:
            stage_queries(qt)
        k0 = pl.multiple_of(kc * tq, tq)
        out = []
        for hp, r in chains:
            nk = (r + 1) * tr if diag else tq
            q = qs_ref[hp, r * tr:(r + 1) * tr, :]
            k = k_ref[pl.ds(k0, nk), hp * kw:(hp + 1) * kw]
            s = lax.dot_general(q, k, _NT, preferred_element_type=F32)
            if diag:
                s = _pad_masked(jnp.where(_causal_mask(tr, nk, r * tr), s, NEG), tq)
            out.append(s)
        return out

    def values(ci, kc):
        hp = chains[ci][0]
        return v_ref[pl.ds(pl.multiple_of(kc * tq, tq), tq), hp * kw:(hp + 1) * kw]

    def finalize(qt):
        for ci, (hp, r) in enumerate(chains):
            rows = pl.ds(pl.multiple_of(qt * tq + r * tr, tr), tr)
            acc = acc_ref[ci]
            o = acc[:, :MLA_V] / acc[:, MLA_V:]
            gate = g_ref[rows, hp * MLA_V:(hp + 1) * MLA_V].astype(F32)
            o_ref[rows, hp * MLA_V:(hp + 1) * MLA_V] = (o * gate).astype(BF16)

    _flash_pipeline(len(chains), nq, [(None, lambda qt: scores(qt, qt, True))],
                    [(lambda qt: qt, lambda qt, kc: scores(qt, kc, False))],
                    values, finalize, s_ref, acc_ref, mpart_ref, macc_ref)


def _mla_attn(qm, kk, vv, rest, nb, s):
    tq = MLA_TQ
    pair_w = 2 * (MLA_NOPE + MLA_ROPE)
    kw = 4 * MLA_NOPE
    gate_blk = GM_OFF // (2 * MLA_V)
    rs = MLA_ROW_SPLIT
    return pl.pallas_call(
        functools.partial(_mla_attn_kernel, tq=tq, rs=rs),
        grid=(nb, MLA_HEADS // 2),
        in_specs=[pl.BlockSpec((s, pair_w), lambda b, hh: (b, hh)),
                  pl.BlockSpec((s, kw), lambda b, hh: (b, hh)),
                  pl.BlockSpec((s, kw), lambda b, hh: (b, hh)),
                  pl.BlockSpec((s, 2 * MLA_V), lambda b, hh: (b, gate_blk + hh))],
        out_specs=pl.BlockSpec((s, 2 * MLA_V), lambda b, hh: (b, hh)),
        out_shape=jax.ShapeDtypeStruct((nb * s, MLA_HEADS * MLA_V), BF16),
        scratch_shapes=[pltpu.VMEM((2 * rs, tq // rs, 2 * MLA_V), F32),
                        pltpu.VMEM((2, tq, 2 * MLA_NOPE), BF16),
                        pltpu.VMEM((2 * rs, tq // rs, tq), F32),
                        pltpu.VMEM((2 * rs, tq // rs, LANES), F32),
                        pltpu.VMEM((2 * rs, tq // rs, LANES), F32)],
        compiler_params=_cparams(("arbitrary", "arbitrary")),
        name="mla_attn",
    )(qm, kk, vv, rest)


def _diff_attn_kernel(ord_ref, q_ref, k_ref, v_ref, g_ref, pq_ref, pk_ref, plane_ref, sl_ref, lam_ref,
                      gs_ref, o_ref, vaug_ref, kaug_ref, acc_ref, qf_ref, s_ref, mpart_ref, macc_ref,
                      sd_ref, mpd_ref, *, tq, rs):
    seq = q_ref.shape[0]
    nq = seq // tq
    tr = tq // rs
    chains = [(c, r) for c in range(2) for r in range(rs)]
    lane = lax.broadcasted_iota(jnp.int32, (tq, LANES), 1)
    ones_col = jnp.ones((tq, LANES), BF16)
    slope2 = sl_ref[0, :, 0:1] * LOG2E
    c_pieces = _bf16_pieces(slope2)
    lane_row = lane[:1]
    c_query = jnp.where(lane_row < 9, _pick3(lane_row // 3, c_pieces), 0.0).astype(BF16)
    c_key = jnp.where(jnp.logical_and(lane_row >= 9, lane_row < 18),
                      _pick3((lane_row - 9) // 3, c_pieces), 0.0).astype(BF16)

    def bias_lanes(rows, key_side):
        pos_lanes = plane_ref[rows, :]
        if key_side:
            return jnp.where(lane < 9, pos_lanes, c_key)
        return jnp.where(lane < 9, c_query, pos_lanes)

    def stage_keys(kc, _):
        rows = pl.ds(pl.multiple_of(kc * tq, tq), tq)
        vaug_ref[rows, :DIFF_V] = v_ref[rows, :]
        vaug_ref[rows, DIFF_V:] = ones_col
        kaug_ref[rows, :2 * DIFF_QK] = k_ref[rows, :]
        kaug_ref[rows, 2 * DIFF_QK:] = bias_lanes(rows, True)
        return 0

    lax.fori_loop(0, nq, stage_keys, 0)

    def stage_queries(qt):
        rows = pl.ds(pl.multiple_of(qt * tq, tq), tq)
        q = q_ref[rows, :]
        q_side = bias_lanes(rows, False)
        qf_ref[0, :, :2 * DIFF_QK] = jnp.where(lane < DIFF_QK, q, 0).astype(BF16)
        qf_ref[1, :, :2 * DIFF_QK] = jnp.where(lane >= DIFF_QK, q, 0).astype(BF16)
        qf_ref[0, :, 2 * DIFF_QK:] = q_side
        qf_ref[1, :, 2 * DIFF_QK:] = q_side

    lq = lam_ref[...]
    lam = (jnp.exp(jnp.sum(lq[0:1] * lq[1:2], axis=-1, keepdims=True))
           - jnp.exp(jnp.sum(lq[2:3] * lq[3:4], axis=-1, keepdims=True)) + LAMBDA_INIT)

    def scores(qt, kc, diag):
        k0 = pl.multiple_of(kc * tq, tq)
        k = k_ref[pl.ds(k0, tq), :]
        pk = slope2 * pk_ref[0, pl.ds(kc, 1), :]
        out = [None] * len(chains)
        for r in range(rs):
            pq = pq_ref[pl.ds(pl.multiple_of(qt * tq + r * tr, tr), tr), :]
            bias = jnp.abs(slope2 * pq - pk)
            mask = _causal_mask(tr, tq, r * tr) if diag else None
            for c in range(2):
                q = qf_ref[c, r * tr:(r + 1) * tr, :2 * DIFF_QK]
                s = lax.dot_general(q, k, _NT, preferred_element_type=F32) - bias
                out[chains.index((c, r))] = jnp.where(mask, s, NEG) if diag else s
        return out

    def scores_diag(qt):
        stage_queries(qt)
        return scores(qt, qt, True)

    def scores_ordered(qt, kc):
        k = kaug_ref[pl.ds(pl.multiple_of(kc * tq, tq), tq), :]
        return [lax.dot_general(qf_ref[c, r * tr:(r + 1) * tr, :], k, _NT, preferred_element_type=F32)
                for c, r in chains]

    def scores_diag_sorted(qt):
        stage_queries(qt)
        k0 = pl.multiple_of(qt * tq, tq)
        out = []
        for c, r in chains:
            nk = (r + 1) * tr
            s = lax.dot_general(qf_ref[c, r * tr:(r + 1) * tr, :], kaug_ref[pl.ds(k0, nk), :], _NT,
                                preferred_element_type=F32)
            out.append(_pad_masked(jnp.where(_causal_mask(tr, nk, r * tr), s, NEG), tq))
        return out

    batch = pl.program_id(0)

    def trips_ordered(qt):
        return jnp.where((ord_ref[batch, qt] & 1) != 0, qt, 0)

    def tile_sorted(qt):
        return (ord_ref[batch, qt] >> 1) & 1

    def values(ci, kc):
        return vaug_ref[pl.ds(pl.multiple_of(kc * tq, tq), tq), :]

    def finalize(qt):
        for r in range(rs):
            rows = pl.ds(pl.multiple_of(qt * tq + r * tr, tr), tr)
            a1 = acc_ref[chains.index((0, r))]
            a2 = acc_ref[chains.index((1, r))]
            o = a1[:, :DIFF_V] / a1[:, DIFF_V:] - lam * (a2[:, :DIFF_V] / a2[:, DIFF_V:])
            ms_o = jnp.mean(o * o, axis=-1, keepdims=True)
            o = o * lax.rsqrt(ms_o + EPS) * gs_ref[...] * (1.0 - LAMBDA_INIT)
            o_ref[rows, :] = (o * g_ref[rows, :].astype(F32)).astype(BF16)

    _flash_pipeline(len(chains), nq,
                    [(tile_sorted, scores_diag_sorted), (lambda qt: 1 - tile_sorted(qt), scores_diag)],
                    [(trips_ordered, scores_ordered),
                     (lambda qt: qt - trips_ordered(qt), lambda qt, kc: scores(qt, kc, False))],
                    values, finalize, s_ref, acc_ref, mpart_ref, macc_ref, sd_ref, mpd_ref)


def _diff_attn(rest, pos_col, pos_row, pos_lanes, ordered, slopes, lam_par, g_subln, nb, s):
    tq = ATTN_TQ
    nq = s // tq
    rs = DIFF_ROW_SPLIT
    tr = tq // rs
    hw = DIFF_V
    return pl.pallas_call(
        functools.partial(_diff_attn_kernel, tq=tq, rs=rs),
        grid=(nb, DIFF_HEADS),
        in_specs=[pl.BlockSpec(memory_space=pltpu.SMEM),
                  pl.BlockSpec((s, hw), lambda b, hd: (b, QD_OFF // hw + hd)),
                  pl.BlockSpec((s, hw), lambda b, hd: (b, KD_OFF // hw + hd)),
                  pl.BlockSpec((s, hw), lambda b, hd: (b, VD_OFF // hw + hd)),
                  pl.BlockSpec((s, hw), lambda b, hd: (b, GD_OFF // hw + hd)),
                  pl.BlockSpec((s, 1), lambda b, hd: (b, 0)),
                  pl.BlockSpec((1, nq, tq), lambda b, hd: (b, 0, 0)),
                  pl.BlockSpec((s, LANES), lambda b, hd: (b, 0)),
                  pl.BlockSpec((1, 1, LANES), lambda b, hd: (hd, 0, 0)),
                  pl.BlockSpec((4, DIFF_QK), lambda b, hd: (0, 0)),
                  pl.BlockSpec((1, DIFF_V), lambda b, hd: (0, 0))],
        out_specs=pl.BlockSpec((s, hw), lambda b, hd: (b, hd)),
        out_shape=jax.ShapeDtypeStruct((nb * s, DIFF_HEADS * DIFF_V), BF16),
        scratch_shapes=[pltpu.VMEM((s, 2 * DIFF_V), BF16),
                        pltpu.VMEM((s, 4 * DIFF_QK), BF16),
                        pltpu.VMEM((2 * rs, tr, 2 * DIFF_V), F32),
                        pltpu.VMEM((2, tq, 4 * DIFF_QK), BF16),
                        pltpu.VMEM((2 * rs, tr, tq), F32),
                        pltpu.VMEM((2 * rs, tr, LANES), F32),
                        pltpu.VMEM((2 * rs, tr, LANES), F32),
                        pltpu.VMEM((2 * rs, tr, tq), F32),
                        pltpu.VMEM((2 * rs, tr, LANES), F32)],
        compiler_params=_cparams(("arbitrary", "arbitrary")),
        name="diff_attn",
    )(ordered, rest, rest, rest, rest, pos_col, pos_row, pos_lanes, slopes, lam_par, g_subln)


def _merge_kernel(a1_ref, a2_ref, w1_ref, w2_ref, s1_ref, s2_ref, o_ref, wb_ref):
    @pl.when(pl.program_id(1) == 0)
    def _():
        wb_ref[0] = w1_ref[...].astype(BF16)
        wb_ref[1] = w2_ref[...].astype(BF16)

    rb = a1_ref.shape[0] // EPILOGUE_ROW_SPLIT
    for t in range(EPILOGUE_ROW_SPLIT):
        rows = slice(t * rb, (t + 1) * rb)
        y1 = jnp.dot(a1_ref[rows, :], wb_ref[0], preferred_element_type=F32)
        y2 = jnp.dot(a2_ref[rows, :], wb_ref[1], preferred_element_type=F32)
        o_ref[rows, :] = (s1_ref[rows, :].astype(F32) * y1
                          + s2_ref[rows, :].astype(F32) * y2).astype(BF16)


def _merge(og_mla, og_diff, w1, w2, rest):
    m, k_dim = og_mla.shape
    n = w1.shape[1]
    tm, tn = MERGE_TM, MERGE_TN
    return pl.pallas_call(
        _merge_kernel,
        grid=(n // tn, m // tm),
        in_specs=[pl.BlockSpec((tm, k_dim), lambda j, i: (i, 0)),
                  pl.BlockSpec((tm, k_dim), lambda j, i: (i, 0)),
                  pl.BlockSpec((k_dim, tn), lambda j, i: (0, j)),
                  pl.BlockSpec((k_dim, tn), lambda j, i: (0, j)),
                  pl.BlockSpec((tm, tn), lambda j, i: (i, MGM_OFF // tn + j)),
                  pl.BlockSpec((tm, tn), lambda j, i: (i, MGD_OFF // tn + j))],
        out_specs=pl.BlockSpec((tm, tn), lambda j, i: (i, j)),
        out_shape=jax.ShapeDtypeStruct((m, n), BF16),
        scratch_shapes=[pltpu.VMEM((2, k_dim, tn), BF16)],
        compiler_params=_cparams(("arbitrary", "arbitrary")),
        name="merge",
    )(og_mla, og_diff, w1, w2, rest, rest)


def _out_kernel(a_ref, w_ref, x_ref, ada_ref, g_ref, o_ref, wb_ref):
    @pl.when(jnp.logical_and(pl.program_id(0) == 0, pl.program_id(1) == 0))
    def _():
        wb_ref[...] = w_ref[...].astype(BF16)

    rb = a_ref.shape[0] // EPILOGUE_ROW_SPLIT
    for t in range(EPILOGUE_ROW_SPLIT):
        rows = slice(t * rb, (t + 1) * rb)
        y = jnp.dot(a_ref[rows, :], wb_ref[...], preferred_element_type=F32)
        ms = jnp.mean(y * y, axis=-1, keepdims=True)
        yn = y * lax.rsqrt(ms + EPS) * g_ref[...]
        o_ref[0, rows, :] = x_ref[0, rows, :] + ada_ref[0, 2:3, :] * yn


def _out(merged, w_out, x, ada3, g_post):
    nb, s, d = x.shape
    tm = OUT_TM
    ns = s // tm
    return pl.pallas_call(
        _out_kernel,
        grid=(nb, ns),
        in_specs=[pl.BlockSpec((tm, d), lambda b, i: (b * ns + i, 0)),
                  pl.BlockSpec((d, d), lambda b, i: (0, 0), pipeline_mode=pl.Buffered(1)),
                  pl.BlockSpec((1, tm, d), lambda b, i: (b, i, 0)),
                  pl.BlockSpec((1, 3, d), lambda b, i: (b, 0, 0)),
                  pl.BlockSpec((1, d), lambda b, i: (0, 0))],
        out_specs=pl.BlockSpec((1, tm, d), lambda b, i: (b, i, 0)),
        out_shape=jax.ShapeDtypeStruct((nb, s, d), F32),
        scratch_shapes=[pltpu.VMEM((d, d), BF16)],
        compiler_params=_cparams(("arbitrary", "arbitrary")),
        name="out_proj",
    )(merged, w_out, x, ada3, g_post)


def kernel(x, c, positions, w_ada, b_ada, g_pre, w_in, g_kv, w_ukv, lambda_q1, lambda_k1,
           lambda_q2, lambda_k2, g_subln, w_o_mla, w_o_diff, w_out, g_post):
    nb, s, d = x.shape
    depth = w_in.shape[0]
    half = MLA_ROPE // 2
    inv = ROPE_THETA ** (-jnp.arange(half, dtype=F32) / half)
    inv_tab = jnp.tile(inv, LANES // half).reshape(1, LANES)
    slopes = 2.0 ** (-8.0 * jnp.arange(1, DIFF_HEADS + 1, dtype=F32) / DIFF_HEADS)
    slopes = jnp.broadcast_to(slopes.reshape(DIFF_HEADS, 1, 1), (DIFF_HEADS, 1, LANES))
    pos_col = positions.reshape(nb * s, 1)
    pos_colf = pos_col.astype(F32)
    pos_chunks = positions.reshape(nb, s // ATTN_TQ, ATTN_TQ)
    pos_row = pos_chunks.astype(F32)
    run_max = lax.cummax(pos_chunks.max(axis=-1), axis=1)
    prev_max = jnp.concatenate(
        [jnp.full((nb, 1), jnp.iinfo(jnp.int32).min, jnp.int32), run_max[:, :-1]], axis=1)
    tile_sorted = jnp.all(pos_chunks[..., 1:] >= pos_chunks[..., :-1], axis=-1)
    ordered = ((prev_max <= pos_chunks.min(axis=-1)).astype(jnp.int32)
               + 2 * tile_sorted.astype(jnp.int32))

    for l in range(depth):
        ada3 = _ada(c, w_ada[l], b_ada[l]).reshape(nb, 3, d)
        h, cos_tab, sin_tab, pos_lanes = _norm(x, ada3, g_pre[l].reshape(1, d), pos_col, inv_tab)
        w_t = jnp.swapaxes(w_in[l], 0, 1)
        qm, rest = _proj(h, w_t, cos_tab, sin_tab)
        kk, vv = _kv(h, w_t, g_kv[l].reshape(1, KV_RANK), w_ukv[l], cos_tab, sin_tab)
        og_mla = _mla_attn(qm, kk, vv, rest, nb, s)
        lam_par = jnp.stack([lambda_q1[l], lambda_k1[l], lambda_q2[l], lambda_k2[l]]).astype(F32)
        og_diff = _diff_attn(rest, pos_colf, pos_row, pos_lanes, ordered, slopes, lam_par,
                             g_subln[l].reshape(1, DIFF_V), nb, s)
        merged = _merge(og_mla, og_diff, w_o_mla[l], w_o_diff[l], rest)
        x = _out(merged, w_out[l], x, ada3, g_post[l].reshape(1, d))
    return x
```

```python
import functools
import math

import jax
import jax.numpy as jnp
from jax import lax
from jax.experimental import pallas as pl
from jax.experimental.pallas import tpu as pltpu

F32 = jnp.float32
BF16 = jnp.bfloat16

D_MODEL = 2048
MLA_HEADS = 8
MLA_NOPE = 128
MLA_ROPE = 64
MLA_V = 128
KV_RANK = 512
ROPE_THETA = 10000.0
DIFF_HEADS = 8
DIFF_QK = 64
DIFF_V = 128
EPS = 1e-6
NEG = -1e30
LAMBDA_INIT = 0.8 - 0.6 * math.exp(-0.3 * 0)

LANES = 128
SUBLANES = 8
VMEM_LIMIT = 56 * 1024 * 1024
ATTN_TQ = 1024
DIFF_ROW_SPLIT = 2
MLA_TQ = 1024
MLA_ROW_SPLIT = 2
PROJ_TM = 1024
REST_TN = 1024
QMLA_PAIRS_PER_TILE = 2
KV_TM = 1024
CAST_STEPS = 4
OUT_TM = 512
EPILOGUE_ROW_SPLIT = 2
LOG2E = math.log2(math.e)

Q_MLA_W = MLA_HEADS * (MLA_NOPE + MLA_ROPE)
KVR_W = KV_RANK + MLA_ROPE
KVR_PAD = 640
REST_OFF = Q_MLA_W + KVR_W
REST_SHIFT = REST_OFF % LANES
QD_OFF, KD_OFF, VD_OFF, GM_OFF, GD_OFF, MGM_OFF, MGD_OFF = 0, 1024, 2048, 3072, 4096, 5120, 7168
REST_W = 9216


def _cparams(sem):
    return pltpu.CompilerParams(dimension_semantics=sem, vmem_limit_bytes=VMEM_LIMIT)


def _ada_kernel(cb_ref, w_ref, b_ref, o_ref):
    k_dim, tn = w_ref.shape
    nb = cb_ref.shape[0]
    nchunk = tn // LANES

    def body(i, accs):
        k0 = pl.multiple_of(i * SUBLANES, SUBLANES)
        out = []
        for b in range(nb):
            cv = cb_ref[b, pl.ds(k0, SUBLANES), :]
            for j in range(nchunk):
                wv = w_ref[pl.ds(k0, SUBLANES), j * LANES:(j + 1) * LANES]
                out.append(accs[b * nchunk + j] + wv * cv)
        return tuple(out)

    init = tuple(jnp.zeros((SUBLANES, LANES), F32) for _ in range(nb * nchunk))
    accs = lax.fori_loop(0, k_dim // SUBLANES, body, init, unroll=8)
    for b in range(nb):
        row = jnp.concatenate(
            [jnp.sum(accs[b * nchunk + j], axis=0, keepdims=True) for j in range(nchunk)], axis=1)
        o_ref[b:b + 1, :] = row + b_ref[...]


def _ada(c, w, bias):
    nb, k_dim = c.shape
    n = w.shape[1]
    tn = 512
    cb = jnp.broadcast_to(c[:, :, None], (nb, k_dim, LANES))
    return pl.pallas_call(
        _ada_kernel,
        grid=(n // tn,),
        in_specs=[pl.BlockSpec((nb, k_dim, LANES), lambda j: (0, 0, 0)),
                  pl.BlockSpec((k_dim, tn), lambda j: (0, j)),
                  pl.BlockSpec((1, tn), lambda j: (0, j))],
        out_specs=pl.BlockSpec((nb, tn), lambda j: (0, j)),
        out_shape=jax.ShapeDtypeStruct((nb, n), F32),
        compiler_params=_cparams(("arbitrary",)),
        name="ada",
    )(cb, w, bias.reshape(1, n))


def _bf16_pieces(x):
    p1 = x.astype(BF16)
    r = x - p1.astype(F32)
    p2 = r.astype(BF16)
    return p1.astype(F32), p2.astype(F32), r - p2.astype(F32)


def _pick3(sel, x3):
    return jnp.where(sel == 0, x3[0], jnp.where(sel == 1, x3[1], x3[2]))


def _norm_kernel(x_ref, ada_ref, g_ref, pos_ref, inv_ref, h_ref, cos_ref, sin_ref, plane_ref):
    x = x_ref[0]
    ms = jnp.mean(x * x, axis=-1, keepdims=True)
    y = x * lax.rsqrt(ms + EPS) * g_ref[...]
    shift = ada_ref[0, 0:1, :]
    scale = ada_ref[0, 1:2, :]
    h_ref[...] = (y * (1.0 + scale) + shift).astype(BF16)
    ang = pos_ref[...].astype(F32) * inv_ref[...]
    lane = lax.broadcasted_iota(jnp.int32, ang.shape, 1)
    sign = jnp.where((lane % MLA_ROPE) < MLA_ROPE // 2, -1.0, 1.0).astype(F32)
    cos_ref[...] = jnp.cos(ang)
    sin_ref[...] = jnp.sin(ang) * sign
    pos3 = _bf16_pieces(pos_ref[...].astype(F32))
    plane_ref[...] = jnp.where(lane < 9, _pick3(lane % 3, pos3),
                               jnp.where(lane < 18, -_pick3((lane - 9) % 3, pos3), 0.0)).astype(BF16)


def _norm(x, ada3, g_pre, pos_col, inv_tab):
    nb, s, d = x.shape
    ts = 512
    ns = s // ts
    row = lambda b, i: (b * ns + i, 0)
    return pl.pallas_call(
        _norm_kernel,
        grid=(nb, ns),
        in_specs=[pl.BlockSpec((1, ts, d), lambda b, i: (b, i, 0)),
                  pl.BlockSpec((1, 3, d), lambda b, i: (b, 0, 0)),
                  pl.BlockSpec((1, d), lambda b, i: (0, 0)),
                  pl.BlockSpec((ts, 1), row),
                  pl.BlockSpec((1, LANES), lambda b, i: (0, 0))],
        out_specs=[pl.BlockSpec((ts, d), row),
                   pl.BlockSpec((ts, LANES), row),
                   pl.BlockSpec((ts, LANES), row),
                   pl.BlockSpec((ts, LANES), row)],
        out_shape=[jax.ShapeDtypeStruct((nb * s, d), BF16),
                   jax.ShapeDtypeStruct((nb * s, LANES), F32),
                   jax.ShapeDtypeStruct((nb * s, LANES), F32),
                   jax.ShapeDtypeStruct((nb * s, LANES), BF16)],
        compiler_params=_cparams(("arbitrary", "arbitrary")),
        name="prenorm",
    )(x, ada3, g_pre, pos_col, inv_tab)


def _rope_cols(r, cos, sin_signed):
    lane = lax.broadcasted_iota(jnp.int32, r.shape, 1)
    half = MLA_ROPE // 2
    partner = jnp.where((lane % MLA_ROPE) < half,
                        pltpu.roll(r, LANES - half, 1), pltpu.roll(r, half, 1))
    return r * cos + partner * sin_signed


def _qmla_kernel(a_ref, w_ref, cos_ref, sin_ref, o_ref, wb_ref, *, scale):
    hd = MLA_NOPE + MLA_ROPE
    pair_w = 2 * hd
    n_pairs = wb_ref.shape[0] // pair_w

    @pl.when(pl.program_id(1) == 0)
    def _():
        for p in range(n_pairs):
            src, dst = w_ref.at[p * pair_w:(p + 1) * pair_w], wb_ref.at[p * pair_w:(p + 1) * pair_w]
            dst[:MLA_NOPE] = src[:MLA_NOPE].astype(BF16)
            dst[MLA_NOPE:2 * MLA_NOPE] = src[hd:hd + MLA_NOPE].astype(BF16)
            dst[2 * MLA_NOPE:2 * MLA_NOPE + MLA_ROPE] = src[MLA_NOPE:hd].astype(BF16)
            dst[2 * MLA_NOPE + MLA_ROPE:] = src[hd + MLA_NOPE:].astype(BF16)

    acc = lax.dot_general(a_ref[...], wb_ref[...], _NT, preferred_element_type=F32)
    for p in range(n_pairs):
        c0 = p * pair_w
        rr = _rope_cols(acc[:, c0 + 2 * MLA_NOPE:c0 + pair_w], cos_ref[...], sin_ref[...])
        o_ref[:, c0:c0 + 2 * MLA_NOPE] = (acc[:, c0:c0 + 2 * MLA_NOPE] * scale).astype(BF16)
        o_ref[:, c0 + 2 * MLA_NOPE:c0 + pair_w] = (rr * scale).astype(BF16)


def _rest_kernel(a_ref, w_ref, o_ref, wb_ref, *, tn):
    j = pl.program_id(0)

    @pl.when(pl.program_id(1) == 0)
    def _():
        wb_ref[...] = w_ref[...].astype(BF16)

    def tile(epilogue):
        acc = lax.dot_general(a_ref[...], wb_ref[...], _NT, preferred_element_type=F32)
        o_ref[...] = epilogue(acc).astype(BF16)

    @pl.when(j < KD_OFF // tn)
    def _():
        tile(lambda acc: acc * (DIFF_QK ** -0.5 * LOG2E))

    @pl.when(jnp.logical_and(j >= KD_OFF // tn, j < GM_OFF // tn))
    def _():
        tile(lambda acc: acc)

    @pl.when(jnp.logical_and(j >= GM_OFF // tn, j < MGM_OFF // tn))
    def _():
        tile(lambda acc: acc * jax.nn.sigmoid(acc))

    @pl.when(j >= MGM_OFF // tn)
    def _():
        tile(jax.nn.sigmoid)


def _proj(h, w_t, cos_tab, sin_tab):
    m, k_dim = h.shape
    tm = PROJ_TM
    pair_w = 2 * (MLA_NOPE + MLA_ROPE)
    q_scale = (MLA_NOPE + MLA_ROPE) ** -0.5 * LOG2E

    def row_window(rows, offset_fn):
        return pl.BlockSpec((pl.Element(rows), pl.Element(k_dim)),
                            lambda *g: (pl.multiple_of(offset_fn(*g), SUBLANES), 0))

    qtn = QMLA_PAIRS_PER_TILE * pair_w
    qm = pl.pallas_call(
        functools.partial(_qmla_kernel, scale=q_scale),
        grid=(Q_MLA_W // qtn, m // tm),
        in_specs=[pl.BlockSpec((tm, k_dim), lambda j, i: (i, 0)),
                  pl.BlockSpec((qtn, k_dim), lambda j, i: (j, 0)),
                  pl.BlockSpec((tm, LANES), lambda j, i: (i, 0)),
                  pl.BlockSpec((tm, LANES), lambda j, i: (i, 0))],
        out_specs=pl.BlockSpec((tm, qtn), lambda j, i: (i, j)),
        out_shape=jax.ShapeDtypeStruct((m, Q_MLA_W), BF16),
        scratch_shapes=[pltpu.VMEM((qtn, k_dim), BF16)],
        compiler_params=_cparams(("arbitrary", "arbitrary")),
        name="proj_qmla",
    )(h, w_t, cos_tab, sin_tab)
    tn = REST_TN
    rest = pl.pallas_call(
        functools.partial(_rest_kernel, tn=tn),
        grid=(REST_W // tn, m // tm),
        in_specs=[pl.BlockSpec((tm, k_dim), lambda j, i: (i, 0)),
                  row_window(tn, lambda j, i: REST_OFF + tn * j)],
        out_specs=pl.BlockSpec((tm, tn), lambda j, i: (i, j)),
        out_shape=jax.ShapeDtypeStruct((m, REST_W), BF16),
        scratch_shapes=[pltpu.VMEM((tn, k_dim), BF16)],
        compiler_params=_cparams(("arbitrary", "arbitrary")),
        name="proj_rest",
    )(h, w_t)
    return qm, rest


def _kv_kernel(a_ref, wp_ref, g_ref, w_ref, cos_ref, sin_ref, k_ref, v_ref, wpb_ref, wb_ref):
    @pl.when(pl.program_id(0) == 0)
    def _():
        wpb_ref[:KVR_W] = wp_ref[...].astype(BF16)
        wpb_ref[KVR_W:] = jnp.zeros((KVR_PAD - KVR_W, wpb_ref.shape[1]), BF16)
        wb_ref[...] = w_ref[...].astype(BF16)

    kw = MLA_NOPE + MLA_V
    rb = a_ref.shape[0] // EPILOGUE_ROW_SPLIT
    for t in range(EPILOGUE_ROW_SPLIT):
        rows = slice(t * rb, (t + 1) * rb)
        p = lax.dot_general(a_ref[rows, :], wpb_ref[...], _NT, preferred_element_type=F32)
        ckv = p[:, :KV_RANK]
        ms = jnp.mean(ckv * ckv, axis=-1, keepdims=True)
        n = (ckv * lax.rsqrt(ms + EPS) * g_ref[...]).astype(BF16)
        kv = jnp.dot(n, wb_ref[...], preferred_element_type=F32)
        kr_even = _rope_cols(p[:, KV_RANK:], cos_ref[rows, :], sin_ref[rows, :])
        kr_odd = pltpu.roll(kr_even, MLA_ROPE, 1)
        ones_col = jnp.ones(kr_even.shape, BF16)
        for hd in range(MLA_HEADS):
            k_ref[rows, hd * kw:hd * kw + MLA_NOPE] = kv[:, hd * kw:hd * kw + MLA_NOPE].astype(BF16)
            k_ref[rows, hd * kw + MLA_NOPE:(hd + 1) * kw] = (
                kr_even if hd % 2 == 0 else kr_odd).astype(BF16)
            v_ref[rows, hd * kw:hd * kw + MLA_V] = kv[:, hd * kw + MLA_NOPE:(hd + 1) * kw].astype(BF16)
            v_ref[rows, hd * kw + MLA_V:(hd + 1) * kw] = ones_col


def _kv(h, w_t, g_kv, w_ukv, cos_tab, sin_tab):
    m, k_dim = h.shape
    tm = KV_TM
    kw = MLA_HEADS * (MLA_NOPE + MLA_V)
    return pl.pallas_call(
        _kv_kernel,
        grid=(m // tm,),
        in_specs=[pl.BlockSpec((tm, k_dim), lambda i: (i, 0)),
                  pl.BlockSpec((pl.Element(KVR_W), pl.Element(k_dim)),
                               lambda i: (pl.multiple_of(Q_MLA_W + 0 * i, SUBLANES), 0)),
                  pl.BlockSpec((1, KV_RANK), lambda i: (0, 0)),
                  pl.BlockSpec((KV_RANK, kw), lambda i: (0, 0)),
                  pl.BlockSpec((tm, LANES), lambda i: (i, 0)),
                  pl.BlockSpec((tm, LANES), lambda i: (i, 0))],
        out_specs=[pl.BlockSpec((tm, kw), lambda i: (i, 0)),
                   pl.BlockSpec((tm, kw), lambda i: (i, 0))],
        out_shape=[jax.ShapeDtypeStruct((m, kw), BF16),
                   jax.ShapeDtypeStruct((m, kw), BF16)],
        scratch_shapes=[pltpu.VMEM((KVR_PAD, k_dim), BF16), pltpu.VMEM((KV_RANK, kw), BF16)],
        compiler_params=_cparams(("arbitrary",)),
        name="kv_up",
    )(h, w_t, g_kv, w_ukv, cos_tab, sin_tab)


def _flash_pipeline(n_chains, nq, diags, loops, value_fn, finalize_fn,
                    s_ref, acc_ref, mpart_ref, macc_ref, sd_ref=None, mpd_ref=None):
    chunk = s_ref.shape[2]

    def lane_tiles(x, n):
        return jnp.concatenate([x] * n, axis=1)

    def qk_phase(scores, stage):
        s_dst, m_dst = stage
        for ci, s in enumerate(scores):
            s_dst[ci] = s
            part = s[:, :LANES]
            for j in range(1, chunk // LANES):
                part = jnp.maximum(part, s[:, j * LANES:(j + 1) * LANES])
            m_dst[ci] = part

    def pv_phase(kc, stage):
        s_src, m_src = stage
        for ci in range(n_chains):
            m_acc = macc_ref[ci]
            m_run = jnp.maximum(m_acc, jnp.max(m_src[ci], axis=-1, keepdims=True))
            macc_ref[ci] = m_run
            p = jnp.exp2(s_src[ci] - lane_tiles(m_run, chunk // LANES))
            alpha = jnp.exp2(m_acc - m_run)
            pv = jnp.dot(p.astype(BF16), value_fn(ci, kc), preferred_element_type=F32)
            acc_ref[ci] = lane_tiles(alpha, acc_ref.shape[2] // LANES) * acc_ref[ci] + pv

    def reset():
        acc_ref[...] = jnp.zeros_like(acc_ref)
        macc_ref[...] = jnp.full(macc_ref.shape, NEG, F32)

    def tile(qt, stage, next_stage):
        cur = qt
        for trips_fn, score_fn in loops:
            def step(kc, cur, score_fn=score_fn):
                pv_phase(cur, stage)
                qk_phase(score_fn(qt, kc), stage)
                return kc

            cur = lax.fori_loop(0, trips_fn(qt), step, cur)
        nxt = jnp.minimum(qt + 1, nq - 1)

        def transition(_, cur, fn):
            if next_stage is stage:
                pv_phase(cur, stage)
                finalize_fn(qt)
                reset()
                qk_phase(fn(nxt), stage)
            else:
                qk_phase(fn(nxt), next_stage)
                pv_phase(cur, stage)
                finalize_fn(qt)
                reset()
            return cur

        for select_fn, fn in diags:
            if select_fn is None:
                transition(0, cur, fn)
            else:
                lax.fori_loop(0, select_fn(nxt), functools.partial(transition, fn=fn), cur)

    stage_a = (s_ref, mpart_ref)
    reset()
    for select_fn, fn in diags:
        if select_fn is None:
            qk_phase(fn(0), stage_a)
        else:
            lax.fori_loop(0, select_fn(0), lambda _, c, fn=fn: qk_phase(fn(0), stage_a) or c, 0)
    if sd_ref is None:
        lax.fori_loop(0, nq, lambda qt, c: tile(qt, stage_a, stage_a) or c, 0)
    else:
        stage_b = (sd_ref, mpd_ref)

        def tile_pair(j, c):
            tile(2 * j, stage_a, stage_b)
            tile(2 * j + 1, stage_b, stage_a)
            return c

        lax.fori_loop(0, nq // 2, tile_pair, 0)


def _causal_mask(rows, cols, row0):
    row = lax.broadcasted_iota(jnp.int32, (rows, cols), 0) + row0
    col = lax.broadcasted_iota(jnp.int32, (rows, cols), 1)
    return col <= row


def _pad_masked(s, cols):
    if s.shape[1] == cols:
        return s
    return jnp.concatenate([s, jnp.full((s.shape[0], cols - s.shape[1]), NEG, s.dtype)], axis=1)


_NT = (((1,), (1,)), ((), ()))


def _mla_attn_kernel(q_ref, k_ref, v_ref, g_ref, o_ref, acc_ref, qs_ref, s_ref, mpart_ref, macc_ref,
                     *, tq, rs):
    kw = 2 * MLA_NOPE
    tr = tq // rs
    nq = q_ref.shape[0] // tq
    chains = [(hp, r) for hp in range(2) for r in range(rs)]

    def stage_queries(qt):
        rows = pl.ds(pl.multiple_of(qt * tq, tq), tq)
        for hp in range(2):
            qs_ref[hp, :, :MLA_NOPE] = q_ref[rows, hp * MLA_NOPE:(hp + 1) * MLA_NOPE]
            qs_ref[hp, :, MLA_NOPE:] = q_ref[rows, 2 * MLA_NOPE:]

    def scores(qt, kc, diag):
        if diag:
            stage_queries(qt)
        k0 = pl.multiple_of(kc * tq, tq)
        out = []
        for hp, r in chains:
            nk = (r + 1) * tr if diag else tq
            q = qs_ref[hp, r * tr:(r + 1) * tr, :]
            k = k_ref[pl.ds(k0, nk), hp * kw:(hp + 1) * kw]
            s = lax.dot_general(q, k, _NT, preferred_element_type=F32)
            if diag:
                s = _pad_masked(jnp.where(_causal_mask(tr, nk, r * tr), s, NEG), tq)
            out.append(s)
        return out

    def values(ci, kc):
        hp = chains[ci][0]
        return v_ref[pl.ds(pl.multiple_of(kc * tq, tq), tq), hp * kw:(hp + 1) * kw]

    def finalize(qt):
        for ci, (hp, r) in enumerate(chains):
            rows = pl.ds(pl.multiple_of(qt * tq + r * tr, tr), tr)
            acc = acc_ref[ci]
            o = acc[:, :MLA_V] / acc[:, MLA_V:]
            gate = g_ref[rows, hp * MLA_V:(hp + 1) * MLA_V].astype(F32)
            o_ref[rows, hp * MLA_V:(hp + 1) * MLA_V] = (o * gate).astype(BF16)

    _flash_pipeline(len(chains), nq, [(None, lambda qt: scores(qt, qt, True))],
                    [(lambda qt: qt, lambda qt, kc: scores(qt, kc, False))],
                    values, finalize, s_ref, acc_ref, mpart_ref, macc_ref)


def _mla_attn(qm, kk, vv, rest, nb, s):
    tq = MLA_TQ
    pair_w = 2 * (MLA_NOPE + MLA_ROPE)
    kw = 4 * MLA_NOPE
    gate_blk = GM_OFF // (2 * MLA_V)
    rs = MLA_ROW_SPLIT
    return pl.pallas_call(
        functools.partial(_mla_attn_kernel, tq=tq, rs=rs),
        grid=(nb, MLA_HEADS // 2),
        in_specs=[pl.BlockSpec((s, pair_w), lambda b, hh: (b, hh)),
                  pl.BlockSpec((s, kw), lambda b, hh: (b, hh)),
                  pl.BlockSpec((s, kw), lambda b, hh: (b, hh)),
                  pl.BlockSpec((s, 2 * MLA_V), lambda b, hh: (b, gate_blk + hh))],
        out_specs=pl.BlockSpec((s, 2 * MLA_V), lambda b, hh: (b, hh)),
        out_shape=jax.ShapeDtypeStruct((nb * s, MLA_HEADS * MLA_V), BF16),
        scratch_shapes=[pltpu.VMEM((2 * rs, tq // rs, 2 * MLA_V), F32),
                        pltpu.VMEM((2, tq, 2 * MLA_NOPE), BF16),
                        pltpu.VMEM((2 * rs, tq // rs, tq), F32),
                        pltpu.VMEM((2 * rs, tq // rs, LANES), F32),
                        pltpu.VMEM((2 * rs, tq // rs, LANES), F32)],
        compiler_params=_cparams(("arbitrary", "arbitrary")),
        name="mla_attn",
    )(qm, kk, vv, rest)


def _diff_attn_kernel(ord_ref, q_ref, k_ref, v_ref, g_ref, pq_ref, pk_ref, plane_ref, sl_ref, lam_ref,
                      gs_ref, o_ref, vaug_ref, kaug_ref, acc_ref, qf_ref, s_ref, mpart_ref, macc_ref,
                      sd_ref, mpd_ref, *, tq, rs):
    seq = q_ref.shape[0]
    nq = seq // tq
    tr = tq // rs
    chains = [(c, r) for c in range(2) for r in range(rs)]
    lane = lax.broadcasted_iota(jnp.int32, (tq, LANES), 1)
    ones_col = jnp.ones((tq, LANES), BF16)
    slope2 = sl_ref[0, :, 0:1] * LOG2E
    c_pieces = _bf16_pieces(slope2)
    lane_row = lane[:1]
    c_query = jnp.where(lane_row < 9, _pick3(lane_row // 3, c_pieces), 0.0).astype(BF16)
    c_key = jnp.where(jnp.logical_and(lane_row >= 9, lane_row < 18),
                      _pick3((lane_row - 9) // 3, c_pieces), 0.0).astype(BF16)

    def bias_lanes(rows, key_side):
        pos_lanes = plane_ref[rows, :]
        if key_side:
            return jnp.where(lane < 9, pos_lanes, c_key)
        return jnp.where(lane < 9, c_query, pos_lanes)

    def stage_keys(kc, _):
        rows = pl.ds(pl.multiple_of(kc * tq, tq), tq)
        vaug_ref[rows, :DIFF_V] = v_ref[rows, :]
        vaug_ref[rows, DIFF_V:] = ones_col
        kaug_ref[rows, :2 * DIFF_QK] = k_ref[rows, :]
        kaug_ref[rows, 2 * DIFF_QK:] = bias_lanes(rows, True)
        return 0

    lax.fori_loop(0, nq, stage_keys, 0)

    def stage_queries(qt):
        rows = pl.ds(pl.multiple_of(qt * tq, tq), tq)
        q = q_ref[rows, :]
        q_side = bias_lanes(rows, False)
        qf_ref[0, :, :2 * DIFF_QK] = jnp.where(lane < DIFF_QK, q, 0).astype(BF16)
        qf_ref[1, :, :2 * DIFF_QK] = jnp.where(lane >= DIFF_QK, q, 0).astype(BF16)
        qf_ref[0, :, 2 * DIFF_QK:] = q_side
        qf_ref[1, :, 2 * DIFF_QK:] = q_side

    lq = lam_ref[...]
    lam = (jnp.exp(jnp.sum(lq[0:1] * lq[1:2], axis=-1, keepdims=True))
           - jnp.exp(jnp.sum(lq[2:3] * lq[3:4], axis=-1, keepdims=True)) + LAMBDA_INIT)

    def scores(qt, kc, diag):
        k0 = pl.multiple_of(kc * tq, tq)
        k = k_ref[pl.ds(k0, tq), :]
        pk = slope2 * pk_ref[0, pl.ds(kc, 1), :]
        out = [None] * len(chains)
        for r in range(rs):
            pq = pq_ref[pl.ds(pl.multiple_of(qt * tq + r * tr, tr), tr), :]
            bias = jnp.abs(slope2 * pq - pk)
            mask = _causal_mask(tr, tq, r * tr) if diag else None
            for c in range(2):
                q = qf_ref[c, r * tr:(r + 1) * tr, :2 * DIFF_QK]
                s = lax.dot_general(q, k, _NT, preferred_element_type=F32) - bias
                out[chains.index((c, r))] = jnp.where(mask, s, NEG) if diag else s
        return out

    def scores_diag(qt):
        stage_queries(qt)
        return scores(qt, qt, True)

    def scores_ordered(qt, kc):
        k = kaug_ref[pl.ds(pl.multiple_of(kc * tq, tq), tq), :]
        return [lax.dot_general(qf_ref[c, r * tr:(r + 1) * tr, :], k, _NT, preferred_element_type=F32)
                for c, r in chains]

    def scores_diag_sorted(qt):
        stage_queries(qt)
        k0 = pl.multiple_of(qt * tq, tq)
        out = []
        for c, r in chains:
            nk = (r + 1) * tr
            s = lax.dot_general(qf_ref[c, r * tr:(r + 1) * tr, :], kaug_ref[pl.ds(k0, nk), :], _NT,
                                preferred_element_type=F32)
            out.append(_pad_masked(jnp.where(_causal_mask(tr, nk, r * tr), s, NEG), tq))
        return out

    batch = pl.program_id(0)

    def trips_ordered(qt):
        return jnp.where((ord_ref[batch, qt] & 1) != 0, qt, 0)

    def tile_sorted(qt):
        return (ord_ref[batch, qt] >> 1) & 1

    def values(ci, kc):
        return vaug_ref[pl.ds(pl.multiple_of(kc * tq, tq), tq), :]

    def finalize(qt):
        for r in range(rs):
            rows = pl.ds(pl.multiple_of(qt * tq + r * tr, tr), tr)
            a1 = acc_ref[chains.index((0, r))]
            a2 = acc_ref[chains.index((1, r))]
            o = a1[:, :DIFF_V] / a1[:, DIFF_V:] - lam * (a2[:, :DIFF_V] / a2[:, DIFF_V:])
            ms_o = jnp.mean(o * o, axis=-1, keepdims=True)
            o = o * lax.rsqrt(ms_o + EPS) * gs_ref[...] * (1.0 - LAMBDA_INIT)
            o_ref[rows, :] = (o * g_ref[rows, :].astype(F32)).astype(BF16)

    _flash_pipeline(len(chains), nq,
                    [(tile_sorted, scores_diag_sorted), (lambda qt: 1 - tile_sorted(qt), scores_diag)],
                    [(trips_ordered, scores_ordered),
                     (lambda qt: qt - trips_ordered(qt), lambda qt, kc: scores(qt, kc, False))],
                    values, finalize, s_ref, acc_ref, mpart_ref, macc_ref, sd_ref, mpd_ref)


def _diff_attn(rest, pos_col, pos_row, pos_lanes, ordered, slopes, lam_par, g_subln, nb, s):
    tq = ATTN_TQ
    nq = s // tq
    rs = DIFF_ROW_SPLIT
    tr = tq // rs
    hw = DIFF_V
    return pl.pallas_call(
        functools.partial(_diff_attn_kernel, tq=tq, rs=rs),
        grid=(nb, DIFF_HEADS),
        in_specs=[pl.BlockSpec(memory_space=pltpu.SMEM),
                  pl.BlockSpec((s, hw), lambda b, hd: (b, QD_OFF // hw + hd)),
                  pl.BlockSpec((s, hw), lambda b, hd: (b, KD_OFF // hw + hd)),
                  pl.BlockSpec((s, hw), lambda b, hd: (b, VD_OFF // hw + hd)),
                  pl.BlockSpec((s, hw), lambda b, hd: (b, GD_OFF // hw + hd)),
                  pl.BlockSpec((s, 1), lambda b, hd: (b, 0)),
                  pl.BlockSpec((1, nq, tq), lambda b, hd: (b, 0, 0)),
                  pl.BlockSpec((s, LANES), lambda b, hd: (b, 0)),
                  pl.BlockSpec((1, 1, LANES), lambda b, hd: (hd, 0, 0)),
                  pl.BlockSpec((4, DIFF_QK), lambda b, hd: (0, 0)),
                  pl.BlockSpec((1, DIFF_V), lambda b, hd: (0, 0))],
        out_specs=pl.BlockSpec((s, hw), lambda b, hd: (b, hd)),
        out_shape=jax.ShapeDtypeStruct((nb * s, DIFF_HEADS * DIFF_V), BF16),
        scratch_shapes=[pltpu.VMEM((s, 2 * DIFF_V), BF16),
                        pltpu.VMEM((s, 4 * DIFF_QK), BF16),
                        pltpu.VMEM((2 * rs, tr, 2 * DIFF_V), F32),
                        pltpu.VMEM((2, tq, 4 * DIFF_QK), BF16),
                        pltpu.VMEM((2 * rs, tr, tq), F32),
                        pltpu.VMEM((2 * rs, tr, LANES), F32),
                        pltpu.VMEM((2 * rs, tr, LANES), F32),
                        pltpu.VMEM((2 * rs, tr, tq), F32),
                        pltpu.VMEM((2 * rs, tr, LANES), F32)],
        compiler_params=_cparams(("arbitrary", "arbitrary")),
        name="diff_attn",
    )(ordered, rest, rest, rest, rest, pos_col, pos_row, pos_lanes, slopes, lam_par, g_subln)


def _cast_kernel(w1_ref, w2_ref, w3_ref, o1_ref, o2_ref, o3_ref):
    o1_ref[...] = w1_ref[...].astype(BF16)
    o2_ref[...] = w2_ref[...].astype(BF16)
    o3_ref[...] = w3_ref[...].astype(BF16)


def _cast_weights(w1, w2, w3):
    steps = CAST_STEPS
    specs = [pl.BlockSpec((w.shape[0] // steps, w.shape[1]), lambda i: (i, 0)) for w in (w1, w2, w3)]
    return pl.pallas_call(
        _cast_kernel,
        grid=(steps,),
        in_specs=specs,
        out_specs=specs,
        out_shape=[jax.ShapeDtypeStruct(w.shape, BF16) for w in (w1, w2, w3)],
        compiler_params=_cparams(("arbitrary",)),
        name="cast_weights",
    )(w1, w2, w3)


def _merge_out_kernel(a1_ref, a2_ref, s1a_ref, s1b_ref, s2a_ref, s2b_ref, w1_ref, w2_ref, wo_ref,
                      x_ref, ada_ref, g_ref, o_ref):
    half = s1a_ref.shape[1]
    rb = a1_ref.shape[0] // EPILOGUE_ROW_SPLIT
    for t in range(EPILOGUE_ROW_SPLIT):
        rows = slice(t * rb, (t + 1) * rb)
        y1 = jnp.dot(a1_ref[rows, :], w1_ref[...], preferred_element_type=F32)
        y2 = jnp.dot(a2_ref[rows, :], w2_ref[...], preferred_element_type=F32)
        merged = jnp.concatenate(
            [s1a_ref[rows, :].astype(F32) * y1[:, :half] + s2a_ref[rows, :].astype(F32) * y2[:, :half],
             s1b_ref[rows, :].astype(F32) * y1[:, half:] + s2b_ref[rows, :].astype(F32) * y2[:, half:]],
            axis=1).astype(BF16)
        y = jnp.dot(merged, wo_ref[...], preferred_element_type=F32)
        ms = jnp.mean(y * y, axis=-1, keepdims=True)
        yn = y * lax.rsqrt(ms + EPS) * g_ref[...]
        o_ref[0, rows, :] = x_ref[0, rows, :] + ada_ref[0, 2:3, :] * yn


def _merge_out(og_mla, og_diff, rest, w1b, w2b, wob, x, ada3, g_post):
    nb, s, d = x.shape
    k_dim = og_mla.shape[1]
    tm = OUT_TM
    ns = s // tm
    half = d // 2
    row = lambda b, i: (b * ns + i, 0)
    gate = lambda blk: pl.BlockSpec((tm, half), lambda b, i: (b * ns + i, blk))
    resident = lambda shape: pl.BlockSpec(shape, lambda b, i: (0, 0), pipeline_mode=pl.Buffered(1))
    return pl.pallas_call(
        _merge_out_kernel,
        grid=(nb, ns),
        in_specs=[pl.BlockSpec((tm, k_dim), row),
                  pl.BlockSpec((tm, k_dim), row),
                  gate(MGM_OFF // half), gate(MGM_OFF // half + 1),
                  gate(MGD_OFF // half), gate(MGD_OFF // half + 1),
                  resident((k_dim, d)), resident((k_dim, d)), resident((d, d)),
                  pl.BlockSpec((1, tm, d), lambda b, i: (b, i, 0)),
                  pl.BlockSpec((1, 3, d), lambda b, i: (b, 0, 0)),
                  pl.BlockSpec((1, d), lambda b, i: (0, 0))],
        out_specs=pl.BlockSpec((1, tm, d), lambda b, i: (b, i, 0)),
        out_shape=jax.ShapeDtypeStruct((nb, s, d), F32),
        compiler_params=_cparams(("arbitrary", "arbitrary")),
        name="merge_out",
    )(og_mla, og_diff, rest, rest, rest, rest, w1b, w2b, wob, x, ada3, g_post)


def kernel(x, c, positions, w_ada, b_ada, g_pre, w_in, g_kv, w_ukv, lambda_q1, lambda_k1,
           lambda_q2, lambda_k2, g_subln, w_o_mla, w_o_diff, w_out, g_post):
    nb, s, d = x.shape
    depth = w_in.shape[0]
    half = MLA_ROPE // 2
    inv = ROPE_THETA ** (-jnp.arange(half, dtype=F32) / half)
    inv_tab = jnp.tile(inv, LANES // half).reshape(1, LANES)
    slopes = 2.0 ** (-8.0 * jnp.arange(1, DIFF_HEADS + 1, dtype=F32) / DIFF_HEADS)
    slopes = jnp.broadcast_to(slopes.reshape(DIFF_HEADS, 1, 1), (DIFF_HEADS, 1, LANES))
    pos_col = positions.reshape(nb * s, 1)
    pos_colf = pos_col.astype(F32)
    pos_chunks = positions.reshape(nb, s // ATTN_TQ, ATTN_TQ)
    pos_row = pos_chunks.astype(F32)
    run_max = lax.cummax(pos_chunks.max(axis=-1), axis=1)
    prev_max = jnp.concatenate(
        [jnp.full((nb, 1), jnp.iinfo(jnp.int32).min, jnp.int32), run_max[:, :-1]], axis=1)
    tile_sorted = jnp.all(pos_chunks[..., 1:] >= pos_chunks[..., :-1], axis=-1)
    ordered = ((prev_max <= pos_chunks.min(axis=-1)).astype(jnp.int32)
               + 2 * tile_sorted.astype(jnp.int32))

    for l in range(depth):
        ada3 = _ada(c, w_ada[l], b_ada[l]).reshape(nb, 3, d)
        h, cos_tab, sin_tab, pos_lanes = _norm(x, ada3, g_pre[l].reshape(1, d), pos_col, inv_tab)
        w_t = jnp.swapaxes(w_in[l], 0, 1)
        qm, rest = _proj(h, w_t, cos_tab, sin_tab)
        kk, vv = _kv(h, w_t, g_kv[l].reshape(1, KV_RANK), w_ukv[l], cos_tab, sin_tab)
        og_mla = _mla_attn(qm, kk, vv, rest, nb, s)
        lam_par = jnp.stack([lambda_q1[l], lambda_k1[l], lambda_q2[l], lambda_k2[l]]).astype(F32)
        og_diff = _diff_attn(rest, pos_colf, pos_row, pos_lanes, ordered, slopes, lam_par,
                             g_subln[l].reshape(1, DIFF_V), nb, s)
        w1b, w2b, wob = _cast_weights(w_o_mla[l], w_o_diff[l], w_out[l])
        x = _merge_out(og_mla, og_diff, rest, w1b, w2b, wob, x, ada3, g_post[l].reshape(1, d))
    return x
```

```python
import functools
import math

import jax
import jax.numpy as jnp
from jax import lax
from jax.experimental import pallas as pl
from jax.experimental.pallas import tpu as pltpu

F32 = jnp.float32
BF16 = jnp.bfloat16

MLA_HEADS = 8
MLA_NOPE = 128
MLA_ROPE = 64
MLA_V = 128
KV_RANK = 512
ROPE_THETA = 10000.0
DIFF_HEADS = 8
DIFF_QK = 64
DIFF_V = 128
EPS = 1e-6
NEG = -1e30

LANES = 128
SUBLANES = 8
VMEM_LIMIT = 56 * 1024 * 1024
ADA_TN = 512
NORM_TS = 512
N_PIECES = 3
N_PAIR_LANES = N_PIECES * N_PIECES
ATTN_TQ = 1024
DIFF_ROW_SPLIT = 2
MLA_TQ = 1024
MLA_ROW_SPLIT = 2
PROJ_TM = 1024
REST_TN = 1024
QMLA_PAIRS_PER_TILE = 2
KV_TM = 1024
CAST_STEPS = 4
OUT_TM = 512
EPILOGUE_ROW_SPLIT = 2
LOG2E = math.log2(math.e)

Q_MLA_W = MLA_HEADS * (MLA_NOPE + MLA_ROPE)
KVR_W = KV_RANK + MLA_ROPE
KVR_PAD = 640
REST_OFF = Q_MLA_W + KVR_W
QD_OFF, KD_OFF, VD_OFF, GM_OFF, GD_OFF, MGM_OFF, MGD_OFF = 0, 1024, 2048, 3072, 4096, 5120, 7168
REST_W = 9216


def _cparams(sem):
    return pltpu.CompilerParams(dimension_semantics=sem, vmem_limit_bytes=VMEM_LIMIT)


def _ada_kernel(cb_ref, w_ref, b_ref, o_ref):
    k_dim, tn = w_ref.shape
    nb = cb_ref.shape[0]
    nchunk = tn // LANES

    def body(i, accs):
        k0 = pl.multiple_of(i * SUBLANES, SUBLANES)
        out = []
        for b in range(nb):
            cv = cb_ref[b, pl.ds(k0, SUBLANES), :]
            for j in range(nchunk):
                wv = w_ref[pl.ds(k0, SUBLANES), j * LANES:(j + 1) * LANES]
                out.append(accs[b * nchunk + j] + wv * cv)
        return tuple(out)

    init = tuple(jnp.zeros((SUBLANES, LANES), F32) for _ in range(nb * nchunk))
    accs = lax.fori_loop(0, k_dim // SUBLANES, body, init, unroll=8)
    for b in range(nb):
        row = jnp.concatenate(
            [jnp.sum(accs[b * nchunk + j], axis=0, keepdims=True) for j in range(nchunk)], axis=1)
        o_ref[b:b + 1, :] = row + b_ref[...]


def _ada(c, w, bias):
    nb, k_dim = c.shape
    n = w.shape[1]
    tn = ADA_TN
    cb = jnp.broadcast_to(c[:, :, None], (nb, k_dim, LANES))
    return pl.pallas_call(
        _ada_kernel,
        grid=(n // tn,),
        in_specs=[pl.BlockSpec((nb, k_dim, LANES), lambda j: (0, 0, 0)),
                  pl.BlockSpec((k_dim, tn), lambda j: (0, j)),
                  pl.BlockSpec((1, tn), lambda j: (0, j))],
        out_specs=pl.BlockSpec((nb, tn), lambda j: (0, j)),
        out_shape=jax.ShapeDtypeStruct((nb, n), F32),
        compiler_params=_cparams(("arbitrary",)),
        name="ada",
    )(cb, w, bias.reshape(1, n))


def _bf16_pieces(x):
    p1 = x.astype(BF16)
    r = x - p1.astype(F32)
    p2 = r.astype(BF16)
    return p1.astype(F32), p2.astype(F32), r - p2.astype(F32)


def _pick3(sel, x3):
    return jnp.where(sel == 0, x3[0], jnp.where(sel == 1, x3[1], x3[2]))


def _norm_kernel(x_ref, ada_ref, g_ref, pos_ref, inv_ref, h_ref, cos_ref, sin_ref, plane_ref):
    x = x_ref[0]
    ms = jnp.mean(x * x, axis=-1, keepdims=True)
    y = x * lax.rsqrt(ms + EPS) * g_ref[...]
    shift = ada_ref[0, 0:1, :]
    scale = ada_ref[0, 1:2, :]
    h_ref[...] = (y * (1.0 + scale) + shift).astype(BF16)
    ang = pos_ref[...].astype(F32) * inv_ref[...]
    lane = lax.broadcasted_iota(jnp.int32, ang.shape, 1)
    sign = jnp.where((lane % MLA_ROPE) < MLA_ROPE // 2, -1.0, 1.0).astype(F32)
    cos_ref[...] = jnp.cos(ang)
    sin_ref[...] = jnp.sin(ang) * sign
    pos3 = _bf16_pieces(pos_ref[...].astype(F32))
    piece = _pick3(lane % N_PIECES, pos3)
    plane_ref[...] = jnp.where(lane < N_PAIR_LANES, piece,
                               jnp.where(lane < 2 * N_PAIR_LANES, -piece, 0.0)).astype(BF16)


def _norm(x, ada3, g_pre, pos_col, inv_tab):
    nb, s, d = x.shape
    ts = NORM_TS
    ns = s // ts
    row = lambda b, i: (b * ns + i, 0)
    return pl.pallas_call(
        _norm_kernel,
        grid=(nb, ns),
        in_specs=[pl.BlockSpec((1, ts, d), lambda b, i: (b, i, 0)),
                  pl.BlockSpec((1, 3, d), lambda b, i: (b, 0, 0)),
                  pl.BlockSpec((1, d), lambda b, i: (0, 0)),
                  pl.BlockSpec((ts, 1), row),
                  pl.BlockSpec((1, LANES), lambda b, i: (0, 0))],
        out_specs=[pl.BlockSpec((ts, d), row),
                   pl.BlockSpec((ts, LANES), row),
                   pl.BlockSpec((ts, LANES), row),
                   pl.BlockSpec((ts, LANES), row)],
        out_shape=[jax.ShapeDtypeStruct((nb * s, d), BF16),
                   jax.ShapeDtypeStruct((nb * s, LANES), F32),
                   jax.ShapeDtypeStruct((nb * s, LANES), F32),
                   jax.ShapeDtypeStruct((nb * s, LANES), BF16)],
        compiler_params=_cparams(("arbitrary", "arbitrary")),
        name="prenorm",
    )(x, ada3, g_pre, pos_col, inv_tab)


def _rope_cols(r, cos, sin_signed):
    lane = lax.broadcasted_iota(jnp.int32, r.shape, 1)
    half = MLA_ROPE // 2
    partner = jnp.where((lane % MLA_ROPE) < half,
                        pltpu.roll(r, LANES - half, 1), pltpu.roll(r, half, 1))
    return r * cos + partner * sin_signed


def _qmla_kernel(a_ref, w_ref, cos_ref, sin_ref, o_ref, wb_ref, *, scale):
    hd = MLA_NOPE + MLA_ROPE
    pair_w = 2 * hd
    n_pairs = wb_ref.shape[0] // pair_w

    @pl.when(pl.program_id(1) == 0)
    def _():
        for p in range(n_pairs):
            src, dst = w_ref.at[p * pair_w:(p + 1) * pair_w], wb_ref.at[p * pair_w:(p + 1) * pair_w]
            dst[:MLA_NOPE] = src[:MLA_NOPE].astype(BF16)
            dst[MLA_NOPE:2 * MLA_NOPE] = src[hd:hd + MLA_NOPE].astype(BF16)
            dst[2 * MLA_NOPE:2 * MLA_NOPE + MLA_ROPE] = src[MLA_NOPE:hd].astype(BF16)
            dst[2 * MLA_NOPE + MLA_ROPE:] = src[hd + MLA_NOPE:].astype(BF16)

    acc = lax.dot_general(a_ref[...], wb_ref[...], _NT, preferred_element_type=F32)
    for p in range(n_pairs):
        c0 = p * pair_w
        rr = _rope_cols(acc[:, c0 + 2 * MLA_NOPE:c0 + pair_w], cos_ref[...], sin_ref[...])
        o_ref[:, c0:c0 + 2 * MLA_NOPE] = (acc[:, c0:c0 + 2 * MLA_NOPE] * scale).astype(BF16)
        o_ref[:, c0 + 2 * MLA_NOPE:c0 + pair_w] = (rr * scale).astype(BF16)


def _sigmoid(x):
    return 0.5 * jnp.tanh(0.5 * x) + 0.5


def _rest_kernel(a_ref, w_ref, o_ref, wb_ref, *, tn):
    j = pl.program_id(0)

    @pl.when(pl.program_id(1) == 0)
    def _():
        wb_ref[...] = w_ref[...].astype(BF16)

    def tile(epilogue):
        acc = lax.dot_general(a_ref[...], wb_ref[...], _NT, preferred_element_type=F32)
        o_ref[...] = epilogue(acc).astype(BF16)

    @pl.when(j < KD_OFF // tn)
    def _():
        tile(lambda acc: acc * (DIFF_QK ** -0.5 * LOG2E))

    @pl.when(jnp.logical_and(j >= KD_OFF // tn, j < GM_OFF // tn))
    def _():
        tile(lambda acc: acc)

    @pl.when(jnp.logical_and(j >= GM_OFF // tn, j < MGM_OFF // tn))
    def _():
        tile(lambda acc: acc * _sigmoid(acc))

    @pl.when(j >= MGM_OFF // tn)
    def _():
        tile(_sigmoid)


def _proj(h, w_t, cos_tab, sin_tab):
    m, k_dim = h.shape
    tm = PROJ_TM
    pair_w = 2 * (MLA_NOPE + MLA_ROPE)
    q_scale = (MLA_NOPE + MLA_ROPE) ** -0.5 * LOG2E

    def row_window(rows, offset_fn):
        return pl.BlockSpec((pl.Element(rows), pl.Element(k_dim)),
                            lambda *g: (pl.multiple_of(offset_fn(*g), SUBLANES), 0))

    qtn = QMLA_PAIRS_PER_TILE * pair_w
    qm = pl.pallas_call(
        functools.partial(_qmla_kernel, scale=q_scale),
        grid=(Q_MLA_W // qtn, m // tm),
        in_specs=[pl.BlockSpec((tm, k_dim), lambda j, i: (i, 0)),
                  pl.BlockSpec((qtn, k_dim), lambda j, i: (j, 0)),
                  pl.BlockSpec((tm, LANES), lambda j, i: (i, 0)),
                  pl.BlockSpec((tm, LANES), lambda j, i: (i, 0))],
        out_specs=pl.BlockSpec((tm, qtn), lambda j, i: (i, j)),
        out_shape=jax.ShapeDtypeStruct((m, Q_MLA_W), BF16),
        scratch_shapes=[pltpu.VMEM((qtn, k_dim), BF16)],
        compiler_params=_cparams(("arbitrary", "arbitrary")),
        name="proj_qmla",
    )(h, w_t, cos_tab, sin_tab)
    tn = REST_TN
    rest = pl.pallas_call(
        functools.partial(_rest_kernel, tn=tn),
        grid=(REST_W // tn, m // tm),
        in_specs=[pl.BlockSpec((tm, k_dim), lambda j, i: (i, 0)),
                  row_window(tn, lambda j, i: REST_OFF + tn * j)],
        out_specs=pl.BlockSpec((tm, tn), lambda j, i: (i, j)),
        out_shape=jax.ShapeDtypeStruct((m, REST_W), BF16),
        scratch_shapes=[pltpu.VMEM((tn, k_dim), BF16)],
        compiler_params=_cparams(("arbitrary", "arbitrary")),
        name="proj_rest",
    )(h, w_t)
    return qm, rest


def _kv_kernel(a_ref, wp_ref, g_ref, w_ref, cos_ref, sin_ref, k_ref, v_ref, wpb_ref, wb_ref):
    @pl.when(pl.program_id(0) == 0)
    def _():
        wpb_ref[:KVR_W] = wp_ref[...].astype(BF16)
        wpb_ref[KVR_W:] = jnp.zeros((KVR_PAD - KVR_W, wpb_ref.shape[1]), BF16)
        wb_ref[...] = w_ref[...].astype(BF16)

    kw = MLA_NOPE + MLA_V
    rb = a_ref.shape[0] // EPILOGUE_ROW_SPLIT
    for t in range(EPILOGUE_ROW_SPLIT):
        rows = slice(t * rb, (t + 1) * rb)
        p = lax.dot_general(a_ref[rows, :], wpb_ref[...], _NT, preferred_element_type=F32)
        ckv = p[:, :KV_RANK]
        ms = jnp.mean(ckv * ckv, axis=-1, keepdims=True)
        n = (ckv * lax.rsqrt(ms + EPS) * g_ref[...]).astype(BF16)
        kv = jnp.dot(n, wb_ref[...], preferred_element_type=F32)
        kr_even = _rope_cols(p[:, KV_RANK:], cos_ref[rows, :], sin_ref[rows, :])
        kr_odd = pltpu.roll(kr_even, MLA_ROPE, 1)
        ones_col = jnp.ones(kr_even.shape, BF16)
        for hd in range(MLA_HEADS):
            k_ref[rows, hd * kw:hd * kw + MLA_NOPE] = kv[:, hd * kw:hd * kw + MLA_NOPE].astype(BF16)
            k_ref[rows, hd * kw + MLA_NOPE:(hd + 1) * kw] = (
                kr_even if hd % 2 == 0 else kr_odd).astype(BF16)
            v_ref[rows, hd * kw:hd * kw + MLA_V] = kv[:, hd * kw + MLA_NOPE:(hd + 1) * kw].astype(BF16)
            v_ref[rows, hd * kw + MLA_V:(hd + 1) * kw] = ones_col


def _kv(h, w_t, g_kv, w_ukv, cos_tab, sin_tab):
    m, k_dim = h.shape
    tm = KV_TM
    kw = MLA_HEADS * (MLA_NOPE + MLA_V)
    return pl.pallas_call(
        _kv_kernel,
        grid=(m // tm,),
        in_specs=[pl.BlockSpec((tm, k_dim), lambda i: (i, 0)),
                  pl.BlockSpec((pl.Element(KVR_W), pl.Element(k_dim)),
                               lambda i: (pl.multiple_of(Q_MLA_W + 0 * i, SUBLANES), 0)),
                  pl.BlockSpec((1, KV_RANK), lambda i: (0, 0)),
                  pl.BlockSpec((KV_RANK, kw), lambda i: (0, 0)),
                  pl.BlockSpec((tm, LANES), lambda i: (i, 0)),
                  pl.BlockSpec((tm, LANES), lambda i: (i, 0))],
        out_specs=[pl.BlockSpec((tm, kw), lambda i: (i, 0)),
                   pl.BlockSpec((tm, kw), lambda i: (i, 0))],
        out_shape=[jax.ShapeDtypeStruct((m, kw), BF16),
                   jax.ShapeDtypeStruct((m, kw), BF16)],
        scratch_shapes=[pltpu.VMEM((KVR_PAD, k_dim), BF16), pltpu.VMEM((KV_RANK, kw), BF16)],
        compiler_params=_cparams(("arbitrary",)),
        name="kv_up",
    )(h, w_t, g_kv, w_ukv, cos_tab, sin_tab)


def _flash_pipeline(n_chains, nq, diags, loops, value_fn, finalize_fn,
                    s_ref, acc_ref, mpart_ref, macc_ref, sd_ref=None, mpd_ref=None):
    assert sd_ref is None or nq % 2 == 0
    chunk = s_ref.shape[2]

    def lane_tiles(x, n):
        return jnp.concatenate([x] * n, axis=1)

    def qk_phase(scores, stage):
        s_dst, m_dst = stage
        for ci, s in enumerate(scores):
            s_dst[ci] = s
            part = s[:, :LANES]
            for j in range(1, chunk // LANES):
                part = jnp.maximum(part, s[:, j * LANES:(j + 1) * LANES])
            m_dst[ci] = part

    def pv_phase(kc, stage):
        s_src, m_src = stage
        for ci in range(n_chains):
            m_acc = macc_ref[ci]
            m_run = jnp.maximum(m_acc, jnp.max(m_src[ci], axis=-1, keepdims=True))
            macc_ref[ci] = m_run
            p = jnp.exp2(s_src[ci] - lane_tiles(m_run, chunk // LANES))
            alpha = jnp.exp2(m_acc - m_run)
            pv = jnp.dot(p.astype(BF16), value_fn(ci, kc), preferred_element_type=F32)
            acc_ref[ci] = lane_tiles(alpha, acc_ref.shape[2] // LANES) * acc_ref[ci] + pv

    def reset():
        acc_ref[...] = jnp.zeros_like(acc_ref)
        macc_ref[...] = jnp.full(macc_ref.shape, NEG, F32)

    def tile(qt, stage, next_stage):
        cur = qt
        for trips_fn, score_fn in loops:
            def step(kc, cur, score_fn=score_fn):
                pv_phase(cur, stage)
                qk_phase(score_fn(qt, kc), stage)
                return kc

            cur = lax.fori_loop(0, trips_fn(qt), step, cur)
        nxt = jnp.minimum(qt + 1, nq - 1)

        def transition(_, cur, fn):
            if next_stage is stage:
                pv_phase(cur, stage)
                finalize_fn(qt)
                reset()
                qk_phase(fn(nxt), stage)
            else:
                qk_phase(fn(nxt), next_stage)
                pv_phase(cur, stage)
                finalize_fn(qt)
                reset()
            return cur

        for select_fn, fn in diags:
            if select_fn is None:
                transition(0, cur, fn)
            else:
                lax.fori_loop(0, select_fn(nxt), functools.partial(transition, fn=fn), cur)

    stage_a = (s_ref, mpart_ref)
    reset()
    for select_fn, fn in diags:
        if select_fn is None:
            qk_phase(fn(0), stage_a)
        else:
            lax.fori_loop(0, select_fn(0), lambda _, c, fn=fn: qk_phase(fn(0), stage_a) or c, 0)
    if sd_ref is None:
        lax.fori_loop(0, nq, lambda qt, c: tile(qt, stage_a, stage_a) or c, 0)
    else:
        stage_b = (sd_ref, mpd_ref)

        def tile_pair(j, c):
            tile(2 * j, stage_a, stage_b)
            tile(2 * j + 1, stage_b, stage_a)
            return c

        lax.fori_loop(0, nq // 2, tile_pair, 0)


def _causal_mask(rows, cols, row0):
    row = lax.broadcasted_iota(jnp.int32, (rows, cols), 0) + row0
    col = lax.broadcasted_iota(jnp.int32, (rows, cols), 1)
    return col <= row


def _pad_masked(s, cols):
    if s.shape[1] == cols:
        return s
    return jnp.concatenate([s, jnp.full((s.shape[0], cols - s.shape[1]), NEG, s.dtype)], axis=1)


_NT = (((1,), (1,)), ((), ()))


def _mla_attn_kernel(q_ref, k_ref, v_ref, g_ref, o_ref, acc_ref, qs_ref, s_ref, mpart_ref, macc_ref,
                     *, tq, rs):
    kw = 2 * MLA_NOPE
    tr = tq // rs
    nq = q_ref.shape[0] // tq
    chains = [(hp, r) for hp in range(2) for r in range(rs)]

    def stage_queries(qt):
        rows = pl.ds(pl.multiple_of(qt * tq, tq), tq)
        for hp in range(2):
            qs_ref[hp, :, :MLA_NOPE] = q_ref[rows, hp * MLA_NOPE:(hp + 1) * MLA_NOPE]
            qs_ref[hp, :, MLA_NOPE:] = q_ref[rows, 2 * MLA_NOPE:]

    def scores(qt, kc, diag):
        if diag:
            stage_queries(qt)
        k0 = pl.multiple_of(kc * tq, tq)
        out = []
        for hp, r in chains:
            nk = (r + 1) * tr if diag else tq
            q = qs_ref[hp, r * tr:(r + 1) * tr, :]
            k = k_ref[pl.ds(k0, nk), hp * kw:(hp + 1) * kw]
            s = lax.dot_general(q, k, _NT, preferred_element_type=F32)
            if diag:
                s = _pad_masked(jnp.where(_causal_mask(tr, nk, r * tr), s, NEG), tq)
            out.append(s)
        return out

    def values(ci, kc):
        hp = chains[ci][0]
        return v_ref[pl.ds(pl.multiple_of(kc * tq, tq), tq), hp * kw:(hp + 1) * kw]

    def finalize(qt):
        for ci, (hp, r) in enumerate(chains):
            rows = pl.ds(pl.multiple_of(qt * tq + r * tr, tr), tr)
            acc = acc_ref[ci]
            o = acc[:, :MLA_V] / acc[:, MLA_V:]
            gate = g_ref[rows, hp * MLA_V:(hp + 1) * MLA_V].astype(F32)
            o_ref[rows, hp * MLA_V:(hp + 1) * MLA_V] = (o * gate).astype(BF16)

    _flash_pipeline(len(chains), nq, [(None, lambda qt: scores(qt, qt, True))],
                    [(lambda qt: qt, lambda qt, kc: scores(qt, kc, False))],
                    values, finalize, s_ref, acc_ref, mpart_ref, macc_ref)


def _mla_attn(qm, kk, vv, rest, nb, s):
    tq = MLA_TQ
    pair_w = 2 * (MLA_NOPE + MLA_ROPE)
    kw = 4 * MLA_NOPE
    gate_blk = GM_OFF // (2 * MLA_V)
    rs = MLA_ROW_SPLIT
    return pl.pallas_call(
        functools.partial(_mla_attn_kernel, tq=tq, rs=rs),
        grid=(nb, MLA_HEADS // 2),
        in_specs=[pl.BlockSpec((s, pair_w), lambda b, hh: (b, hh)),
                  pl.BlockSpec((s, kw), lambda b, hh: (b, hh)),
                  pl.BlockSpec((s, kw), lambda b, hh: (b, hh)),
                  pl.BlockSpec((s, 2 * MLA_V), lambda b, hh: (b, gate_blk + hh))],
        out_specs=pl.BlockSpec((s, 2 * MLA_V), lambda b, hh: (b, hh)),
        out_shape=jax.ShapeDtypeStruct((nb * s, MLA_HEADS * MLA_V), BF16),
        scratch_shapes=[pltpu.VMEM((2 * rs, tq // rs, 2 * MLA_V), F32),
                        pltpu.VMEM((2, tq, 2 * MLA_NOPE), BF16),
                        pltpu.VMEM((2 * rs, tq // rs, tq), F32),
                        pltpu.VMEM((2 * rs, tq // rs, LANES), F32),
                        pltpu.VMEM((2 * rs, tq // rs, LANES), F32)],
        compiler_params=_cparams(("arbitrary", "arbitrary")),
        name="mla_attn",
    )(qm, kk, vv, rest)


def _diff_attn_kernel(ord_ref, q_ref, k_ref, v_ref, g_ref, pq_ref, pk_ref, plane_ref, sl_ref, lam_ref,
                      gs_ref, o_ref, vaug_ref, kaug_ref, acc_ref, qf_ref, s_ref, mpart_ref, macc_ref,
                      sd_ref, mpd_ref, *, tq, rs, lambda_init):
    seq = q_ref.shape[0]
    nq = seq // tq
    tr = tq // rs
    chains = [(c, r) for c in range(2) for r in range(rs)]
    lane = lax.broadcasted_iota(jnp.int32, (tq, LANES), 1)
    ones_col = jnp.ones((tq, LANES), BF16)
    slope2 = sl_ref[0, :, 0:1] * LOG2E
    c_pieces = _bf16_pieces(slope2)
    lane_row = lane[:1]
    c_query = jnp.where(lane_row < N_PAIR_LANES, _pick3(lane_row // N_PIECES, c_pieces), 0.0).astype(BF16)
    c_key = jnp.where(jnp.logical_and(lane_row >= N_PAIR_LANES, lane_row < 2 * N_PAIR_LANES),
                      _pick3((lane_row - N_PAIR_LANES) // N_PIECES, c_pieces), 0.0).astype(BF16)

    def bias_lanes(rows, key_side):
        pos_lanes = plane_ref[rows, :]
        if key_side:
            return jnp.where(lane < N_PAIR_LANES, pos_lanes, c_key)
        return jnp.where(lane < N_PAIR_LANES, c_query, pos_lanes)

    def stage_keys(kc, _):
        rows = pl.ds(pl.multiple_of(kc * tq, tq), tq)
        vaug_ref[rows, :DIFF_V] = v_ref[rows, :]
        vaug_ref[rows, DIFF_V:] = ones_col
        kaug_ref[rows, :2 * DIFF_QK] = k_ref[rows, :]
        kaug_ref[rows, 2 * DIFF_QK:] = bias_lanes(rows, True)
        return 0

    lax.fori_loop(0, nq, stage_keys, 0)

    def stage_queries(qt):
        rows = pl.ds(pl.multiple_of(qt * tq, tq), tq)
        q = q_ref[rows, :]
        q_side = bias_lanes(rows, False)
        qf_ref[0, :, :2 * DIFF_QK] = jnp.where(lane < DIFF_QK, q, 0).astype(BF16)
        qf_ref[1, :, :2 * DIFF_QK] = jnp.where(lane >= DIFF_QK, q, 0).astype(BF16)
        qf_ref[0, :, 2 * DIFF_QK:] = q_side
        qf_ref[1, :, 2 * DIFF_QK:] = q_side

    lq = lam_ref[...]
    lam = (jnp.exp(jnp.sum(lq[0:1] * lq[1:2], axis=-1, keepdims=True))
           - jnp.exp(jnp.sum(lq[2:3] * lq[3:4], axis=-1, keepdims=True)) + lambda_init)

    def scores(qt, kc, diag):
        k0 = pl.multiple_of(kc * tq, tq)
        k = k_ref[pl.ds(k0, tq), :]
        pk = slope2 * pk_ref[0, pl.ds(kc, 1), :]
        out = [None] * len(chains)
        for r in range(rs):
            pq = pq_ref[pl.ds(pl.multiple_of(qt * tq + r * tr, tr), tr), :]
            bias = jnp.abs(slope2 * pq - pk)
            mask = _causal_mask(tr, tq, r * tr) if diag else None
            for c in range(2):
                q = qf_ref[c, r * tr:(r + 1) * tr, :2 * DIFF_QK]
                s = lax.dot_general(q, k, _NT, preferred_element_type=F32) - bias
                out[chains.index((c, r))] = jnp.where(mask, s, NEG) if diag else s
        return out

    def scores_diag(qt):
        stage_queries(qt)
        return scores(qt, qt, True)

    def scores_ordered(qt, kc):
        k = kaug_ref[pl.ds(pl.multiple_of(kc * tq, tq), tq), :]
        return [lax.dot_general(qf_ref[c, r * tr:(r + 1) * tr, :], k, _NT, preferred_element_type=F32)
                for c, r in chains]

    def scores_diag_sorted(qt):
        stage_queries(qt)
        k0 = pl.multiple_of(qt * tq, tq)
        out = []
        for c, r in chains:
            nk = (r + 1) * tr
            s = lax.dot_general(qf_ref[c, r * tr:(r + 1) * tr, :], kaug_ref[pl.ds(k0, nk), :], _NT,
                                preferred_element_type=F32)
            out.append(_pad_masked(jnp.where(_causal_mask(tr, nk, r * tr), s, NEG), tq))
        return out

    batch = pl.program_id(0)

    def trips_ordered(qt):
        return jnp.where((ord_ref[batch, qt] & 1) != 0, qt, 0)

    def tile_sorted(qt):
        return (ord_ref[batch, qt] >> 1) & 1

    def values(ci, kc):
        return vaug_ref[pl.ds(pl.multiple_of(kc * tq, tq), tq), :]

    def finalize(qt):
        for r in range(rs):
            rows = pl.ds(pl.multiple_of(qt * tq + r * tr, tr), tr)
            a1 = acc_ref[chains.index((0, r))]
            a2 = acc_ref[chains.index((1, r))]
            o = a1[:, :DIFF_V] / a1[:, DIFF_V:] - lam * (a2[:, :DIFF_V] / a2[:, DIFF_V:])
            ms_o = jnp.mean(o * o, axis=-1, keepdims=True)
            o = o * lax.rsqrt(ms_o + EPS) * gs_ref[...] * (1.0 - lambda_init)
            o_ref[rows, :] = (o * g_ref[rows, :].astype(F32)).astype(BF16)

    _flash_pipeline(len(chains), nq,
                    [(tile_sorted, scores_diag_sorted), (lambda qt: 1 - tile_sorted(qt), scores_diag)],
                    [(trips_ordered, scores_ordered),
                     (lambda qt: qt - trips_ordered(qt), lambda qt, kc: scores(qt, kc, False))],
                    values, finalize, s_ref, acc_ref, mpart_ref, macc_ref, sd_ref, mpd_ref)


def _diff_attn(rest, pos_col, pos_row, pos_lanes, ordered, slopes, lam_par, g_subln, lambda_init,
               nb, s):
    tq = ATTN_TQ
    nq = s // tq
    rs = DIFF_ROW_SPLIT
    tr = tq // rs
    hw = DIFF_V
    return pl.pallas_call(
        functools.partial(_diff_attn_kernel, tq=tq, rs=rs, lambda_init=lambda_init),
        grid=(nb, DIFF_HEADS),
        in_specs=[pl.BlockSpec(memory_space=pltpu.SMEM),
                  pl.BlockSpec((s, hw), lambda b, hd: (b, QD_OFF // hw + hd)),
                  pl.BlockSpec((s, hw), lambda b, hd: (b, KD_OFF // hw + hd)),
                  pl.BlockSpec((s, hw), lambda b, hd: (b, VD_OFF // hw + hd)),
                  pl.BlockSpec((s, hw), lambda b, hd: (b, GD_OFF // hw + hd)),
                  pl.BlockSpec((s, 1), lambda b, hd: (b, 0)),
                  pl.BlockSpec((1, nq, tq), lambda b, hd: (b, 0, 0)),
                  pl.BlockSpec((s, LANES), lambda b, hd: (b, 0)),
                  pl.BlockSpec((1, 1, LANES), lambda b, hd: (hd, 0, 0)),
                  pl.BlockSpec((4, DIFF_QK), lambda b, hd: (0, 0)),
                  pl.BlockSpec((1, DIFF_V), lambda b, hd: (0, 0))],
        out_specs=pl.BlockSpec((s, hw), lambda b, hd: (b, hd)),
        out_shape=jax.ShapeDtypeStruct((nb * s, DIFF_HEADS * DIFF_V), BF16),
        scratch_shapes=[pltpu.VMEM((s, 2 * DIFF_V), BF16),
                        pltpu.VMEM((s, 4 * DIFF_QK), BF16),
                        pltpu.VMEM((2 * rs, tr, 2 * DIFF_V), F32),
                        pltpu.VMEM((2, tq, 4 * DIFF_QK), BF16),
                        pltpu.VMEM((2 * rs, tr, tq), F32),
                        pltpu.VMEM((2 * rs, tr, LANES), F32),
                        pltpu.VMEM((2 * rs, tr, LANES), F32),
                        pltpu.VMEM((2 * rs, tr, tq), F32),
                        pltpu.VMEM((2 * rs, tr, LANES), F32)],
        compiler_params=_cparams(("arbitrary", "arbitrary")),
        name="diff_attn",
    )(ordered, rest, rest, rest, rest, pos_col, pos_row, pos_lanes, slopes, lam_par, g_subln)


def _cast_kernel(w1_ref, w2_ref, w3_ref, o1_ref, o2_ref, o3_ref):
    o1_ref[...] = w1_ref[...].astype(BF16)
    o2_ref[...] = w2_ref[...].astype(BF16)
    o3_ref[...] = w3_ref[...].astype(BF16)


def _cast_weights(w1, w2, w3):
    steps = CAST_STEPS
    specs = [pl.BlockSpec((w.shape[0] // steps, w.shape[1]), lambda i: (i, 0)) for w in (w1, w2, w3)]
    return pl.pallas_call(
        _cast_kernel,
        grid=(steps,),
        in_specs=specs,
        out_specs=specs,
        out_shape=[jax.ShapeDtypeStruct(w.shape, BF16) for w in (w1, w2, w3)],
        compiler_params=_cparams(("arbitrary",)),
        name="cast_weights",
    )(w1, w2, w3)


def _merge_out_kernel(a1_ref, a2_ref, s1a_ref, s1b_ref, s2a_ref, s2b_ref, w1_ref, w2_ref, wo_ref,
                      x_ref, ada_ref, g_ref, o_ref):
    half = s1a_ref.shape[1]
    rb = a1_ref.shape[0] // EPILOGUE_ROW_SPLIT
    for t in range(EPILOGUE_ROW_SPLIT):
        rows = slice(t * rb, (t + 1) * rb)
        y1 = jnp.dot(a1_ref[rows, :], w1_ref[...], preferred_element_type=F32)
        y2 = jnp.dot(a2_ref[rows, :], w2_ref[...], preferred_element_type=F32)
        merged = jnp.concatenate(
            [s1a_ref[rows, :].astype(F32) * y1[:, :half] + s2a_ref[rows, :].astype(F32) * y2[:, :half],
             s1b_ref[rows, :].astype(F32) * y1[:, half:] + s2b_ref[rows, :].astype(F32) * y2[:, half:]],
            axis=1).astype(BF16)
        y = jnp.dot(merged, wo_ref[...], preferred_element_type=F32)
        ms = jnp.mean(y * y, axis=-1, keepdims=True)
        yn = y * lax.rsqrt(ms + EPS) * g_ref[...]
        o_ref[0, rows, :] = x_ref[0, rows, :] + ada_ref[0, 2:3, :] * yn


def _merge_out(og_mla, og_diff, rest, w1b, w2b, wob, x, ada3, g_post):
    nb, s, d = x.shape
    k_dim = og_mla.shape[1]
    tm = OUT_TM
    ns = s // tm
    half = d // 2
    row = lambda b, i: (b * ns + i, 0)
    gate = lambda blk: pl.BlockSpec((tm, half), lambda b, i: (b * ns + i, blk))
    resident = lambda shape: pl.BlockSpec(shape, lambda b, i: (0, 0), pipeline_mode=pl.Buffered(1))
    return pl.pallas_call(
        _merge_out_kernel,
        grid=(nb, ns),
        in_specs=[pl.BlockSpec((tm, k_dim), row),
                  pl.BlockSpec((tm, k_dim), row),
                  gate(MGM_OFF // half), gate(MGM_OFF // half + 1),
                  gate(MGD_OFF // half), gate(MGD_OFF // half + 1),
                  resident((k_dim, d)), resident((k_dim, d)), resident((d, d)),
                  pl.BlockSpec((1, tm, d), lambda b, i: (b, i, 0)),
                  pl.BlockSpec((1, 3, d), lambda b, i: (b, 0, 0)),
                  pl.BlockSpec((1, d), lambda b, i: (0, 0))],
        out_specs=pl.BlockSpec((1, tm, d), lambda b, i: (b, i, 0)),
        out_shape=jax.ShapeDtypeStruct((nb, s, d), F32),
        compiler_params=_cparams(("arbitrary", "arbitrary")),
        name="merge_out",
    )(og_mla, og_diff, rest, rest, rest, rest, w1b, w2b, wob, x, ada3, g_post)


def kernel(x, c, positions, w_ada, b_ada, g_pre, w_in, g_kv, w_ukv, lambda_q1, lambda_k1,
           lambda_q2, lambda_k2, g_subln, w_o_mla, w_o_diff, w_out, g_post):
    nb, s, d = x.shape
    depth = w_in.shape[0]
    assert w_in.shape[1:] == (d, REST_OFF + REST_W) and w_out.shape[1:] == (d, d)
    assert s % ATTN_TQ == 0 and s % MLA_TQ == 0 and (nb * s) % PROJ_TM == 0
    half = MLA_ROPE // 2
    inv = ROPE_THETA ** (-jnp.arange(half, dtype=F32) / half)
    inv_tab = jnp.tile(inv, LANES // half).reshape(1, LANES)
    slopes = 2.0 ** (-8.0 * jnp.arange(1, DIFF_HEADS + 1, dtype=F32) / DIFF_HEADS)
    slopes = jnp.broadcast_to(slopes.reshape(DIFF_HEADS, 1, 1), (DIFF_HEADS, 1, LANES))
    pos_col = positions.reshape(nb * s, 1)
    pos_colf = pos_col.astype(F32)
    pos_chunks = positions.reshape(nb, s // ATTN_TQ, ATTN_TQ)
    pos_row = pos_chunks.astype(F32)
    run_max = lax.cummax(pos_chunks.max(axis=-1), axis=1)
    prev_max = jnp.concatenate(
        [jnp.full((nb, 1), jnp.iinfo(jnp.int32).min, jnp.int32), run_max[:, :-1]], axis=1)
    tile_sorted = jnp.all(pos_chunks[..., 1:] >= pos_chunks[..., :-1], axis=-1)
    ordered = ((prev_max <= pos_chunks.min(axis=-1)).astype(jnp.int32)
               + 2 * tile_sorted.astype(jnp.int32))

    for l in range(depth):
        ada3 = _ada(c, w_ada[l], b_ada[l]).reshape(nb, 3, d)
        h, cos_tab, sin_tab, pos_lanes = _norm(x, ada3, g_pre[l].reshape(1, d), pos_col, inv_tab)
        w_t = jnp.swapaxes(w_in[l], 0, 1)
        qm, rest = _proj(h, w_t, cos_tab, sin_tab)
        kk, vv = _kv(h, w_t, g_kv[l].reshape(1, KV_RANK), w_ukv[l], cos_tab, sin_tab)
        og_mla = _mla_attn(qm, kk, vv, rest, nb, s)
        lam_par = jnp.stack([lambda_q1[l], lambda_k1[l], lambda_q2[l], lambda_k2[l]]).astype(F32)
        og_diff = _diff_attn(rest, pos_colf, pos_row, pos_lanes, ordered, slopes, lam_par,
                             g_subln[l].reshape(1, DIFF_V), 0.8 - 0.6 * math.exp(-0.3 * l), nb, s)
        w1b, w2b, wob = _cast_weights(w_o_mla[l], w_o_diff[l], w_out[l])
        x = _merge_out(og_mla, og_diff, rest, w1b, w2b, wob, x, ada3, g_post[l].reshape(1, d))
    return x
```

```python
import functools
import math

import jax
import jax.numpy as jnp
from jax import lax
from jax.experimental import pallas as pl
from jax.experimental.pallas import tpu as pltpu

F32 = jnp.float32
BF16 = jnp.bfloat16

MLA_HEADS = 8
MLA_NOPE = 128
MLA_ROPE = 64
MLA_V = 128
KV_RANK = 512
ROPE_THETA = 10000.0
DIFF_HEADS = 8
DIFF_QK = 64
DIFF_V = 128
EPS = 1e-6
NEG = -1e30

LANES = 128
SUBLANES = 8
VMEM_LIMIT = 56 * 1024 * 1024
ADA_TN = 512
NORM_TS = 512
N_PIECES = 3
N_PAIR_LANES = N_PIECES * N_PIECES
ATTN_TQ = 1024
DIFF_ROW_SPLIT = 2
MLA_TQ = 1024
MLA_ROW_SPLIT = 2
PROJ_TM = 2048
REST_TN = 1024
QMLA_PAIRS_PER_TILE = 2
KV_TM = 1024
CAST_STEPS = 4
OUT_TM = 512
EPILOGUE_ROW_SPLIT = 2
LOG2E = math.log2(math.e)

Q_MLA_W = MLA_HEADS * (MLA_NOPE + MLA_ROPE)
KVR_W = KV_RANK + MLA_ROPE
KVR_PAD = 640
REST_OFF = Q_MLA_W + KVR_W
QD_OFF, KD_OFF, VD_OFF, GM_OFF, GD_OFF, MGM_OFF, MGD_OFF = 0, 1024, 2048, 3072, 4096, 5120, 7168
REST_W = 9216


def _cparams(sem):
    return pltpu.CompilerParams(dimension_semantics=sem, vmem_limit_bytes=VMEM_LIMIT)


def _ada_kernel(cb_ref, w_ref, b_ref, o_ref):
    k_dim, tn = w_ref.shape
    nb = cb_ref.shape[0]
    nchunk = tn // LANES

    def body(i, accs):
        k0 = pl.multiple_of(i * SUBLANES, SUBLANES)
        out = []
        for b in range(nb):
            cv = cb_ref[b, pl.ds(k0, SUBLANES), :]
            for j in range(nchunk):
                wv = w_ref[pl.ds(k0, SUBLANES), j * LANES:(j + 1) * LANES]
                out.append(accs[b * nchunk + j] + wv * cv)
        return tuple(out)

    init = tuple(jnp.zeros((SUBLANES, LANES), F32) for _ in range(nb * nchunk))
    accs = lax.fori_loop(0, k_dim // SUBLANES, body, init, unroll=8)
    for b in range(nb):
        row = jnp.concatenate(
            [jnp.sum(accs[b * nchunk + j], axis=0, keepdims=True) for j in range(nchunk)], axis=1)
        o_ref[b:b + 1, :] = row + b_ref[...]


def _ada(c, w, bias):
    nb, k_dim = c.shape
    n = w.shape[1]
    tn = ADA_TN
    cb = jnp.broadcast_to(c[:, :, None], (nb, k_dim, LANES))
    return pl.pallas_call(
        _ada_kernel,
        grid=(n // tn,),
        in_specs=[pl.BlockSpec((nb, k_dim, LANES), lambda j: (0, 0, 0)),
                  pl.BlockSpec((k_dim, tn), lambda j: (0, j)),
                  pl.BlockSpec((1, tn), lambda j: (0, j))],
        out_specs=pl.BlockSpec((nb, tn), lambda j: (0, j)),
        out_shape=jax.ShapeDtypeStruct((nb, n), F32),
        compiler_params=_cparams(("arbitrary",)),
        name="ada",
    )(cb, w, bias.reshape(1, n))


def _bf16_pieces(x):
    p1 = x.astype(BF16)
    r = x - p1.astype(F32)
    p2 = r.astype(BF16)
    return p1.astype(F32), p2.astype(F32), r - p2.astype(F32)


def _pick3(sel, x3):
    return jnp.where(sel == 0, x3[0], jnp.where(sel == 1, x3[1], x3[2]))


def _norm_kernel(x_ref, ada_ref, g_ref, pos_ref, inv_ref, h_ref, cos_ref, sin_ref, plane_ref):
    x = x_ref[0]
    ms = jnp.mean(x * x, axis=-1, keepdims=True)
    y = x * lax.rsqrt(ms + EPS) * g_ref[...]
    shift = ada_ref[0, 0:1, :]
    scale = ada_ref[0, 1:2, :]
    h_ref[...] = (y * (1.0 + scale) + shift).astype(BF16)
    ang = pos_ref[...].astype(F32) * inv_ref[...]
    lane = lax.broadcasted_iota(jnp.int32, ang.shape, 1)
    sign = jnp.where((lane % MLA_ROPE) < MLA_ROPE // 2, -1.0, 1.0).astype(F32)
    cos_ref[...] = jnp.cos(ang)
    sin_ref[...] = jnp.sin(ang) * sign
    pos3 = _bf16_pieces(pos_ref[...].astype(F32))
    piece = _pick3(lane % N_PIECES, pos3)
    plane_ref[...] = jnp.where(lane < N_PAIR_LANES, piece,
                               jnp.where(lane < 2 * N_PAIR_LANES, -piece, 0.0)).astype(BF16)


def _norm(x, ada3, g_pre, pos_col, inv_tab):
    nb, s, d = x.shape
    ts = NORM_TS
    ns = s // ts
    row = lambda b, i: (b * ns + i, 0)
    return pl.pallas_call(
        _norm_kernel,
        grid=(nb, ns),
        in_specs=[pl.BlockSpec((1, ts, d), lambda b, i: (b, i, 0)),
                  pl.BlockSpec((1, 3, d), lambda b, i: (b, 0, 0)),
                  pl.BlockSpec((1, d), lambda b, i: (0, 0)),
                  pl.BlockSpec((ts, 1), row),
                  pl.BlockSpec((1, LANES), lambda b, i: (0, 0))],
        out_specs=[pl.BlockSpec((ts, d), row),
                   pl.BlockSpec((ts, LANES), row),
                   pl.BlockSpec((ts, LANES), row),
                   pl.BlockSpec((ts, LANES), row)],
        out_shape=[jax.ShapeDtypeStruct((nb * s, d), BF16),
                   jax.ShapeDtypeStruct((nb * s, LANES), F32),
                   jax.ShapeDtypeStruct((nb * s, LANES), F32),
                   jax.ShapeDtypeStruct((nb * s, LANES), BF16)],
        compiler_params=_cparams(("arbitrary", "arbitrary")),
        name="prenorm",
    )(x, ada3, g_pre, pos_col, inv_tab)


def _rope_cols(r, cos, sin_signed):
    lane = lax.broadcasted_iota(jnp.int32, r.shape, 1)
    half = MLA_ROPE // 2
    partner = jnp.where((lane % MLA_ROPE) < half,
                        pltpu.roll(r, LANES - half, 1), pltpu.roll(r, half, 1))
    return r * cos + partner * sin_signed


def _qmla_kernel(a_ref, w_ref, cos_ref, sin_ref, o_ref, wb_ref, *, scale):
    hd = MLA_NOPE + MLA_ROPE
    pair_w = 2 * hd
    n_pairs = wb_ref.shape[0] // pair_w

    @pl.when(pl.program_id(1) == 0)
    def _():
        for p in range(n_pairs):
            src, dst = w_ref.at[p * pair_w:(p + 1) * pair_w], wb_ref.at[p * pair_w:(p + 1) * pair_w]
            dst[:MLA_NOPE] = src[:MLA_NOPE].astype(BF16)
            dst[MLA_NOPE:2 * MLA_NOPE] = src[hd:hd + MLA_NOPE].astype(BF16)
            dst[2 * MLA_NOPE:2 * MLA_NOPE + MLA_ROPE] = src[MLA_NOPE:hd].astype(BF16)
            dst[2 * MLA_NOPE + MLA_ROPE:] = src[hd + MLA_NOPE:].astype(BF16)

    acc = lax.dot_general(a_ref[...], wb_ref[...], _NT, preferred_element_type=F32)
    for p in range(n_pairs):
        c0 = p * pair_w
        rr = _rope_cols(acc[:, c0 + 2 * MLA_NOPE:c0 + pair_w], cos_ref[...], sin_ref[...])
        o_ref[:, c0:c0 + 2 * MLA_NOPE] = (acc[:, c0:c0 + 2 * MLA_NOPE] * scale).astype(BF16)
        o_ref[:, c0 + 2 * MLA_NOPE:c0 + pair_w] = (rr * scale).astype(BF16)


def _sigmoid(x):
    return 0.5 * jnp.tanh(0.5 * x) + 0.5


def _rest_kernel(a_ref, w_ref, o_ref, wb_ref, *, tn):
    j = pl.program_id(0)

    @pl.when(pl.program_id(1) == 0)
    def _():
        wb_ref[...] = w_ref[...].astype(BF16)

    def tile(epilogue):
        acc = lax.dot_general(a_ref[...], wb_ref[...], _NT, preferred_element_type=F32)
        o_ref[...] = epilogue(acc).astype(BF16)

    @pl.when(j < KD_OFF // tn)
    def _():
        tile(lambda acc: acc * (DIFF_QK ** -0.5 * LOG2E))

    @pl.when(jnp.logical_and(j >= KD_OFF // tn, j < GM_OFF // tn))
    def _():
        tile(lambda acc: acc)

    @pl.when(jnp.logical_and(j >= GM_OFF // tn, j < MGM_OFF // tn))
    def _():
        tile(lambda acc: acc * _sigmoid(acc))

    @pl.when(j >= MGM_OFF // tn)
    def _():
        tile(_sigmoid)


def _proj(h, w_t, cos_tab, sin_tab):
    m, k_dim = h.shape
    tm = PROJ_TM
    pair_w = 2 * (MLA_NOPE + MLA_ROPE)
    q_scale = (MLA_NOPE + MLA_ROPE) ** -0.5 * LOG2E

    def row_window(rows, offset_fn):
        return pl.BlockSpec((pl.Element(rows), pl.Element(k_dim)),
                            lambda *g: (pl.multiple_of(offset_fn(*g), SUBLANES), 0))

    qtn = QMLA_PAIRS_PER_TILE * pair_w
    qm = pl.pallas_call(
        functools.partial(_qmla_kernel, scale=q_scale),
        grid=(Q_MLA_W // qtn, m // tm),
        in_specs=[pl.BlockSpec((tm, k_dim), lambda j, i: (i, 0)),
                  pl.BlockSpec((qtn, k_dim), lambda j, i: (j, 0)),
                  pl.BlockSpec((tm, LANES), lambda j, i: (i, 0)),
                  pl.BlockSpec((tm, LANES), lambda j, i: (i, 0))],
        out_specs=pl.BlockSpec((tm, qtn), lambda j, i: (i, j)),
        out_shape=jax.ShapeDtypeStruct((m, Q_MLA_W), BF16),
        scratch_shapes=[pltpu.VMEM((qtn, k_dim), BF16)],
        compiler_params=_cparams(("arbitrary", "arbitrary")),
        name="proj_qmla",
    )(h, w_t, cos_tab, sin_tab)
    tn = REST_TN
    rest = pl.pallas_call(
        functools.partial(_rest_kernel, tn=tn),
        grid=(REST_W // tn, m // tm),
        in_specs=[pl.BlockSpec((tm, k_dim), lambda j, i: (i, 0)),
                  row_window(tn, lambda j, i: REST_OFF + tn * j)],
        out_specs=pl.BlockSpec((tm, tn), lambda j, i: (i, j)),
        out_shape=jax.ShapeDtypeStruct((m, REST_W), BF16),
        scratch_shapes=[pltpu.VMEM((tn, k_dim), BF16)],
        compiler_params=_cparams(("arbitrary", "arbitrary")),
        name="proj_rest",
    )(h, w_t)
    return qm, rest


def _kv_kernel(a_ref, wp_ref, g_ref, w_ref, cos_ref, sin_ref, k_ref, v_ref, wpb_ref, wb_ref):
    @pl.when(pl.program_id(0) == 0)
    def _():
        wpb_ref[:KVR_W] = wp_ref[...].astype(BF16)
        wpb_ref[KVR_W:] = jnp.zeros((KVR_PAD - KVR_W, wpb_ref.shape[1]), BF16)
        wb_ref[...] = w_ref[...].astype(BF16)

    kw = MLA_NOPE + MLA_V
    rb = a_ref.shape[0] // EPILOGUE_ROW_SPLIT
    for t in range(EPILOGUE_ROW_SPLIT):
        rows = slice(t * rb, (t + 1) * rb)
        p = lax.dot_general(a_ref[rows, :], wpb_ref[...], _NT, preferred_element_type=F32)
        ckv = p[:, :KV_RANK]
        ms = jnp.mean(ckv * ckv, axis=-1, keepdims=True)
        n = (ckv * lax.rsqrt(ms + EPS) * g_ref[...]).astype(BF16)
        kv = jnp.dot(n, wb_ref[...], preferred_element_type=F32)
        kr_even = _rope_cols(p[:, KV_RANK:], cos_ref[rows, :], sin_ref[rows, :])
        kr_odd = pltpu.roll(kr_even, MLA_ROPE, 1)
        ones_col = jnp.ones(kr_even.shape, BF16)
        for hd in range(MLA_HEADS):
            k_ref[rows, hd * kw:hd * kw + MLA_NOPE] = kv[:, hd * kw:hd * kw + MLA_NOPE].astype(BF16)
            k_ref[rows, hd * kw + MLA_NOPE:(hd + 1) * kw] = (
                kr_even if hd % 2 == 0 else kr_odd).astype(BF16)
            v_ref[rows, hd * kw:hd * kw + MLA_V] = kv[:, hd * kw + MLA_NOPE:(hd + 1) * kw].astype(BF16)
            v_ref[rows, hd * kw + MLA_V:(hd + 1) * kw] = ones_col


def _kv(h, w_t, g_kv, w_ukv, cos_tab, sin_tab):
    m, k_dim = h.shape
    tm = KV_TM
    kw = MLA_HEADS * (MLA_NOPE + MLA_V)
    return pl.pallas_call(
        _kv_kernel,
        grid=(m // tm,),
        in_specs=[pl.BlockSpec((tm, k_dim), lambda i: (i, 0)),
                  pl.BlockSpec((pl.Element(KVR_W), pl.Element(k_dim)),
                               lambda i: (pl.multiple_of(Q_MLA_W + 0 * i, SUBLANES), 0)),
                  pl.BlockSpec((1, KV_RANK), lambda i: (0, 0)),
                  pl.BlockSpec((KV_RANK, kw), lambda i: (0, 0)),
                  pl.BlockSpec((tm, LANES), lambda i: (i, 0)),
                  pl.BlockSpec((tm, LANES), lambda i: (i, 0))],
        out_specs=[pl.BlockSpec((tm, kw), lambda i: (i, 0)),
                   pl.BlockSpec((tm, kw), lambda i: (i, 0))],
        out_shape=[jax.ShapeDtypeStruct((m, kw), BF16),
                   jax.ShapeDtypeStruct((m, kw), BF16)],
        scratch_shapes=[pltpu.VMEM((KVR_PAD, k_dim), BF16), pltpu.VMEM((KV_RANK, kw), BF16)],
        compiler_params=_cparams(("arbitrary",)),
        name="kv_up",
    )(h, w_t, g_kv, w_ukv, cos_tab, sin_tab)


def _flash_pipeline(n_chains, nq, diags, loops, value_fn, finalize_fn,
                    s_ref, acc_ref, mpart_ref, macc_ref, sd_ref=None, mpd_ref=None):
    assert sd_ref is None or nq % 2 == 0
    chunk = s_ref.shape[2]

    def lane_tiles(x, n):
        return jnp.concatenate([x] * n, axis=1)

    def qk_phase(scores, stage):
        s_dst, m_dst = stage
        for ci, s in enumerate(scores):
            s_dst[ci] = s
            part = s[:, :LANES]
            for j in range(1, chunk // LANES):
                part = jnp.maximum(part, s[:, j * LANES:(j + 1) * LANES])
            m_dst[ci] = part

    def pv_phase(kc, stage):
        s_src, m_src = stage
        for ci in range(n_chains):
            m_acc = macc_ref[ci]
            m_run = jnp.maximum(m_acc, jnp.max(m_src[ci], axis=-1, keepdims=True))
            macc_ref[ci] = m_run
            p = jnp.exp2(s_src[ci] - lane_tiles(m_run, chunk // LANES))
            alpha = jnp.exp2(m_acc - m_run)
            pv = jnp.dot(p.astype(BF16), value_fn(ci, kc), preferred_element_type=F32)
            acc_ref[ci] = lane_tiles(alpha, acc_ref.shape[2] // LANES) * acc_ref[ci] + pv

    def reset():
        acc_ref[...] = jnp.zeros_like(acc_ref)
        macc_ref[...] = jnp.full(macc_ref.shape, NEG, F32)

    def tile(qt, stage, next_stage):
        cur = qt
        for trips_fn, score_fn in loops:
            def step(kc, cur, score_fn=score_fn):
                pv_phase(cur, stage)
                qk_phase(score_fn(qt, kc), stage)
                return kc

            cur = lax.fori_loop(0, trips_fn(qt), step, cur)
        nxt = jnp.minimum(qt + 1, nq - 1)

        def transition(_, cur, fn):
            if next_stage is stage:
                pv_phase(cur, stage)
                finalize_fn(qt)
                reset()
                qk_phase(fn(nxt), stage)
            else:
                qk_phase(fn(nxt), next_stage)
                pv_phase(cur, stage)
                finalize_fn(qt)
                reset()
            return cur

        for select_fn, fn in diags:
            if select_fn is None:
                transition(0, cur, fn)
            else:
                lax.fori_loop(0, select_fn(nxt), functools.partial(transition, fn=fn), cur)

    stage_a = (s_ref, mpart_ref)
    reset()
    for select_fn, fn in diags:
        if select_fn is None:
            qk_phase(fn(0), stage_a)
        else:
            lax.fori_loop(0, select_fn(0), lambda _, c, fn=fn: qk_phase(fn(0), stage_a) or c, 0)
    if sd_ref is None:
        lax.fori_loop(0, nq, lambda qt, c: tile(qt, stage_a, stage_a) or c, 0)
    else:
        stage_b = (sd_ref, mpd_ref)

        def tile_pair(j, c):
            tile(2 * j, stage_a, stage_b)
            tile(2 * j + 1, stage_b, stage_a)
            return c

        lax.fori_loop(0, nq // 2, tile_pair, 0)


def _causal_mask(rows, cols, row0):
    row = lax.broadcasted_iota(jnp.int32, (rows, cols), 0) + row0
    col = lax.broadcasted_iota(jnp.int32, (rows, cols), 1)
    return col <= row


def _pad_masked(s, cols):
    if s.shape[1] == cols:
        return s
    return jnp.concatenate([s, jnp.full((s.shape[0], cols - s.shape[1]), NEG, s.dtype)], axis=1)


_NT = (((1,), (1,)), ((), ()))


def _mla_attn_kernel(q_ref, k_ref, v_ref, g_ref, o_ref, acc_ref, qs_ref, s_ref, mpart_ref, macc_ref,
                     *, tq, rs):
    kw = 2 * MLA_NOPE
    tr = tq // rs
    nq = q_ref.shape[0] // tq
    chains = [(hp, r) for hp in range(2) for r in range(rs)]

    def stage_queries(qt):
        rows = pl.ds(pl.multiple_of(qt * tq, tq), tq)
        for hp in range(2):
            qs_ref[hp, :, :MLA_NOPE] = q_ref[rows, hp * MLA_NOPE:(hp + 1) * MLA_NOPE]
            qs_ref[hp, :, MLA_NOPE:] = q_ref[rows, 2 * MLA_NOPE:]

    def scores(qt, kc, diag):
        if diag:
            stage_queries(qt)
        k0 = pl.multiple_of(kc * tq, tq)
        out = []
        for hp, r in chains:
            nk = (r + 1) * tr if diag else tq
            q = qs_ref[hp, r * tr:(r + 1) * tr, :]
            k = k_ref[pl.ds(k0, nk), hp * kw:(hp + 1) * kw]
            s = lax.dot_general(q, k, _NT, preferred_element_type=F32)
            if diag:
                s = _pad_masked(jnp.where(_causal_mask(tr, nk, r * tr), s, NEG), tq)
            out.append(s)
        return out

    def values(ci, kc):
        hp = chains[ci][0]
        return v_ref[pl.ds(pl.multiple_of(kc * tq, tq), tq), hp * kw:(hp + 1) * kw]

    def finalize(qt):
        for ci, (hp, r) in enumerate(chains):
            rows = pl.ds(pl.multiple_of(qt * tq + r * tr, tr), tr)
            acc = acc_ref[ci]
            o = acc[:, :MLA_V] / acc[:, MLA_V:]
            gate = g_ref[rows, hp * MLA_V:(hp + 1) * MLA_V].astype(F32)
            o_ref[rows, hp * MLA_V:(hp + 1) * MLA_V] = (o * gate).astype(BF16)

    _flash_pipeline(len(chains), nq, [(None, lambda qt: scores(qt, qt, True))],
                    [(lambda qt: qt, lambda qt, kc: scores(qt, kc, False))],
                    values, finalize, s_ref, acc_ref, mpart_ref, macc_ref)


def _mla_attn(qm, kk, vv, rest, nb, s):
    tq = MLA_TQ
    pair_w = 2 * (MLA_NOPE + MLA_ROPE)
    kw = 4 * MLA_NOPE
    gate_blk = GM_OFF // (2 * MLA_V)
    rs = MLA_ROW_SPLIT
    return pl.pallas_call(
        functools.partial(_mla_attn_kernel, tq=tq, rs=rs),
        grid=(nb, MLA_HEADS // 2),
        in_specs=[pl.BlockSpec((s, pair_w), lambda b, hh: (b, hh)),
                  pl.BlockSpec((s, kw), lambda b, hh: (b, hh)),
                  pl.BlockSpec((s, kw), lambda b, hh: (b, hh)),
                  pl.BlockSpec((s, 2 * MLA_V), lambda b, hh: (b, gate_blk + hh))],
        out_specs=pl.BlockSpec((s, 2 * MLA_V), lambda b, hh: (b, hh)),
        out_shape=jax.ShapeDtypeStruct((nb * s, MLA_HEADS * MLA_V), BF16),
        scratch_shapes=[pltpu.VMEM((2 * rs, tq // rs, 2 * MLA_V), F32),
                        pltpu.VMEM((2, tq, 2 * MLA_NOPE), BF16),
                        pltpu.VMEM((2 * rs, tq // rs, tq), F32),
                        pltpu.VMEM((2 * rs, tq // rs, LANES), F32),
                        pltpu.VMEM((2 * rs, tq // rs, LANES), F32)],
        compiler_params=_cparams(("arbitrary", "arbitrary")),
        name="mla_attn",
    )(qm, kk, vv, rest)


def _diff_attn_kernel(ord_ref, q_ref, k_ref, v_ref, g_ref, pq_ref, pk_ref, plane_ref, sl_ref, lam_ref,
                      gs_ref, o_ref, vaug_ref, kaug_ref, acc_ref, qf_ref, s_ref, mpart_ref, macc_ref,
                      sd_ref, mpd_ref, *, tq, rs, lambda_init):
    seq = q_ref.shape[0]
    nq = seq // tq
    tr = tq // rs
    chains = [(c, r) for c in range(2) for r in range(rs)]
    lane = lax.broadcasted_iota(jnp.int32, (tq, LANES), 1)
    ones_col = jnp.ones((tq, LANES), BF16)
    slope2 = sl_ref[0, :, 0:1] * LOG2E
    c_pieces = _bf16_pieces(slope2)
    lane_row = lane[:1]
    c_query = jnp.where(lane_row < N_PAIR_LANES, _pick3(lane_row // N_PIECES, c_pieces), 0.0).astype(BF16)
    c_key = jnp.where(jnp.logical_and(lane_row >= N_PAIR_LANES, lane_row < 2 * N_PAIR_LANES),
                      _pick3((lane_row - N_PAIR_LANES) // N_PIECES, c_pieces), 0.0).astype(BF16)

    def bias_lanes(rows, key_side):
        pos_lanes = plane_ref[rows, :]
        if key_side:
            return jnp.where(lane < N_PAIR_LANES, pos_lanes, c_key)
        return jnp.where(lane < N_PAIR_LANES, c_query, pos_lanes)

    def stage_keys(kc, _):
        rows = pl.ds(pl.multiple_of(kc * tq, tq), tq)
        vaug_ref[rows, :DIFF_V] = v_ref[rows, :]
        vaug_ref[rows, DIFF_V:] = ones_col
        kaug_ref[rows, :2 * DIFF_QK] = k_ref[rows, :]
        kaug_ref[rows, 2 * DIFF_QK:] = bias_lanes(rows, True)
        return 0

    lax.fori_loop(0, nq, stage_keys, 0)

    def stage_queries(qt):
        rows = pl.ds(pl.multiple_of(qt * tq, tq), tq)
        q = q_ref[rows, :]
        q_side = bias_lanes(rows, False)
        qf_ref[0, :, :2 * DIFF_QK] = jnp.where(lane < DIFF_QK, q, 0).astype(BF16)
        qf_ref[1, :, :2 * DIFF_QK] = jnp.where(lane >= DIFF_QK, q, 0).astype(BF16)
        qf_ref[0, :, 2 * DIFF_QK:] = q_side
        qf_ref[1, :, 2 * DIFF_QK:] = q_side

    lq = lam_ref[...]
    lam = (jnp.exp(jnp.sum(lq[0:1] * lq[1:2], axis=-1, keepdims=True))
           - jnp.exp(jnp.sum(lq[2:3] * lq[3:4], axis=-1, keepdims=True)) + lambda_init)

    def scores(qt, kc, diag):
        k0 = pl.multiple_of(kc * tq, tq)
        k = k_ref[pl.ds(k0, tq), :]
        pk = slope2 * pk_ref[0, pl.ds(kc, 1), :]
        out = [None] * len(chains)
        for r in range(rs):
            pq = pq_ref[pl.ds(pl.multiple_of(qt * tq + r * tr, tr), tr), :]
            bias = jnp.abs(slope2 * pq - pk)
            mask = _causal_mask(tr, tq, r * tr) if diag else None
            for c in range(2):
                q = qf_ref[c, r * tr:(r + 1) * tr, :2 * DIFF_QK]
                s = lax.dot_general(q, k, _NT, preferred_element_type=F32) - bias
                out[chains.index((c, r))] = jnp.where(mask, s, NEG) if diag else s
        return out

    def scores_diag(qt):
        stage_queries(qt)
        return scores(qt, qt, True)

    def scores_ordered(qt, kc):
        k = kaug_ref[pl.ds(pl.multiple_of(kc * tq, tq), tq), :]
        return [lax.dot_general(qf_ref[c, r * tr:(r + 1) * tr, :], k, _NT, preferred_element_type=F32)
                for c, r in chains]

    def scores_diag_sorted(qt):
        stage_queries(qt)
        k0 = pl.multiple_of(qt * tq, tq)
        out = []
        for c, r in chains:
            nk = (r + 1) * tr
            s = lax.dot_general(qf_ref[c, r * tr:(r + 1) * tr, :], kaug_ref[pl.ds(k0, nk), :], _NT,
                                preferred_element_type=F32)
            out.append(_pad_masked(jnp.where(_causal_mask(tr, nk, r * tr), s, NEG), tq))
        return out

    batch = pl.program_id(0)

    def trips_ordered(qt):
        return jnp.where((ord_ref[batch, qt] & 1) != 0, qt, 0)

    def tile_sorted(qt):
        return (ord_ref[batch, qt] >> 1) & 1

    def values(ci, kc):
        return vaug_ref[pl.ds(pl.multiple_of(kc * tq, tq), tq), :]

    def finalize(qt):
        for r in range(rs):
            rows = pl.ds(pl.multiple_of(qt * tq + r * tr, tr), tr)
            a1 = acc_ref[chains.index((0, r))]
            a2 = acc_ref[chains.index((1, r))]
            o = a1[:, :DIFF_V] / a1[:, DIFF_V:] - lam * (a2[:, :DIFF_V] / a2[:, DIFF_V:])
            ms_o = jnp.mean(o * o, axis=-1, keepdims=True)
            o = o * lax.rsqrt(ms_o + EPS) * gs_ref[...] * (1.0 - lambda_init)
            o_ref[rows, :] = (o * g_ref[rows, :].astype(F32)).astype(BF16)

    _flash_pipeline(len(chains), nq,
                    [(tile_sorted, scores_diag_sorted), (lambda qt: 1 - tile_sorted(qt), scores_diag)],
                    [(trips_ordered, scores_ordered),
                     (lambda qt: qt - trips_ordered(qt), lambda qt, kc: scores(qt, kc, False))],
                    values, finalize, s_ref, acc_ref, mpart_ref, macc_ref, sd_ref, mpd_ref)


def _diff_attn(rest, pos_col, pos_row, pos_lanes, ordered, slopes, lam_par, g_subln, lambda_init,
               nb, s):
    tq = ATTN_TQ
    nq = s // tq
    rs = DIFF_ROW_SPLIT
    tr = tq // rs
    hw = DIFF_V
    return pl.pallas_call(
        functools.partial(_diff_attn_kernel, tq=tq, rs=rs, lambda_init=lambda_init),
        grid=(nb, DIFF_HEADS),
        in_specs=[pl.BlockSpec(memory_space=pltpu.SMEM),
                  pl.BlockSpec((s, hw), lambda b, hd: (b, QD_OFF // hw + hd)),
                  pl.BlockSpec((s, hw), lambda b, hd: (b, KD_OFF // hw + hd)),
                  pl.BlockSpec((s, hw), lambda b, hd: (b, VD_OFF // hw + hd)),
                  pl.BlockSpec((s, hw), lambda b, hd: (b, GD_OFF // hw + hd)),
                  pl.BlockSpec((s, 1), lambda b, hd: (b, 0)),
                  pl.BlockSpec((1, nq, tq), lambda b, hd: (b, 0, 0)),
                  pl.BlockSpec((s, LANES), lambda b, hd: (b, 0)),
                  pl.BlockSpec((1, 1, LANES), lambda b, hd: (hd, 0, 0)),
                  pl.BlockSpec((4, DIFF_QK), lambda b, hd: (0, 0)),
                  pl.BlockSpec((1, DIFF_V), lambda b, hd: (0, 0))],
        out_specs=pl.BlockSpec((s, hw), lambda b, hd: (b, hd)),
        out_shape=jax.ShapeDtypeStruct((nb * s, DIFF_HEADS * DIFF_V), BF16),
        scratch_shapes=[pltpu.VMEM((s, 2 * DIFF_V), BF16),
                        pltpu.VMEM((s, 4 * DIFF_QK), BF16),
                        pltpu.VMEM((2 * rs, tr, 2 * DIFF_V), F32),
                        pltpu.VMEM((2, tq, 4 * DIFF_QK), BF16),
                        pltpu.VMEM((2 * rs, tr, tq), F32),
                        pltpu.VMEM((2 * rs, tr, LANES), F32),
                        pltpu.VMEM((2 * rs, tr, LANES), F32),
                        pltpu.VMEM((2 * rs, tr, tq), F32),
                        pltpu.VMEM((2 * rs, tr, LANES), F32)],
        compiler_params=_cparams(("arbitrary", "arbitrary")),
        name="diff_attn",
    )(ordered, rest, rest, rest, rest, pos_col, pos_row, pos_lanes, slopes, lam_par, g_subln)


def _cast_kernel(w1_ref, w2_ref, w3_ref, o1_ref, o2_ref, o3_ref):
    o1_ref[...] = w1_ref[...].astype(BF16)
    o2_ref[...] = w2_ref[...].astype(BF16)
    o3_ref[...] = w3_ref[...].astype(BF16)


def _cast_weights(w1, w2, w3):
    steps = CAST_STEPS
    specs = [pl.BlockSpec((w.shape[0] // steps, w.shape[1]), lambda i: (i, 0)) for w in (w1, w2, w3)]
    return pl.pallas_call(
        _cast_kernel,
        grid=(steps,),
        in_specs=specs,
        out_specs=specs,
        out_shape=[jax.ShapeDtypeStruct(w.shape, BF16) for w in (w1, w2, w3)],
        compiler_params=_cparams(("arbitrary",)),
        name="cast_weights",
    )(w1, w2, w3)


def _merge_out_kernel(a1_ref, a2_ref, s1a_ref, s1b_ref, s2a_ref, s2b_ref, w1_ref, w2_ref, wo_ref,
                      x_ref, ada_ref, g_ref, o_ref):
    half = s1a_ref.shape[1]
    rb = a1_ref.shape[0] // EPILOGUE_ROW_SPLIT
    for t in range(EPILOGUE_ROW_SPLIT):
        rows = slice(t * rb, (t + 1) * rb)
        y1 = jnp.dot(a1_ref[rows, :], w1_ref[...], preferred_element_type=F32)
        y2 = jnp.dot(a2_ref[rows, :], w2_ref[...], preferred_element_type=F32)
        merged = jnp.concatenate(
            [s1a_ref[rows, :].astype(F32) * y1[:, :half] + s2a_ref[rows, :].astype(F32) * y2[:, :half],
             s1b_ref[rows, :].astype(F32) * y1[:, half:] + s2b_ref[rows, :].astype(F32) * y2[:, half:]],
            axis=1).astype(BF16)
        y = jnp.dot(merged, wo_ref[...], preferred_element_type=F32)
        ms = jnp.mean(y * y, axis=-1, keepdims=True)
        yn = y * lax.rsqrt(ms + EPS) * g_ref[...]
        o_ref[0, rows, :] = x_ref[0, rows, :] + ada_ref[0, 2:3, :] * yn


def _merge_out(og_mla, og_diff, rest, w1b, w2b, wob, x, ada3, g_post):
    nb, s, d = x.shape
    k_dim = og_mla.shape[1]
    tm = OUT_TM
    ns = s // tm
    half = d // 2
    row = lambda b, i: (b * ns + i, 0)
    gate = lambda blk: pl.BlockSpec((tm, half), lambda b, i: (b * ns + i, blk))
    resident = lambda shape: pl.BlockSpec(shape, lambda b, i: (0, 0), pipeline_mode=pl.Buffered(1))
    return pl.pallas_call(
        _merge_out_kernel,
        grid=(nb, ns),
        in_specs=[pl.BlockSpec((tm, k_dim), row),
                  pl.BlockSpec((tm, k_dim), row),
                  gate(MGM_OFF // half), gate(MGM_OFF // half + 1),
                  gate(MGD_OFF // half), gate(MGD_OFF // half + 1),
                  resident((k_dim, d)), resident((k_dim, d)), resident((d, d)),
                  pl.BlockSpec((1, tm, d), lambda b, i: (b, i, 0)),
                  pl.BlockSpec((1, 3, d), lambda b, i: (b, 0, 0)),
                  pl.BlockSpec((1, d), lambda b, i: (0, 0))],
        out_specs=pl.BlockSpec((1, tm, d), lambda b, i: (b, i, 0)),
        out_shape=jax.ShapeDtypeStruct((nb, s, d), F32),
        compiler_params=_cparams(("arbitrary", "arbitrary")),
        name="merge_out",
    )(og_mla, og_diff, rest, rest, rest, rest, w1b, w2b, wob, x, ada3, g_post)


def kernel(x, c, positions, w_ada, b_ada, g_pre, w_in, g_kv, w_ukv, lambda_q1, lambda_k1,
           lambda_q2, lambda_k2, g_subln, w_o_mla, w_o_diff, w_out, g_post):
    nb, s, d = x.shape
    depth = w_in.shape[0]
    assert w_in.shape[1:] == (d, REST_OFF + REST_W) and w_out.shape[1:] == (d, d)
    assert s % ATTN_TQ == 0 and s % MLA_TQ == 0 and (nb * s) % PROJ_TM == 0
    half = MLA_ROPE // 2
    inv = ROPE_THETA ** (-jnp.arange(half, dtype=F32) / half)
    inv_tab = jnp.tile(inv, LANES // half).reshape(1, LANES)
    slopes = 2.0 ** (-8.0 * jnp.arange(1, DIFF_HEADS + 1, dtype=F32) / DIFF_HEADS)
    slopes = jnp.broadcast_to(slopes.reshape(DIFF_HEADS, 1, 1), (DIFF_HEADS, 1, LANES))
    pos_col = positions.reshape(nb * s, 1)
    pos_colf = pos_col.astype(F32)
    pos_chunks = positions.reshape(nb, s // ATTN_TQ, ATTN_TQ)
    pos_row = pos_chunks.astype(F32)
    run_max = lax.cummax(pos_chunks.max(axis=-1), axis=1)
    prev_max = jnp.concatenate(
        [jnp.full((nb, 1), jnp.iinfo(jnp.int32).min, jnp.int32), run_max[:, :-1]], axis=1)
    tile_sorted = jnp.all(pos_chunks[..., 1:] >= pos_chunks[..., :-1], axis=-1)
    ordered = ((prev_max <= pos_chunks.min(axis=-1)).astype(jnp.int32)
               + 2 * tile_sorted.astype(jnp.int32))

    for l in range(depth):
        ada3 = _ada(c, w_ada[l], b_ada[l]).reshape(nb, 3, d)
        h, cos_tab, sin_tab, pos_lanes = _norm(x, ada3, g_pre[l].reshape(1, d), pos_col, inv_tab)
        w_t = jnp.swapaxes(w_in[l], 0, 1)
        qm, rest = _proj(h, w_t, cos_tab, sin_tab)
        kk, vv = _kv(h, w_t, g_kv[l].reshape(1, KV_RANK), w_ukv[l], cos_tab, sin_tab)
        og_mla = _mla_attn(qm, kk, vv, rest, nb, s)
        lam_par = jnp.stack([lambda_q1[l], lambda_k1[l], lambda_q2[l], lambda_k2[l]]).astype(F32)
        og_diff = _diff_attn(rest, pos_colf, pos_row, pos_lanes, ordered, slopes, lam_par,
                             g_subln[l].reshape(1, DIFF_V), 0.8 - 0.6 * math.exp(-0.3 * l), nb, s)
        w1b, w2b, wob = _cast_weights(w_o_mla[l], w_o_diff[l], w_out[l])
        x = _merge_out(og_mla, og_diff, rest, w1b, w2b, wob, x, ada3, g_post[l].reshape(1, d))
    return x
```

```python
import functools
import math

import jax
import jax.numpy as jnp
from jax import lax
from jax.experimental import pallas as pl
from jax.experimental.pallas import tpu as pltpu

F32 = jnp.float32
BF16 = jnp.bfloat16

MLA_HEADS = 8
MLA_NOPE = 128
MLA_ROPE = 64
MLA_V = 128
KV_RANK = 512
ROPE_THETA = 10000.0
DIFF_HEADS = 8
DIFF_QK = 64
DIFF_V = 128
EPS = 1e-6
NEG = -1e30

LANES = 128
SUBLANES = 8
VMEM_LIMIT = 56 * 1024 * 1024
ADA_TN = 1024
NORM_TS = 1024
N_PIECES = 3
N_PAIR_LANES = N_PIECES * N_PIECES
ATTN_TQ = 1024
DIFF_ROW_SPLIT = 2
MLA_TQ = 1024
MLA_ROW_SPLIT = 2
PROJ_TM = 1024
REST_TN = 1024
QMLA_PAIRS_PER_TILE = 2
KV_TM = 1024
CAST_STEPS = 4
OUT_TM = 512
EPILOGUE_ROW_SPLIT = 2
LOG2E = math.log2(math.e)

Q_MLA_W = MLA_HEADS * (MLA_NOPE + MLA_ROPE)
KVR_W = KV_RANK + MLA_ROPE
KVR_PAD = 640
REST_OFF = Q_MLA_W + KVR_W
QD_OFF, KD_OFF, VD_OFF, GM_OFF, GD_OFF, MGM_OFF, MGD_OFF = 0, 1024, 2048, 3072, 4096, 5120, 7168
REST_W = 9216


def _cparams(sem):
    return pltpu.CompilerParams(dimension_semantics=sem, vmem_limit_bytes=VMEM_LIMIT)


def _ada_kernel(cb_ref, w_ref, b_ref, o_ref):
    k_dim, tn = w_ref.shape
    nb = cb_ref.shape[0]
    nchunk = tn // LANES

    def body(i, accs):
        k0 = pl.multiple_of(i * SUBLANES, SUBLANES)
        out = []
        for b in range(nb):
            cv = cb_ref[b, pl.ds(k0, SUBLANES), :]
            for j in range(nchunk):
                wv = w_ref[pl.ds(k0, SUBLANES), j * LANES:(j + 1) * LANES]
                out.append(accs[b * nchunk + j] + wv * cv)
        return tuple(out)

    init = tuple(jnp.zeros((SUBLANES, LANES), F32) for _ in range(nb * nchunk))
    accs = lax.fori_loop(0, k_dim // SUBLANES, body, init, unroll=8)
    for b in range(nb):
        row = jnp.concatenate(
            [jnp.sum(accs[b * nchunk + j], axis=0, keepdims=True) for j in range(nchunk)], axis=1)
        o_ref[b:b + 1, :] = row + b_ref[...]


def _ada(c, w, bias):
    nb, k_dim = c.shape
    n = w.shape[1]
    tn = ADA_TN
    cb = jnp.broadcast_to(c[:, :, None], (nb, k_dim, LANES))
    return pl.pallas_call(
        _ada_kernel,
        grid=(n // tn,),
        in_specs=[pl.BlockSpec((nb, k_dim, LANES), lambda j: (0, 0, 0)),
                  pl.BlockSpec((k_dim, tn), lambda j: (0, j)),
                  pl.BlockSpec((1, tn), lambda j: (0, j))],
        out_specs=pl.BlockSpec((nb, tn), lambda j: (0, j)),
        out_shape=jax.ShapeDtypeStruct((nb, n), F32),
        compiler_params=_cparams(("arbitrary",)),
        name="ada",
    )(cb, w, bias.reshape(1, n))


def _bf16_pieces(x):
    p1 = x.astype(BF16)
    r = x - p1.astype(F32)
    p2 = r.astype(BF16)
    return p1.astype(F32), p2.astype(F32), r - p2.astype(F32)


def _pick3(sel, x3):
    return jnp.where(sel == 0, x3[0], jnp.where(sel == 1, x3[1], x3[2]))


def _norm_kernel(x_ref, ada_ref, g_ref, pos_ref, inv_ref, h_ref, cos_ref, sin_ref, plane_ref):
    x = x_ref[0]
    ms = jnp.mean(x * x, axis=-1, keepdims=True)
    y = x * lax.rsqrt(ms + EPS) * g_ref[...]
    shift = ada_ref[0, 0:1, :]
    scale = ada_ref[0, 1:2, :]
    h_ref[...] = (y * (1.0 + scale) + shift).astype(BF16)
    ang = pos_ref[...].astype(F32) * inv_ref[...]
    lane = lax.broadcasted_iota(jnp.int32, ang.shape, 1)
    sign = jnp.where((lane % MLA_ROPE) < MLA_ROPE // 2, -1.0, 1.0).astype(F32)
    cos_ref[...] = jnp.cos(ang)
    sin_ref[...] = jnp.sin(ang) * sign
    pos3 = _bf16_pieces(pos_ref[...].astype(F32))
    piece = _pick3(lane % N_PIECES, pos3)
    plane_ref[...] = jnp.where(lane < N_PAIR_LANES, piece,
                               jnp.where(lane < 2 * N_PAIR_LANES, -piece, 0.0)).astype(BF16)


def _norm(x, ada3, g_pre, pos_col, inv_tab):
    nb, s, d = x.shape
    ts = NORM_TS
    ns = s // ts
    row = lambda b, i: (b * ns + i, 0)
    return pl.pallas_call(
        _norm_kernel,
        grid=(nb, ns),
        in_specs=[pl.BlockSpec((1, ts, d), lambda b, i: (b, i, 0)),
                  pl.BlockSpec((1, 3, d), lambda b, i: (b, 0, 0)),
                  pl.BlockSpec((1, d), lambda b, i: (0, 0)),
                  pl.BlockSpec((ts, 1), row),
                  pl.BlockSpec((1, LANES), lambda b, i: (0, 0))],
        out_specs=[pl.BlockSpec((ts, d), row),
                   pl.BlockSpec((ts, LANES), row),
                   pl.BlockSpec((ts, LANES), row),
                   pl.BlockSpec((ts, LANES), row)],
        out_shape=[jax.ShapeDtypeStruct((nb * s, d), BF16),
                   jax.ShapeDtypeStruct((nb * s, LANES), F32),
                   jax.ShapeDtypeStruct((nb * s, LANES), F32),
                   jax.ShapeDtypeStruct((nb * s, LANES), BF16)],
        compiler_params=_cparams(("arbitrary", "arbitrary")),
        name="prenorm",
    )(x, ada3, g_pre, pos_col, inv_tab)


def _rope_cols(r, cos, sin_signed):
    lane = lax.broadcasted_iota(jnp.int32, r.shape, 1)
    half = MLA_ROPE // 2
    partner = jnp.where((lane % MLA_ROPE) < half,
                        pltpu.roll(r, LANES - half, 1), pltpu.roll(r, half, 1))
    return r * cos + partner * sin_signed


def _qmla_kernel(a_ref, w_ref, cos_ref, sin_ref, o_ref, wb_ref, *, scale):
    hd = MLA_NOPE + MLA_ROPE
    pair_w = 2 * hd
    n_pairs = wb_ref.shape[0] // pair_w

    @pl.when(pl.program_id(1) == 0)
    def _():
        for p in range(n_pairs):
            src, dst = w_ref.at[p * pair_w:(p + 1) * pair_w], wb_ref.at[p * pair_w:(p + 1) * pair_w]
            dst[:MLA_NOPE] = src[:MLA_NOPE].astype(BF16)
            dst[MLA_NOPE:2 * MLA_NOPE] = src[hd:hd + MLA_NOPE].astype(BF16)
            dst[2 * MLA_NOPE:2 * MLA_NOPE + MLA_ROPE] = src[MLA_NOPE:hd].astype(BF16)
            dst[2 * MLA_NOPE + MLA_ROPE:] = src[hd + MLA_NOPE:].astype(BF16)

    acc = lax.dot_general(a_ref[...], wb_ref[...], _NT, preferred_element_type=F32)
    for p in range(n_pairs):
        c0 = p * pair_w
        rr = _rope_cols(acc[:, c0 + 2 * MLA_NOPE:c0 + pair_w], cos_ref[...], sin_ref[...])
        o_ref[:, c0:c0 + 2 * MLA_NOPE] = (acc[:, c0:c0 + 2 * MLA_NOPE] * scale).astype(BF16)
        o_ref[:, c0 + 2 * MLA_NOPE:c0 + pair_w] = (rr * scale).astype(BF16)


def _sigmoid(x):
    return 0.5 * jnp.tanh(0.5 * x) + 0.5


def _rest_kernel(a_ref, w_ref, o_ref, wb_ref, *, tn):
    j = pl.program_id(0)

    @pl.when(pl.program_id(1) == 0)
    def _():
        wb_ref[...] = w_ref[...].astype(BF16)

    def tile(epilogue):
        acc = lax.dot_general(a_ref[...], wb_ref[...], _NT, preferred_element_type=F32)
        o_ref[...] = epilogue(acc).astype(BF16)

    @pl.when(j < KD_OFF // tn)
    def _():
        tile(lambda acc: acc * (DIFF_QK ** -0.5 * LOG2E))

    @pl.when(jnp.logical_and(j >= KD_OFF // tn, j < GM_OFF // tn))
    def _():
        tile(lambda acc: acc)

    @pl.when(jnp.logical_and(j >= GM_OFF // tn, j < MGM_OFF // tn))
    def _():
        tile(lambda acc: acc * _sigmoid(acc))

    @pl.when(j >= MGM_OFF // tn)
    def _():
        tile(_sigmoid)


def _proj(h, w_t, cos_tab, sin_tab):
    m, k_dim = h.shape
    tm = PROJ_TM
    pair_w = 2 * (MLA_NOPE + MLA_ROPE)
    q_scale = (MLA_NOPE + MLA_ROPE) ** -0.5 * LOG2E

    def row_window(rows, offset_fn):
        return pl.BlockSpec((pl.Element(rows), pl.Element(k_dim)),
                            lambda *g: (pl.multiple_of(offset_fn(*g), SUBLANES), 0))

    qtn = QMLA_PAIRS_PER_TILE * pair_w
    qm = pl.pallas_call(
        functools.partial(_qmla_kernel, scale=q_scale),
        grid=(Q_MLA_W // qtn, m // tm),
        in_specs=[pl.BlockSpec((tm, k_dim), lambda j, i: (i, 0)),
                  pl.BlockSpec((qtn, k_dim), lambda j, i: (j, 0)),
                  pl.BlockSpec((tm, LANES), lambda j, i: (i, 0)),
                  pl.BlockSpec((tm, LANES), lambda j, i: (i, 0))],
        out_specs=pl.BlockSpec((tm, qtn), lambda j, i: (i, j)),
        out_shape=jax.ShapeDtypeStruct((m, Q_MLA_W), BF16),
        scratch_shapes=[pltpu.VMEM((qtn, k_dim), BF16)],
        compiler_params=_cparams(("arbitrary", "arbitrary")),
        name="proj_qmla",
    )(h, w_t, cos_tab, sin_tab)
    tn = REST_TN
    rest = pl.pallas_call(
        functools.partial(_rest_kernel, tn=tn),
        grid=(REST_W // tn, m // tm),
        in_specs=[pl.BlockSpec((tm, k_dim), lambda j, i: (i, 0)),
                  row_window(tn, lambda j, i: REST_OFF + tn * j)],
        out_specs=pl.BlockSpec((tm, tn), lambda j, i: (i, j)),
        out_shape=jax.ShapeDtypeStruct((m, REST_W), BF16),
        scratch_shapes=[pltpu.VMEM((tn, k_dim), BF16)],
        compiler_params=_cparams(("arbitrary", "arbitrary")),
        name="proj_rest",
    )(h, w_t)
    return qm, rest


def _kv_kernel(a_ref, wp_ref, g_ref, w_ref, cos_ref, sin_ref, k_ref, v_ref, wpb_ref, wb_ref):
    @pl.when(pl.program_id(0) == 0)
    def _():
        wpb_ref[:KVR_W] = wp_ref[...].astype(BF16)
        wpb_ref[KVR_W:] = jnp.zeros((KVR_PAD - KVR_W, wpb_ref.shape[1]), BF16)
        wb_ref[...] = w_ref[...].astype(BF16)

    kw = MLA_NOPE + MLA_V
    rb = a_ref.shape[0] // EPILOGUE_ROW_SPLIT
    for t in range(EPILOGUE_ROW_SPLIT):
        rows = slice(t * rb, (t + 1) * rb)
        p = lax.dot_general(a_ref[rows, :], wpb_ref[...], _NT, preferred_element_type=F32)
        ckv = p[:, :KV_RANK]
        ms = jnp.mean(ckv * ckv, axis=-1, keepdims=True)
        n = (ckv * lax.rsqrt(ms + EPS) * g_ref[...]).astype(BF16)
        kv = jnp.dot(n, wb_ref[...], preferred_element_type=F32)
        kr_even = _rope_cols(p[:, KV_RANK:], cos_ref[rows, :], sin_ref[rows, :])
        kr_odd = pltpu.roll(kr_even, MLA_ROPE, 1)
        ones_col = jnp.ones(kr_even.shape, BF16)
        for hd in range(MLA_HEADS):
            k_ref[rows, hd * kw:hd * kw + MLA_NOPE] = kv[:, hd * kw:hd * kw + MLA_NOPE].astype(BF16)
            k_ref[rows, hd * kw + MLA_NOPE:(hd + 1) * kw] = (
                kr_even if hd % 2 == 0 else kr_odd).astype(BF16)
            v_ref[rows, hd * kw:hd * kw + MLA_V] = kv[:, hd * kw + MLA_NOPE:(hd + 1) * kw].astype(BF16)
            v_ref[rows, hd * kw + MLA_V:(hd + 1) * kw] = ones_col


def _kv(h, w_t, g_kv, w_ukv, cos_tab, sin_tab):
    m, k_dim = h.shape
    tm = KV_TM
    kw = MLA_HEADS * (MLA_NOPE + MLA_V)
    return pl.pallas_call(
        _kv_kernel,
        grid=(m // tm,),
        in_specs=[pl.BlockSpec((tm, k_dim), lambda i: (i, 0)),
                  pl.BlockSpec((pl.Element(KVR_W), pl.Element(k_dim)),
                               lambda i: (pl.multiple_of(Q_MLA_W + 0 * i, SUBLANES), 0)),
                  pl.BlockSpec((1, KV_RANK), lambda i: (0, 0)),
                  pl.BlockSpec((KV_RANK, kw), lambda i: (0, 0)),
                  pl.BlockSpec((tm, LANES), lambda i: (i, 0)),
                  pl.BlockSpec((tm, LANES), lambda i: (i, 0))],
        out_specs=[pl.BlockSpec((tm, kw), lambda i: (i, 0)),
                   pl.BlockSpec((tm, kw), lambda i: (i, 0))],
        out_shape=[jax.ShapeDtypeStruct((m, kw), BF16),
                   jax.ShapeDtypeStruct((m, kw), BF16)],
        scratch_shapes=[pltpu.VMEM((KVR_PAD, k_dim), BF16), pltpu.VMEM((KV_RANK, kw), BF16)],
        compiler_params=_cparams(("arbitrary",)),
        name="kv_up",
    )(h, w_t, g_kv, w_ukv, cos_tab, sin_tab)


def _flash_pipeline(n_chains, nq, diags, loops, value_fn, finalize_fn,
                    s_ref, acc_ref, mpart_ref, macc_ref, sd_ref=None, mpd_ref=None):
    assert sd_ref is None or nq % 2 == 0
    chunk = s_ref.shape[2]

    def lane_tiles(x, n):
        return jnp.concatenate([x] * n, axis=1)

    def qk_phase(scores, stage):
        s_dst, m_dst = stage
        for ci, s in enumerate(scores):
            s_dst[ci] = s
            part = s[:, :LANES]
            for j in range(1, chunk // LANES):
                part = jnp.maximum(part, s[:, j * LANES:(j + 1) * LANES])
            m_dst[ci] = part

    def pv_phase(kc, stage):
        s_src, m_src = stage
        for ci in range(n_chains):
            m_acc = macc_ref[ci]
            m_run = jnp.maximum(m_acc, jnp.max(m_src[ci], axis=-1, keepdims=True))
            macc_ref[ci] = m_run
            p = jnp.exp2(s_src[ci] - lane_tiles(m_run, chunk // LANES))
            alpha = jnp.exp2(m_acc - m_run)
            pv = jnp.dot(p.astype(BF16), value_fn(ci, kc), preferred_element_type=F32)
            acc_ref[ci] = lane_tiles(alpha, acc_ref.shape[2] // LANES) * acc_ref[ci] + pv

    def reset():
        acc_ref[...] = jnp.zeros_like(acc_ref)
        macc_ref[...] = jnp.full(macc_ref.shape, NEG, F32)

    def tile(qt, stage, next_stage):
        cur = qt
        for trips_fn, score_fn in loops:
            def step(kc, cur, score_fn=score_fn):
                pv_phase(cur, stage)
                qk_phase(score_fn(qt, kc), stage)
                return kc

            cur = lax.fori_loop(0, trips_fn(qt), step, cur)
        nxt = jnp.minimum(qt + 1, nq - 1)

        def transition(_, cur, fn):
            if next_stage is stage:
                pv_phase(cur, stage)
                finalize_fn(qt)
                reset()
                qk_phase(fn(nxt), stage)
            else:
                qk_phase(fn(nxt), next_stage)
                pv_phase(cur, stage)
                finalize_fn(qt)
                reset()
            return cur

        for select_fn, fn in diags:
            if select_fn is None:
                transition(0, cur, fn)
            else:
                lax.fori_loop(0, select_fn(nxt), functools.partial(transition, fn=fn), cur)

    stage_a = (s_ref, mpart_ref)
    reset()
    for select_fn, fn in diags:
        if select_fn is None:
            qk_phase(fn(0), stage_a)
        else:
            lax.fori_loop(0, select_fn(0), lambda _, c, fn=fn: qk_phase(fn(0), stage_a) or c, 0)
    if sd_ref is None:
        lax.fori_loop(0, nq, lambda qt, c: tile(qt, stage_a, stage_a) or c, 0)
    else:
        stage_b = (sd_ref, mpd_ref)

        def tile_pair(j, c):
            tile(2 * j, stage_a, stage_b)
            tile(2 * j + 1, stage_b, stage_a)
            return c

        lax.fori_loop(0, nq // 2, tile_pair, 0)


def _causal_mask(rows, cols, row0):
    row = lax.broadcasted_iota(jnp.int32, (rows, cols), 0) + row0
    col = lax.broadcasted_iota(jnp.int32, (rows, cols), 1)
    return col <= row


def _pad_masked(s, cols):
    if s.shape[1] == cols:
        return s
    return jnp.concatenate([s, jnp.full((s.shape[0], cols - s.shape[1]), NEG, s.dtype)], axis=1)


_NT = (((1,), (1,)), ((), ()))


def _mla_attn_kernel(q_ref, k_ref, v_ref, g_ref, o_ref, acc_ref, qs_ref, s_ref, mpart_ref, macc_ref,
                     *, tq, rs):
    kw = 2 * MLA_NOPE
    tr = tq // rs
    nq = q_ref.shape[0] // tq
    chains = [(hp, r) for hp in range(2) for r in range(rs)]

    def stage_queries(qt):
        rows = pl.ds(pl.multiple_of(qt * tq, tq), tq)
        for hp in range(2):
            qs_ref[hp, :, :MLA_NOPE] = q_ref[rows, hp * MLA_NOPE:(hp + 1) * MLA_NOPE]
            qs_ref[hp, :, MLA_NOPE:] = q_ref[rows, 2 * MLA_NOPE:]

    def scores(qt, kc, diag):
        if diag:
            stage_queries(qt)
        k0 = pl.multiple_of(kc * tq, tq)
        out = []
        for hp, r in chains:
            nk = (r + 1) * tr if diag else tq
            q = qs_ref[hp, r * tr:(r + 1) * tr, :]
            k = k_ref[pl.ds(k0, nk), hp * kw:(hp + 1) * kw]
            s = lax.dot_general(q, k, _NT, preferred_element_type=F32)
            if diag:
                s = _pad_masked(jnp.where(_causal_mask(tr, nk, r * tr), s, NEG), tq)
            out.append(s)
        return out

    def values(ci, kc):
        hp = chains[ci][0]
        return v_ref[pl.ds(pl.multiple_of(kc * tq, tq), tq), hp * kw:(hp + 1) * kw]

    def finalize(qt):
        for ci, (hp, r) in enumerate(chains):
            rows = pl.ds(pl.multiple_of(qt * tq + r * tr, tr), tr)
            acc = acc_ref[ci]
            o = acc[:, :MLA_V] / acc[:, MLA_V:]
            gate = g_ref[rows, hp * MLA_V:(hp + 1) * MLA_V].astype(F32)
            o_ref[rows, hp * MLA_V:(hp + 1) * MLA_V] = (o * gate).astype(BF16)

    _flash_pipeline(len(chains), nq, [(None, lambda qt: scores(qt, qt, True))],
                    [(lambda qt: qt, lambda qt, kc: scores(qt, kc, False))],
                    values, finalize, s_ref, acc_ref, mpart_ref, macc_ref)


def _mla_attn(qm, kk, vv, rest, nb, s):
    tq = MLA_TQ
    pair_w = 2 * (MLA_NOPE + MLA_ROPE)
    kw = 4 * MLA_NOPE
    gate_blk = GM_OFF // (2 * MLA_V)
    rs = MLA_ROW_SPLIT
    return pl.pallas_call(
        functools.partial(_mla_attn_kernel, tq=tq, rs=rs),
        grid=(nb, MLA_HEADS // 2),
        in_specs=[pl.BlockSpec((s, pair_w), lambda b, hh: (b, hh)),
                  pl.BlockSpec((s, kw), lambda b, hh: (b, hh)),
                  pl.BlockSpec((s, kw), lambda b, hh: (b, hh)),
                  pl.BlockSpec((s, 2 * MLA_V), lambda b, hh: (b, gate_blk + hh))],
        out_specs=pl.BlockSpec((s, 2 * MLA_V), lambda b, hh: (b, hh)),
        out_shape=jax.ShapeDtypeStruct((nb * s, MLA_HEADS * MLA_V), BF16),
        scratch_shapes=[pltpu.VMEM((2 * rs, tq // rs, 2 * MLA_V), F32),
                        pltpu.VMEM((2, tq, 2 * MLA_NOPE), BF16),
                        pltpu.VMEM((2 * rs, tq // rs, tq), F32),
                        pltpu.VMEM((2 * rs, tq // rs, LANES), F32),
                        pltpu.VMEM((2 * rs, tq // rs, LANES), F32)],
        compiler_params=_cparams(("arbitrary", "arbitrary")),
        name="mla_attn",
    )(qm, kk, vv, rest)


def _diff_attn_kernel(ord_ref, q_ref, k_ref, v_ref, g_ref, pq_ref, pk_ref, plane_ref, sl_ref, lam_ref,
                      gs_ref, o_ref, vaug_ref, kaug_ref, acc_ref, qf_ref, s_ref, mpart_ref, macc_ref,
                      sd_ref, mpd_ref, *, tq, rs, lambda_init):
    seq = q_ref.shape[0]
    nq = seq // tq
    tr = tq // rs
    chains = [(c, r) for c in range(2) for r in range(rs)]
    lane = lax.broadcasted_iota(jnp.int32, (tq, LANES), 1)
    ones_col = jnp.ones((tq, LANES), BF16)
    slope2 = sl_ref[0, :, 0:1] * LOG2E
    c_pieces = _bf16_pieces(slope2)
    lane_row = lane[:1]
    c_query = jnp.where(lane_row < N_PAIR_LANES, _pick3(lane_row // N_PIECES, c_pieces), 0.0).astype(BF16)
    c_key = jnp.where(jnp.logical_and(lane_row >= N_PAIR_LANES, lane_row < 2 * N_PAIR_LANES),
                      _pick3((lane_row - N_PAIR_LANES) // N_PIECES, c_pieces), 0.0).astype(BF16)

    def bias_lanes(rows, key_side):
        pos_lanes = plane_ref[rows, :]
        if key_side:
            return jnp.where(lane < N_PAIR_LANES, pos_lanes, c_key)
        return jnp.where(lane < N_PAIR_LANES, c_query, pos_lanes)

    def stage_keys(kc, _):
        rows = pl.ds(pl.multiple_of(kc * tq, tq), tq)
        vaug_ref[rows, :DIFF_V] = v_ref[rows, :]
        vaug_ref[rows, DIFF_V:] = ones_col
        kaug_ref[rows, :2 * DIFF_QK] = k_ref[rows, :]
        kaug_ref[rows, 2 * DIFF_QK:] = bias_lanes(rows, True)
        return 0

    lax.fori_loop(0, nq, stage_keys, 0)

    def stage_queries(qt):
        rows = pl.ds(pl.multiple_of(qt * tq, tq), tq)
        q = q_ref[rows, :]
        q_side = bias_lanes(rows, False)
        qf_ref[0, :, :2 * DIFF_QK] = jnp.where(lane < DIFF_QK, q, 0).astype(BF16)
        qf_ref[1, :, :2 * DIFF_QK] = jnp.where(lane >= DIFF_QK, q, 0).astype(BF16)
        qf_ref[0, :, 2 * DIFF_QK:] = q_side
        qf_ref[1, :, 2 * DIFF_QK:] = q_side

    lq = lam_ref[...]
    lam = (jnp.exp(jnp.sum(lq[0:1] * lq[1:2], axis=-1, keepdims=True))
           - jnp.exp(jnp.sum(lq[2:3] * lq[3:4], axis=-1, keepdims=True)) + lambda_init)

    def scores(qt, kc, diag):
        k0 = pl.multiple_of(kc * tq, tq)
        k = k_ref[pl.ds(k0, tq), :]
        pk = slope2 * pk_ref[0, pl.ds(kc, 1), :]
        out = [None] * len(chains)
        for r in range(rs):
            pq = pq_ref[pl.ds(pl.multiple_of(qt * tq + r * tr, tr), tr), :]
            bias = jnp.abs(slope2 * pq - pk)
            mask = _causal_mask(tr, tq, r * tr) if diag else None
            for c in range(2):
                q = qf_ref[c, r * tr:(r + 1) * tr, :2 * DIFF_QK]
                s = lax.dot_general(q, k, _NT, preferred_element_type=F32) - bias
                out[chains.index((c, r))] = jnp.where(mask, s, NEG) if diag else s
        return out

    def scores_diag(qt):
        stage_queries(qt)
        return scores(qt, qt, True)

    def scores_ordered(qt, kc):
        k = kaug_ref[pl.ds(pl.multiple_of(kc * tq, tq), tq), :]
        return [lax.dot_general(qf_ref[c, r * tr:(r + 1) * tr, :], k, _NT, preferred_element_type=F32)
                for c, r in chains]

    def scores_diag_sorted(qt):
        stage_queries(qt)
        k0 = pl.multiple_of(qt * tq, tq)
        out = []
        for c, r in chains:
            nk = (r + 1) * tr
            s = lax.dot_general(qf_ref[c, r * tr:(r + 1) * tr, :], kaug_ref[pl.ds(k0, nk), :], _NT,
                                preferred_element_type=F32)
            out.append(_pad_masked(jnp.where(_causal_mask(tr, nk, r * tr), s, NEG), tq))
        return out

    batch = pl.program_id(0)

    def trips_ordered(qt):
        return jnp.where((ord_ref[batch, qt] & 1) != 0, qt, 0)

    def tile_sorted(qt):
        return (ord_ref[batch, qt] >> 1) & 1

    def values(ci, kc):
        return vaug_ref[pl.ds(pl.multiple_of(kc * tq, tq), tq), :]

    def finalize(qt):
        for r in range(rs):
            rows = pl.ds(pl.multiple_of(qt * tq + r * tr, tr), tr)
            a1 = acc_ref[chains.index((0, r))]
            a2 = acc_ref[chains.index((1, r))]
            o = a1[:, :DIFF_V] / a1[:, DIFF_V:] - lam * (a2[:, :DIFF_V] / a2[:, DIFF_V:])
            ms_o = jnp.mean(o * o, axis=-1, keepdims=True)
            o = o * lax.rsqrt(ms_o + EPS) * gs_ref[...] * (1.0 - lambda_init)
            o_ref[rows, :] = (o * g_ref[rows, :].astype(F32)).astype(BF16)

    _flash_pipeline(len(chains), nq,
                    [(tile_sorted, scores_diag_sorted), (lambda qt: 1 - tile_sorted(qt), scores_diag)],
                    [(trips_ordered, scores_ordered),
                     (lambda qt: qt - trips_ordered(qt), lambda qt, kc: scores(qt, kc, False))],
                    values, finalize, s_ref, acc_ref, mpart_ref, macc_ref, sd_ref, mpd_ref)


def _diff_attn(rest, pos_col, pos_row, pos_lanes, ordered, slopes, lam_par, g_subln, lambda_init,
               nb, s):
    tq = ATTN_TQ
    nq = s // tq
    rs = DIFF_ROW_SPLIT
    tr = tq // rs
    hw = DIFF_V
    return pl.pallas_call(
        functools.partial(_diff_attn_kernel, tq=tq, rs=rs, lambda_init=lambda_init),
        grid=(nb, DIFF_HEADS),
        in_specs=[pl.BlockSpec(memory_space=pltpu.SMEM),
                  pl.BlockSpec((s, hw), lambda b, hd: (b, QD_OFF // hw + hd)),
                  pl.BlockSpec((s, hw), lambda b, hd: (b, KD_OFF // hw + hd)),
                  pl.BlockSpec((s, hw), lambda b, hd: (b, VD_OFF // hw + hd)),
                  pl.BlockSpec((s, hw), lambda b, hd: (b, GD_OFF // hw + hd)),
                  pl.BlockSpec((s, 1), lambda b, hd: (b, 0)),
                  pl.BlockSpec((1, nq, tq), lambda b, hd: (b, 0, 0)),
                  pl.BlockSpec((s, LANES), lambda b, hd: (b, 0)),
                  pl.BlockSpec((1, 1, LANES), lambda b, hd: (hd, 0, 0)),
                  pl.BlockSpec((4, DIFF_QK), lambda b, hd: (0, 0)),
                  pl.BlockSpec((1, DIFF_V), lambda b, hd: (0, 0))],
        out_specs=pl.BlockSpec((s, hw), lambda b, hd: (b, hd)),
        out_shape=jax.ShapeDtypeStruct((nb * s, DIFF_HEADS * DIFF_V), BF16),
        scratch_shapes=[pltpu.VMEM((s, 2 * DIFF_V), BF16),
                        pltpu.VMEM((s, 4 * DIFF_QK), BF16),
                        pltpu.VMEM((2 * rs, tr, 2 * DIFF_V), F32),
                        pltpu.VMEM((2, tq, 4 * DIFF_QK), BF16),
                        pltpu.VMEM((2 * rs, tr, tq), F32),
                        pltpu.VMEM((2 * rs, tr, LANES), F32),
                        pltpu.VMEM((2 * rs, tr, LANES), F32),
                        pltpu.VMEM((2 * rs, tr, tq), F32),
                        pltpu.VMEM((2 * rs, tr, LANES), F32)],
        compiler_params=_cparams(("arbitrary", "arbitrary")),
        name="diff_attn",
    )(ordered, rest, rest, rest, rest, pos_col, pos_row, pos_lanes, slopes, lam_par, g_subln)


def _cast_kernel(w1_ref, w2_ref, w3_ref, o1_ref, o2_ref, o3_ref):
    o1_ref[...] = w1_ref[...].astype(BF16)
    o2_ref[...] = w2_ref[...].astype(BF16)
    o3_ref[...] = w3_ref[...].astype(BF16)


def _cast_weights(w1, w2, w3):
    steps = CAST_STEPS
    specs = [pl.BlockSpec((w.shape[0] // steps, w.shape[1]), lambda i: (i, 0)) for w in (w1, w2, w3)]
    return pl.pallas_call(
        _cast_kernel,
        grid=(steps,),
        in_specs=specs,
        out_specs=specs,
        out_shape=[jax.ShapeDtypeStruct(w.shape, BF16) for w in (w1, w2, w3)],
        compiler_params=_cparams(("arbitrary",)),
        name="cast_weights",
    )(w1, w2, w3)


def _merge_out_kernel(a1_ref, a2_ref, s1a_ref, s1b_ref, s2a_ref, s2b_ref, w1_ref, w2_ref, wo_ref,
                      x_ref, ada_ref, g_ref, o_ref):
    half = s1a_ref.shape[1]
    rb = a1_ref.shape[0] // EPILOGUE_ROW_SPLIT
    for t in range(EPILOGUE_ROW_SPLIT):
        rows = slice(t * rb, (t + 1) * rb)
        y1 = jnp.dot(a1_ref[rows, :], w1_ref[...], preferred_element_type=F32)
        y2 = jnp.dot(a2_ref[rows, :], w2_ref[...], preferred_element_type=F32)
        merged = jnp.concatenate(
            [s1a_ref[rows, :].astype(F32) * y1[:, :half] + s2a_ref[rows, :].astype(F32) * y2[:, :half],
             s1b_ref[rows, :].astype(F32) * y1[:, half:] + s2b_ref[rows, :].astype(F32) * y2[:, half:]],
            axis=1).astype(BF16)
        y = jnp.dot(merged, wo_ref[...], preferred_element_type=F32)
        ms = jnp.mean(y * y, axis=-1, keepdims=True)
        yn = y * lax.rsqrt(ms + EPS) * g_ref[...]
        o_ref[0, rows, :] = x_ref[0, rows, :] + ada_ref[0, 2:3, :] * yn


def _merge_out(og_mla, og_diff, rest, w1b, w2b, wob, x, ada3, g_post):
    nb, s, d = x.shape
    k_dim = og_mla.shape[1]
    tm = OUT_TM
    ns = s // tm
    half = d // 2
    row = lambda b, i: (b * ns + i, 0)
    gate = lambda blk: pl.BlockSpec((tm, half), lambda b, i: (b * ns + i, blk))
    resident = lambda shape: pl.BlockSpec(shape, lambda b, i: (0, 0), pipeline_mode=pl.Buffered(1))
    return pl.pallas_call(
        _merge_out_kernel,
        grid=(nb, ns),
        in_specs=[pl.BlockSpec((tm, k_dim), row),
                  pl.BlockSpec((tm, k_dim), row),
                  gate(MGM_OFF // half), gate(MGM_OFF // half + 1),
                  gate(MGD_OFF // half), gate(MGD_OFF // half + 1),
                  resident((k_dim, d)), resident((k_dim, d)), resident((d, d)),
                  pl.BlockSpec((1, tm, d), lambda b, i: (b, i, 0)),
                  pl.BlockSpec((1, 3, d), lambda b, i: (b, 0, 0)),
                  pl.BlockSpec((1, d), lambda b, i: (0, 0))],
        out_specs=pl.BlockSpec((1, tm, d), lambda b, i: (b, i, 0)),
        out_shape=jax.ShapeDtypeStruct((nb, s, d), F32),
        compiler_params=_cparams(("arbitrary", "arbitrary")),
        name="merge_out",
    )(og_mla, og_diff, rest, rest, rest, rest, w1b, w2b, wob, x, ada3, g_post)


def kernel(x, c, positions, w_ada, b_ada, g_pre, w_in, g_kv, w_ukv, lambda_q1, lambda_k1,
           lambda_q2, lambda_k2, g_subln, w_o_mla, w_o_diff, w_out, g_post):
    nb, s, d = x.shape
    depth = w_in.shape[0]
    assert w_in.shape[1:] == (d, REST_OFF + REST_W) and w_out.shape[1:] == (d, d)
    assert s % ATTN_TQ == 0 and s % MLA_TQ == 0 and (nb * s) % PROJ_TM == 0
    half = MLA_ROPE // 2
    inv = ROPE_THETA ** (-jnp.arange(half, dtype=F32) / half)
    inv_tab = jnp.tile(inv, LANES // half).reshape(1, LANES)
    slopes = 2.0 ** (-8.0 * jnp.arange(1, DIFF_HEADS + 1, dtype=F32) / DIFF_HEADS)
    slopes = jnp.broadcast_to(slopes.reshape(DIFF_HEADS, 1, 1), (DIFF_HEADS, 1, LANES))
    pos_col = positions.reshape(nb * s, 1)
    pos_colf = pos_col.astype(F32)
    pos_chunks = positions.reshape(nb, s // ATTN_TQ, ATTN_TQ)
    pos_row = pos_chunks.astype(F32)
    run_max = lax.cummax(pos_chunks.max(axis=-1), axis=1)
    prev_max = jnp.concatenate(
        [jnp.full((nb, 1), jnp.iinfo(jnp.int32).min, jnp.int32), run_max[:, :-1]], axis=1)
    tile_sorted = jnp.all(pos_chunks[..., 1:] >= pos_chunks[..., :-1], axis=-1)
    ordered = ((prev_max <= pos_chunks.min(axis=-1)).astype(jnp.int32)
               + 2 * tile_sorted.astype(jnp.int32))

    for l in range(depth):
        ada3 = _ada(c, w_ada[l], b_ada[l]).reshape(nb, 3, d)
        h, cos_tab, sin_tab, pos_lanes = _norm(x, ada3, g_pre[l].reshape(1, d), pos_col, inv_tab)
        w_t = jnp.swapaxes(w_in[l], 0, 1)
        qm, rest = _proj(h, w_t, cos_tab, sin_tab)
        kk, vv = _kv(h, w_t, g_kv[l].reshape(1, KV_RANK), w_ukv[l], cos_tab, sin_tab)
        og_mla = _mla_attn(qm, kk, vv, rest, nb, s)
        lam_par = jnp.stack([lambda_q1[l], lambda_k1[l], lambda_q2[l], lambda_k2[l]]).astype(F32)
        og_diff = _diff_attn(rest, pos_colf, pos_row, pos_lanes, ordered, slopes, lam_par,
                             g_subln[l].reshape(1, DIFF_V), 0.8 - 0.6 * math.exp(-0.3 * l), nb, s)
        w1b, w2b, wob = _cast_weights(w_o_mla[l], w_o_diff[l], w_out[l])
        x = _merge_out(og_mla, og_diff, rest, w1b, w2b, wob, x, ada3, g_post[l].reshape(1, d))
    return x
```

```python
import functools
import math

import jax
import jax.numpy as jnp
from jax import lax
from jax.experimental import pallas as pl
from jax.experimental.pallas import tpu as pltpu

F32 = jnp.float32
BF16 = jnp.bfloat16

MLA_HEADS = 8
MLA_NOPE = 128
MLA_ROPE = 64
MLA_V = 128
KV_RANK = 512
ROPE_THETA = 10000.0
DIFF_HEADS = 8
DIFF_QK = 64
DIFF_V = 128
EPS = 1e-6
NEG = -1e30

LANES = 128
SUBLANES = 8
VMEM_LIMIT = 56 * 1024 * 1024
ADA_TN = 1024
NORM_TS = 1024
N_PIECES = 3
N_PAIR_LANES = N_PIECES * N_PIECES
ATTN_TQ = 1024
DIFF_ROW_SPLIT = 4
MLA_TQ = 1024
MLA_ROW_SPLIT = 4
PROJ_TM = 1024
REST_TN = 1024
QMLA_PAIRS_PER_TILE = 2
KV_TM = 1024
CAST_STEPS = 4
OUT_TM = 512
EPILOGUE_ROW_SPLIT = 2
LOG2E = math.log2(math.e)

Q_MLA_W = MLA_HEADS * (MLA_NOPE + MLA_ROPE)
KVR_W = KV_RANK + MLA_ROPE
KVR_PAD = 640
REST_OFF = Q_MLA_W + KVR_W
QD_OFF, KD_OFF, VD_OFF, GM_OFF, GD_OFF, MGM_OFF, MGD_OFF = 0, 1024, 2048, 3072, 4096, 5120, 7168
REST_W = 9216


def _cparams(sem):
    return pltpu.CompilerParams(dimension_semantics=sem, vmem_limit_bytes=VMEM_LIMIT)


def _ada_kernel(cb_ref, w_ref, b_ref, o_ref):
    k_dim, tn = w_ref.shape
    nb = cb_ref.shape[0]
    nchunk = tn // LANES

    def body(i, accs):
        k0 = pl.multiple_of(i * SUBLANES, SUBLANES)
        out = []
        for b in range(nb):
            cv = cb_ref[b, pl.ds(k0, SUBLANES), :]
            for j in range(nchunk):
                wv = w_ref[pl.ds(k0, SUBLANES), j * LANES:(j + 1) * LANES]
                out.append(accs[b * nchunk + j] + wv * cv)
        return tuple(out)

    init = tuple(jnp.zeros((SUBLANES, LANES), F32) for _ in range(nb * nchunk))
    accs = lax.fori_loop(0, k_dim // SUBLANES, body, init, unroll=8)
    for b in range(nb):
        row = jnp.concatenate(
            [jnp.sum(accs[b * nchunk + j], axis=0, keepdims=True) for j in range(nchunk)], axis=1)
        o_ref[b:b + 1, :] = row + b_ref[...]


def _ada(c, w, bias):
    nb, k_dim = c.shape
    n = w.shape[1]
    tn = ADA_TN
    cb = jnp.broadcast_to(c[:, :, None], (nb, k_dim, LANES))
    return pl.pallas_call(
        _ada_kernel,
        grid=(n // tn,),
        in_specs=[pl.BlockSpec((nb, k_dim, LANES), lambda j: (0, 0, 0)),
                  pl.BlockSpec((k_dim, tn), lambda j: (0, j)),
                  pl.BlockSpec((1, tn), lambda j: (0, j))],
        out_specs=pl.BlockSpec((nb, tn), lambda j: (0, j)),
        out_shape=jax.ShapeDtypeStruct((nb, n), F32),
        compiler_params=_cparams(("arbitrary",)),
        name="ada",
    )(cb, w, bias.reshape(1, n))


def _bf16_pieces(x):
    p1 = x.astype(BF16)
    r = x - p1.astype(F32)
    p2 = r.astype(BF16)
    return p1.astype(F32), p2.astype(F32), r - p2.astype(F32)


def _pick3(sel, x3):
    return jnp.where(sel == 0, x3[0], jnp.where(sel == 1, x3[1], x3[2]))


def _norm_kernel(x_ref, ada_ref, g_ref, pos_ref, inv_ref, h_ref, cos_ref, sin_ref, plane_ref):
    x = x_ref[0]
    ms = jnp.mean(x * x, axis=-1, keepdims=True)
    y = x * lax.rsqrt(ms + EPS) * g_ref[...]
    shift = ada_ref[0, 0:1, :]
    scale = ada_ref[0, 1:2, :]
    h_ref[...] = (y * (1.0 + scale) + shift).astype(BF16)
    ang = pos_ref[...].astype(F32) * inv_ref[...]
    lane = lax.broadcasted_iota(jnp.int32, ang.shape, 1)
    sign = jnp.where((lane % MLA_ROPE) < MLA_ROPE // 2, -1.0, 1.0).astype(F32)
    cos_ref[...] = jnp.cos(ang)
    sin_ref[...] = jnp.sin(ang) * sign
    pos3 = _bf16_pieces(pos_ref[...].astype(F32))
    piece = _pick3(lane % N_PIECES, pos3)
    plane_ref[...] = jnp.where(lane < N_PAIR_LANES, piece,
                               jnp.where(lane < 2 * N_PAIR_LANES, -piece, 0.0)).astype(BF16)


def _norm(x, ada3, g_pre, pos_col, inv_tab):
    nb, s, d = x.shape
    ts = NORM_TS
    ns = s // ts
    row = lambda b, i: (b * ns + i, 0)
    return pl.pallas_call(
        _norm_kernel,
        grid=(nb, ns),
        in_specs=[pl.BlockSpec((1, ts, d), lambda b, i: (b, i, 0)),
                  pl.BlockSpec((1, 3, d), lambda b, i: (b, 0, 0)),
                  pl.BlockSpec((1, d), lambda b, i: (0, 0)),
                  pl.BlockSpec((ts, 1), row),
                  pl.BlockSpec((1, LANES), lambda b, i: (0, 0))],
        out_specs=[pl.BlockSpec((ts, d), row),
                   pl.BlockSpec((ts, LANES), row),
                   pl.BlockSpec((ts, LANES), row),
                   pl.BlockSpec((ts, LANES), row)],
        out_shape=[jax.ShapeDtypeStruct((nb * s, d), BF16),
                   jax.ShapeDtypeStruct((nb * s, LANES), F32),
                   jax.ShapeDtypeStruct((nb * s, LANES), F32),
                   jax.ShapeDtypeStruct((nb * s, LANES), BF16)],
        compiler_params=_cparams(("arbitrary", "arbitrary")),
        name="prenorm",
    )(x, ada3, g_pre, pos_col, inv_tab)


def _rope_cols(r, cos, sin_signed):
    lane = lax.broadcasted_iota(jnp.int32, r.shape, 1)
    half = MLA_ROPE // 2
    partner = jnp.where((lane % MLA_ROPE) < half,
                        pltpu.roll(r, LANES - half, 1), pltpu.roll(r, half, 1))
    return r * cos + partner * sin_signed


def _qmla_kernel(a_ref, w_ref, cos_ref, sin_ref, o_ref, wb_ref, *, scale):
    hd = MLA_NOPE + MLA_ROPE
    pair_w = 2 * hd
    n_pairs = wb_ref.shape[0] // pair_w

    @pl.when(pl.program_id(1) == 0)
    def _():
        for p in range(n_pairs):
            src, dst = w_ref.at[p * pair_w:(p + 1) * pair_w], wb_ref.at[p * pair_w:(p + 1) * pair_w]
            dst[:MLA_NOPE] = src[:MLA_NOPE].astype(BF16)
            dst[MLA_NOPE:2 * MLA_NOPE] = src[hd:hd + MLA_NOPE].astype(BF16)
            dst[2 * MLA_NOPE:2 * MLA_NOPE + MLA_ROPE] = src[MLA_NOPE:hd].astype(BF16)
            dst[2 * MLA_NOPE + MLA_ROPE:] = src[hd + MLA_NOPE:].astype(BF16)

    acc = lax.dot_general(a_ref[...], wb_ref[...], _NT, preferred_element_type=F32)
    for p in range(n_pairs):
        c0 = p * pair_w
        rr = _rope_cols(acc[:, c0 + 2 * MLA_NOPE:c0 + pair_w], cos_ref[...], sin_ref[...])
        o_ref[:, c0:c0 + 2 * MLA_NOPE] = (acc[:, c0:c0 + 2 * MLA_NOPE] * scale).astype(BF16)
        o_ref[:, c0 + 2 * MLA_NOPE:c0 + pair_w] = (rr * scale).astype(BF16)


def _sigmoid(x):
    return 0.5 * jnp.tanh(0.5 * x) + 0.5


def _rest_kernel(a_ref, w_ref, o_ref, wb_ref, *, tn):
    j = pl.program_id(0)

    @pl.when(pl.program_id(1) == 0)
    def _():
        wb_ref[...] = w_ref[...].astype(BF16)

    def tile(epilogue):
        acc = lax.dot_general(a_ref[...], wb_ref[...], _NT, preferred_element_type=F32)
        o_ref[...] = epilogue(acc).astype(BF16)

    @pl.when(j < KD_OFF // tn)
    def _():
        tile(lambda acc: acc * (DIFF_QK ** -0.5 * LOG2E))

    @pl.when(jnp.logical_and(j >= KD_OFF // tn, j < GM_OFF // tn))
    def _():
        tile(lambda acc: acc)

    @pl.when(jnp.logical_and(j >= GM_OFF // tn, j < MGM_OFF // tn))
    def _():
        tile(lambda acc: acc * _sigmoid(acc))

    @pl.when(j >= MGM_OFF // tn)
    def _():
        tile(_sigmoid)


def _proj(h, w_t, cos_tab, sin_tab):
    m, k_dim = h.shape
    tm = PROJ_TM
    pair_w = 2 * (MLA_NOPE + MLA_ROPE)
    q_scale = (MLA_NOPE + MLA_ROPE) ** -0.5 * LOG2E

    def row_window(rows, offset_fn):
        return pl.BlockSpec((pl.Element(rows), pl.Element(k_dim)),
                            lambda *g: (pl.multiple_of(offset_fn(*g), SUBLANES), 0))

    qtn = QMLA_PAIRS_PER_TILE * pair_w
    qm = pl.pallas_call(
        functools.partial(_qmla_kernel, scale=q_scale),
        grid=(Q_MLA_W // qtn, m // tm),
        in_specs=[pl.BlockSpec((tm, k_dim), lambda j, i: (i, 0)),
                  pl.BlockSpec((qtn, k_dim), lambda j, i: (j, 0)),
                  pl.BlockSpec((tm, LANES), lambda j, i: (i, 0)),
                  pl.BlockSpec((tm, LANES), lambda j, i: (i, 0))],
        out_specs=pl.BlockSpec((tm, qtn), lambda j, i: (i, j)),
        out_shape=jax.ShapeDtypeStruct((m, Q_MLA_W), BF16),
        scratch_shapes=[pltpu.VMEM((qtn, k_dim), BF16)],
        compiler_params=_cparams(("arbitrary", "arbitrary")),
        name="proj_qmla",
    )(h, w_t, cos_tab, sin_tab)
    tn = REST_TN
    rest = pl.pallas_call(
        functools.partial(_rest_kernel, tn=tn),
        grid=(REST_W // tn, m // tm),
        in_specs=[pl.BlockSpec((tm, k_dim), lambda j, i: (i, 0)),
                  row_window(tn, lambda j, i: REST_OFF + tn * j)],
        out_specs=pl.BlockSpec((tm, tn), lambda j, i: (i, j)),
        out_shape=jax.ShapeDtypeStruct((m, REST_W), BF16),
        scratch_shapes=[pltpu.VMEM((tn, k_dim), BF16)],
        compiler_params=_cparams(("arbitrary", "arbitrary")),
        name="proj_rest",
    )(h, w_t)
    return qm, rest


def _kv_kernel(a_ref, wp_ref, g_ref, w_ref, cos_ref, sin_ref, k_ref, v_ref, wpb_ref, wb_ref):
    @pl.when(pl.program_id(0) == 0)
    def _():
        wpb_ref[:KVR_W] = wp_ref[...].astype(BF16)
        wpb_ref[KVR_W:] = jnp.zeros((KVR_PAD - KVR_W, wpb_ref.shape[1]), BF16)
        wb_ref[...] = w_ref[...].astype(BF16)

    kw = MLA_NOPE + MLA_V
    rb = a_ref.shape[0] // EPILOGUE_ROW_SPLIT
    for t in range(EPILOGUE_ROW_SPLIT):
        rows = slice(t * rb, (t + 1) * rb)
        p = lax.dot_general(a_ref[rows, :], wpb_ref[...], _NT, preferred_element_type=F32)
        ckv = p[:, :KV_RANK]
        ms = jnp.mean(ckv * ckv, axis=-1, keepdims=True)
        n = (ckv * lax.rsqrt(ms + EPS) * g_ref[...]).astype(BF16)
        kv = jnp.dot(n, wb_ref[...], preferred_element_type=F32)
        kr_even = _rope_cols(p[:, KV_RANK:], cos_ref[rows, :], sin_ref[rows, :])
        kr_odd = pltpu.roll(kr_even, MLA_ROPE, 1)
        ones_col = jnp.ones(kr_even.shape, BF16)
        for hd in range(MLA_HEADS):
            k_ref[rows, hd * kw:hd * kw + MLA_NOPE] = kv[:, hd * kw:hd * kw + MLA_NOPE].astype(BF16)
            k_ref[rows, hd * kw + MLA_NOPE:(hd + 1) * kw] = (
                kr_even if hd % 2 == 0 else kr_odd).astype(BF16)
            v_ref[rows, hd * kw:hd * kw + MLA_V] = kv[:, hd * kw + MLA_NOPE:(hd + 1) * kw].astype(BF16)
            v_ref[rows, hd * kw + MLA_V:(hd + 1) * kw] = ones_col


def _kv(h, w_t, g_kv, w_ukv, cos_tab, sin_tab):
    m, k_dim = h.shape
    tm = KV_TM
    kw = MLA_HEADS * (MLA_NOPE + MLA_V)
    return pl.pallas_call(
        _kv_kernel,
        grid=(m // tm,),
        in_specs=[pl.BlockSpec((tm, k_dim), lambda i: (i, 0)),
                  pl.BlockSpec((pl.Element(KVR_W), pl.Element(k_dim)),
                               lambda i: (pl.multiple_of(Q_MLA_W + 0 * i, SUBLANES), 0)),
                  pl.BlockSpec((1, KV_RANK), lambda i: (0, 0)),
                  pl.BlockSpec((KV_RANK, kw), lambda i: (0, 0)),
                  pl.BlockSpec((tm, LANES), lambda i: (i, 0)),
                  pl.BlockSpec((tm, LANES), lambda i: (i, 0))],
        out_specs=[pl.BlockSpec((tm, kw), lambda i: (i, 0)),
                   pl.BlockSpec((tm, kw), lambda i: (i, 0))],
        out_shape=[jax.ShapeDtypeStruct((m, kw), BF16),
                   jax.ShapeDtypeStruct((m, kw), BF16)],
        scratch_shapes=[pltpu.VMEM((KVR_PAD, k_dim), BF16), pltpu.VMEM((KV_RANK, kw), BF16)],
        compiler_params=_cparams(("arbitrary",)),
        name="kv_up",
    )(h, w_t, g_kv, w_ukv, cos_tab, sin_tab)


def _flash_pipeline(n_chains, nq, diags, loops, value_fn, finalize_fn,
                    s_ref, acc_ref, mpart_ref, macc_ref, sd_ref=None, mpd_ref=None):
    assert sd_ref is None or nq % 2 == 0
    chunk = s_ref.shape[2]

    def lane_tiles(x, n):
        return jnp.concatenate([x] * n, axis=1)

    def qk_phase(scores, stage):
        s_dst, m_dst = stage
        for ci, s in enumerate(scores):
            s_dst[ci] = s
            part = s[:, :LANES]
            for j in range(1, chunk // LANES):
                part = jnp.maximum(part, s[:, j * LANES:(j + 1) * LANES])
            m_dst[ci] = part

    def pv_phase(kc, stage):
        s_src, m_src = stage
        for ci in range(n_chains):
            m_acc = macc_ref[ci]
            m_run = jnp.maximum(m_acc, jnp.max(m_src[ci], axis=-1, keepdims=True))
            macc_ref[ci] = m_run
            p = jnp.exp2(s_src[ci] - lane_tiles(m_run, chunk // LANES))
            alpha = jnp.exp2(m_acc - m_run)
            pv = jnp.dot(p.astype(BF16), value_fn(ci, kc), preferred_element_type=F32)
            acc_ref[ci] = lane_tiles(alpha, acc_ref.shape[2] // LANES) * acc_ref[ci] + pv

    def reset():
        acc_ref[...] = jnp.zeros_like(acc_ref)
        macc_ref[...] = jnp.full(macc_ref.shape, NEG, F32)

    def tile(qt, stage, next_stage):
        cur = qt
        for trips_fn, score_fn in loops:
            def step(kc, cur, score_fn=score_fn):
                pv_phase(cur, stage)
                qk_phase(score_fn(qt, kc), stage)
                return kc

            cur = lax.fori_loop(0, trips_fn(qt), step, cur)
        nxt = jnp.minimum(qt + 1, nq - 1)

        def transition(_, cur, fn):
            if next_stage is stage:
                pv_phase(cur, stage)
                finalize_fn(qt)
                reset()
                qk_phase(fn(nxt), stage)
            else:
                qk_phase(fn(nxt), next_stage)
                pv_phase(cur, stage)
                finalize_fn(qt)
                reset()
            return cur

        for select_fn, fn in diags:
            if select_fn is None:
                transition(0, cur, fn)
            else:
                lax.fori_loop(0, select_fn(nxt), functools.partial(transition, fn=fn), cur)

    stage_a = (s_ref, mpart_ref)
    reset()
    for select_fn, fn in diags:
        if select_fn is None:
            qk_phase(fn(0), stage_a)
        else:
            lax.fori_loop(0, select_fn(0), lambda _, c, fn=fn: qk_phase(fn(0), stage_a) or c, 0)
    if sd_ref is None:
        lax.fori_loop(0, nq, lambda qt, c: tile(qt, stage_a, stage_a) or c, 0)
    else:
        stage_b = (sd_ref, mpd_ref)

        def tile_pair(j, c):
            tile(2 * j, stage_a, stage_b)
            tile(2 * j + 1, stage_b, stage_a)
            return c

        lax.fori_loop(0, nq // 2, tile_pair, 0)


def _causal_mask(rows, cols, row0):
    row = lax.broadcasted_iota(jnp.int32, (rows, cols), 0) + row0
    col = lax.broadcasted_iota(jnp.int32, (rows, cols), 1)
    return col <= row


def _pad_masked(s, cols):
    if s.shape[1] == cols:
        return s
    return jnp.concatenate([s, jnp.full((s.shape[0], cols - s.shape[1]), NEG, s.dtype)], axis=1)


_NT = (((1,), (1,)), ((), ()))


def _mla_attn_kernel(q_ref, k_ref, v_ref, g_ref, o_ref, acc_ref, qs_ref, s_ref, mpart_ref, macc_ref,
                     *, tq, rs):
    kw = 2 * MLA_NOPE
    tr = tq // rs
    nq = q_ref.shape[0] // tq
    chains = [(hp, r) for hp in range(2) for r in range(rs)]

    def stage_queries(qt):
        rows = pl.ds(pl.multiple_of(qt * tq, tq), tq)
        for hp in range(2):
            qs_ref[hp, :, :MLA_NOPE] = q_ref[rows, hp * MLA_NOPE:(hp + 1) * MLA_NOPE]
            qs_ref[hp, :, MLA_NOPE:] = q_ref[rows, 2 * MLA_NOPE:]

    def scores(qt, kc, diag):
        if diag:
            stage_queries(qt)
        k0 = pl.multiple_of(kc * tq, tq)
        out = []
        for hp, r in chains:
            nk = (r + 1) * tr if diag else tq
            q = qs_ref[hp, r * tr:(r + 1) * tr, :]
            k = k_ref[pl.ds(k0, nk), hp * kw:(hp + 1) * kw]
            s = lax.dot_general(q, k, _NT, preferred_element_type=F32)
            if diag:
                s = _pad_masked(jnp.where(_causal_mask(tr, nk, r * tr), s, NEG), tq)
            out.append(s)
        return out

    def values(ci, kc):
        hp = chains[ci][0]
        return v_ref[pl.ds(pl.multiple_of(kc * tq, tq), tq), hp * kw:(hp + 1) * kw]

    def finalize(qt):
        for ci, (hp, r) in enumerate(chains):
            rows = pl.ds(pl.multiple_of(qt * tq + r * tr, tr), tr)
            acc = acc_ref[ci]
            o = acc[:, :MLA_V] / acc[:, MLA_V:]
            gate = g_ref[rows, hp * MLA_V:(hp + 1) * MLA_V].astype(F32)
            o_ref[rows, hp * MLA_V:(hp + 1) * MLA_V] = (o * gate).astype(BF16)

    _flash_pipeline(len(chains), nq, [(None, lambda qt: scores(qt, qt, True))],
                    [(lambda qt: qt, lambda qt, kc: scores(qt, kc, False))],
                    values, finalize, s_ref, acc_ref, mpart_ref, macc_ref)


def _mla_attn(qm, kk, vv, rest, nb, s):
    tq = MLA_TQ
    pair_w = 2 * (MLA_NOPE + MLA_ROPE)
    kw = 4 * MLA_NOPE
    gate_blk = GM_OFF // (2 * MLA_V)
    rs = MLA_ROW_SPLIT
    return pl.pallas_call(
        functools.partial(_mla_attn_kernel, tq=tq, rs=rs),
        grid=(nb, MLA_HEADS // 2),
        in_specs=[pl.BlockSpec((s, pair_w), lambda b, hh: (b, hh)),
                  pl.BlockSpec((s, kw), lambda b, hh: (b, hh)),
                  pl.BlockSpec((s, kw), lambda b, hh: (b, hh)),
                  pl.BlockSpec((s, 2 * MLA_V), lambda b, hh: (b, gate_blk + hh))],
        out_specs=pl.BlockSpec((s, 2 * MLA_V), lambda b, hh: (b, hh)),
        out_shape=jax.ShapeDtypeStruct((nb * s, MLA_HEADS * MLA_V), BF16),
        scratch_shapes=[pltpu.VMEM((2 * rs, tq // rs, 2 * MLA_V), F32),
                        pltpu.VMEM((2, tq, 2 * MLA_NOPE), BF16),
                        pltpu.VMEM((2 * rs, tq // rs, tq), F32),
                        pltpu.VMEM((2 * rs, tq // rs, LANES), F32),
                        pltpu.VMEM((2 * rs, tq // rs, LANES), F32)],
        compiler_params=_cparams(("arbitrary", "arbitrary")),
        name="mla_attn",
    )(qm, kk, vv, rest)


def _diff_attn_kernel(ord_ref, q_ref, k_ref, v_ref, g_ref, pq_ref, pk_ref, plane_ref, sl_ref, lam_ref,
                      gs_ref, o_ref, vaug_ref, kaug_ref, acc_ref, qf_ref, s_ref, mpart_ref, macc_ref,
                      sd_ref, mpd_ref, *, tq, rs, lambda_init):
    seq = q_ref.shape[0]
    nq = seq // tq
    tr = tq // rs
    chains = [(c, r) for c in range(2) for r in range(rs)]
    lane = lax.broadcasted_iota(jnp.int32, (tq, LANES), 1)
    ones_col = jnp.ones((tq, LANES), BF16)
    slope2 = sl_ref[0, :, 0:1] * LOG2E
    c_pieces = _bf16_pieces(slope2)
    lane_row = lane[:1]
    c_query = jnp.where(lane_row < N_PAIR_LANES, _pick3(lane_row // N_PIECES, c_pieces), 0.0).astype(BF16)
    c_key = jnp.where(jnp.logical_and(lane_row >= N_PAIR_LANES, lane_row < 2 * N_PAIR_LANES),
                      _pick3((lane_row - N_PAIR_LANES) // N_PIECES, c_pieces), 0.0).astype(BF16)

    def bias_lanes(rows, key_side):
        pos_lanes = plane_ref[rows, :]
        if key_side:
            return jnp.where(lane < N_PAIR_LANES, pos_lanes, c_key)
        return jnp.where(lane < N_PAIR_LANES, c_query, pos_lanes)

    def stage_keys(kc, _):
        rows = pl.ds(pl.multiple_of(kc * tq, tq), tq)
        vaug_ref[rows, :DIFF_V] = v_ref[rows, :]
        vaug_ref[rows, DIFF_V:] = ones_col
        kaug_ref[rows, :2 * DIFF_QK] = k_ref[rows, :]
        kaug_ref[rows, 2 * DIFF_QK:] = bias_lanes(rows, True)
        return 0

    lax.fori_loop(0, nq, stage_keys, 0)

    def stage_queries(qt):
        rows = pl.ds(pl.multiple_of(qt * tq, tq), tq)
        q = q_ref[rows, :]
        q_side = bias_lanes(rows, False)
        qf_ref[0, :, :2 * DIFF_QK] = jnp.where(lane < DIFF_QK, q, 0).astype(BF16)
        qf_ref[1, :, :2 * DIFF_QK] = jnp.where(lane >= DIFF_QK, q, 0).astype(BF16)
        qf_ref[0, :, 2 * DIFF_QK:] = q_side
        qf_ref[1, :, 2 * DIFF_QK:] = q_side

    lq = lam_ref[...]
    lam = (jnp.exp(jnp.sum(lq[0:1] * lq[1:2], axis=-1, keepdims=True))
           - jnp.exp(jnp.sum(lq[2:3] * lq[3:4], axis=-1, keepdims=True)) + lambda_init)

    def scores(qt, kc, diag):
        k0 = pl.multiple_of(kc * tq, tq)
        k = k_ref[pl.ds(k0, tq), :]
        pk = slope2 * pk_ref[0, pl.ds(kc, 1), :]
        out = [None] * len(chains)
        for r in range(rs):
            pq = pq_ref[pl.ds(pl.multiple_of(qt * tq + r * tr, tr), tr), :]
            bias = jnp.abs(slope2 * pq - pk)
            mask = _causal_mask(tr, tq, r * tr) if diag else None
            for c in range(2):
                q = qf_ref[c, r * tr:(r + 1) * tr, :2 * DIFF_QK]
                s = lax.dot_general(q, k, _NT, preferred_element_type=F32) - bias
                out[chains.index((c, r))] = jnp.where(mask, s, NEG) if diag else s
        return out

    def scores_diag(qt):
        stage_queries(qt)
        return scores(qt, qt, True)

    def scores_ordered(qt, kc):
        k = kaug_ref[pl.ds(pl.multiple_of(kc * tq, tq), tq), :]
        return [lax.dot_general(qf_ref[c, r * tr:(r + 1) * tr, :], k, _NT, preferred_element_type=F32)
                for c, r in chains]

    def scores_diag_sorted(qt):
        stage_queries(qt)
        k0 = pl.multiple_of(qt * tq, tq)
        out = []
        for c, r in chains:
            nk = (r + 1) * tr
            s = lax.dot_general(qf_ref[c, r * tr:(r + 1) * tr, :], kaug_ref[pl.ds(k0, nk), :], _NT,
                                preferred_element_type=F32)
            out.append(_pad_masked(jnp.where(_causal_mask(tr, nk, r * tr), s, NEG), tq))
        return out

    batch = pl.program_id(0)

    def trips_ordered(qt):
        return jnp.where((ord_ref[batch, qt] & 1) != 0, qt, 0)

    def tile_sorted(qt):
        return (ord_ref[batch, qt] >> 1) & 1

    def values(ci, kc):
        return vaug_ref[pl.ds(pl.multiple_of(kc * tq, tq), tq), :]

    def finalize(qt):
        for r in range(rs):
            rows = pl.ds(pl.multiple_of(qt * tq + r * tr, tr), tr)
            a1 = acc_ref[chains.index((0, r))]
            a2 = acc_ref[chains.index((1, r))]
            o = a1[:, :DIFF_V] / a1[:, DIFF_V:] - lam * (a2[:, :DIFF_V] / a2[:, DIFF_V:])
            ms_o = jnp.mean(o * o, axis=-1, keepdims=True)
            o = o * lax.rsqrt(ms_o + EPS) * gs_ref[...] * (1.0 - lambda_init)
            o_ref[rows, :] = (o * g_ref[rows, :].astype(F32)).astype(BF16)

    _flash_pipeline(len(chains), nq,
                    [(tile_sorted, scores_diag_sorted), (lambda qt: 1 - tile_sorted(qt), scores_diag)],
                    [(trips_ordered, scores_ordered),
                     (lambda qt: qt - trips_ordered(qt), lambda qt, kc: scores(qt, kc, False))],
                    values, finalize, s_ref, acc_ref, mpart_ref, macc_ref, sd_ref, mpd_ref)


def _diff_attn(rest, pos_col, pos_row, pos_lanes, ordered, slopes, lam_par, g_subln, lambda_init,
               nb, s):
    tq = ATTN_TQ
    nq = s // tq
    rs = DIFF_ROW_SPLIT
    tr = tq // rs
    hw = DIFF_V
    return pl.pallas_call(
        functools.partial(_diff_attn_kernel, tq=tq, rs=rs, lambda_init=lambda_init),
        grid=(nb, DIFF_HEADS),
        in_specs=[pl.BlockSpec(memory_space=pltpu.SMEM),
                  pl.BlockSpec((s, hw), lambda b, hd: (b, QD_OFF // hw + hd)),
                  pl.BlockSpec((s, hw), lambda b, hd: (b, KD_OFF // hw + hd)),
                  pl.BlockSpec((s, hw), lambda b, hd: (b, VD_OFF // hw + hd)),
                  pl.BlockSpec((s, hw), lambda b, hd: (b, GD_OFF // hw + hd)),
                  pl.BlockSpec((s, 1), lambda b, hd: (b, 0)),
                  pl.BlockSpec((1, nq, tq), lambda b, hd: (b, 0, 0)),
                  pl.BlockSpec((s, LANES), lambda b, hd: (b, 0)),
                  pl.BlockSpec((1, 1, LANES), lambda b, hd: (hd, 0, 0)),
                  pl.BlockSpec((4, DIFF_QK), lambda b, hd: (0, 0)),
                  pl.BlockSpec((1, DIFF_V), lambda b, hd: (0, 0))],
        out_specs=pl.BlockSpec((s, hw), lambda b, hd: (b, hd)),
        out_shape=jax.ShapeDtypeStruct((nb * s, DIFF_HEADS * DIFF_V), BF16),
        scratch_shapes=[pltpu.VMEM((s, 2 * DIFF_V), BF16),
                        pltpu.VMEM((s, 4 * DIFF_QK), BF16),
                        pltpu.VMEM((2 * rs, tr, 2 * DIFF_V), F32),
                        pltpu.VMEM((2, tq, 4 * DIFF_QK), BF16),
                        pltpu.VMEM((2 * rs, tr, tq), F32),
                        pltpu.VMEM((2 * rs, tr, LANES), F32),
                        pltpu.VMEM((2 * rs, tr, LANES), F32),
                        pltpu.VMEM((2 * rs, tr, tq), F32),
                        pltpu.VMEM((2 * rs, tr, LANES), F32)],
        compiler_params=_cparams(("arbitrary", "arbitrary")),
        name="diff_attn",
    )(ordered, rest, rest, rest, rest, pos_col, pos_row, pos_lanes, slopes, lam_par, g_subln)


def _cast_kernel(w1_ref, w2_ref, w3_ref, o1_ref, o2_ref, o3_ref):
    o1_ref[...] = w1_ref[...].astype(BF16)
    o2_ref[...] = w2_ref[...].astype(BF16)
    o3_ref[...] = w3_ref[...].astype(BF16)


def _cast_weights(w1, w2, w3):
    steps = CAST_STEPS
    specs = [pl.BlockSpec((w.shape[0] // steps, w.shape[1]), lambda i: (i, 0)) for w in (w1, w2, w3)]
    return pl.pallas_call(
        _cast_kernel,
        grid=(steps,),
        in_specs=specs,
        out_specs=specs,
        out_shape=[jax.ShapeDtypeStruct(w.shape, BF16) for w in (w1, w2, w3)],
        compiler_params=_cparams(("arbitrary",)),
        name="cast_weights",
    )(w1, w2, w3)


def _merge_out_kernel(a1_ref, a2_ref, s1a_ref, s1b_ref, s2a_ref, s2b_ref, w1_ref, w2_ref, wo_ref,
                      x_ref, ada_ref, g_ref, o_ref):
    half = s1a_ref.shape[1]
    rb = a1_ref.shape[0] // EPILOGUE_ROW_SPLIT
    for t in range(EPILOGUE_ROW_SPLIT):
        rows = slice(t * rb, (t + 1) * rb)
        y1 = jnp.dot(a1_ref[rows, :], w1_ref[...], preferred_element_type=F32)
        y2 = jnp.dot(a2_ref[rows, :], w2_ref[...], preferred_element_type=F32)
        merged = jnp.concatenate(
            [s1a_ref[rows, :].astype(F32) * y1[:, :half] + s2a_ref[rows, :].astype(F32) * y2[:, :half],
             s1b_ref[rows, :].astype(F32) * y1[:, half:] + s2b_ref[rows, :].astype(F32) * y2[:, half:]],
            axis=1).astype(BF16)
        y = jnp.dot(merged, wo_ref[...], preferred_element_type=F32)
        ms = jnp.mean(y * y, axis=-1, keepdims=True)
        yn = y * lax.rsqrt(ms + EPS) * g_ref[...]
        o_ref[0, rows, :] = x_ref[0, rows, :] + ada_ref[0, 2:3, :] * yn


def _merge_out(og_mla, og_diff, rest, w1b, w2b, wob, x, ada3, g_post):
    nb, s, d = x.shape
    k_dim = og_mla.shape[1]
    tm = OUT_TM
    ns = s // tm
    half = d // 2
    row = lambda b, i: (b * ns + i, 0)
    gate = lambda blk: pl.BlockSpec((tm, half), lambda b, i: (b * ns + i, blk))
    resident = lambda shape: pl.BlockSpec(shape, lambda b, i: (0, 0), pipeline_mode=pl.Buffered(1))
    return pl.pallas_call(
        _merge_out_kernel,
        grid=(nb, ns),
        in_specs=[pl.BlockSpec((tm, k_dim), row),
                  pl.BlockSpec((tm, k_dim), row),
                  gate(MGM_OFF // half), gate(MGM_OFF // half + 1),
                  gate(MGD_OFF // half), gate(MGD_OFF // half + 1),
                  resident((k_dim, d)), resident((k_dim, d)), resident((d, d)),
                  pl.BlockSpec((1, tm, d), lambda b, i: (b, i, 0)),
                  pl.BlockSpec((1, 3, d), lambda b, i: (b, 0, 0)),
                  pl.BlockSpec((1, d), lambda b, i: (0, 0))],
        out_specs=pl.BlockSpec((1, tm, d), lambda b, i: (b, i, 0)),
        out_shape=jax.ShapeDtypeStruct((nb, s, d), F32),
        compiler_params=_cparams(("arbitrary", "arbitrary")),
        name="merge_out",
    )(og_mla, og_diff, rest, rest, rest, rest, w1b, w2b, wob, x, ada3, g_post)


def kernel(x, c, positions, w_ada, b_ada, g_pre, w_in, g_kv, w_ukv, lambda_q1, lambda_k1,
           lambda_q2, lambda_k2, g_subln, w_o_mla, w_o_diff, w_out, g_post):
    nb, s, d = x.shape
    depth = w_in.shape[0]
    assert w_in.shape[1:] == (d, REST_OFF + REST_W) and w_out.shape[1:] == (d, d)
    assert s % ATTN_TQ == 0 and s % MLA_TQ == 0 and (nb * s) % PROJ_TM == 0
    half = MLA_ROPE // 2
    inv = ROPE_THETA ** (-jnp.arange(half, dtype=F32) / half)
    inv_tab = jnp.tile(inv, LANES // half).reshape(1, LANES)
    slopes = 2.0 ** (-8.0 * jnp.arange(1, DIFF_HEADS + 1, dtype=F32) / DIFF_HEADS)
    slopes = jnp.broadcast_to(slopes.reshape(DIFF_HEADS, 1, 1), (DIFF_HEADS, 1, LANES))
    pos_col = positions.reshape(nb * s, 1)
    pos_colf = pos_col.astype(F32)
    pos_chunks = positions.reshape(nb, s // ATTN_TQ, ATTN_TQ)
    pos_row = pos_chunks.astype(F32)
    run_max = lax.cummax(pos_chunks.max(axis=-1), axis=1)
    prev_max = jnp.concatenate(
        [jnp.full((nb, 1), jnp.iinfo(jnp.int32).min, jnp.int32), run_max[:, :-1]], axis=1)
    tile_sorted = jnp.all(pos_chunks[..., 1:] >= pos_chunks[..., :-1], axis=-1)
    ordered = ((prev_max <= pos_chunks.min(axis=-1)).astype(jnp.int32)
               + 2 * tile_sorted.astype(jnp.int32))

    for l in range(depth):
        ada3 = _ada(c, w_ada[l], b_ada[l]).reshape(nb, 3, d)
        h, cos_tab, sin_tab, pos_lanes = _norm(x, ada3, g_pre[l].reshape(1, d), pos_col, inv_tab)
        w_t = jnp.swapaxes(w_in[l], 0, 1)
        qm, rest = _proj(h, w_t, cos_tab, sin_tab)
        kk, vv = _kv(h, w_t, g_kv[l].reshape(1, KV_RANK), w_ukv[l], cos_tab, sin_tab)
        og_mla = _mla_attn(qm, kk, vv, rest, nb, s)
        lam_par = jnp.stack([lambda_q1[l], lambda_k1[l], lambda_q2[l], lambda_k2[l]]).astype(F32)
        og_diff = _diff_attn(rest, pos_colf, pos_row, pos_lanes, ordered, slopes, lam_par,
                             g_subln[l].reshape(1, DIFF_V), 0.8 - 0.6 * math.exp(-0.3 * l), nb, s)
        w1b, w2b, wob = _cast_weights(w_o_mla[l], w_o_diff[l], w_out[l])
        x = _merge_out(og_mla, og_diff, rest, w1b, w2b, wob, x, ada3, g_post[l].reshape(1, d))
    return x
```

```python
import functools
import math

import jax
import jax.numpy as jnp
from jax import lax
from jax.experimental import pallas as pl
from jax.experimental.pallas import tpu as pltpu

F32 = jnp.float32
BF16 = jnp.bfloat16

MLA_HEADS = 8
MLA_NOPE = 128
MLA_ROPE = 64
MLA_V = 128
KV_RANK = 512
ROPE_THETA = 10000.0
DIFF_HEADS = 8
DIFF_QK = 64
DIFF_V = 128
EPS = 1e-6
NEG = -1e30

LANES = 128
SUBLANES = 8
VMEM_LIMIT = 56 * 1024 * 1024
ADA_TN = 1024
NORM_TS = 1024
MAX_FOLD_POS = 1 << 16
N_PIECES = 3
N_PAIR_LANES = N_PIECES * N_PIECES
ATTN_TQ = 1024
DIFF_ROW_SPLIT = 2
MLA_TQ = 1024
MLA_ROW_SPLIT = 2
PROJ_TM = 1024
REST_TN = 1024
QMLA_PAIRS_PER_TILE = 2
KV_TM = 1024
CAST_STEPS = 4
OUT_TM = 512
EPILOGUE_ROW_SPLIT = 2
LOG2E = math.log2(math.e)

Q_MLA_W = MLA_HEADS * (MLA_NOPE + MLA_ROPE)
KVR_W = KV_RANK + MLA_ROPE
KVR_PAD = 640
REST_OFF = Q_MLA_W + KVR_W
QD_OFF, KD_OFF, VD_OFF, GM_OFF, GD_OFF, MGM_OFF, MGD_OFF = 0, 1024, 2048, 3072, 4096, 5120, 7168
REST_W = 9216


def _cparams(sem):
    return pltpu.CompilerParams(dimension_semantics=sem, vmem_limit_bytes=VMEM_LIMIT)


def _ada_kernel(cb_ref, w_ref, b_ref, o_ref):
    k_dim, tn = w_ref.shape
    nb = cb_ref.shape[0]
    nchunk = tn // LANES

    def body(i, accs):
        k0 = pl.multiple_of(i * SUBLANES, SUBLANES)
        out = []
        for b in range(nb):
            cv = cb_ref[b, pl.ds(k0, SUBLANES), :]
            for j in range(nchunk):
                wv = w_ref[pl.ds(k0, SUBLANES), j * LANES:(j + 1) * LANES]
                out.append(accs[b * nchunk + j] + wv * cv)
        return tuple(out)

    init = tuple(jnp.zeros((SUBLANES, LANES), F32) for _ in range(nb * nchunk))
    accs = lax.fori_loop(0, k_dim // SUBLANES, body, init, unroll=8)
    for b in range(nb):
        row = jnp.concatenate(
            [jnp.sum(accs[b * nchunk + j], axis=0, keepdims=True) for j in range(nchunk)], axis=1)
        o_ref[b:b + 1, :] = row + b_ref[...]


def _ada(c, w, bias):
    nb, k_dim = c.shape
    n = w.shape[1]
    tn = ADA_TN
    cb = jnp.broadcast_to(c[:, :, None], (nb, k_dim, LANES))
    return pl.pallas_call(
        _ada_kernel,
        grid=(n // tn,),
        in_specs=[pl.BlockSpec((nb, k_dim, LANES), lambda j: (0, 0, 0)),
                  pl.BlockSpec((k_dim, tn), lambda j: (0, j)),
                  pl.BlockSpec((1, tn), lambda j: (0, j))],
        out_specs=pl.BlockSpec((nb, tn), lambda j: (0, j)),
        out_shape=jax.ShapeDtypeStruct((nb, n), F32),
        compiler_params=_cparams(("arbitrary",)),
        name="ada",
    )(cb, w, bias.reshape(1, n))


def _bf16_pieces(x):
    p1 = x.astype(BF16)
    r = x - p1.astype(F32)
    p2 = r.astype(BF16)
    return p1.astype(F32), p2.astype(F32), r - p2.astype(F32)


def _pick3(sel, x3):
    return jnp.where(sel == 0, x3[0], jnp.where(sel == 1, x3[1], x3[2]))


def _norm_kernel(x_ref, ada_ref, g_ref, pos_ref, inv_ref, h_ref, cos_ref, sin_ref, plane_ref):
    x = x_ref[0]
    ms = jnp.mean(x * x, axis=-1, keepdims=True)
    y = x * lax.rsqrt(ms + EPS) * g_ref[...]
    shift = ada_ref[0, 0:1, :]
    scale = ada_ref[0, 1:2, :]
    h_ref[...] = (y * (1.0 + scale) + shift).astype(BF16)
    ang = pos_ref[...].astype(F32) * inv_ref[...]
    lane = lax.broadcasted_iota(jnp.int32, ang.shape, 1)
    sign = jnp.where((lane % MLA_ROPE) < MLA_ROPE // 2, -1.0, 1.0).astype(F32)
    cos_ref[...] = jnp.cos(ang)
    sin_ref[...] = jnp.sin(ang) * sign
    pos3 = _bf16_pieces(pos_ref[...].astype(F32))
    piece = _pick3(lane % N_PIECES, pos3)
    plane_ref[...] = jnp.where(lane < N_PAIR_LANES, piece,
                               jnp.where(lane < 2 * N_PAIR_LANES, -piece, 0.0)).astype(BF16)


def _norm(x, ada3, g_pre, pos_col, inv_tab):
    nb, s, d = x.shape
    ts = NORM_TS
    ns = s // ts
    row = lambda b, i: (b * ns + i, 0)
    return pl.pallas_call(
        _norm_kernel,
        grid=(nb, ns),
        in_specs=[pl.BlockSpec((1, ts, d), lambda b, i: (b, i, 0)),
                  pl.BlockSpec((1, 3, d), lambda b, i: (b, 0, 0)),
                  pl.BlockSpec((1, d), lambda b, i: (0, 0)),
                  pl.BlockSpec((ts, 1), row),
                  pl.BlockSpec((1, LANES), lambda b, i: (0, 0))],
        out_specs=[pl.BlockSpec((ts, d), row),
                   pl.BlockSpec((ts, LANES), row),
                   pl.BlockSpec((ts, LANES), row),
                   pl.BlockSpec((ts, LANES), row)],
        out_shape=[jax.ShapeDtypeStruct((nb * s, d), BF16),
                   jax.ShapeDtypeStruct((nb * s, LANES), F32),
                   jax.ShapeDtypeStruct((nb * s, LANES), F32),
                   jax.ShapeDtypeStruct((nb * s, LANES), BF16)],
        compiler_params=_cparams(("arbitrary", "arbitrary")),
        name="prenorm",
    )(x, ada3, g_pre, pos_col, inv_tab)


def _rope_cols(r, cos, sin_signed):
    lane = lax.broadcasted_iota(jnp.int32, r.shape, 1)
    half = MLA_ROPE // 2
    partner = jnp.where((lane % MLA_ROPE) < half,
                        pltpu.roll(r, LANES - half, 1), pltpu.roll(r, half, 1))
    return r * cos + partner * sin_signed


def _qmla_kernel(a_ref, w_ref, cos_ref, sin_ref, o_ref, wb_ref, *, scale):
    hd = MLA_NOPE + MLA_ROPE
    pair_w = 2 * hd
    n_pairs = wb_ref.shape[0] // pair_w

    @pl.when(pl.program_id(1) == 0)
    def _():
        for p in range(n_pairs):
            src, dst = w_ref.at[p * pair_w:(p + 1) * pair_w], wb_ref.at[p * pair_w:(p + 1) * pair_w]
            dst[:MLA_NOPE] = src[:MLA_NOPE].astype(BF16)
            dst[MLA_NOPE:2 * MLA_NOPE] = src[hd:hd + MLA_NOPE].astype(BF16)
            dst[2 * MLA_NOPE:2 * MLA_NOPE + MLA_ROPE] = src[MLA_NOPE:hd].astype(BF16)
            dst[2 * MLA_NOPE + MLA_ROPE:] = src[hd + MLA_NOPE:].astype(BF16)

    acc = lax.dot_general(a_ref[...], wb_ref[...], _NT, preferred_element_type=F32)
    for p in range(n_pairs):
        c0 = p * pair_w
        rr = _rope_cols(acc[:, c0 + 2 * MLA_NOPE:c0 + pair_w], cos_ref[...], sin_ref[...])
        o_ref[:, c0:c0 + 2 * MLA_NOPE] = (acc[:, c0:c0 + 2 * MLA_NOPE] * scale).astype(BF16)
        o_ref[:, c0 + 2 * MLA_NOPE:c0 + pair_w] = (rr * scale).astype(BF16)


def _sigmoid(x):
    return 0.5 * jnp.tanh(0.5 * x) + 0.5


def _rest_kernel(a_ref, w_ref, o_ref, wb_ref, *, tn):
    j = pl.program_id(0)

    @pl.when(pl.program_id(1) == 0)
    def _():
        wb_ref[...] = w_ref[...].astype(BF16)

    def tile(epilogue):
        acc = lax.dot_general(a_ref[...], wb_ref[...], _NT, preferred_element_type=F32)
        o_ref[...] = epilogue(acc).astype(BF16)

    @pl.when(j < KD_OFF // tn)
    def _():
        tile(lambda acc: acc * (DIFF_QK ** -0.5 * LOG2E))

    @pl.when(jnp.logical_and(j >= KD_OFF // tn, j < GM_OFF // tn))
    def _():
        tile(lambda acc: acc)

    @pl.when(jnp.logical_and(j >= GM_OFF // tn, j < MGM_OFF // tn))
    def _():
        tile(lambda acc: acc * _sigmoid(acc))

    @pl.when(j >= MGM_OFF // tn)
    def _():
        tile(_sigmoid)


def _proj(h, w_t, cos_tab, sin_tab):
    m, k_dim = h.shape
    tm = PROJ_TM
    pair_w = 2 * (MLA_NOPE + MLA_ROPE)
    q_scale = (MLA_NOPE + MLA_ROPE) ** -0.5 * LOG2E

    def row_window(rows, offset_fn):
        return pl.BlockSpec((pl.Element(rows), pl.Element(k_dim)),
                            lambda *g: (pl.multiple_of(offset_fn(*g), SUBLANES), 0))

    qtn = QMLA_PAIRS_PER_TILE * pair_w
    qm = pl.pallas_call(
        functools.partial(_qmla_kernel, scale=q_scale),
        grid=(Q_MLA_W // qtn, m // tm),
        in_specs=[pl.BlockSpec((tm, k_dim), lambda j, i: (i, 0)),
                  pl.BlockSpec((qtn, k_dim), lambda j, i: (j, 0)),
                  pl.BlockSpec((tm, LANES), lambda j, i: (i, 0)),
                  pl.BlockSpec((tm, LANES), lambda j, i: (i, 0))],
        out_specs=pl.BlockSpec((tm, qtn), lambda j, i: (i, j)),
        out_shape=jax.ShapeDtypeStruct((m, Q_MLA_W), BF16),
        scratch_shapes=[pltpu.VMEM((qtn, k_dim), BF16)],
        compiler_params=_cparams(("arbitrary", "arbitrary")),
        name="proj_qmla",
    )(h, w_t, cos_tab, sin_tab)
    tn = REST_TN
    rest = pl.pallas_call(
        functools.partial(_rest_kernel, tn=tn),
        grid=(REST_W // tn, m // tm),
        in_specs=[pl.BlockSpec((tm, k_dim), lambda j, i: (i, 0)),
                  row_window(tn, lambda j, i: REST_OFF + tn * j)],
        out_specs=pl.BlockSpec((tm, tn), lambda j, i: (i, j)),
        out_shape=jax.ShapeDtypeStruct((m, REST_W), BF16),
        scratch_shapes=[pltpu.VMEM((tn, k_dim), BF16)],
        compiler_params=_cparams(("arbitrary", "arbitrary")),
        name="proj_rest",
    )(h, w_t)
    return qm, rest


def _kv_kernel(a_ref, wp_ref, g_ref, w_ref, cos_ref, sin_ref, k_ref, v_ref, wpb_ref, wb_ref):
    @pl.when(pl.program_id(0) == 0)
    def _():
        wpb_ref[:KVR_W] = wp_ref[...].astype(BF16)
        wpb_ref[KVR_W:] = jnp.zeros((KVR_PAD - KVR_W, wpb_ref.shape[1]), BF16)
        wb_ref[...] = w_ref[...].astype(BF16)

    kw = MLA_NOPE + MLA_V
    rb = a_ref.shape[0] // EPILOGUE_ROW_SPLIT
    for t in range(EPILOGUE_ROW_SPLIT):
        rows = slice(t * rb, (t + 1) * rb)
        p = lax.dot_general(a_ref[rows, :], wpb_ref[...], _NT, preferred_element_type=F32)
        ckv = p[:, :KV_RANK]
        ms = jnp.mean(ckv * ckv, axis=-1, keepdims=True)
        n = (ckv * lax.rsqrt(ms + EPS) * g_ref[...]).astype(BF16)
        kv = jnp.dot(n, wb_ref[...], preferred_element_type=F32)
        kr_even = _rope_cols(p[:, KV_RANK:], cos_ref[rows, :], sin_ref[rows, :])
        kr_odd = pltpu.roll(kr_even, MLA_ROPE, 1)
        ones_col = jnp.ones(kr_even.shape, BF16)
        for hd in range(MLA_HEADS):
            k_ref[rows, hd * kw:hd * kw + MLA_NOPE] = kv[:, hd * kw:hd * kw + MLA_NOPE].astype(BF16)
            k_ref[rows, hd * kw + MLA_NOPE:(hd + 1) * kw] = (
                kr_even if hd % 2 == 0 else kr_odd).astype(BF16)
            v_ref[rows, hd * kw:hd * kw + MLA_V] = kv[:, hd * kw + MLA_NOPE:(hd + 1) * kw].astype(BF16)
            v_ref[rows, hd * kw + MLA_V:(hd + 1) * kw] = ones_col


def _kv(h, w_t, g_kv, w_ukv, cos_tab, sin_tab):
    m, k_dim = h.shape
    tm = KV_TM
    kw = MLA_HEADS * (MLA_NOPE + MLA_V)
    return pl.pallas_call(
        _kv_kernel,
        grid=(m // tm,),
        in_specs=[pl.BlockSpec((tm, k_dim), lambda i: (i, 0)),
                  pl.BlockSpec((pl.Element(KVR_W), pl.Element(k_dim)),
                               lambda i: (pl.multiple_of(Q_MLA_W + 0 * i, SUBLANES), 0)),
                  pl.BlockSpec((1, KV_RANK), lambda i: (0, 0)),
                  pl.BlockSpec((KV_RANK, kw), lambda i: (0, 0)),
                  pl.BlockSpec((tm, LANES), lambda i: (i, 0)),
                  pl.BlockSpec((tm, LANES), lambda i: (i, 0))],
        out_specs=[pl.BlockSpec((tm, kw), lambda i: (i, 0)),
                   pl.BlockSpec((tm, kw), lambda i: (i, 0))],
        out_shape=[jax.ShapeDtypeStruct((m, kw), BF16),
                   jax.ShapeDtypeStruct((m, kw), BF16)],
        scratch_shapes=[pltpu.VMEM((KVR_PAD, k_dim), BF16), pltpu.VMEM((KV_RANK, kw), BF16)],
        compiler_params=_cparams(("arbitrary",)),
        name="kv_up",
    )(h, w_t, g_kv, w_ukv, cos_tab, sin_tab)


def _flash_pipeline(n_chains, nq, diags, loops, value_fn, finalize_fn,
                    s_ref, acc_ref, mpart_ref, macc_ref, sd_ref=None, mpd_ref=None):
    assert sd_ref is None or nq % 2 == 0
    chunk = s_ref.shape[2]

    def lane_tiles(x, n):
        return jnp.concatenate([x] * n, axis=1)

    def qk_phase(scores, stage):
        s_dst, m_dst = stage
        for ci, s in enumerate(scores):
            s_dst[ci] = s
            part = s[:, :LANES]
            for j in range(1, chunk // LANES):
                part = jnp.maximum(part, s[:, j * LANES:(j + 1) * LANES])
            m_dst[ci] = part

    def pv_phase(kc, stage):
        s_src, m_src = stage
        for ci in range(n_chains):
            m_acc = macc_ref[ci]
            m_run = jnp.maximum(m_acc, jnp.max(m_src[ci], axis=-1, keepdims=True))
            macc_ref[ci] = m_run
            p = jnp.exp2(s_src[ci] - lane_tiles(m_run, chunk // LANES))
            alpha = jnp.exp2(m_acc - m_run)
            pv = jnp.dot(p.astype(BF16), value_fn(ci, kc), preferred_element_type=F32)
            acc_ref[ci] = lane_tiles(alpha, acc_ref.shape[2] // LANES) * acc_ref[ci] + pv

    def reset():
        acc_ref[...] = jnp.zeros_like(acc_ref)
        macc_ref[...] = jnp.full(macc_ref.shape, NEG, F32)

    def tile(qt, stage, next_stage):
        cur = qt
        for trips_fn, score_fn in loops:
            def step(kc, cur, score_fn=score_fn):
                pv_phase(cur, stage)
                qk_phase(score_fn(qt, kc), stage)
                return kc

            cur = lax.fori_loop(0, trips_fn(qt), step, cur)
        nxt = jnp.minimum(qt + 1, nq - 1)

        def transition(_, cur, fn):
            if next_stage is stage:
                pv_phase(cur, stage)
                finalize_fn(qt)
                reset()
                qk_phase(fn(nxt), stage)
            else:
                qk_phase(fn(nxt), next_stage)
                pv_phase(cur, stage)
                finalize_fn(qt)
                reset()
            return cur

        for select_fn, fn in diags:
            if select_fn is None:
                transition(0, cur, fn)
            else:
                lax.fori_loop(0, select_fn(nxt), functools.partial(transition, fn=fn), cur)

    stage_a = (s_ref, mpart_ref)
    reset()
    for select_fn, fn in diags:
        if select_fn is None:
            qk_phase(fn(0), stage_a)
        else:
            lax.fori_loop(0, select_fn(0), lambda _, c, fn=fn: qk_phase(fn(0), stage_a) or c, 0)
    if sd_ref is None:
        lax.fori_loop(0, nq, lambda qt, c: tile(qt, stage_a, stage_a) or c, 0)
    else:
        stage_b = (sd_ref, mpd_ref)

        def tile_pair(j, c):
            tile(2 * j, stage_a, stage_b)
            tile(2 * j + 1, stage_b, stage_a)
            return c

        lax.fori_loop(0, nq // 2, tile_pair, 0)


def _causal_mask(rows, cols, row0):
    row = lax.broadcasted_iota(jnp.int32, (rows, cols), 0) + row0
    col = lax.broadcasted_iota(jnp.int32, (rows, cols), 1)
    return col <= row


def _pad_masked(s, cols):
    if s.shape[1] == cols:
        return s
    return jnp.concatenate([s, jnp.full((s.shape[0], cols - s.shape[1]), NEG, s.dtype)], axis=1)


_NT = (((1,), (1,)), ((), ()))


def _mla_attn_kernel(q_ref, k_ref, v_ref, g_ref, o_ref, acc_ref, qs_ref, s_ref, mpart_ref, macc_ref,
                     *, tq, rs):
    kw = 2 * MLA_NOPE
    tr = tq // rs
    nq = q_ref.shape[0] // tq
    chains = [(hp, r) for hp in range(2) for r in range(rs)]

    def stage_queries(qt):
        rows = pl.ds(pl.multiple_of(qt * tq, tq), tq)
        for hp in range(2):
            qs_ref[hp, :, :MLA_NOPE] = q_ref[rows, hp * MLA_NOPE:(hp + 1) * MLA_NOPE]
            qs_ref[hp, :, MLA_NOPE:] = q_ref[rows, 2 * MLA_NOPE:]

    def scores(qt, kc, diag):
        if diag:
            stage_queries(qt)
        k0 = pl.multiple_of(kc * tq, tq)
        out = []
        for hp, r in chains:
            nk = (r + 1) * tr if diag else tq
            q = qs_ref[hp, r * tr:(r + 1) * tr, :]
            k = k_ref[pl.ds(k0, nk), hp * kw:(hp + 1) * kw]
            s = lax.dot_general(q, k, _NT, preferred_element_type=F32)
            if diag:
                s = _pad_masked(jnp.where(_causal_mask(tr, nk, r * tr), s, NEG), tq)
            out.append(s)
        return out

    def values(ci, kc):
        hp = chains[ci][0]
        return v_ref[pl.ds(pl.multiple_of(kc * tq, tq), tq), hp * kw:(hp + 1) * kw]

    def finalize(qt):
        for ci, (hp, r) in enumerate(chains):
            rows = pl.ds(pl.multiple_of(qt * tq + r * tr, tr), tr)
            acc = acc_ref[ci]
            o = acc[:, :MLA_V] / acc[:, MLA_V:]
            gate = g_ref[rows, hp * MLA_V:(hp + 1) * MLA_V].astype(F32)
            o_ref[rows, hp * MLA_V:(hp + 1) * MLA_V] = (o * gate).astype(BF16)

    _flash_pipeline(len(chains), nq, [(None, lambda qt: scores(qt, qt, True))],
                    [(lambda qt: qt, lambda qt, kc: scores(qt, kc, False))],
                    values, finalize, s_ref, acc_ref, mpart_ref, macc_ref)


def _mla_attn(qm, kk, vv, rest, nb, s):
    tq = MLA_TQ
    pair_w = 2 * (MLA_NOPE + MLA_ROPE)
    kw = 4 * MLA_NOPE
    gate_blk = GM_OFF // (2 * MLA_V)
    rs = MLA_ROW_SPLIT
    return pl.pallas_call(
        functools.partial(_mla_attn_kernel, tq=tq, rs=rs),
        grid=(nb, MLA_HEADS // 2),
        in_specs=[pl.BlockSpec((s, pair_w), lambda b, hh: (b, hh)),
                  pl.BlockSpec((s, kw), lambda b, hh: (b, hh)),
                  pl.BlockSpec((s, kw), lambda b, hh: (b, hh)),
                  pl.BlockSpec((s, 2 * MLA_V), lambda b, hh: (b, gate_blk + hh))],
        out_specs=pl.BlockSpec((s, 2 * MLA_V), lambda b, hh: (b, hh)),
        out_shape=jax.ShapeDtypeStruct((nb * s, MLA_HEADS * MLA_V), BF16),
        scratch_shapes=[pltpu.VMEM((2 * rs, tq // rs, 2 * MLA_V), F32),
                        pltpu.VMEM((2, tq, 2 * MLA_NOPE), BF16),
                        pltpu.VMEM((2 * rs, tq // rs, tq), F32),
                        pltpu.VMEM((2 * rs, tq // rs, LANES), F32),
                        pltpu.VMEM((2 * rs, tq // rs, LANES), F32)],
        compiler_params=_cparams(("arbitrary", "arbitrary")),
        name="mla_attn",
    )(qm, kk, vv, rest)


def _diff_attn_kernel(ord_ref, q_ref, k_ref, v_ref, g_ref, pq_ref, pk_ref, plane_ref, sl_ref, lam_ref,
                      gs_ref, o_ref, vaug_ref, kaug_ref, acc_ref, qf_ref, s_ref, mpart_ref, macc_ref,
                      sd_ref, mpd_ref, *, tq, rs, lambda_init):
    seq = q_ref.shape[0]
    nq = seq // tq
    tr = tq // rs
    chains = [(c, r) for c in range(2) for r in range(rs)]
    lane = lax.broadcasted_iota(jnp.int32, (tq, LANES), 1)
    ones_col = jnp.ones((tq, LANES), BF16)
    slope2 = sl_ref[0, :, 0:1] * LOG2E
    c_pieces = _bf16_pieces(slope2)
    lane_row = lane[:1]
    c_query = jnp.where(lane_row < N_PAIR_LANES, _pick3(lane_row // N_PIECES, c_pieces), 0.0).astype(BF16)
    c_key = jnp.where(jnp.logical_and(lane_row >= N_PAIR_LANES, lane_row < 2 * N_PAIR_LANES),
                      _pick3((lane_row - N_PAIR_LANES) // N_PIECES, c_pieces), 0.0).astype(BF16)

    def bias_lanes(rows, key_side):
        pos_lanes = plane_ref[rows, :]
        if key_side:
            return jnp.where(lane < N_PAIR_LANES, pos_lanes, c_key)
        return jnp.where(lane < N_PAIR_LANES, c_query, pos_lanes)

    def stage_keys(kc, _):
        rows = pl.ds(pl.multiple_of(kc * tq, tq), tq)
        vaug_ref[rows, :DIFF_V] = v_ref[rows, :]
        vaug_ref[rows, DIFF_V:] = ones_col
        kaug_ref[rows, :2 * DIFF_QK] = k_ref[rows, :]
        kaug_ref[rows, 2 * DIFF_QK:] = bias_lanes(rows, True)
        return 0

    lax.fori_loop(0, nq, stage_keys, 0)

    def stage_queries(qt):
        rows = pl.ds(pl.multiple_of(qt * tq, tq), tq)
        q = q_ref[rows, :]
        q_side = bias_lanes(rows, False)
        qf_ref[0, :, :2 * DIFF_QK] = jnp.where(lane < DIFF_QK, q, 0).astype(BF16)
        qf_ref[1, :, :2 * DIFF_QK] = jnp.where(lane >= DIFF_QK, q, 0).astype(BF16)
        qf_ref[0, :, 2 * DIFF_QK:] = q_side
        qf_ref[1, :, 2 * DIFF_QK:] = q_side

    lq = lam_ref[...]
    lam = (jnp.exp(jnp.sum(lq[0:1] * lq[1:2], axis=-1, keepdims=True))
           - jnp.exp(jnp.sum(lq[2:3] * lq[3:4], axis=-1, keepdims=True)) + lambda_init)

    def scores(qt, kc, diag):
        k0 = pl.multiple_of(kc * tq, tq)
        k = k_ref[pl.ds(k0, tq), :]
        pk = pk_ref[0, pl.ds(kc, 1), :]
        out = [None] * len(chains)
        for r in range(rs):
            pq = pq_ref[pl.ds(pl.multiple_of(qt * tq + r * tr, tr), tr), :]
            bias = slope2 * jnp.abs(pq - pk)
            mask = _causal_mask(tr, tq, r * tr) if diag else None
            for c in range(2):
                q = qf_ref[c, r * tr:(r + 1) * tr, :2 * DIFF_QK]
                s = lax.dot_general(q, k, _NT, preferred_element_type=F32) - bias
                out[chains.index((c, r))] = jnp.where(mask, s, NEG) if diag else s
        return out

    def scores_diag(qt):
        stage_queries(qt)
        return scores(qt, qt, True)

    def scores_ordered(qt, kc):
        k = kaug_ref[pl.ds(pl.multiple_of(kc * tq, tq), tq), :]
        return [lax.dot_general(qf_ref[c, r * tr:(r + 1) * tr, :], k, _NT, preferred_element_type=F32)
                for c, r in chains]

    def scores_diag_sorted(qt):
        stage_queries(qt)
        k0 = pl.multiple_of(qt * tq, tq)
        out = []
        for c, r in chains:
            nk = (r + 1) * tr
            s = lax.dot_general(qf_ref[c, r * tr:(r + 1) * tr, :], kaug_ref[pl.ds(k0, nk), :], _NT,
                                preferred_element_type=F32)
            out.append(_pad_masked(jnp.where(_causal_mask(tr, nk, r * tr), s, NEG), tq))
        return out

    batch = pl.program_id(0)

    def trips_ordered(qt):
        return jnp.where((ord_ref[batch, qt] & 1) != 0, qt, 0)

    def tile_sorted(qt):
        return (ord_ref[batch, qt] >> 1) & 1

    def values(ci, kc):
        return vaug_ref[pl.ds(pl.multiple_of(kc * tq, tq), tq), :]

    def finalize(qt):
        for r in range(rs):
            rows = pl.ds(pl.multiple_of(qt * tq + r * tr, tr), tr)
            a1 = acc_ref[chains.index((0, r))]
            a2 = acc_ref[chains.index((1, r))]
            o = a1[:, :DIFF_V] / a1[:, DIFF_V:] - lam * (a2[:, :DIFF_V] / a2[:, DIFF_V:])
            ms_o = jnp.mean(o * o, axis=-1, keepdims=True)
            o = o * lax.rsqrt(ms_o + EPS) * gs_ref[...] * (1.0 - lambda_init)
            o_ref[rows, :] = (o * g_ref[rows, :].astype(F32)).astype(BF16)

    _flash_pipeline(len(chains), nq,
                    [(tile_sorted, scores_diag_sorted), (lambda qt: 1 - tile_sorted(qt), scores_diag)],
                    [(trips_ordered, scores_ordered),
                     (lambda qt: qt - trips_ordered(qt), lambda qt, kc: scores(qt, kc, False))],
                    values, finalize, s_ref, acc_ref, mpart_ref, macc_ref, sd_ref, mpd_ref)


def _diff_attn(rest, pos_col, pos_row, pos_lanes, ordered, slopes, lam_par, g_subln, lambda_init,
               nb, s):
    tq = ATTN_TQ
    nq = s // tq
    rs = DIFF_ROW_SPLIT
    tr = tq // rs
    hw = DIFF_V
    return pl.pallas_call(
        functools.partial(_diff_attn_kernel, tq=tq, rs=rs, lambda_init=lambda_init),
        grid=(nb, DIFF_HEADS),
        in_specs=[pl.BlockSpec(memory_space=pltpu.SMEM),
                  pl.BlockSpec((s, hw), lambda b, hd: (b, QD_OFF // hw + hd)),
                  pl.BlockSpec((s, hw), lambda b, hd: (b, KD_OFF // hw + hd)),
                  pl.BlockSpec((s, hw), lambda b, hd: (b, VD_OFF // hw + hd)),
                  pl.BlockSpec((s, hw), lambda b, hd: (b, GD_OFF // hw + hd)),
                  pl.BlockSpec((s, 1), lambda b, hd: (b, 0)),
                  pl.BlockSpec((1, nq, tq), lambda b, hd: (b, 0, 0)),
                  pl.BlockSpec((s, LANES), lambda b, hd: (b, 0)),
                  pl.BlockSpec((1, 1, LANES), lambda b, hd: (hd, 0, 0)),
                  pl.BlockSpec((4, DIFF_QK), lambda b, hd: (0, 0)),
                  pl.BlockSpec((1, DIFF_V), lambda b, hd: (0, 0))],
        out_specs=pl.BlockSpec((s, hw), lambda b, hd: (b, hd)),
        out_shape=jax.ShapeDtypeStruct((nb * s, DIFF_HEADS * DIFF_V), BF16),
        scratch_shapes=[pltpu.VMEM((s, 2 * DIFF_V), BF16),
                        pltpu.VMEM((s, 4 * DIFF_QK), BF16),
                        pltpu.VMEM((2 * rs, tr, 2 * DIFF_V), F32),
                        pltpu.VMEM((2, tq, 4 * DIFF_QK), BF16),
                        pltpu.VMEM((2 * rs, tr, tq), F32),
                        pltpu.VMEM((2 * rs, tr, LANES), F32),
                        pltpu.VMEM((2 * rs, tr, LANES), F32),
                        pltpu.VMEM((2 * rs, tr, tq), F32),
                        pltpu.VMEM((2 * rs, tr, LANES), F32)],
        compiler_params=_cparams(("arbitrary", "arbitrary")),
        name="diff_attn",
    )(ordered, rest, rest, rest, rest, pos_col, pos_row, pos_lanes, slopes, lam_par, g_subln)


def _cast_kernel(w1_ref, w2_ref, w3_ref, o1_ref, o2_ref, o3_ref):
    o1_ref[...] = w1_ref[...].astype(BF16)
    o2_ref[...] = w2_ref[...].astype(BF16)
    o3_ref[...] = w3_ref[...].astype(BF16)


def _cast_weights(w1, w2, w3):
    steps = CAST_STEPS
    specs = [pl.BlockSpec((w.shape[0] // steps, w.shape[1]), lambda i: (i, 0)) for w in (w1, w2, w3)]
    return pl.pallas_call(
        _cast_kernel,
        grid=(steps,),
        in_specs=specs,
        out_specs=specs,
        out_shape=[jax.ShapeDtypeStruct(w.shape, BF16) for w in (w1, w2, w3)],
        compiler_params=_cparams(("arbitrary",)),
        name="cast_weights",
    )(w1, w2, w3)


def _merge_out_kernel(a1_ref, a2_ref, s1a_ref, s1b_ref, s2a_ref, s2b_ref, w1_ref, w2_ref, wo_ref,
                      x_ref, ada_ref, g_ref, o_ref):
    half = s1a_ref.shape[1]
    rb = a1_ref.shape[0] // EPILOGUE_ROW_SPLIT
    for t in range(EPILOGUE_ROW_SPLIT):
        rows = slice(t * rb, (t + 1) * rb)
        y1 = jnp.dot(a1_ref[rows, :], w1_ref[...], preferred_element_type=F32)
        y2 = jnp.dot(a2_ref[rows, :], w2_ref[...], preferred_element_type=F32)
        merged = jnp.concatenate(
            [s1a_ref[rows, :].astype(F32) * y1[:, :half] + s2a_ref[rows, :].astype(F32) * y2[:, :half],
             s1b_ref[rows, :].astype(F32) * y1[:, half:] + s2b_ref[rows, :].astype(F32) * y2[:, half:]],
            axis=1).astype(BF16)
        y = jnp.dot(merged, wo_ref[...], preferred_element_type=F32)
        ms = jnp.mean(y * y, axis=-1, keepdims=True)
        yn = y * lax.rsqrt(ms + EPS) * g_ref[...]
        o_ref[0, rows, :] = x_ref[0, rows, :] + ada_ref[0, 2:3, :] * yn


def _merge_out(og_mla, og_diff, rest, w1b, w2b, wob, x, ada3, g_post):
    nb, s, d = x.shape
    k_dim = og_mla.shape[1]
    tm = OUT_TM
    ns = s // tm
    half = d // 2
    row = lambda b, i: (b * ns + i, 0)
    gate = lambda blk: pl.BlockSpec((tm, half), lambda b, i: (b * ns + i, blk))
    resident = lambda shape: pl.BlockSpec(shape, lambda b, i: (0, 0), pipeline_mode=pl.Buffered(1))
    return pl.pallas_call(
        _merge_out_kernel,
        grid=(nb, ns),
        in_specs=[pl.BlockSpec((tm, k_dim), row),
                  pl.BlockSpec((tm, k_dim), row),
                  gate(MGM_OFF // half), gate(MGM_OFF // half + 1),
                  gate(MGD_OFF // half), gate(MGD_OFF // half + 1),
                  resident((k_dim, d)), resident((k_dim, d)), resident((d, d)),
                  pl.BlockSpec((1, tm, d), lambda b, i: (b, i, 0)),
                  pl.BlockSpec((1, 3, d), lambda b, i: (b, 0, 0)),
                  pl.BlockSpec((1, d), lambda b, i: (0, 0))],
        out_specs=pl.BlockSpec((1, tm, d), lambda b, i: (b, i, 0)),
        out_shape=jax.ShapeDtypeStruct((nb, s, d), F32),
        compiler_params=_cparams(("arbitrary", "arbitrary")),
        name="merge_out",
    )(og_mla, og_diff, rest, rest, rest, rest, w1b, w2b, wob, x, ada3, g_post)


def kernel(x, c, positions, w_ada, b_ada, g_pre, w_in, g_kv, w_ukv, lambda_q1, lambda_k1,
           lambda_q2, lambda_k2, g_subln, w_o_mla, w_o_diff, w_out, g_post):
    nb, s, d = x.shape
    depth = w_in.shape[0]
    assert w_in.shape[1:] == (d, REST_OFF + REST_W) and w_out.shape[1:] == (d, d)
    assert s % ATTN_TQ == 0 and s % MLA_TQ == 0 and (nb * s) % PROJ_TM == 0
    half = MLA_ROPE // 2
    inv = ROPE_THETA ** (-jnp.arange(half, dtype=F32) / half)
    inv_tab = jnp.tile(inv, LANES // half).reshape(1, LANES)
    slopes = 2.0 ** (-8.0 * jnp.arange(1, DIFF_HEADS + 1, dtype=F32) / DIFF_HEADS)
    slopes = jnp.broadcast_to(slopes.reshape(DIFF_HEADS, 1, 1), (DIFF_HEADS, 1, LANES))
    pos_col = positions.reshape(nb * s, 1)
    pos_colf = pos_col.astype(F32)
    pos_chunks = positions.reshape(nb, s // ATTN_TQ, ATTN_TQ)
    pos_row = pos_chunks.astype(F32)
    run_max = lax.cummax(pos_chunks.max(axis=-1), axis=1)
    prev_max = jnp.concatenate(
        [jnp.full((nb, 1), jnp.iinfo(jnp.int32).min, jnp.int32), run_max[:, :-1]], axis=1)
    tile_sorted = jnp.all(pos_chunks[..., 1:] >= pos_chunks[..., :-1], axis=-1)
    foldable = jnp.max(jnp.abs(pos_chunks), axis=(1, 2), keepdims=False)[:, None] < MAX_FOLD_POS
    earlier_ordered = prev_max <= pos_chunks.min(axis=-1)
    ordered = (jnp.logical_and(foldable, earlier_ordered).astype(jnp.int32)
               + 2 * jnp.logical_and(foldable, tile_sorted).astype(jnp.int32))

    for l in range(depth):
        ada3 = _ada(c, w_ada[l], b_ada[l]).reshape(nb, 3, d)
        h, cos_tab, sin_tab, pos_lanes = _norm(x, ada3, g_pre[l].reshape(1, d), pos_col, inv_tab)
        w_t = jnp.swapaxes(w_in[l], 0, 1)
        qm, rest = _proj(h, w_t, cos_tab, sin_tab)
        kk, vv = _kv(h, w_t, g_kv[l].reshape(1, KV_RANK), w_ukv[l], cos_tab, sin_tab)
        og_mla = _mla_attn(qm, kk, vv, rest, nb, s)
        lam_par = jnp.stack([lambda_q1[l], lambda_k1[l], lambda_q2[l], lambda_k2[l]]).astype(F32)
        og_diff = _diff_attn(rest, pos_colf, pos_row, pos_lanes, ordered, slopes, lam_par,
                             g_subln[l].reshape(1, DIFF_V), 0.8 - 0.6 * math.exp(-0.3 * l), nb, s)
        w1b, w2b, wob = _cast_weights(w_o_mla[l], w_o_diff[l], w_out[l])
        x = _merge_out(og_mla, og_diff, rest, w1b, w2b, wob, x, ada3, g_post[l].reshape(1, d))
    return x
```

```python
import functools
import math

import jax
import jax.numpy as jnp
from jax import lax
from jax.experimental import pallas as pl
from jax.experimental.pallas import tpu as pltpu

F32 = jnp.float32
BF16 = jnp.bfloat16

MLA_HEADS = 8
MLA_NOPE = 128
MLA_ROPE = 64
MLA_V = 128
KV_RANK = 512
ROPE_THETA = 10000.0
DIFF_HEADS = 8
DIFF_QK = 64
DIFF_V = 128
EPS = 1e-6
NEG = -1e30

LANES = 128
SUBLANES = 8
VMEM_LIMIT = 56 * 1024 * 1024
ADA_TN = 1024
NORM_TS = 512
MAX_FOLD_POS = 1 << 16
N_PIECES = 3
N_PAIR_LANES = N_PIECES * N_PIECES
ATTN_TQ = 1024
DIFF_ROW_SPLIT = 2
MLA_TQ = 1024
MLA_ROW_SPLIT = 2
PROJ_TM = 1024
REST_TN = 1024
QMLA_PAIRS_PER_TILE = 2
KV_TM = 1024
CAST_STEPS = 4
OUT_TM = 512
EPILOGUE_ROW_SPLIT = 2
LOG2E = math.log2(math.e)

Q_MLA_W = MLA_HEADS * (MLA_NOPE + MLA_ROPE)
KVR_W = KV_RANK + MLA_ROPE
KVR_PAD = 640
REST_OFF = Q_MLA_W + KVR_W
QD_OFF, KD_OFF, VD_OFF, GM_OFF, GD_OFF, MGM_OFF, MGD_OFF = 0, 1024, 2048, 3072, 4096, 5120, 7168
REST_W = 9216


def _cparams(sem):
    return pltpu.CompilerParams(dimension_semantics=sem, vmem_limit_bytes=VMEM_LIMIT)


def _ada_kernel(cb_ref, w_ref, b_ref, o_ref):
    k_dim, tn = w_ref.shape
    nb = cb_ref.shape[0]
    nchunk = tn // LANES

    def body(i, accs):
        k0 = pl.multiple_of(i * SUBLANES, SUBLANES)
        out = []
        for b in range(nb):
            cv = cb_ref[b, pl.ds(k0, SUBLANES), :]
            for j in range(nchunk):
                wv = w_ref[pl.ds(k0, SUBLANES), j * LANES:(j + 1) * LANES]
                out.append(accs[b * nchunk + j] + wv * cv)
        return tuple(out)

    init = tuple(jnp.zeros((SUBLANES, LANES), F32) for _ in range(nb * nchunk))
    accs = lax.fori_loop(0, k_dim // SUBLANES, body, init, unroll=8)
    for b in range(nb):
        row = jnp.concatenate(
            [jnp.sum(accs[b * nchunk + j], axis=0, keepdims=True) for j in range(nchunk)], axis=1)
        o_ref[b:b + 1, :] = row + b_ref[...]


def _ada(c, w, bias):
    nb, k_dim = c.shape
    n = w.shape[1]
    tn = ADA_TN
    cb = jnp.broadcast_to(c[:, :, None], (nb, k_dim, LANES))
    return pl.pallas_call(
        _ada_kernel,
        grid=(n // tn,),
        in_specs=[pl.BlockSpec((nb, k_dim, LANES), lambda j: (0, 0, 0)),
                  pl.BlockSpec((k_dim, tn), lambda j: (0, j)),
                  pl.BlockSpec((1, tn), lambda j: (0, j))],
        out_specs=pl.BlockSpec((nb, tn), lambda j: (0, j)),
        out_shape=jax.ShapeDtypeStruct((nb, n), F32),
        compiler_params=_cparams(("arbitrary",)),
        name="ada",
    )(cb, w, bias.reshape(1, n))


def _bf16_pieces(x):
    p1 = x.astype(BF16)
    r = x - p1.astype(F32)
    p2 = r.astype(BF16)
    return p1.astype(F32), p2.astype(F32), r - p2.astype(F32)


def _pick3(sel, x3):
    return jnp.where(sel == 0, x3[0], jnp.where(sel == 1, x3[1], x3[2]))


def _norm_kernel(x_ref, ada_ref, g_ref, pos_ref, inv_ref, h_ref, cos_ref, sin_ref, plane_ref):
    x = x_ref[0]
    ms = jnp.mean(x * x, axis=-1, keepdims=True)
    y = x * lax.rsqrt(ms + EPS) * g_ref[...]
    shift = ada_ref[0, 0:1, :]
    scale = ada_ref[0, 1:2, :]
    h_ref[...] = (y * (1.0 + scale) + shift).astype(BF16)
    ang = pos_ref[...].astype(F32) * inv_ref[...]
    lane = lax.broadcasted_iota(jnp.int32, ang.shape, 1)
    sign = jnp.where((lane % MLA_ROPE) < MLA_ROPE // 2, -1.0, 1.0).astype(F32)
    cos_ref[...] = jnp.cos(ang)
    sin_ref[...] = jnp.sin(ang) * sign
    pos3 = _bf16_pieces(pos_ref[...].astype(F32))
    piece = _pick3(lane % N_PIECES, pos3)
    plane_ref[...] = jnp.where(lane < N_PAIR_LANES, piece,
                               jnp.where(lane < 2 * N_PAIR_LANES, -piece, 0.0)).astype(BF16)


def _norm(x, ada3, g_pre, pos_col, inv_tab):
    nb, s, d = x.shape
    ts = NORM_TS
    ns = s // ts
    row = lambda b, i: (b * ns + i, 0)
    return pl.pallas_call(
        _norm_kernel,
        grid=(nb, ns),
        in_specs=[pl.BlockSpec((1, ts, d), lambda b, i: (b, i, 0)),
                  pl.BlockSpec((1, 3, d), lambda b, i: (b, 0, 0)),
                  pl.BlockSpec((1, d), lambda b, i: (0, 0)),
                  pl.BlockSpec((ts, 1), row),
                  pl.BlockSpec((1, LANES), lambda b, i: (0, 0))],
        out_specs=[pl.BlockSpec((ts, d), row),
                   pl.BlockSpec((ts, LANES), row),
                   pl.BlockSpec((ts, LANES), row),
                   pl.BlockSpec((ts, LANES), row)],
        out_shape=[jax.ShapeDtypeStruct((nb * s, d), BF16),
                   jax.ShapeDtypeStruct((nb * s, LANES), F32),
                   jax.ShapeDtypeStruct((nb * s, LANES), F32),
                   jax.ShapeDtypeStruct((nb * s, LANES), BF16)],
        compiler_params=_cparams(("arbitrary", "arbitrary")),
        name="prenorm",
    )(x, ada3, g_pre, pos_col, inv_tab)


def _rope_cols(r, cos, sin_signed):
    lane = lax.broadcasted_iota(jnp.int32, r.shape, 1)
    half = MLA_ROPE // 2
    partner = jnp.where((lane % MLA_ROPE) < half,
                        pltpu.roll(r, LANES - half, 1), pltpu.roll(r, half, 1))
    return r * cos + partner * sin_signed


def _qmla_kernel(a_ref, w_ref, cos_ref, sin_ref, o_ref, wb_ref, *, scale):
    hd = MLA_NOPE + MLA_ROPE
    pair_w = 2 * hd
    n_pairs = wb_ref.shape[0] // pair_w

    @pl.when(pl.program_id(1) == 0)
    def _():
        for p in range(n_pairs):
            src, dst = w_ref.at[p * pair_w:(p + 1) * pair_w], wb_ref.at[p * pair_w:(p + 1) * pair_w]
            dst[:MLA_NOPE] = src[:MLA_NOPE].astype(BF16)
            dst[MLA_NOPE:2 * MLA_NOPE] = src[hd:hd + MLA_NOPE].astype(BF16)
            dst[2 * MLA_NOPE:2 * MLA_NOPE + MLA_ROPE] = src[MLA_NOPE:hd].astype(BF16)
            dst[2 * MLA_NOPE + MLA_ROPE:] = src[hd + MLA_NOPE:].astype(BF16)

    acc = lax.dot_general(a_ref[...], wb_ref[...], _NT, preferred_element_type=F32)
    for p in range(n_pairs):
        c0 = p * pair_w
        rr = _rope_cols(acc[:, c0 + 2 * MLA_NOPE:c0 + pair_w], cos_ref[...], sin_ref[...])
        o_ref[:, c0:c0 + 2 * MLA_NOPE] = (acc[:, c0:c0 + 2 * MLA_NOPE] * scale).astype(BF16)
        o_ref[:, c0 + 2 * MLA_NOPE:c0 + pair_w] = (rr * scale).astype(BF16)


def _sigmoid(x):
    return 0.5 * jnp.tanh(0.5 * x) + 0.5


def _rest_kernel(a_ref, w_ref, o_ref, wb_ref, *, tn):
    j = pl.program_id(0)

    @pl.when(pl.program_id(1) == 0)
    def _():
        wb_ref[...] = w_ref[...].astype(BF16)

    def tile(epilogue):
        acc = lax.dot_general(a_ref[...], wb_ref[...], _NT, preferred_element_type=F32)
        o_ref[...] = epilogue(acc).astype(BF16)

    @pl.when(j < KD_OFF // tn)
    def _():
        tile(lambda acc: acc * (DIFF_QK ** -0.5 * LOG2E))

    @pl.when(jnp.logical_and(j >= KD_OFF // tn, j < GM_OFF // tn))
    def _():
        tile(lambda acc: acc)

    @pl.when(jnp.logical_and(j >= GM_OFF // tn, j < MGM_OFF // tn))
    def _():
        tile(lambda acc: acc * _sigmoid(acc))

    @pl.when(j >= MGM_OFF // tn)
    def _():
        tile(_sigmoid)


def _proj(h, w_t, cos_tab, sin_tab):
    m, k_dim = h.shape
    tm = PROJ_TM
    pair_w = 2 * (MLA_NOPE + MLA_ROPE)
    q_scale = (MLA_NOPE + MLA_ROPE) ** -0.5 * LOG2E

    def row_window(rows, offset_fn):
        return pl.BlockSpec((pl.Element(rows), pl.Element(k_dim)),
                            lambda *g: (pl.multiple_of(offset_fn(*g), SUBLANES), 0))

    qtn = QMLA_PAIRS_PER_TILE * pair_w
    qm = pl.pallas_call(
        functools.partial(_qmla_kernel, scale=q_scale),
        grid=(Q_MLA_W // qtn, m // tm),
        in_specs=[pl.BlockSpec((tm, k_dim), lambda j, i: (i, 0)),
                  pl.BlockSpec((qtn, k_dim), lambda j, i: (j, 0)),
                  pl.BlockSpec((tm, LANES), lambda j, i: (i, 0)),
                  pl.BlockSpec((tm, LANES), lambda j, i: (i, 0))],
        out_specs=pl.BlockSpec((tm, qtn), lambda j, i: (i, j)),
        out_shape=jax.ShapeDtypeStruct((m, Q_MLA_W), BF16),
        scratch_shapes=[pltpu.VMEM((qtn, k_dim), BF16)],
        compiler_params=_cparams(("arbitrary", "arbitrary")),
        name="proj_qmla",
    )(h, w_t, cos_tab, sin_tab)
    tn = REST_TN
    rest = pl.pallas_call(
        functools.partial(_rest_kernel, tn=tn),
        grid=(REST_W // tn, m // tm),
        in_specs=[pl.BlockSpec((tm, k_dim), lambda j, i: (i, 0)),
                  row_window(tn, lambda j, i: REST_OFF + tn * j)],
        out_specs=pl.BlockSpec((tm, tn), lambda j, i: (i, j)),
        out_shape=jax.ShapeDtypeStruct((m, REST_W), BF16),
        scratch_shapes=[pltpu.VMEM((tn, k_dim), BF16)],
        compiler_params=_cparams(("arbitrary", "arbitrary")),
        name="proj_rest",
    )(h, w_t)
    return qm, rest


def _kv_kernel(a_ref, wp_ref, g_ref, w_ref, cos_ref, sin_ref, k_ref, v_ref, wpb_ref, wb_ref):
    @pl.when(pl.program_id(0) == 0)
    def _():
        wpb_ref[:KVR_W] = wp_ref[...].astype(BF16)
        wpb_ref[KVR_W:] = jnp.zeros((KVR_PAD - KVR_W, wpb_ref.shape[1]), BF16)
        wb_ref[...] = w_ref[...].astype(BF16)

    kw = MLA_NOPE + MLA_V
    rb = a_ref.shape[0] // EPILOGUE_ROW_SPLIT
    for t in range(EPILOGUE_ROW_SPLIT):
        rows = slice(t * rb, (t + 1) * rb)
        p = lax.dot_general(a_ref[rows, :], wpb_ref[...], _NT, preferred_element_type=F32)
        ckv = p[:, :KV_RANK]
        ms = jnp.mean(ckv * ckv, axis=-1, keepdims=True)
        n = (ckv * lax.rsqrt(ms + EPS) * g_ref[...]).astype(BF16)
        kv = jnp.dot(n, wb_ref[...], preferred_element_type=F32)
        kr_even = _rope_cols(p[:, KV_RANK:], cos_ref[rows, :], sin_ref[rows, :])
        kr_odd = pltpu.roll(kr_even, MLA_ROPE, 1)
        ones_col = jnp.ones(kr_even.shape, BF16)
        for hd in range(MLA_HEADS):
            k_ref[rows, hd * kw:hd * kw + MLA_NOPE] = kv[:, hd * kw:hd * kw + MLA_NOPE].astype(BF16)
            k_ref[rows, hd * kw + MLA_NOPE:(hd + 1) * kw] = (
                kr_even if hd % 2 == 0 else kr_odd).astype(BF16)
            v_ref[rows, hd * kw:hd * kw + MLA_V] = kv[:, hd * kw + MLA_NOPE:(hd + 1) * kw].astype(BF16)
            v_ref[rows, hd * kw + MLA_V:(hd + 1) * kw] = ones_col


def _kv(h, w_t, g_kv, w_ukv, cos_tab, sin_tab):
    m, k_dim = h.shape
    tm = KV_TM
    kw = MLA_HEADS * (MLA_NOPE + MLA_V)
    return pl.pallas_call(
        _kv_kernel,
        grid=(m // tm,),
        in_specs=[pl.BlockSpec((tm, k_dim), lambda i: (i, 0)),
                  pl.BlockSpec((pl.Element(KVR_W), pl.Element(k_dim)),
                               lambda i: (pl.multiple_of(Q_MLA_W + 0 * i, SUBLANES), 0)),
                  pl.BlockSpec((1, KV_RANK), lambda i: (0, 0)),
                  pl.BlockSpec((KV_RANK, kw), lambda i: (0, 0)),
                  pl.BlockSpec((tm, LANES), lambda i: (i, 0)),
                  pl.BlockSpec((tm, LANES), lambda i: (i, 0))],
        out_specs=[pl.BlockSpec((tm, kw), lambda i: (i, 0)),
                   pl.BlockSpec((tm, kw), lambda i: (i, 0))],
        out_shape=[jax.ShapeDtypeStruct((m, kw), BF16),
                   jax.ShapeDtypeStruct((m, kw), BF16)],
        scratch_shapes=[pltpu.VMEM((KVR_PAD, k_dim), BF16), pltpu.VMEM((KV_RANK, kw), BF16)],
        compiler_params=_cparams(("arbitrary",)),
        name="kv_up",
    )(h, w_t, g_kv, w_ukv, cos_tab, sin_tab)


def _flash_pipeline(n_chains, nq, diags, loops, value_fn, finalize_fn,
                    s_ref, acc_ref, mpart_ref, macc_ref, sd_ref=None, mpd_ref=None):
    assert sd_ref is None or nq % 2 == 0
    chunk = s_ref.shape[2]

    def lane_tiles(x, n):
        return jnp.concatenate([x] * n, axis=1)

    def qk_phase(scores, stage):
        s_dst, m_dst = stage
        for ci, s in enumerate(scores):
            s_dst[ci] = s
            part = s[:, :LANES]
            for j in range(1, chunk // LANES):
                part = jnp.maximum(part, s[:, j * LANES:(j + 1) * LANES])
            m_dst[ci] = part

    def pv_phase(kc, stage):
        s_src, m_src = stage
        for ci in range(n_chains):
            m_acc = macc_ref[ci]
            m_run = jnp.maximum(m_acc, jnp.max(m_src[ci], axis=-1, keepdims=True))
            macc_ref[ci] = m_run
            p = jnp.exp2(s_src[ci] - lane_tiles(m_run, chunk // LANES))
            alpha = jnp.exp2(m_acc - m_run)
            pv = jnp.dot(p.astype(BF16), value_fn(ci, kc), preferred_element_type=F32)
            acc_ref[ci] = lane_tiles(alpha, acc_ref.shape[2] // LANES) * acc_ref[ci] + pv

    def reset():
        acc_ref[...] = jnp.zeros_like(acc_ref)
        macc_ref[...] = jnp.full(macc_ref.shape, NEG, F32)

    def tile(qt, stage, next_stage):
        cur = qt
        for trips_fn, score_fn in loops:
            def step(kc, cur, score_fn=score_fn):
                pv_phase(cur, stage)
                qk_phase(score_fn(qt, kc), stage)
                return kc

            cur = lax.fori_loop(0, trips_fn(qt), step, cur)
        nxt = jnp.minimum(qt + 1, nq - 1)

        def transition(_, cur, fn):
            if next_stage is stage:
                pv_phase(cur, stage)
                finalize_fn(qt)
                reset()
                qk_phase(fn(nxt), stage)
            else:
                qk_phase(fn(nxt), next_stage)
                pv_phase(cur, stage)
                finalize_fn(qt)
                reset()
            return cur

        for select_fn, fn in diags:
            if select_fn is None:
                transition(0, cur, fn)
            else:
                lax.fori_loop(0, select_fn(nxt), functools.partial(transition, fn=fn), cur)

    stage_a = (s_ref, mpart_ref)
    reset()
    for select_fn, fn in diags:
        if select_fn is None:
            qk_phase(fn(0), stage_a)
        else:
            lax.fori_loop(0, select_fn(0), lambda _, c, fn=fn: qk_phase(fn(0), stage_a) or c, 0)
    if sd_ref is None:
        lax.fori_loop(0, nq, lambda qt, c: tile(qt, stage_a, stage_a) or c, 0)
    else:
        stage_b = (sd_ref, mpd_ref)

        def tile_pair(j, c):
            tile(2 * j, stage_a, stage_b)
            tile(2 * j + 1, stage_b, stage_a)
            return c

        lax.fori_loop(0, nq // 2, tile_pair, 0)


def _causal_mask(rows, cols, row0):
    row = lax.broadcasted_iota(jnp.int32, (rows, cols), 0) + row0
    col = lax.broadcasted_iota(jnp.int32, (rows, cols), 1)
    return col <= row


def _pad_masked(s, cols):
    if s.shape[1] == cols:
        return s
    return jnp.concatenate([s, jnp.full((s.shape[0], cols - s.shape[1]), NEG, s.dtype)], axis=1)


_NT = (((1,), (1,)), ((), ()))


def _mla_attn_kernel(q_ref, k_ref, v_ref, g_ref, o_ref, acc_ref, qs_ref, s_ref, mpart_ref, macc_ref,
                     *, tq, rs):
    kw = 2 * MLA_NOPE
    tr = tq // rs
    nq = q_ref.shape[0] // tq
    chains = [(hp, r) for hp in range(2) for r in range(rs)]

    def stage_queries(qt):
        rows = pl.ds(pl.multiple_of(qt * tq, tq), tq)
        for hp in range(2):
            qs_ref[hp, :, :MLA_NOPE] = q_ref[rows, hp * MLA_NOPE:(hp + 1) * MLA_NOPE]
            qs_ref[hp, :, MLA_NOPE:] = q_ref[rows, 2 * MLA_NOPE:]

    def scores(qt, kc, diag):
        if diag:
            stage_queries(qt)
        k0 = pl.multiple_of(kc * tq, tq)
        out = []
        for hp, r in chains:
            nk = (r + 1) * tr if diag else tq
            q = qs_ref[hp, r * tr:(r + 1) * tr, :]
            k = k_ref[pl.ds(k0, nk), hp * kw:(hp + 1) * kw]
            s = lax.dot_general(q, k, _NT, preferred_element_type=F32)
            if diag:
                s = _pad_masked(jnp.where(_causal_mask(tr, nk, r * tr), s, NEG), tq)
            out.append(s)
        return out

    def values(ci, kc):
        hp = chains[ci][0]
        return v_ref[pl.ds(pl.multiple_of(kc * tq, tq), tq), hp * kw:(hp + 1) * kw]

    def finalize(qt):
        for ci, (hp, r) in enumerate(chains):
            rows = pl.ds(pl.multiple_of(qt * tq + r * tr, tr), tr)
            acc = acc_ref[ci]
            o = acc[:, :MLA_V] / acc[:, MLA_V:]
            gate = g_ref[rows, hp * MLA_V:(hp + 1) * MLA_V].astype(F32)
            o_ref[rows, hp * MLA_V:(hp + 1) * MLA_V] = (o * gate).astype(BF16)

    _flash_pipeline(len(chains), nq, [(None, lambda qt: scores(qt, qt, True))],
                    [(lambda qt: qt, lambda qt, kc: scores(qt, kc, False))],
                    values, finalize, s_ref, acc_ref, mpart_ref, macc_ref)


def _mla_attn(qm, kk, vv, rest, nb, s):
    tq = MLA_TQ
    pair_w = 2 * (MLA_NOPE + MLA_ROPE)
    kw = 4 * MLA_NOPE
    gate_blk = GM_OFF // (2 * MLA_V)
    rs = MLA_ROW_SPLIT
    return pl.pallas_call(
        functools.partial(_mla_attn_kernel, tq=tq, rs=rs),
        grid=(nb, MLA_HEADS // 2),
        in_specs=[pl.BlockSpec((s, pair_w), lambda b, hh: (b, hh)),
                  pl.BlockSpec((s, kw), lambda b, hh: (b, hh)),
                  pl.BlockSpec((s, kw), lambda b, hh: (b, hh)),
                  pl.BlockSpec((s, 2 * MLA_V), lambda b, hh: (b, gate_blk + hh))],
        out_specs=pl.BlockSpec((s, 2 * MLA_V), lambda b, hh: (b, hh)),
        out_shape=jax.ShapeDtypeStruct((nb * s, MLA_HEADS * MLA_V), BF16),
        scratch_shapes=[pltpu.VMEM((2 * rs, tq // rs, 2 * MLA_V), F32),
                        pltpu.VMEM((2, tq, 2 * MLA_NOPE), BF16),
                        pltpu.VMEM((2 * rs, tq // rs, tq), F32),
                        pltpu.VMEM((2 * rs, tq // rs, LANES), F32),
                        pltpu.VMEM((2 * rs, tq // rs, LANES), F32)],
        compiler_params=_cparams(("arbitrary", "arbitrary")),
        name="mla_attn",
    )(qm, kk, vv, rest)


def _diff_attn_kernel(ord_ref, q_ref, k_ref, v_ref, g_ref, pq_ref, pk_ref, plane_ref, sl_ref, lam_ref,
                      gs_ref, o_ref, vaug_ref, kaug_ref, acc_ref, qf_ref, s_ref, mpart_ref, macc_ref,
                      sd_ref, mpd_ref, *, tq, rs, lambda_init):
    seq = q_ref.shape[0]
    nq = seq // tq
    tr = tq // rs
    chains = [(c, r) for c in range(2) for r in range(rs)]
    lane = lax.broadcasted_iota(jnp.int32, (tq, LANES), 1)
    ones_col = jnp.ones((tq, LANES), BF16)
    slope2 = sl_ref[0, :, 0:1] * LOG2E
    c_pieces = _bf16_pieces(slope2)
    lane_row = lane[:1]
    c_query = jnp.where(lane_row < N_PAIR_LANES, _pick3(lane_row // N_PIECES, c_pieces), 0.0).astype(BF16)
    c_key = jnp.where(jnp.logical_and(lane_row >= N_PAIR_LANES, lane_row < 2 * N_PAIR_LANES),
                      _pick3((lane_row - N_PAIR_LANES) // N_PIECES, c_pieces), 0.0).astype(BF16)

    def bias_lanes(rows, key_side):
        pos_lanes = plane_ref[rows, :]
        if key_side:
            return jnp.where(lane < N_PAIR_LANES, pos_lanes, c_key)
        return jnp.where(lane < N_PAIR_LANES, c_query, pos_lanes)

    def stage_keys(kc, _):
        rows = pl.ds(pl.multiple_of(kc * tq, tq), tq)
        vaug_ref[rows, :DIFF_V] = v_ref[rows, :]
        vaug_ref[rows, DIFF_V:] = ones_col
        kaug_ref[rows, :2 * DIFF_QK] = k_ref[rows, :]
        kaug_ref[rows, 2 * DIFF_QK:] = bias_lanes(rows, True)
        return 0

    lax.fori_loop(0, nq, stage_keys, 0)

    def stage_queries(qt):
        rows = pl.ds(pl.multiple_of(qt * tq, tq), tq)
        q = q_ref[rows, :]
        q_side = bias_lanes(rows, False)
        qf_ref[0, :, :2 * DIFF_QK] = jnp.where(lane < DIFF_QK, q, 0).astype(BF16)
        qf_ref[1, :, :2 * DIFF_QK] = jnp.where(lane >= DIFF_QK, q, 0).astype(BF16)
        qf_ref[0, :, 2 * DIFF_QK:] = q_side
        qf_ref[1, :, 2 * DIFF_QK:] = q_side

    lq = lam_ref[...]
    lam = (jnp.exp(jnp.sum(lq[0:1] * lq[1:2], axis=-1, keepdims=True))
           - jnp.exp(jnp.sum(lq[2:3] * lq[3:4], axis=-1, keepdims=True)) + lambda_init)

    def scores(qt, kc, diag):
        k0 = pl.multiple_of(kc * tq, tq)
        k = k_ref[pl.ds(k0, tq), :]
        pk = pk_ref[0, pl.ds(kc, 1), :]
        out = [None] * len(chains)
        for r in range(rs):
            pq = pq_ref[pl.ds(pl.multiple_of(qt * tq + r * tr, tr), tr), :]
            bias = slope2 * jnp.abs(pq - pk)
            mask = _causal_mask(tr, tq, r * tr) if diag else None
            for c in range(2):
                q = qf_ref[c, r * tr:(r + 1) * tr, :2 * DIFF_QK]
                s = lax.dot_general(q, k, _NT, preferred_element_type=F32) - bias
                out[chains.index((c, r))] = jnp.where(mask, s, NEG) if diag else s
        return out

    def scores_diag(qt):
        stage_queries(qt)
        return scores(qt, qt, True)

    def scores_ordered(qt, kc):
        k = kaug_ref[pl.ds(pl.multiple_of(kc * tq, tq), tq), :]
        return [lax.dot_general(qf_ref[c, r * tr:(r + 1) * tr, :], k, _NT, preferred_element_type=F32)
                for c, r in chains]

    def scores_diag_sorted(qt):
        stage_queries(qt)
        k0 = pl.multiple_of(qt * tq, tq)
        out = []
        for c, r in chains:
            nk = (r + 1) * tr
            s = lax.dot_general(qf_ref[c, r * tr:(r + 1) * tr, :], kaug_ref[pl.ds(k0, nk), :], _NT,
                                preferred_element_type=F32)
            out.append(_pad_masked(jnp.where(_causal_mask(tr, nk, r * tr), s, NEG), tq))
        return out

    batch = pl.program_id(0)

    def trips_ordered(qt):
        return jnp.where((ord_ref[batch, qt] & 1) != 0, qt, 0)

    def tile_sorted(qt):
        return (ord_ref[batch, qt] >> 1) & 1

    def values(ci, kc):
        return vaug_ref[pl.ds(pl.multiple_of(kc * tq, tq), tq), :]

    def finalize(qt):
        for r in range(rs):
            rows = pl.ds(pl.multiple_of(qt * tq + r * tr, tr), tr)
            a1 = acc_ref[chains.index((0, r))]
            a2 = acc_ref[chains.index((1, r))]
            o = a1[:, :DIFF_V] / a1[:, DIFF_V:] - lam * (a2[:, :DIFF_V] / a2[:, DIFF_V:])
            ms_o = jnp.mean(o * o, axis=-1, keepdims=True)
            o = o * lax.rsqrt(ms_o + EPS) * gs_ref[...] * (1.0 - lambda_init)
            o_ref[rows, :] = (o * g_ref[rows, :].astype(F32)).astype(BF16)

    _flash_pipeline(len(chains), nq,
                    [(tile_sorted, scores_diag_sorted), (lambda qt: 1 - tile_sorted(qt), scores_diag)],
                    [(trips_ordered, scores_ordered),
                     (lambda qt: qt - trips_ordered(qt), lambda qt, kc: scores(qt, kc, False))],
                    values, finalize, s_ref, acc_ref, mpart_ref, macc_ref, sd_ref, mpd_ref)


def _diff_attn(rest, pos_col, pos_row, pos_lanes, ordered, slopes, lam_par, g_subln, lambda_init,
               nb, s):
    tq = ATTN_TQ
    nq = s // tq
    rs = DIFF_ROW_SPLIT
    tr = tq // rs
    hw = DIFF_V
    return pl.pallas_call(
        functools.partial(_diff_attn_kernel, tq=tq, rs=rs, lambda_init=lambda_init),
        grid=(nb, DIFF_HEADS),
        in_specs=[pl.BlockSpec(memory_space=pltpu.SMEM),
                  pl.BlockSpec((s, hw), lambda b, hd: (b, QD_OFF // hw + hd)),
                  pl.BlockSpec((s, hw), lambda b, hd: (b, KD_OFF // hw + hd)),
                  pl.BlockSpec((s, hw), lambda b, hd: (b, VD_OFF // hw + hd)),
                  pl.BlockSpec((s, hw), lambda b, hd: (b, GD_OFF // hw + hd)),
                  pl.BlockSpec((s, 1), lambda b, hd: (b, 0)),
                  pl.BlockSpec((1, nq, tq), lambda b, hd: (b, 0, 0)),
                  pl.BlockSpec((s, LANES), lambda b, hd: (b, 0)),
                  pl.BlockSpec((1, 1, LANES), lambda b, hd: (hd, 0, 0)),
                  pl.BlockSpec((4, DIFF_QK), lambda b, hd: (0, 0)),
                  pl.BlockSpec((1, DIFF_V), lambda b, hd: (0, 0))],
        out_specs=pl.BlockSpec((s, hw), lambda b, hd: (b, hd)),
        out_shape=jax.ShapeDtypeStruct((nb * s, DIFF_HEADS * DIFF_V), BF16),
        scratch_shapes=[pltpu.VMEM((s, 2 * DIFF_V), BF16),
                        pltpu.VMEM((s, 4 * DIFF_QK), BF16),
                        pltpu.VMEM((2 * rs, tr, 2 * DIFF_V), F32),
                        pltpu.VMEM((2, tq, 4 * DIFF_QK), BF16),
                        pltpu.VMEM((2 * rs, tr, tq), F32),
                        pltpu.VMEM((2 * rs, tr, LANES), F32),
                        pltpu.VMEM((2 * rs, tr, LANES), F32),
                        pltpu.VMEM((2 * rs, tr, tq), F32),
                        pltpu.VMEM((2 * rs, tr, LANES), F32)],
        compiler_params=_cparams(("arbitrary", "arbitrary")),
        name="diff_attn",
    )(ordered, rest, rest, rest, rest, pos_col, pos_row, pos_lanes, slopes, lam_par, g_subln)


def _cast_kernel(w1_ref, w2_ref, w3_ref, o1_ref, o2_ref, o3_ref):
    o1_ref[...] = w1_ref[...].astype(BF16)
    o2_ref[...] = w2_ref[...].astype(BF16)
    o3_ref[...] = w3_ref[...].astype(BF16)


def _cast_weights(w1, w2, w3):
    steps = CAST_STEPS
    specs = [pl.BlockSpec((w.shape[0] // steps, w.shape[1]), lambda i: (i, 0)) for w in (w1, w2, w3)]
    return pl.pallas_call(
        _cast_kernel,
        grid=(steps,),
        in_specs=specs,
        out_specs=specs,
        out_shape=[jax.ShapeDtypeStruct(w.shape, BF16) for w in (w1, w2, w3)],
        compiler_params=_cparams(("arbitrary",)),
        name="cast_weights",
    )(w1, w2, w3)


def _merge_out_kernel(a1_ref, a2_ref, s1a_ref, s1b_ref, s2a_ref, s2b_ref, w1_ref, w2_ref, wo_ref,
                      x_ref, ada_ref, g_ref, o_ref):
    half = s1a_ref.shape[1]
    rb = a1_ref.shape[0] // EPILOGUE_ROW_SPLIT
    for t in range(EPILOGUE_ROW_SPLIT):
        rows = slice(t * rb, (t + 1) * rb)
        y1 = jnp.dot(a1_ref[rows, :], w1_ref[...], preferred_element_type=F32)
        y2 = jnp.dot(a2_ref[rows, :], w2_ref[...], preferred_element_type=F32)
        merged = jnp.concatenate(
            [s1a_ref[rows, :].astype(F32) * y1[:, :half] + s2a_ref[rows, :].astype(F32) * y2[:, :half],
             s1b_ref[rows, :].astype(F32) * y1[:, half:] + s2b_ref[rows, :].astype(F32) * y2[:, half:]],
            axis=1).astype(BF16)
        y = jnp.dot(merged, wo_ref[...], preferred_element_type=F32)
        ms = jnp.mean(y * y, axis=-1, keepdims=True)
        yn = y * lax.rsqrt(ms + EPS) * g_ref[...]
        o_ref[0, rows, :] = x_ref[0, rows, :] + ada_ref[0, 2:3, :] * yn


def _merge_out(og_mla, og_diff, rest, w1b, w2b, wob, x, ada3, g_post):
    nb, s, d = x.shape
    k_dim = og_mla.shape[1]
    tm = OUT_TM
    ns = s // tm
    half = d // 2
    row = lambda b, i: (b * ns + i, 0)
    gate = lambda blk: pl.BlockSpec((tm, half), lambda b, i: (b * ns + i, blk))
    resident = lambda shape: pl.BlockSpec(shape, lambda b, i: (0, 0), pipeline_mode=pl.Buffered(1))
    return pl.pallas_call(
        _merge_out_kernel,
        grid=(nb, ns),
        in_specs=[pl.BlockSpec((tm, k_dim), row),
                  pl.BlockSpec((tm, k_dim), row),
                  gate(MGM_OFF // half), gate(MGM_OFF // half + 1),
                  gate(MGD_OFF // half), gate(MGD_OFF // half + 1),
                  resident((k_dim, d)), resident((k_dim, d)), resident((d, d)),
                  pl.BlockSpec((1, tm, d), lambda b, i: (b, i, 0)),
                  pl.BlockSpec((1, 3, d), lambda b, i: (b, 0, 0)),
                  pl.BlockSpec((1, d), lambda b, i: (0, 0))],
        out_specs=pl.BlockSpec((1, tm, d), lambda b, i: (b, i, 0)),
        out_shape=jax.ShapeDtypeStruct((nb, s, d), F32),
        compiler_params=_cparams(("arbitrary", "arbitrary")),
        name="merge_out",
    )(og_mla, og_diff, rest, rest, rest, rest, w1b, w2b, wob, x, ada3, g_post)


def kernel(x, c, positions, w_ada, b_ada, g_pre, w_in, g_kv, w_ukv, lambda_q1, lambda_k1,
           lambda_q2, lambda_k2, g_subln, w_o_mla, w_o_diff, w_out, g_post):
    nb, s, d = x.shape
    depth = w_in.shape[0]
    assert w_in.shape[1:] == (d, REST_OFF + REST_W) and w_out.shape[1:] == (d, d)
    assert s % ATTN_TQ == 0 and s % MLA_TQ == 0 and (nb * s) % PROJ_TM == 0
    half = MLA_ROPE // 2
    inv = ROPE_THETA ** (-jnp.arange(half, dtype=F32) / half)
    inv_tab = jnp.tile(inv, LANES // half).reshape(1, LANES)
    slopes = 2.0 ** (-8.0 * jnp.arange(1, DIFF_HEADS + 1, dtype=F32) / DIFF_HEADS)
    slopes = jnp.broadcast_to(slopes.reshape(DIFF_HEADS, 1, 1), (DIFF_HEADS, 1, LANES))
    pos_col = positions.reshape(nb * s, 1)
    pos_colf = pos_col.astype(F32)
    pos_chunks = positions.reshape(nb, s // ATTN_TQ, ATTN_TQ)
    pos_row = pos_chunks.astype(F32)
    run_max = lax.cummax(pos_chunks.max(axis=-1), axis=1)
    prev_max = jnp.concatenate(
        [jnp.full((nb, 1), jnp.iinfo(jnp.int32).min, jnp.int32), run_max[:, :-1]], axis=1)
    tile_sorted = jnp.all(pos_chunks[..., 1:] >= pos_chunks[..., :-1], axis=-1)
    foldable = jnp.max(jnp.abs(pos_chunks), axis=(1, 2), keepdims=False)[:, None] < MAX_FOLD_POS
    earlier_ordered = prev_max <= pos_chunks.min(axis=-1)
    ordered = (jnp.logical_and(foldable, earlier_ordered).astype(jnp.int32)
               + 2 * jnp.logical_and(foldable, tile_sorted).astype(jnp.int32))

    for l in range(depth):
        ada3 = _ada(c, w_ada[l], b_ada[l]).reshape(nb, 3, d)
        h, cos_tab, sin_tab, pos_lanes = _norm(x, ada3, g_pre[l].reshape(1, d), pos_col, inv_tab)
        w_t = jnp.swapaxes(w_in[l], 0, 1)
        qm, rest = _proj(h, w_t, cos_tab, sin_tab)
        kk, vv = _kv(h, w_t, g_kv[l].reshape(1, KV_RANK), w_ukv[l], cos_tab, sin_tab)
        og_mla = _mla_attn(qm, kk, vv, rest, nb, s)
        lam_par = jnp.stack([lambda_q1[l], lambda_k1[l], lambda_q2[l], lambda_k2[l]]).astype(F32)
        og_diff = _diff_attn(rest, pos_colf, pos_row, pos_lanes, ordered, slopes, lam_par,
                             g_subln[l].reshape(1, DIFF_V), 0.8 - 0.6 * math.exp(-0.3 * l), nb, s)
        w1b, w2b, wob = _cast_weights(w_o_mla[l], w_o_diff[l], w_out[l])
        x = _merge_out(og_mla, og_diff, rest, w1b, w2b, wob, x, ada3, g_post[l].reshape(1, d))
    return x
```

```python
import functools
import math

import jax
import jax.numpy as jnp
from jax import lax
from jax.experimental import pallas as pl
from jax.experimental.pallas import tpu as pltpu

F32 = jnp.float32
BF16 = jnp.bfloat16

MLA_HEADS = 8
MLA_NOPE = 128
MLA_ROPE = 64
MLA_V = 128
KV_RANK = 512
ROPE_THETA = 10000.0
DIFF_HEADS = 8
DIFF_QK = 64
DIFF_V = 128
EPS = 1e-6
NEG = -1e30

LANES = 128
SUBLANES = 8
VMEM_LIMIT = 56 * 1024 * 1024
ADA_TN = 1024
NORM_TS = 512
MAX_FOLD_POS = 1 << 16
N_PIECES = 3
N_PAIR_LANES = N_PIECES * N_PIECES
ATTN_TQ = 1024
DIFF_ROW_SPLIT = 2
MLA_TQ = 1024
MLA_ROW_SPLIT = 2
PROJ_TM = 1024
REST_TN = 1024
QMLA_PAIRS_PER_TILE = 4
KV_TM = 1024
CAST_STEPS = 4
OUT_TM = 512
EPILOGUE_ROW_SPLIT = 2
LOG2E = math.log2(math.e)

Q_MLA_W = MLA_HEADS * (MLA_NOPE + MLA_ROPE)
KVR_W = KV_RANK + MLA_ROPE
KVR_PAD = 640
REST_OFF = Q_MLA_W + KVR_W
QD_OFF, KD_OFF, VD_OFF, GM_OFF, GD_OFF, MGM_OFF, MGD_OFF = 0, 1024, 2048, 3072, 4096, 5120, 7168
REST_W = 9216


def _cparams(sem):
    return pltpu.CompilerParams(dimension_semantics=sem, vmem_limit_bytes=VMEM_LIMIT)


def _ada_kernel(cb_ref, w_ref, b_ref, o_ref):
    k_dim, tn = w_ref.shape
    nb = cb_ref.shape[0]
    nchunk = tn // LANES

    def body(i, accs):
        k0 = pl.multiple_of(i * SUBLANES, SUBLANES)
        out = []
        for b in range(nb):
            cv = cb_ref[b, pl.ds(k0, SUBLANES), :]
            for j in range(nchunk):
                wv = w_ref[pl.ds(k0, SUBLANES), j * LANES:(j + 1) * LANES]
                out.append(accs[b * nchunk + j] + wv * cv)
        return tuple(out)

    init = tuple(jnp.zeros((SUBLANES, LANES), F32) for _ in range(nb * nchunk))
    accs = lax.fori_loop(0, k_dim // SUBLANES, body, init, unroll=8)
    for b in range(nb):
        row = jnp.concatenate(
            [jnp.sum(accs[b * nchunk + j], axis=0, keepdims=True) for j in range(nchunk)], axis=1)
        o_ref[b:b + 1, :] = row + b_ref[...]


def _ada(c, w, bias):
    nb, k_dim = c.shape
    n = w.shape[1]
    tn = ADA_TN
    cb = jnp.broadcast_to(c[:, :, None], (nb, k_dim, LANES))
    return pl.pallas_call(
        _ada_kernel,
        grid=(n // tn,),
        in_specs=[pl.BlockSpec((nb, k_dim, LANES), lambda j: (0, 0, 0)),
                  pl.BlockSpec((k_dim, tn), lambda j: (0, j)),
                  pl.BlockSpec((1, tn), lambda j: (0, j))],
        out_specs=pl.BlockSpec((nb, tn), lambda j: (0, j)),
        out_shape=jax.ShapeDtypeStruct((nb, n), F32),
        compiler_params=_cparams(("arbitrary",)),
        name="ada",
    )(cb, w, bias.reshape(1, n))


def _bf16_pieces(x):
    p1 = x.astype(BF16)
    r = x - p1.astype(F32)
    p2 = r.astype(BF16)
    return p1.astype(F32), p2.astype(F32), r - p2.astype(F32)


def _pick3(sel, x3):
    return jnp.where(sel == 0, x3[0], jnp.where(sel == 1, x3[1], x3[2]))


def _norm_kernel(x_ref, ada_ref, g_ref, pos_ref, inv_ref, h_ref, cos_ref, sin_ref, plane_ref):
    x = x_ref[0]
    ms = jnp.mean(x * x, axis=-1, keepdims=True)
    y = x * lax.rsqrt(ms + EPS) * g_ref[...]
    shift = ada_ref[0, 0:1, :]
    scale = ada_ref[0, 1:2, :]
    h_ref[...] = (y * (1.0 + scale) + shift).astype(BF16)
    ang = pos_ref[...].astype(F32) * inv_ref[...]
    lane = lax.broadcasted_iota(jnp.int32, ang.shape, 1)
    sign = jnp.where((lane % MLA_ROPE) < MLA_ROPE // 2, -1.0, 1.0).astype(F32)
    cos_ref[...] = jnp.cos(ang)
    sin_ref[...] = jnp.sin(ang) * sign
    pos3 = _bf16_pieces(pos_ref[...].astype(F32))
    piece = _pick3(lane % N_PIECES, pos3)
    plane_ref[...] = jnp.where(lane < N_PAIR_LANES, piece,
                               jnp.where(lane < 2 * N_PAIR_LANES, -piece, 0.0)).astype(BF16)


def _norm(x, ada3, g_pre, pos_col, inv_tab):
    nb, s, d = x.shape
    ts = NORM_TS
    ns = s // ts
    row = lambda b, i: (b * ns + i, 0)
    return pl.pallas_call(
        _norm_kernel,
        grid=(nb, ns),
        in_specs=[pl.BlockSpec((1, ts, d), lambda b, i: (b, i, 0)),
                  pl.BlockSpec((1, 3, d), lambda b, i: (b, 0, 0)),
                  pl.BlockSpec((1, d), lambda b, i: (0, 0)),
                  pl.BlockSpec((ts, 1), row),
                  pl.BlockSpec((1, LANES), lambda b, i: (0, 0))],
        out_specs=[pl.BlockSpec((ts, d), row),
                   pl.BlockSpec((ts, LANES), row),
                   pl.BlockSpec((ts, LANES), row),
                   pl.BlockSpec((ts, LANES), row)],
        out_shape=[jax.ShapeDtypeStruct((nb * s, d), BF16),
                   jax.ShapeDtypeStruct((nb * s, LANES), F32),
                   jax.ShapeDtypeStruct((nb * s, LANES), F32),
                   jax.ShapeDtypeStruct((nb * s, LANES), BF16)],
        compiler_params=_cparams(("arbitrary", "arbitrary")),
        name="prenorm",
    )(x, ada3, g_pre, pos_col, inv_tab)


def _rope_cols(r, cos, sin_signed):
    lane = lax.broadcasted_iota(jnp.int32, r.shape, 1)
    half = MLA_ROPE // 2
    partner = jnp.where((lane % MLA_ROPE) < half,
                        pltpu.roll(r, LANES - half, 1), pltpu.roll(r, half, 1))
    return r * cos + partner * sin_signed


def _qmla_kernel(a_ref, w_ref, cos_ref, sin_ref, o_ref, wb_ref, *, scale):
    hd = MLA_NOPE + MLA_ROPE
    pair_w = 2 * hd
    n_pairs = wb_ref.shape[0] // pair_w

    @pl.when(pl.program_id(1) == 0)
    def _():
        for p in range(n_pairs):
            src, dst = w_ref.at[p * pair_w:(p + 1) * pair_w], wb_ref.at[p * pair_w:(p + 1) * pair_w]
            dst[:MLA_NOPE] = src[:MLA_NOPE].astype(BF16)
            dst[MLA_NOPE:2 * MLA_NOPE] = src[hd:hd + MLA_NOPE].astype(BF16)
            dst[2 * MLA_NOPE:2 * MLA_NOPE + MLA_ROPE] = src[MLA_NOPE:hd].astype(BF16)
            dst[2 * MLA_NOPE + MLA_ROPE:] = src[hd + MLA_NOPE:].astype(BF16)

    acc = lax.dot_general(a_ref[...], wb_ref[...], _NT, preferred_element_type=F32)
    for p in range(n_pairs):
        c0 = p * pair_w
        rr = _rope_cols(acc[:, c0 + 2 * MLA_NOPE:c0 + pair_w], cos_ref[...], sin_ref[...])
        o_ref[:, c0:c0 + 2 * MLA_NOPE] = (acc[:, c0:c0 + 2 * MLA_NOPE] * scale).astype(BF16)
        o_ref[:, c0 + 2 * MLA_NOPE:c0 + pair_w] = (rr * scale).astype(BF16)


def _sigmoid(x):
    return 0.5 * jnp.tanh(0.5 * x) + 0.5


def _rest_kernel(a_ref, w_ref, o_ref, wb_ref, *, tn):
    j = pl.program_id(0)

    @pl.when(pl.program_id(1) == 0)
    def _():
        wb_ref[...] = w_ref[...].astype(BF16)

    def tile(epilogue):
        acc = lax.dot_general(a_ref[...], wb_ref[...], _NT, preferred_element_type=F32)
        o_ref[...] = epilogue(acc).astype(BF16)

    @pl.when(j < KD_OFF // tn)
    def _():
        tile(lambda acc: acc * (DIFF_QK ** -0.5 * LOG2E))

    @pl.when(jnp.logical_and(j >= KD_OFF // tn, j < GM_OFF // tn))
    def _():
        tile(lambda acc: acc)

    @pl.when(jnp.logical_and(j >= GM_OFF // tn, j < MGM_OFF // tn))
    def _():
        tile(lambda acc: acc * _sigmoid(acc))

    @pl.when(j >= MGM_OFF // tn)
    def _():
        tile(_sigmoid)


def _proj(h, w_t, cos_tab, sin_tab):
    m, k_dim = h.shape
    tm = PROJ_TM
    pair_w = 2 * (MLA_NOPE + MLA_ROPE)
    q_scale = (MLA_NOPE + MLA_ROPE) ** -0.5 * LOG2E

    def row_window(rows, offset_fn):
        return pl.BlockSpec((pl.Element(rows), pl.Element(k_dim)),
                            lambda *g: (pl.multiple_of(offset_fn(*g), SUBLANES), 0))

    qtn = QMLA_PAIRS_PER_TILE * pair_w
    qm = pl.pallas_call(
        functools.partial(_qmla_kernel, scale=q_scale),
        grid=(Q_MLA_W // qtn, m // tm),
        in_specs=[pl.BlockSpec((tm, k_dim), lambda j, i: (i, 0)),
                  pl.BlockSpec((qtn, k_dim), lambda j, i: (j, 0)),
                  pl.BlockSpec((tm, LANES), lambda j, i: (i, 0)),
                  pl.BlockSpec((tm, LANES), lambda j, i: (i, 0))],
        out_specs=pl.BlockSpec((tm, qtn), lambda j, i: (i, j)),
        out_shape=jax.ShapeDtypeStruct((m, Q_MLA_W), BF16),
        scratch_shapes=[pltpu.VMEM((qtn, k_dim), BF16)],
        compiler_params=_cparams(("arbitrary", "arbitrary")),
        name="proj_qmla",
    )(h, w_t, cos_tab, sin_tab)
    tn = REST_TN
    rest = pl.pallas_call(
        functools.partial(_rest_kernel, tn=tn),
        grid=(REST_W // tn, m // tm),
        in_specs=[pl.BlockSpec((tm, k_dim), lambda j, i: (i, 0)),
                  row_window(tn, lambda j, i: REST_OFF + tn * j)],
        out_specs=pl.BlockSpec((tm, tn), lambda j, i: (i, j)),
        out_shape=jax.ShapeDtypeStruct((m, REST_W), BF16),
        scratch_shapes=[pltpu.VMEM((tn, k_dim), BF16)],
        compiler_params=_cparams(("arbitrary", "arbitrary")),
        name="proj_rest",
    )(h, w_t)
    return qm, rest


def _kv_kernel(a_ref, wp_ref, g_ref, w_ref, cos_ref, sin_ref, k_ref, v_ref, wpb_ref, wb_ref):
    @pl.when(pl.program_id(0) == 0)
    def _():
        wpb_ref[:KVR_W] = wp_ref[...].astype(BF16)
        wpb_ref[KVR_W:] = jnp.zeros((KVR_PAD - KVR_W, wpb_ref.shape[1]), BF16)
        wb_ref[...] = w_ref[...].astype(BF16)

    kw = MLA_NOPE + MLA_V
    rb = a_ref.shape[0] // EPILOGUE_ROW_SPLIT
    for t in range(EPILOGUE_ROW_SPLIT):
        rows = slice(t * rb, (t + 1) * rb)
        p = lax.dot_general(a_ref[rows, :], wpb_ref[...], _NT, preferred_element_type=F32)
        ckv = p[:, :KV_RANK]
        ms = jnp.mean(ckv * ckv, axis=-1, keepdims=True)
        n = (ckv * lax.rsqrt(ms + EPS) * g_ref[...]).astype(BF16)
        kv = jnp.dot(n, wb_ref[...], preferred_element_type=F32)
        kr_even = _rope_cols(p[:, KV_RANK:], cos_ref[rows, :], sin_ref[rows, :])
        kr_odd = pltpu.roll(kr_even, MLA_ROPE, 1)
        ones_col = jnp.ones(kr_even.shape, BF16)
        for hd in range(MLA_HEADS):
            k_ref[rows, hd * kw:hd * kw + MLA_NOPE] = kv[:, hd * kw:hd * kw + MLA_NOPE].astype(BF16)
            k_ref[rows, hd * kw + MLA_NOPE:(hd + 1) * kw] = (
                kr_even if hd % 2 == 0 else kr_odd).astype(BF16)
            v_ref[rows, hd * kw:hd * kw + MLA_V] = kv[:, hd * kw + MLA_NOPE:(hd + 1) * kw].astype(BF16)
            v_ref[rows, hd * kw + MLA_V:(hd + 1) * kw] = ones_col


def _kv(h, w_t, g_kv, w_ukv, cos_tab, sin_tab):
    m, k_dim = h.shape
    tm = KV_TM
    kw = MLA_HEADS * (MLA_NOPE + MLA_V)
    return pl.pallas_call(
        _kv_kernel,
        grid=(m // tm,),
        in_specs=[pl.BlockSpec((tm, k_dim), lambda i: (i, 0)),
                  pl.BlockSpec((pl.Element(KVR_W), pl.Element(k_dim)),
                               lambda i: (pl.multiple_of(Q_MLA_W + 0 * i, SUBLANES), 0)),
                  pl.BlockSpec((1, KV_RANK), lambda i: (0, 0)),
                  pl.BlockSpec((KV_RANK, kw), lambda i: (0, 0)),
                  pl.BlockSpec((tm, LANES), lambda i: (i, 0)),
                  pl.BlockSpec((tm, LANES), lambda i: (i, 0))],
        out_specs=[pl.BlockSpec((tm, kw), lambda i: (i, 0)),
                   pl.BlockSpec((tm, kw), lambda i: (i, 0))],
        out_shape=[jax.ShapeDtypeStruct((m, kw), BF16),
                   jax.ShapeDtypeStruct((m, kw), BF16)],
        scratch_shapes=[pltpu.VMEM((KVR_PAD, k_dim), BF16), pltpu.VMEM((KV_RANK, kw), BF16)],
        compiler_params=_cparams(("arbitrary",)),
        name="kv_up",
    )(h, w_t, g_kv, w_ukv, cos_tab, sin_tab)


def _flash_pipeline(n_chains, nq, diags, loops, value_fn, finalize_fn,
                    s_ref, acc_ref, mpart_ref, macc_ref, sd_ref=None, mpd_ref=None):
    assert sd_ref is None or nq % 2 == 0
    chunk = s_ref.shape[2]

    def lane_tiles(x, n):
        return jnp.concatenate([x] * n, axis=1)

    def qk_phase(scores, stage):
        s_dst, m_dst = stage
        for ci, s in enumerate(scores):
            s_dst[ci] = s
            part = s[:, :LANES]
            for j in range(1, chunk // LANES):
                part = jnp.maximum(part, s[:, j * LANES:(j + 1) * LANES])
            m_dst[ci] = part

    def pv_phase(kc, stage):
        s_src, m_src = stage
        for ci in range(n_chains):
            m_acc = macc_ref[ci]
            m_run = jnp.maximum(m_acc, jnp.max(m_src[ci], axis=-1, keepdims=True))
            macc_ref[ci] = m_run
            p = jnp.exp2(s_src[ci] - lane_tiles(m_run, chunk // LANES))
            alpha = jnp.exp2(m_acc - m_run)
            pv = jnp.dot(p.astype(BF16), value_fn(ci, kc), preferred_element_type=F32)
            acc_ref[ci] = lane_tiles(alpha, acc_ref.shape[2] // LANES) * acc_ref[ci] + pv

    def reset():
        acc_ref[...] = jnp.zeros_like(acc_ref)
        macc_ref[...] = jnp.full(macc_ref.shape, NEG, F32)

    def tile(qt, stage, next_stage):
        cur = qt
        for trips_fn, score_fn in loops:
            def step(kc, cur, score_fn=score_fn):
                pv_phase(cur, stage)
                qk_phase(score_fn(qt, kc), stage)
                return kc

            cur = lax.fori_loop(0, trips_fn(qt), step, cur)
        nxt = jnp.minimum(qt + 1, nq - 1)

        def transition(_, cur, fn):
            if next_stage is stage:
                pv_phase(cur, stage)
                finalize_fn(qt)
                reset()
                qk_phase(fn(nxt), stage)
            else:
                qk_phase(fn(nxt), next_stage)
                pv_phase(cur, stage)
                finalize_fn(qt)
                reset()
            return cur

        for select_fn, fn in diags:
            if select_fn is None:
                transition(0, cur, fn)
            else:
                lax.fori_loop(0, select_fn(nxt), functools.partial(transition, fn=fn), cur)

    stage_a = (s_ref, mpart_ref)
    reset()
    for select_fn, fn in diags:
        if select_fn is None:
            qk_phase(fn(0), stage_a)
        else:
            lax.fori_loop(0, select_fn(0), lambda _, c, fn=fn: qk_phase(fn(0), stage_a) or c, 0)
    if sd_ref is None:
        lax.fori_loop(0, nq, lambda qt, c: tile(qt, stage_a, stage_a) or c, 0)
    else:
        stage_b = (sd_ref, mpd_ref)

        def tile_pair(j, c):
            tile(2 * j, stage_a, stage_b)
            tile(2 * j + 1, stage_b, stage_a)
            return c

        lax.fori_loop(0, nq // 2, tile_pair, 0)


def _causal_mask(rows, cols, row0):
    row = lax.broadcasted_iota(jnp.int32, (rows, cols), 0) + row0
    col = lax.broadcasted_iota(jnp.int32, (rows, cols), 1)
    return col <= row


def _pad_masked(s, cols):
    if s.shape[1] == cols:
        return s
    return jnp.concatenate([s, jnp.full((s.shape[0], cols - s.shape[1]), NEG, s.dtype)], axis=1)


_NT = (((1,), (1,)), ((), ()))


def _mla_attn_kernel(q_ref, k_ref, v_ref, g_ref, o_ref, acc_ref, qs_ref, s_ref, mpart_ref, macc_ref,
                     *, tq, rs):
    kw = 2 * MLA_NOPE
    tr = tq // rs
    nq = q_ref.shape[0] // tq
    chains = [(hp, r) for hp in range(2) for r in range(rs)]

    def stage_queries(qt):
        rows = pl.ds(pl.multiple_of(qt * tq, tq), tq)
        for hp in range(2):
            qs_ref[hp, :, :MLA_NOPE] = q_ref[rows, hp * MLA_NOPE:(hp + 1) * MLA_NOPE]
            qs_ref[hp, :, MLA_NOPE:] = q_ref[rows, 2 * MLA_NOPE:]

    def scores(qt, kc, diag):
        if diag:
            stage_queries(qt)
        k0 = pl.multiple_of(kc * tq, tq)
        out = []
        for hp, r in chains:
            nk = (r + 1) * tr if diag else tq
            q = qs_ref[hp, r * tr:(r + 1) * tr, :]
            k = k_ref[pl.ds(k0, nk), hp * kw:(hp + 1) * kw]
            s = lax.dot_general(q, k, _NT, preferred_element_type=F32)
            if diag:
                s = _pad_masked(jnp.where(_causal_mask(tr, nk, r * tr), s, NEG), tq)
            out.append(s)
        return out

    def values(ci, kc):
        hp = chains[ci][0]
        return v_ref[pl.ds(pl.multiple_of(kc * tq, tq), tq), hp * kw:(hp + 1) * kw]

    def finalize(qt):
        for ci, (hp, r) in enumerate(chains):
            rows = pl.ds(pl.multiple_of(qt * tq + r * tr, tr), tr)
            acc = acc_ref[ci]
            o = acc[:, :MLA_V] / acc[:, MLA_V:]
            gate = g_ref[rows, hp * MLA_V:(hp + 1) * MLA_V].astype(F32)
            o_ref[rows, hp * MLA_V:(hp + 1) * MLA_V] = (o * gate).astype(BF16)

    _flash_pipeline(len(chains), nq, [(None, lambda qt: scores(qt, qt, True))],
                    [(lambda qt: qt, lambda qt, kc: scores(qt, kc, False))],
                    values, finalize, s_ref, acc_ref, mpart_ref, macc_ref)


def _mla_attn(qm, kk, vv, rest, nb, s):
    tq = MLA_TQ
    pair_w = 2 * (MLA_NOPE + MLA_ROPE)
    kw = 4 * MLA_NOPE
    gate_blk = GM_OFF // (2 * MLA_V)
    rs = MLA_ROW_SPLIT
    return pl.pallas_call(
        functools.partial(_mla_attn_kernel, tq=tq, rs=rs),
        grid=(nb, MLA_HEADS // 2),
        in_specs=[pl.BlockSpec((s, pair_w), lambda b, hh: (b, hh)),
                  pl.BlockSpec((s, kw), lambda b, hh: (b, hh)),
                  pl.BlockSpec((s, kw), lambda b, hh: (b, hh)),
                  pl.BlockSpec((s, 2 * MLA_V), lambda b, hh: (b, gate_blk + hh))],
        out_specs=pl.BlockSpec((s, 2 * MLA_V), lambda b, hh: (b, hh)),
        out_shape=jax.ShapeDtypeStruct((nb * s, MLA_HEADS * MLA_V), BF16),
        scratch_shapes=[pltpu.VMEM((2 * rs, tq // rs, 2 * MLA_V), F32),
                        pltpu.VMEM((2, tq, 2 * MLA_NOPE), BF16),
                        pltpu.VMEM((2 * rs, tq // rs, tq), F32),
                        pltpu.VMEM((2 * rs, tq // rs, LANES), F32),
                        pltpu.VMEM((2 * rs, tq // rs, LANES), F32)],
        compiler_params=_cparams(("arbitrary", "arbitrary")),
        name="mla_attn",
    )(qm, kk, vv, rest)


def _diff_attn_kernel(ord_ref, q_ref, k_ref, v_ref, g_ref, pq_ref, pk_ref, plane_ref, sl_ref, lam_ref,
                      gs_ref, o_ref, vaug_ref, kaug_ref, acc_ref, qf_ref, s_ref, mpart_ref, macc_ref,
                      sd_ref, mpd_ref, *, tq, rs, lambda_init):
    seq = q_ref.shape[0]
    nq = seq // tq
    tr = tq // rs
    chains = [(c, r) for c in range(2) for r in range(rs)]
    lane = lax.broadcasted_iota(jnp.int32, (tq, LANES), 1)
    ones_col = jnp.ones((tq, LANES), BF16)
    slope2 = sl_ref[0, :, 0:1] * LOG2E
    c_pieces = _bf16_pieces(slope2)
    lane_row = lane[:1]
    c_query = jnp.where(lane_row < N_PAIR_LANES, _pick3(lane_row // N_PIECES, c_pieces), 0.0).astype(BF16)
    c_key = jnp.where(jnp.logical_and(lane_row >= N_PAIR_LANES, lane_row < 2 * N_PAIR_LANES),
                      _pick3((lane_row - N_PAIR_LANES) // N_PIECES, c_pieces), 0.0).astype(BF16)

    def bias_lanes(rows, key_side):
        pos_lanes = plane_ref[rows, :]
        if key_side:
            return jnp.where(lane < N_PAIR_LANES, pos_lanes, c_key)
        return jnp.where(lane < N_PAIR_LANES, c_query, pos_lanes)

    def stage_keys(kc, _):
        rows = pl.ds(pl.multiple_of(kc * tq, tq), tq)
        vaug_ref[rows, :DIFF_V] = v_ref[rows, :]
        vaug_ref[rows, DIFF_V:] = ones_col
        kaug_ref[rows, :2 * DIFF_QK] = k_ref[rows, :]
        kaug_ref[rows, 2 * DIFF_QK:] = bias_lanes(rows, True)
        return 0

    lax.fori_loop(0, nq, stage_keys, 0)

    def stage_queries(qt):
        rows = pl.ds(pl.multiple_of(qt * tq, tq), tq)
        q = q_ref[rows, :]
        q_side = bias_lanes(rows, False)
        qf_ref[0, :, :2 * DIFF_QK] = jnp.where(lane < DIFF_QK, q, 0).astype(BF16)
        qf_ref[1, :, :2 * DIFF_QK] = jnp.where(lane >= DIFF_QK, q, 0).astype(BF16)
        qf_ref[0, :, 2 * DIFF_QK:] = q_side
        qf_ref[1, :, 2 * DIFF_QK:] = q_side

    lq = lam_ref[...]
    lam = (jnp.exp(jnp.sum(lq[0:1] * lq[1:2], axis=-1, keepdims=True))
           - jnp.exp(jnp.sum(lq[2:3] * lq[3:4], axis=-1, keepdims=True)) + lambda_init)

    def scores(qt, kc, diag):
        k0 = pl.multiple_of(kc * tq, tq)
        k = k_ref[pl.ds(k0, tq), :]
        pk = pk_ref[0, pl.ds(kc, 1), :]
        out = [None] * len(chains)
        for r in range(rs):
            pq = pq_ref[pl.ds(pl.multiple_of(qt * tq + r * tr, tr), tr), :]
            bias = slope2 * jnp.abs(pq - pk)
            mask = _causal_mask(tr, tq, r * tr) if diag else None
            for c in range(2):
                q = qf_ref[c, r * tr:(r + 1) * tr, :2 * DIFF_QK]
                s = lax.dot_general(q, k, _NT, preferred_element_type=F32) - bias
                out[chains.index((c, r))] = jnp.where(mask, s, NEG) if diag else s
        return out

    def scores_diag(qt):
        stage_queries(qt)
        return scores(qt, qt, True)

    def scores_ordered(qt, kc):
        k = kaug_ref[pl.ds(pl.multiple_of(kc * tq, tq), tq), :]
        return [lax.dot_general(qf_ref[c, r * tr:(r + 1) * tr, :], k, _NT, preferred_element_type=F32)
                for c, r in chains]

    def scores_diag_sorted(qt):
        stage_queries(qt)
        k0 = pl.multiple_of(qt * tq, tq)
        out = []
        for c, r in chains:
            nk = (r + 1) * tr
            s = lax.dot_general(qf_ref[c, r * tr:(r + 1) * tr, :], kaug_ref[pl.ds(k0, nk), :], _NT,
                                preferred_element_type=F32)
            out.append(_pad_masked(jnp.where(_causal_mask(tr, nk, r * tr), s, NEG), tq))
        return out

    batch = pl.program_id(0)

    def trips_ordered(qt):
        return jnp.where((ord_ref[batch, qt] & 1) != 0, qt, 0)

    def tile_sorted(qt):
        return (ord_ref[batch, qt] >> 1) & 1

    def values(ci, kc):
        return vaug_ref[pl.ds(pl.multiple_of(kc * tq, tq), tq), :]

    def finalize(qt):
        for r in range(rs):
            rows = pl.ds(pl.multiple_of(qt * tq + r * tr, tr), tr)
            a1 = acc_ref[chains.index((0, r))]
            a2 = acc_ref[chains.index((1, r))]
            o = a1[:, :DIFF_V] / a1[:, DIFF_V:] - lam * (a2[:, :DIFF_V] / a2[:, DIFF_V:])
            ms_o = jnp.mean(o * o, axis=-1, keepdims=True)
            o = o * lax.rsqrt(ms_o + EPS) * gs_ref[...] * (1.0 - lambda_init)
            o_ref[rows, :] = (o * g_ref[rows, :].astype(F32)).astype(BF16)

    _flash_pipeline(len(chains), nq,
                    [(tile_sorted, scores_diag_sorted), (lambda qt: 1 - tile_sorted(qt), scores_diag)],
                    [(trips_ordered, scores_ordered),
                     (lambda qt: qt - trips_ordered(qt), lambda qt, kc: scores(qt, kc, False))],
                    values, finalize, s_ref, acc_ref, mpart_ref, macc_ref, sd_ref, mpd_ref)


def _diff_attn(rest, pos_col, pos_row, pos_lanes, ordered, slopes, lam_par, g_subln, lambda_init,
               nb, s):
    tq = ATTN_TQ
    nq = s // tq
    rs = DIFF_ROW_SPLIT
    tr = tq // rs
    hw = DIFF_V
    return pl.pallas_call(
        functools.partial(_diff_attn_kernel, tq=tq, rs=rs, lambda_init=lambda_init),
        grid=(nb, DIFF_HEADS),
        in_specs=[pl.BlockSpec(memory_space=pltpu.SMEM),
                  pl.BlockSpec((s, hw), lambda b, hd: (b, QD_OFF // hw + hd)),
                  pl.BlockSpec((s, hw), lambda b, hd: (b, KD_OFF // hw + hd)),
                  pl.BlockSpec((s, hw), lambda b, hd: (b, VD_OFF // hw + hd)),
                  pl.BlockSpec((s, hw), lambda b, hd: (b, GD_OFF // hw + hd)),
                  pl.BlockSpec((s, 1), lambda b, hd: (b, 0)),
                  pl.BlockSpec((1, nq, tq), lambda b, hd: (b, 0, 0)),
                  pl.BlockSpec((s, LANES), lambda b, hd: (b, 0)),
                  pl.BlockSpec((1, 1, LANES), lambda b, hd: (hd, 0, 0)),
                  pl.BlockSpec((4, DIFF_QK), lambda b, hd: (0, 0)),
                  pl.BlockSpec((1, DIFF_V), lambda b, hd: (0, 0))],
        out_specs=pl.BlockSpec((s, hw), lambda b, hd: (b, hd)),
        out_shape=jax.ShapeDtypeStruct((nb * s, DIFF_HEADS * DIFF_V), BF16),
        scratch_shapes=[pltpu.VMEM((s, 2 * DIFF_V), BF16),
                        pltpu.VMEM((s, 4 * DIFF_QK), BF16),
                        pltpu.VMEM((2 * rs, tr, 2 * DIFF_V), F32),
                        pltpu.VMEM((2, tq, 4 * DIFF_QK), BF16),
                        pltpu.VMEM((2 * rs, tr, tq), F32),
                        pltpu.VMEM((2 * rs, tr, LANES), F32),
                        pltpu.VMEM((2 * rs, tr, LANES), F32),
                        pltpu.VMEM((2 * rs, tr, tq), F32),
                        pltpu.VMEM((2 * rs, tr, LANES), F32)],
        compiler_params=_cparams(("arbitrary", "arbitrary")),
        name="diff_attn",
    )(ordered, rest, rest, rest, rest, pos_col, pos_row, pos_lanes, slopes, lam_par, g_subln)


def _cast_kernel(w1_ref, w2_ref, w3_ref, o1_ref, o2_ref, o3_ref):
    o1_ref[...] = w1_ref[...].astype(BF16)
    o2_ref[...] = w2_ref[...].astype(BF16)
    o3_ref[...] = w3_ref[...].astype(BF16)


def _cast_weights(w1, w2, w3):
    steps = CAST_STEPS
    specs = [pl.BlockSpec((w.shape[0] // steps, w.shape[1]), lambda i: (i, 0)) for w in (w1, w2, w3)]
    return pl.pallas_call(
        _cast_kernel,
        grid=(steps,),
        in_specs=specs,
        out_specs=specs,
        out_shape=[jax.ShapeDtypeStruct(w.shape, BF16) for w in (w1, w2, w3)],
        compiler_params=_cparams(("arbitrary",)),
        name="cast_weights",
    )(w1, w2, w3)


def _merge_out_kernel(a1_ref, a2_ref, s1a_ref, s1b_ref, s2a_ref, s2b_ref, w1_ref, w2_ref, wo_ref,
                      x_ref, ada_ref, g_ref, o_ref):
    half = s1a_ref.shape[1]
    rb = a1_ref.shape[0] // EPILOGUE_ROW_SPLIT
    for t in range(EPILOGUE_ROW_SPLIT):
        rows = slice(t * rb, (t + 1) * rb)
        y1 = jnp.dot(a1_ref[rows, :], w1_ref[...], preferred_element_type=F32)
        y2 = jnp.dot(a2_ref[rows, :], w2_ref[...], preferred_element_type=F32)
        merged = jnp.concatenate(
            [s1a_ref[rows, :].astype(F32) * y1[:, :half] + s2a_ref[rows, :].astype(F32) * y2[:, :half],
             s1b_ref[rows, :].astype(F32) * y1[:, half:] + s2b_ref[rows, :].astype(F32) * y2[:, half:]],
            axis=1).astype(BF16)
        y = jnp.dot(merged, wo_ref[...], preferred_element_type=F32)
        ms = jnp.mean(y * y, axis=-1, keepdims=True)
        yn = y * lax.rsqrt(ms + EPS) * g_ref[...]
        o_ref[0, rows, :] = x_ref[0, rows, :] + ada_ref[0, 2:3, :] * yn


def _merge_out(og_mla, og_diff, rest, w1b, w2b, wob, x, ada3, g_post):
    nb, s, d = x.shape
    k_dim = og_mla.shape[1]
    tm = OUT_TM
    ns = s // tm
    half = d // 2
    row = lambda b, i: (b * ns + i, 0)
    gate = lambda blk: pl.BlockSpec((tm, half), lambda b, i: (b * ns + i, blk))
    resident = lambda shape: pl.BlockSpec(shape, lambda b, i: (0, 0), pipeline_mode=pl.Buffered(1))
    return pl.pallas_call(
        _merge_out_kernel,
        grid=(nb, ns),
        in_specs=[pl.BlockSpec((tm, k_dim), row),
                  pl.BlockSpec((tm, k_dim), row),
                  gate(MGM_OFF // half), gate(MGM_OFF // half + 1),
                  gate(MGD_OFF // half), gate(MGD_OFF // half + 1),
                  resident((k_dim, d)), resident((k_dim, d)), resident((d, d)),
                  pl.BlockSpec((1, tm, d), lambda b, i: (b, i, 0)),
                  pl.BlockSpec((1, 3, d), lambda b, i: (b, 0, 0)),
                  pl.BlockSpec((1, d), lambda b, i: (0, 0))],
        out_specs=pl.BlockSpec((1, tm, d), lambda b, i: (b, i, 0)),
        out_shape=jax.ShapeDtypeStruct((nb, s, d), F32),
        compiler_params=_cparams(("arbitrary", "arbitrary")),
        name="merge_out",
    )(og_mla, og_diff, rest, rest, rest, rest, w1b, w2b, wob, x, ada3, g_post)


def kernel(x, c, positions, w_ada, b_ada, g_pre, w_in, g_kv, w_ukv, lambda_q1, lambda_k1,
           lambda_q2, lambda_k2, g_subln, w_o_mla, w_o_diff, w_out, g_post):
    nb, s, d = x.shape
    depth = w_in.shape[0]
    assert w_in.shape[1:] == (d, REST_OFF + REST_W) and w_out.shape[1:] == (d, d)
    assert s % ATTN_TQ == 0 and s % MLA_TQ == 0 and (nb * s) % PROJ_TM == 0
    half = MLA_ROPE // 2
    inv = ROPE_THETA ** (-jnp.arange(half, dtype=F32) / half)
    inv_tab = jnp.tile(inv, LANES // half).reshape(1, LANES)
    slopes = 2.0 ** (-8.0 * jnp.arange(1, DIFF_HEADS + 1, dtype=F32) / DIFF_HEADS)
    slopes = jnp.broadcast_to(slopes.reshape(DIFF_HEADS, 1, 1), (DIFF_HEADS, 1, LANES))
    pos_col = positions.reshape(nb * s, 1)
    pos_colf = pos_col.astype(F32)
    pos_chunks = positions.reshape(nb, s // ATTN_TQ, ATTN_TQ)
    pos_row = pos_chunks.astype(F32)
    run_max = lax.cummax(pos_chunks.max(axis=-1), axis=1)
    prev_max = jnp.concatenate(
        [jnp.full((nb, 1), jnp.iinfo(jnp.int32).min, jnp.int32), run_max[:, :-1]], axis=1)
    tile_sorted = jnp.all(pos_chunks[..., 1:] >= pos_chunks[..., :-1], axis=-1)
    foldable = jnp.max(jnp.abs(pos_chunks), axis=(1, 2), keepdims=False)[:, None] < MAX_FOLD_POS
    earlier_ordered = prev_max <= pos_chunks.min(axis=-1)
    ordered = (jnp.logical_and(foldable, earlier_ordered).astype(jnp.int32)
               + 2 * jnp.logical_and(foldable, tile_sorted).astype(jnp.int32))

    for l in range(depth):
        ada3 = _ada(c, w_ada[l], b_ada[l]).reshape(nb, 3, d)
        h, cos_tab, sin_tab, pos_lanes = _norm(x, ada3, g_pre[l].reshape(1, d), pos_col, inv_tab)
        w_t = jnp.swapaxes(w_in[l], 0, 1)
        qm, rest = _proj(h, w_t, cos_tab, sin_tab)
        kk, vv = _kv(h, w_t, g_kv[l].reshape(1, KV_RANK), w_ukv[l], cos_tab, sin_tab)
        og_mla = _mla_attn(qm, kk, vv, rest, nb, s)
        lam_par = jnp.stack([lambda_q1[l], lambda_k1[l], lambda_q2[l], lambda_k2[l]]).astype(F32)
        og_diff = _diff_attn(rest, pos_colf, pos_row, pos_lanes, ordered, slopes, lam_par,
                             g_subln[l].reshape(1, DIFF_V), 0.8 - 0.6 * math.exp(-0.3 * l), nb, s)
        w1b, w2b, wob = _cast_weights(w_o_mla[l], w_o_diff[l], w_out[l])
        x = _merge_out(og_mla, og_diff, rest, w1b, w2b, wob, x, ada3, g_post[l].reshape(1, d))
    return x
```

```python
import functools
import math

import jax
import jax.numpy as jnp
from jax import lax
from jax.experimental import pallas as pl
from jax.experimental.pallas import tpu as pltpu

F32 = jnp.float32
BF16 = jnp.bfloat16

MLA_HEADS = 8
MLA_NOPE = 128
MLA_ROPE = 64
MLA_V = 128
KV_RANK = 512
ROPE_THETA = 10000.0
DIFF_HEADS = 8
DIFF_QK = 64
DIFF_V = 128
EPS = 1e-6
NEG = -1e30

LANES = 128
SUBLANES = 8
VMEM_LIMIT = 56 * 1024 * 1024
ADA_TN = 1024
NORM_TS = 512
MAX_FOLD_POS = 1 << 16
N_PIECES = 3
N_PAIR_LANES = N_PIECES * N_PIECES
ATTN_TQ = 1024
DIFF_ROW_SPLIT = 2
DIFF_HEADS_PER_STEP = 2
MLA_TQ = 1024
MLA_ROW_SPLIT = 2
PROJ_TM = 1024
REST_TN = 1024
QMLA_PAIRS_PER_TILE = 2
KV_TM = 1024
CAST_STEPS = 4
OUT_TM = 512
EPILOGUE_ROW_SPLIT = 2
LOG2E = math.log2(math.e)

Q_MLA_W = MLA_HEADS * (MLA_NOPE + MLA_ROPE)
KVR_W = KV_RANK + MLA_ROPE
KVR_PAD = 640
REST_OFF = Q_MLA_W + KVR_W
QD_OFF, KD_OFF, VD_OFF, GM_OFF, GD_OFF, MGM_OFF, MGD_OFF = 0, 1024, 2048, 3072, 4096, 5120, 7168
REST_W = 9216


def _cparams(sem):
    return pltpu.CompilerParams(dimension_semantics=sem, vmem_limit_bytes=VMEM_LIMIT)


def _ada_kernel(cb_ref, w_ref, b_ref, o_ref):
    k_dim, tn = w_ref.shape
    nb = cb_ref.shape[0]
    nchunk = tn // LANES

    def body(i, accs):
        k0 = pl.multiple_of(i * SUBLANES, SUBLANES)
        out = []
        for b in range(nb):
            cv = cb_ref[b, pl.ds(k0, SUBLANES), :]
            for j in range(nchunk):
                wv = w_ref[pl.ds(k0, SUBLANES), j * LANES:(j + 1) * LANES]
                out.append(accs[b * nchunk + j] + wv * cv)
        return tuple(out)

    init = tuple(jnp.zeros((SUBLANES, LANES), F32) for _ in range(nb * nchunk))
    accs = lax.fori_loop(0, k_dim // SUBLANES, body, init, unroll=8)
    for b in range(nb):
        row = jnp.concatenate(
            [jnp.sum(accs[b * nchunk + j], axis=0, keepdims=True) for j in range(nchunk)], axis=1)
        o_ref[b:b + 1, :] = row + b_ref[...]


def _ada(c, w, bias):
    nb, k_dim = c.shape
    n = w.shape[1]
    tn = ADA_TN
    cb = jnp.broadcast_to(c[:, :, None], (nb, k_dim, LANES))
    return pl.pallas_call(
        _ada_kernel,
        grid=(n // tn,),
        in_specs=[pl.BlockSpec((nb, k_dim, LANES), lambda j: (0, 0, 0)),
                  pl.BlockSpec((k_dim, tn), lambda j: (0, j)),
                  pl.BlockSpec((1, tn), lambda j: (0, j))],
        out_specs=pl.BlockSpec((nb, tn), lambda j: (0, j)),
        out_shape=jax.ShapeDtypeStruct((nb, n), F32),
        compiler_params=_cparams(("arbitrary",)),
        name="ada",
    )(cb, w, bias.reshape(1, n))


def _bf16_pieces(x):
    p1 = x.astype(BF16)
    r = x - p1.astype(F32)
    p2 = r.astype(BF16)
    return p1.astype(F32), p2.astype(F32), r - p2.astype(F32)


def _pick3(sel, x3):
    return jnp.where(sel == 0, x3[0], jnp.where(sel == 1, x3[1], x3[2]))


def _norm_kernel(x_ref, ada_ref, g_ref, pos_ref, inv_ref, h_ref, cos_ref, sin_ref, plane_ref):
    x = x_ref[0]
    ms = jnp.mean(x * x, axis=-1, keepdims=True)
    y = x * lax.rsqrt(ms + EPS) * g_ref[...]
    shift = ada_ref[0, 0:1, :]
    scale = ada_ref[0, 1:2, :]
    h_ref[...] = (y * (1.0 + scale) + shift).astype(BF16)
    ang = pos_ref[...].astype(F32) * inv_ref[...]
    lane = lax.broadcasted_iota(jnp.int32, ang.shape, 1)
    sign = jnp.where((lane % MLA_ROPE) < MLA_ROPE // 2, -1.0, 1.0).astype(F32)
    cos_ref[...] = jnp.cos(ang)
    sin_ref[...] = jnp.sin(ang) * sign
    pos3 = _bf16_pieces(pos_ref[...].astype(F32))
    piece = _pick3(lane % N_PIECES, pos3)
    plane_ref[...] = jnp.where(lane < N_PAIR_LANES, piece,
                               jnp.where(lane < 2 * N_PAIR_LANES, -piece, 0.0)).astype(BF16)


def _norm(x, ada3, g_pre, pos_col, inv_tab):
    nb, s, d = x.shape
    ts = NORM_TS
    ns = s // ts
    row = lambda b, i: (b * ns + i, 0)
    return pl.pallas_call(
        _norm_kernel,
        grid=(nb, ns),
        in_specs=[pl.BlockSpec((1, ts, d), lambda b, i: (b, i, 0)),
                  pl.BlockSpec((1, 3, d), lambda b, i: (b, 0, 0)),
                  pl.BlockSpec((1, d), lambda b, i: (0, 0)),
                  pl.BlockSpec((ts, 1), row),
                  pl.BlockSpec((1, LANES), lambda b, i: (0, 0))],
        out_specs=[pl.BlockSpec((ts, d), row),
                   pl.BlockSpec((ts, LANES), row),
                   pl.BlockSpec((ts, LANES), row),
                   pl.BlockSpec((ts, LANES), row)],
        out_shape=[jax.ShapeDtypeStruct((nb * s, d), BF16),
                   jax.ShapeDtypeStruct((nb * s, LANES), F32),
                   jax.ShapeDtypeStruct((nb * s, LANES), F32),
                   jax.ShapeDtypeStruct((nb * s, LANES), BF16)],
        compiler_params=_cparams(("arbitrary", "arbitrary")),
        name="prenorm",
    )(x, ada3, g_pre, pos_col, inv_tab)


def _rope_cols(r, cos, sin_signed):
    lane = lax.broadcasted_iota(jnp.int32, r.shape, 1)
    half = MLA_ROPE // 2
    partner = jnp.where((lane % MLA_ROPE) < half,
                        pltpu.roll(r, LANES - half, 1), pltpu.roll(r, half, 1))
    return r * cos + partner * sin_signed


def _qmla_kernel(a_ref, w_ref, cos_ref, sin_ref, o_ref, wb_ref, *, scale):
    hd = MLA_NOPE + MLA_ROPE
    pair_w = 2 * hd
    n_pairs = wb_ref.shape[0] // pair_w

    @pl.when(pl.program_id(1) == 0)
    def _():
        for p in range(n_pairs):
            src, dst = w_ref.at[p * pair_w:(p + 1) * pair_w], wb_ref.at[p * pair_w:(p + 1) * pair_w]
            dst[:MLA_NOPE] = src[:MLA_NOPE].astype(BF16)
            dst[MLA_NOPE:2 * MLA_NOPE] = src[hd:hd + MLA_NOPE].astype(BF16)
            dst[2 * MLA_NOPE:2 * MLA_NOPE + MLA_ROPE] = src[MLA_NOPE:hd].astype(BF16)
            dst[2 * MLA_NOPE + MLA_ROPE:] = src[hd + MLA_NOPE:].astype(BF16)

    acc = lax.dot_general(a_ref[...], wb_ref[...], _NT, preferred_element_type=F32)
    for p in range(n_pairs):
        c0 = p * pair_w
        rr = _rope_cols(acc[:, c0 + 2 * MLA_NOPE:c0 + pair_w], cos_ref[...], sin_ref[...])
        o_ref[:, c0:c0 + 2 * MLA_NOPE] = (acc[:, c0:c0 + 2 * MLA_NOPE] * scale).astype(BF16)
        o_ref[:, c0 + 2 * MLA_NOPE:c0 + pair_w] = (rr * scale).astype(BF16)


def _sigmoid(x):
    return 0.5 * jnp.tanh(0.5 * x) + 0.5


def _rest_kernel(a_ref, w_ref, o_ref, wb_ref, *, tn):
    j = pl.program_id(0)

    @pl.when(pl.program_id(1) == 0)
    def _():
        wb_ref[...] = w_ref[...].astype(BF16)

    def tile(epilogue):
        acc = lax.dot_general(a_ref[...], wb_ref[...], _NT, preferred_element_type=F32)
        o_ref[...] = epilogue(acc).astype(BF16)

    @pl.when(j < KD_OFF // tn)
    def _():
        tile(lambda acc: acc * (DIFF_QK ** -0.5 * LOG2E))

    @pl.when(jnp.logical_and(j >= KD_OFF // tn, j < GM_OFF // tn))
    def _():
        tile(lambda acc: acc)

    @pl.when(jnp.logical_and(j >= GM_OFF // tn, j < MGM_OFF // tn))
    def _():
        tile(lambda acc: acc * _sigmoid(acc))

    @pl.when(j >= MGM_OFF // tn)
    def _():
        tile(_sigmoid)


def _proj(h, w_t, cos_tab, sin_tab):
    m, k_dim = h.shape
    tm = PROJ_TM
    pair_w = 2 * (MLA_NOPE + MLA_ROPE)
    q_scale = (MLA_NOPE + MLA_ROPE) ** -0.5 * LOG2E

    def row_window(rows, offset_fn):
        return pl.BlockSpec((pl.Element(rows), pl.Element(k_dim)),
                            lambda *g: (pl.multiple_of(offset_fn(*g), SUBLANES), 0))

    qtn = QMLA_PAIRS_PER_TILE * pair_w
    qm = pl.pallas_call(
        functools.partial(_qmla_kernel, scale=q_scale),
        grid=(Q_MLA_W // qtn, m // tm),
        in_specs=[pl.BlockSpec((tm, k_dim), lambda j, i: (i, 0)),
                  pl.BlockSpec((qtn, k_dim), lambda j, i: (j, 0)),
                  pl.BlockSpec((tm, LANES), lambda j, i: (i, 0)),
                  pl.BlockSpec((tm, LANES), lambda j, i: (i, 0))],
        out_specs=pl.BlockSpec((tm, qtn), lambda j, i: (i, j)),
        out_shape=jax.ShapeDtypeStruct((m, Q_MLA_W), BF16),
        scratch_shapes=[pltpu.VMEM((qtn, k_dim), BF16)],
        compiler_params=_cparams(("arbitrary", "arbitrary")),
        name="proj_qmla",
    )(h, w_t, cos_tab, sin_tab)
    tn = REST_TN
    rest = pl.pallas_call(
        functools.partial(_rest_kernel, tn=tn),
        grid=(REST_W // tn, m // tm),
        in_specs=[pl.BlockSpec((tm, k_dim), lambda j, i: (i, 0)),
                  row_window(tn, lambda j, i: REST_OFF + tn * j)],
        out_specs=pl.BlockSpec((tm, tn), lambda j, i: (i, j)),
        out_shape=jax.ShapeDtypeStruct((m, REST_W), BF16),
        scratch_shapes=[pltpu.VMEM((tn, k_dim), BF16)],
        compiler_params=_cparams(("arbitrary", "arbitrary")),
        name="proj_rest",
    )(h, w_t)
    return qm, rest


def _kv_kernel(a_ref, wp_ref, g_ref, w_ref, cos_ref, sin_ref, k_ref, v_ref, wpb_ref, wb_ref):
    @pl.when(pl.program_id(0) == 0)
    def _():
        wpb_ref[:KVR_W] = wp_ref[...].astype(BF16)
        wpb_ref[KVR_W:] = jnp.zeros((KVR_PAD - KVR_W, wpb_ref.shape[1]), BF16)
        wb_ref[...] = w_ref[...].astype(BF16)

    kw = MLA_NOPE + MLA_V
    rb = a_ref.shape[0] // EPILOGUE_ROW_SPLIT
    for t in range(EPILOGUE_ROW_SPLIT):
        rows = slice(t * rb, (t + 1) * rb)
        p = lax.dot_general(a_ref[rows, :], wpb_ref[...], _NT, preferred_element_type=F32)
        ckv = p[:, :KV_RANK]
        ms = jnp.mean(ckv * ckv, axis=-1, keepdims=True)
        n = (ckv * lax.rsqrt(ms + EPS) * g_ref[...]).astype(BF16)
        kv = jnp.dot(n, wb_ref[...], preferred_element_type=F32)
        kr_even = _rope_cols(p[:, KV_RANK:], cos_ref[rows, :], sin_ref[rows, :])
        kr_odd = pltpu.roll(kr_even, MLA_ROPE, 1)
        ones_col = jnp.ones(kr_even.shape, BF16)
        for hd in range(MLA_HEADS):
            k_ref[rows, hd * kw:hd * kw + MLA_NOPE] = kv[:, hd * kw:hd * kw + MLA_NOPE].astype(BF16)
            k_ref[rows, hd * kw + MLA_NOPE:(hd + 1) * kw] = (
                kr_even if hd % 2 == 0 else kr_odd).astype(BF16)
            v_ref[rows, hd * kw:hd * kw + MLA_V] = kv[:, hd * kw + MLA_NOPE:(hd + 1) * kw].astype(BF16)
            v_ref[rows, hd * kw + MLA_V:(hd + 1) * kw] = ones_col


def _kv(h, w_t, g_kv, w_ukv, cos_tab, sin_tab):
    m, k_dim = h.shape
    tm = KV_TM
    kw = MLA_HEADS * (MLA_NOPE + MLA_V)
    return pl.pallas_call(
        _kv_kernel,
        grid=(m // tm,),
        in_specs=[pl.BlockSpec((tm, k_dim), lambda i: (i, 0)),
                  pl.BlockSpec((pl.Element(KVR_W), pl.Element(k_dim)),
                               lambda i: (pl.multiple_of(Q_MLA_W + 0 * i, SUBLANES), 0)),
                  pl.BlockSpec((1, KV_RANK), lambda i: (0, 0)),
                  pl.BlockSpec((KV_RANK, kw), lambda i: (0, 0)),
                  pl.BlockSpec((tm, LANES), lambda i: (i, 0)),
                  pl.BlockSpec((tm, LANES), lambda i: (i, 0))],
        out_specs=[pl.BlockSpec((tm, kw), lambda i: (i, 0)),
                   pl.BlockSpec((tm, kw), lambda i: (i, 0))],
        out_shape=[jax.ShapeDtypeStruct((m, kw), BF16),
                   jax.ShapeDtypeStruct((m, kw), BF16)],
        scratch_shapes=[pltpu.VMEM((KVR_PAD, k_dim), BF16), pltpu.VMEM((KV_RANK, kw), BF16)],
        compiler_params=_cparams(("arbitrary",)),
        name="kv_up",
    )(h, w_t, g_kv, w_ukv, cos_tab, sin_tab)


def _flash_pipeline(n_chains, nq, diags, loops, value_fn, finalize_fn,
                    s_ref, acc_ref, mpart_ref, macc_ref, sd_ref=None, mpd_ref=None):
    assert sd_ref is None or nq % 2 == 0
    chunk = s_ref.shape[2]

    def lane_tiles(x, n):
        return jnp.concatenate([x] * n, axis=1)

    def qk_phase(scores, stage):
        s_dst, m_dst = stage
        for ci, s in enumerate(scores):
            s_dst[ci] = s
            part = s[:, :LANES]
            for j in range(1, chunk // LANES):
                part = jnp.maximum(part, s[:, j * LANES:(j + 1) * LANES])
            m_dst[ci] = part

    def pv_phase(kc, stage):
        s_src, m_src = stage
        for ci in range(n_chains):
            m_acc = macc_ref[ci]
            m_run = jnp.maximum(m_acc, jnp.max(m_src[ci], axis=-1, keepdims=True))
            macc_ref[ci] = m_run
            p = jnp.exp2(s_src[ci] - lane_tiles(m_run, chunk // LANES))
            alpha = jnp.exp2(m_acc - m_run)
            pv = jnp.dot(p.astype(BF16), value_fn(ci, kc), preferred_element_type=F32)
            acc_ref[ci] = lane_tiles(alpha, acc_ref.shape[2] // LANES) * acc_ref[ci] + pv

    def reset():
        acc_ref[...] = jnp.zeros_like(acc_ref)
        macc_ref[...] = jnp.full(macc_ref.shape, NEG, F32)

    def tile(qt, stage, next_stage):
        cur = qt
        for trips_fn, score_fn in loops:
            def step(kc, cur, score_fn=score_fn):
                pv_phase(cur, stage)
                qk_phase(score_fn(qt, kc), stage)
                return kc

            cur = lax.fori_loop(0, trips_fn(qt), step, cur)
        nxt = jnp.minimum(qt + 1, nq - 1)

        def transition(_, cur, fn):
            if next_stage is stage:
                pv_phase(cur, stage)
                finalize_fn(qt)
                reset()
                qk_phase(fn(nxt), stage)
            else:
                qk_phase(fn(nxt), next_stage)
                pv_phase(cur, stage)
                finalize_fn(qt)
                reset()
            return cur

        for select_fn, fn in diags:
            if select_fn is None:
                transition(0, cur, fn)
            else:
                lax.fori_loop(0, select_fn(nxt), functools.partial(transition, fn=fn), cur)

    stage_a = (s_ref, mpart_ref)
    reset()
    for select_fn, fn in diags:
        if select_fn is None:
            qk_phase(fn(0), stage_a)
        else:
            lax.fori_loop(0, select_fn(0), lambda _, c, fn=fn: qk_phase(fn(0), stage_a) or c, 0)
    if sd_ref is None:
        lax.fori_loop(0, nq, lambda qt, c: tile(qt, stage_a, stage_a) or c, 0)
    else:
        stage_b = (sd_ref, mpd_ref)

        def tile_pair(j, c):
            tile(2 * j, stage_a, stage_b)
            tile(2 * j + 1, stage_b, stage_a)
            return c

        lax.fori_loop(0, nq // 2, tile_pair, 0)


def _causal_mask(rows, cols, row0):
    row = lax.broadcasted_iota(jnp.int32, (rows, cols), 0) + row0
    col = lax.broadcasted_iota(jnp.int32, (rows, cols), 1)
    return col <= row


def _pad_masked(s, cols):
    if s.shape[1] == cols:
        return s
    return jnp.concatenate([s, jnp.full((s.shape[0], cols - s.shape[1]), NEG, s.dtype)], axis=1)


_NT = (((1,), (1,)), ((), ()))


def _mla_attn_kernel(q_ref, k_ref, v_ref, g_ref, o_ref, acc_ref, qs_ref, s_ref, mpart_ref, macc_ref,
                     *, tq, rs):
    kw = 2 * MLA_NOPE
    tr = tq // rs
    nq = q_ref.shape[0] // tq
    chains = [(hp, r) for hp in range(2) for r in range(rs)]

    def stage_queries(qt):
        rows = pl.ds(pl.multiple_of(qt * tq, tq), tq)
        for hp in range(2):
            qs_ref[hp, :, :MLA_NOPE] = q_ref[rows, hp * MLA_NOPE:(hp + 1) * MLA_NOPE]
            qs_ref[hp, :, MLA_NOPE:] = q_ref[rows, 2 * MLA_NOPE:]

    def scores(qt, kc, diag):
        if diag:
            stage_queries(qt)
        k0 = pl.multiple_of(kc * tq, tq)
        out = []
        for hp, r in chains:
            nk = (r + 1) * tr if diag else tq
            q = qs_ref[hp, r * tr:(r + 1) * tr, :]
            k = k_ref[pl.ds(k0, nk), hp * kw:(hp + 1) * kw]
            s = lax.dot_general(q, k, _NT, preferred_element_type=F32)
            if diag:
                s = _pad_masked(jnp.where(_causal_mask(tr, nk, r * tr), s, NEG), tq)
            out.append(s)
        return out

    def values(ci, kc):
        hp = chains[ci][0]
        return v_ref[pl.ds(pl.multiple_of(kc * tq, tq), tq), hp * kw:(hp + 1) * kw]

    def finalize(qt):
        for ci, (hp, r) in enumerate(chains):
            rows = pl.ds(pl.multiple_of(qt * tq + r * tr, tr), tr)
            acc = acc_ref[ci]
            o = acc[:, :MLA_V] / acc[:, MLA_V:]
            gate = g_ref[rows, hp * MLA_V:(hp + 1) * MLA_V].astype(F32)
            o_ref[rows, hp * MLA_V:(hp + 1) * MLA_V] = (o * gate).astype(BF16)

    _flash_pipeline(len(chains), nq, [(None, lambda qt: scores(qt, qt, True))],
                    [(lambda qt: qt, lambda qt, kc: scores(qt, kc, False))],
                    values, finalize, s_ref, acc_ref, mpart_ref, macc_ref)


def _mla_attn(qm, kk, vv, rest, nb, s):
    tq = MLA_TQ
    pair_w = 2 * (MLA_NOPE + MLA_ROPE)
    kw = 4 * MLA_NOPE
    gate_blk = GM_OFF // (2 * MLA_V)
    rs = MLA_ROW_SPLIT
    return pl.pallas_call(
        functools.partial(_mla_attn_kernel, tq=tq, rs=rs),
        grid=(nb, MLA_HEADS // 2),
        in_specs=[pl.BlockSpec((s, pair_w), lambda b, hh: (b, hh)),
                  pl.BlockSpec((s, kw), lambda b, hh: (b, hh)),
                  pl.BlockSpec((s, kw), lambda b, hh: (b, hh)),
                  pl.BlockSpec((s, 2 * MLA_V), lambda b, hh: (b, gate_blk + hh))],
        out_specs=pl.BlockSpec((s, 2 * MLA_V), lambda b, hh: (b, hh)),
        out_shape=jax.ShapeDtypeStruct((nb * s, MLA_HEADS * MLA_V), BF16),
        scratch_shapes=[pltpu.VMEM((2 * rs, tq // rs, 2 * MLA_V), F32),
                        pltpu.VMEM((2, tq, 2 * MLA_NOPE), BF16),
                        pltpu.VMEM((2 * rs, tq // rs, tq), F32),
                        pltpu.VMEM((2 * rs, tq // rs, LANES), F32),
                        pltpu.VMEM((2 * rs, tq // rs, LANES), F32)],
        compiler_params=_cparams(("arbitrary", "arbitrary")),
        name="mla_attn",
    )(qm, kk, vv, rest)


def _diff_attn_kernel(ord_ref, q_ref, k_ref, v_ref, g_ref, pq_ref, pk_ref, plane_ref, sl_ref, lam_ref,
                      gs_ref, o_ref, *scratch, tq, rs, lambda_init):
    for hd in range(DIFF_HEADS_PER_STEP):
        cols = slice(hd * DIFF_V, (hd + 1) * DIFF_V)
        _diff_attn_head(ord_ref, q_ref.at[:, cols], k_ref.at[:, cols], v_ref.at[:, cols],
                        g_ref.at[:, cols], pq_ref, pk_ref, plane_ref, sl_ref.at[hd:hd + 1], lam_ref,
                        gs_ref, o_ref.at[:, cols], *scratch, tq=tq, rs=rs, lambda_init=lambda_init)


def _diff_attn_head(ord_ref, q_ref, k_ref, v_ref, g_ref, pq_ref, pk_ref, plane_ref, sl_ref, lam_ref,
                    gs_ref, o_ref, vaug_ref, kaug_ref, acc_ref, qf_ref, s_ref, mpart_ref, macc_ref,
                    sd_ref, mpd_ref, *, tq, rs, lambda_init):
    seq = q_ref.shape[0]
    nq = seq // tq
    tr = tq // rs
    chains = [(c, r) for c in range(2) for r in range(rs)]
    lane = lax.broadcasted_iota(jnp.int32, (tq, LANES), 1)
    ones_col = jnp.ones((tq, LANES), BF16)
    slope2 = sl_ref[0, :, 0:1] * LOG2E
    c_pieces = _bf16_pieces(slope2)
    lane_row = lane[:1]
    c_query = jnp.where(lane_row < N_PAIR_LANES, _pick3(lane_row // N_PIECES, c_pieces), 0.0).astype(BF16)
    c_key = jnp.where(jnp.logical_and(lane_row >= N_PAIR_LANES, lane_row < 2 * N_PAIR_LANES),
                      _pick3((lane_row - N_PAIR_LANES) // N_PIECES, c_pieces), 0.0).astype(BF16)

    def bias_lanes(rows, key_side):
        pos_lanes = plane_ref[rows, :]
        if key_side:
            return jnp.where(lane < N_PAIR_LANES, pos_lanes, c_key)
        return jnp.where(lane < N_PAIR_LANES, c_query, pos_lanes)

    def stage_keys(kc, _):
        rows = pl.ds(pl.multiple_of(kc * tq, tq), tq)
        vaug_ref[rows, :DIFF_V] = v_ref[rows, :]
        vaug_ref[rows, DIFF_V:] = ones_col
        kaug_ref[rows, :2 * DIFF_QK] = k_ref[rows, :]
        kaug_ref[rows, 2 * DIFF_QK:] = bias_lanes(rows, True)
        return 0

    lax.fori_loop(0, nq, stage_keys, 0)

    def stage_queries(qt):
        rows = pl.ds(pl.multiple_of(qt * tq, tq), tq)
        q = q_ref[rows, :]
        q_side = bias_lanes(rows, False)
        qf_ref[0, :, :2 * DIFF_QK] = jnp.where(lane < DIFF_QK, q, 0).astype(BF16)
        qf_ref[1, :, :2 * DIFF_QK] = jnp.where(lane >= DIFF_QK, q, 0).astype(BF16)
        qf_ref[0, :, 2 * DIFF_QK:] = q_side
        qf_ref[1, :, 2 * DIFF_QK:] = q_side

    lq = lam_ref[...]
    lam = (jnp.exp(jnp.sum(lq[0:1] * lq[1:2], axis=-1, keepdims=True))
           - jnp.exp(jnp.sum(lq[2:3] * lq[3:4], axis=-1, keepdims=True)) + lambda_init)

    def scores(qt, kc, diag):
        k0 = pl.multiple_of(kc * tq, tq)
        k = k_ref[pl.ds(k0, tq), :]
        pk = pk_ref[0, pl.ds(kc, 1), :]
        out = [None] * len(chains)
        for r in range(rs):
            pq = pq_ref[pl.ds(pl.multiple_of(qt * tq + r * tr, tr), tr), :]
            bias = slope2 * jnp.abs(pq - pk)
            mask = _causal_mask(tr, tq, r * tr) if diag else None
            for c in range(2):
                q = qf_ref[c, r * tr:(r + 1) * tr, :2 * DIFF_QK]
                s = lax.dot_general(q, k, _NT, preferred_element_type=F32) - bias
                out[chains.index((c, r))] = jnp.where(mask, s, NEG) if diag else s
        return out

    def scores_diag(qt):
        stage_queries(qt)
        return scores(qt, qt, True)

    def scores_ordered(qt, kc):
        k = kaug_ref[pl.ds(pl.multiple_of(kc * tq, tq), tq), :]
        return [lax.dot_general(qf_ref[c, r * tr:(r + 1) * tr, :], k, _NT, preferred_element_type=F32)
                for c, r in chains]

    def scores_diag_sorted(qt):
        stage_queries(qt)
        k0 = pl.multiple_of(qt * tq, tq)
        out = []
        for c, r in chains:
            nk = (r + 1) * tr
            s = lax.dot_general(qf_ref[c, r * tr:(r + 1) * tr, :], kaug_ref[pl.ds(k0, nk), :], _NT,
                                preferred_element_type=F32)
            out.append(_pad_masked(jnp.where(_causal_mask(tr, nk, r * tr), s, NEG), tq))
        return out

    batch = pl.program_id(0)

    def trips_ordered(qt):
        return jnp.where((ord_ref[batch, qt] & 1) != 0, qt, 0)

    def tile_sorted(qt):
        return (ord_ref[batch, qt] >> 1) & 1

    def values(ci, kc):
        return vaug_ref[pl.ds(pl.multiple_of(kc * tq, tq), tq), :]

    def finalize(qt):
        for r in range(rs):
            rows = pl.ds(pl.multiple_of(qt * tq + r * tr, tr), tr)
            a1 = acc_ref[chains.index((0, r))]
            a2 = acc_ref[chains.index((1, r))]
            o = a1[:, :DIFF_V] / a1[:, DIFF_V:] - lam * (a2[:, :DIFF_V] / a2[:, DIFF_V:])
            ms_o = jnp.mean(o * o, axis=-1, keepdims=True)
            o = o * lax.rsqrt(ms_o + EPS) * gs_ref[...] * (1.0 - lambda_init)
            o_ref[rows, :] = (o * g_ref[rows, :].astype(F32)).astype(BF16)

    _flash_pipeline(len(chains), nq,
                    [(tile_sorted, scores_diag_sorted), (lambda qt: 1 - tile_sorted(qt), scores_diag)],
                    [(trips_ordered, scores_ordered),
                     (lambda qt: qt - trips_ordered(qt), lambda qt, kc: scores(qt, kc, False))],
                    values, finalize, s_ref, acc_ref, mpart_ref, macc_ref, sd_ref, mpd_ref)


def _diff_attn(rest, pos_col, pos_row, pos_lanes, ordered, slopes, lam_par, g_subln, lambda_init,
               nb, s):
    tq = ATTN_TQ
    nq = s // tq
    rs = DIFF_ROW_SPLIT
    tr = tq // rs
    hw = DIFF_HEADS_PER_STEP * DIFF_V
    return pl.pallas_call(
        functools.partial(_diff_attn_kernel, tq=tq, rs=rs, lambda_init=lambda_init),
        grid=(nb, DIFF_HEADS // DIFF_HEADS_PER_STEP),
        in_specs=[pl.BlockSpec(memory_space=pltpu.SMEM),
                  pl.BlockSpec((s, hw), lambda b, hd: (b, QD_OFF // hw + hd)),
                  pl.BlockSpec((s, hw), lambda b, hd: (b, KD_OFF // hw + hd)),
                  pl.BlockSpec((s, hw), lambda b, hd: (b, VD_OFF // hw + hd)),
                  pl.BlockSpec((s, hw), lambda b, hd: (b, GD_OFF // hw + hd)),
                  pl.BlockSpec((s, 1), lambda b, hd: (b, 0), pipeline_mode=pl.Buffered(1)),
                  pl.BlockSpec((1, nq, tq), lambda b, hd: (b, 0, 0)),
                  pl.BlockSpec((s, LANES), lambda b, hd: (b, 0), pipeline_mode=pl.Buffered(1)),
                  pl.BlockSpec((DIFF_HEADS_PER_STEP, 1, LANES), lambda b, hd: (hd, 0, 0)),
                  pl.BlockSpec((4, DIFF_QK), lambda b, hd: (0, 0)),
                  pl.BlockSpec((1, DIFF_V), lambda b, hd: (0, 0))],
        out_specs=pl.BlockSpec((s, hw), lambda b, hd: (b, hd)),
        out_shape=jax.ShapeDtypeStruct((nb * s, DIFF_HEADS * DIFF_V), BF16),
        scratch_shapes=[pltpu.VMEM((s, 2 * DIFF_V), BF16),
                        pltpu.VMEM((s, 4 * DIFF_QK), BF16),
                        pltpu.VMEM((2 * rs, tr, 2 * DIFF_V), F32),
                        pltpu.VMEM((2, tq, 4 * DIFF_QK), BF16),
                        pltpu.VMEM((2 * rs, tr, tq), F32),
                        pltpu.VMEM((2 * rs, tr, LANES), F32),
                        pltpu.VMEM((2 * rs, tr, LANES), F32),
                        pltpu.VMEM((2 * rs, tr, tq), F32),
                        pltpu.VMEM((2 * rs, tr, LANES), F32)],
        compiler_params=_cparams(("arbitrary", "arbitrary")),
        name="diff_attn",
    )(ordered, rest, rest, rest, rest, pos_col, pos_row, pos_lanes, slopes, lam_par, g_subln)


def _cast_kernel(w1_ref, w2_ref, w3_ref, o1_ref, o2_ref, o3_ref):
    o1_ref[...] = w1_ref[...].astype(BF16)
    o2_ref[...] = w2_ref[...].astype(BF16)
    o3_ref[...] = w3_ref[...].astype(BF16)


def _cast_weights(w1, w2, w3):
    steps = CAST_STEPS
    specs = [pl.BlockSpec((w.shape[0] // steps, w.shape[1]), lambda i: (i, 0)) for w in (w1, w2, w3)]
    return pl.pallas_call(
        _cast_kernel,
        grid=(steps,),
        in_specs=specs,
        out_specs=specs,
        out_shape=[jax.ShapeDtypeStruct(w.shape, BF16) for w in (w1, w2, w3)],
        compiler_params=_cparams(("arbitrary",)),
        name="cast_weights",
    )(w1, w2, w3)


def _merge_out_kernel(a1_ref, a2_ref, s1a_ref, s1b_ref, s2a_ref, s2b_ref, w1_ref, w2_ref, wo_ref,
                      x_ref, ada_ref, g_ref, o_ref):
    half = s1a_ref.shape[1]
    rb = a1_ref.shape[0] // EPILOGUE_ROW_SPLIT
    for t in range(EPILOGUE_ROW_SPLIT):
        rows = slice(t * rb, (t + 1) * rb)
        y1 = jnp.dot(a1_ref[rows, :], w1_ref[...], preferred_element_type=F32)
        y2 = jnp.dot(a2_ref[rows, :], w2_ref[...], preferred_element_type=F32)
        merged = jnp.concatenate(
            [s1a_ref[rows, :].astype(F32) * y1[:, :half] + s2a_ref[rows, :].astype(F32) * y2[:, :half],
             s1b_ref[rows, :].astype(F32) * y1[:, half:] + s2b_ref[rows, :].astype(F32) * y2[:, half:]],
            axis=1).astype(BF16)
        y = jnp.dot(merged, wo_ref[...], preferred_element_type=F32)
        ms = jnp.mean(y * y, axis=-1, keepdims=True)
        yn = y * lax.rsqrt(ms + EPS) * g_ref[...]
        o_ref[0, rows, :] = x_ref[0, rows, :] + ada_ref[0, 2:3, :] * yn


def _merge_out(og_mla, og_diff, rest, w1b, w2b, wob, x, ada3, g_post):
    nb, s, d = x.shape
    k_dim = og_mla.shape[1]
    tm = OUT_TM
    ns = s // tm
    half = d // 2
    row = lambda b, i: (b * ns + i, 0)
    gate = lambda blk: pl.BlockSpec((tm, half), lambda b, i: (b * ns + i, blk))
    resident = lambda shape: pl.BlockSpec(shape, lambda b, i: (0, 0), pipeline_mode=pl.Buffered(1))
    return pl.pallas_call(
        _merge_out_kernel,
        grid=(nb, ns),
        in_specs=[pl.BlockSpec((tm, k_dim), row),
                  pl.BlockSpec((tm, k_dim), row),
                  gate(MGM_OFF // half), gate(MGM_OFF // half + 1),
                  gate(MGD_OFF // half), gate(MGD_OFF // half + 1),
                  resident((k_dim, d)), resident((k_dim, d)), resident((d, d)),
                  pl.BlockSpec((1, tm, d), lambda b, i: (b, i, 0)),
                  pl.BlockSpec((1, 3, d), lambda b, i: (b, 0, 0)),
                  pl.BlockSpec((1, d), lambda b, i: (0, 0))],
        out_specs=pl.BlockSpec((1, tm, d), lambda b, i: (b, i, 0)),
        out_shape=jax.ShapeDtypeStruct((nb, s, d), F32),
        compiler_params=_cparams(("arbitrary", "arbitrary")),
        name="merge_out",
    )(og_mla, og_diff, rest, rest, rest, rest, w1b, w2b, wob, x, ada3, g_post)


def kernel(x, c, positions, w_ada, b_ada, g_pre, w_in, g_kv, w_ukv, lambda_q1, lambda_k1,
           lambda_q2, lambda_k2, g_subln, w_o_mla, w_o_diff, w_out, g_post):
    nb, s, d = x.shape
    depth = w_in.shape[0]
    assert w_in.shape[1:] == (d, REST_OFF + REST_W) and w_out.shape[1:] == (d, d)
    assert s % ATTN_TQ == 0 and s % MLA_TQ == 0 and (nb * s) % PROJ_TM == 0
    half = MLA_ROPE // 2
    inv = ROPE_THETA ** (-jnp.arange(half, dtype=F32) / half)
    inv_tab = jnp.tile(inv, LANES // half).reshape(1, LANES)
    slopes = 2.0 ** (-8.0 * jnp.arange(1, DIFF_HEADS + 1, dtype=F32) / DIFF_HEADS)
    slopes = jnp.broadcast_to(slopes.reshape(DIFF_HEADS, 1, 1), (DIFF_HEADS, 1, LANES))
    pos_col = positions.reshape(nb * s, 1)
    pos_colf = pos_col.astype(F32)
    pos_chunks = positions.reshape(nb, s // ATTN_TQ, ATTN_TQ)
    pos_row = pos_chunks.astype(F32)
    run_max = lax.cummax(pos_chunks.max(axis=-1), axis=1)
    prev_max = jnp.concatenate(
        [jnp.full((nb, 1), jnp.iinfo(jnp.int32).min, jnp.int32), run_max[:, :-1]], axis=1)
    tile_sorted = jnp.all(pos_chunks[..., 1:] >= pos_chunks[..., :-1], axis=-1)
    foldable = jnp.max(jnp.abs(pos_chunks), axis=(1, 2), keepdims=False)[:, None] < MAX_FOLD_POS
    earlier_ordered = prev_max <= pos_chunks.min(axis=-1)
    ordered = (jnp.logical_and(foldable, earlier_ordered).astype(jnp.int32)
               + 2 * jnp.logical_and(foldable, tile_sorted).astype(jnp.int32))

    for l in range(depth):
        ada3 = _ada(c, w_ada[l], b_ada[l]).reshape(nb, 3, d)
        h, cos_tab, sin_tab, pos_lanes = _norm(x, ada3, g_pre[l].reshape(1, d), pos_col, inv_tab)
        w_t = jnp.swapaxes(w_in[l], 0, 1)
        qm, rest = _proj(h, w_t, cos_tab, sin_tab)
        kk, vv = _kv(h, w_t, g_kv[l].reshape(1, KV_RANK), w_ukv[l], cos_tab, sin_tab)
        og_mla = _mla_attn(qm, kk, vv, rest, nb, s)
        lam_par = jnp.stack([lambda_q1[l], lambda_k1[l], lambda_q2[l], lambda_k2[l]]).astype(F32)
        og_diff = _diff_attn(rest, pos_colf, pos_row, pos_lanes, ordered, slopes, lam_par,
                             g_subln[l].reshape(1, DIFF_V), 0.8 - 0.6 * math.exp(-0.3 * l), nb, s)
        w1b, w2b, wob = _cast_weights(w_o_mla[l], w_o_diff[l], w_out[l])
        x = _merge_out(og_mla, og_diff, rest, w1b, w2b, wob, x, ada3, g_post[l].reshape(1, d))
    return x
```

```python
import functools
import math

import jax
import jax.numpy as jnp
from jax import lax
from jax.experimental import pallas as pl
from jax.experimental.pallas import tpu as pltpu

F32 = jnp.float32
BF16 = jnp.bfloat16

MLA_HEADS = 8
MLA_NOPE = 128
MLA_ROPE = 64
MLA_V = 128
KV_RANK = 512
ROPE_THETA = 10000.0
DIFF_HEADS = 8
DIFF_QK = 64
DIFF_V = 128
EPS = 1e-6
NEG = -1e30

LANES = 128
SUBLANES = 8
VMEM_LIMIT = 56 * 1024 * 1024
ADA_TN = 1024
NORM_TS = 512
MAX_FOLD_POS = 1 << 16
N_PIECES = 3
N_PAIR_LANES = N_PIECES * N_PIECES
ATTN_TQ = 1024
DIFF_ROW_SPLIT = 2
MLA_TQ = 1024
MLA_ROW_SPLIT = 2
PROJ_TM = 1024
REST_TN = 1024
QMLA_PAIRS_PER_TILE = 2
KV_TM = 1024
CAST_STEPS = 4
OUT_TM = 512
EPILOGUE_ROW_SPLIT = 2
LOG2E = math.log2(math.e)

Q_MLA_W = MLA_HEADS * (MLA_NOPE + MLA_ROPE)
KVR_W = KV_RANK + MLA_ROPE
KVR_PAD = 640
REST_OFF = Q_MLA_W + KVR_W
QD_OFF, KD_OFF, VD_OFF, GM_OFF, GD_OFF, MGM_OFF, MGD_OFF = 0, 1024, 2048, 3072, 4096, 5120, 7168
REST_W = 9216


def _cparams(sem):
    return pltpu.CompilerParams(dimension_semantics=sem, vmem_limit_bytes=VMEM_LIMIT)


def _ada_kernel(cb_ref, w_ref, b_ref, o_ref):
    k_dim, tn = w_ref.shape
    nb = cb_ref.shape[0]
    nchunk = tn // LANES

    def body(i, accs):
        k0 = pl.multiple_of(i * SUBLANES, SUBLANES)
        out = []
        for b in range(nb):
            cv = cb_ref[b, pl.ds(k0, SUBLANES), :]
            for j in range(nchunk):
                wv = w_ref[pl.ds(k0, SUBLANES), j * LANES:(j + 1) * LANES]
                out.append(accs[b * nchunk + j] + wv * cv)
        return tuple(out)

    init = tuple(jnp.zeros((SUBLANES, LANES), F32) for _ in range(nb * nchunk))
    accs = lax.fori_loop(0, k_dim // SUBLANES, body, init, unroll=8)
    for b in range(nb):
        row = jnp.concatenate(
            [jnp.sum(accs[b * nchunk + j], axis=0, keepdims=True) for j in range(nchunk)], axis=1)
        o_ref[b:b + 1, :] = row + b_ref[...]


def _ada(c, w, bias):
    nb, k_dim = c.shape
    n = w.shape[1]
    tn = ADA_TN
    cb = jnp.broadcast_to(c[:, :, None], (nb, k_dim, LANES))
    return pl.pallas_call(
        _ada_kernel,
        grid=(n // tn,),
        in_specs=[pl.BlockSpec((nb, k_dim, LANES), lambda j: (0, 0, 0)),
                  pl.BlockSpec((k_dim, tn), lambda j: (0, j)),
                  pl.BlockSpec((1, tn), lambda j: (0, j))],
        out_specs=pl.BlockSpec((nb, tn), lambda j: (0, j)),
        out_shape=jax.ShapeDtypeStruct((nb, n), F32),
        compiler_params=_cparams(("arbitrary",)),
        name="ada",
    )(cb, w, bias.reshape(1, n))


def _bf16_pieces(x):
    p1 = x.astype(BF16)
    r = x - p1.astype(F32)
    p2 = r.astype(BF16)
    return p1.astype(F32), p2.astype(F32), r - p2.astype(F32)


def _pick3(sel, x3):
    return jnp.where(sel == 0, x3[0], jnp.where(sel == 1, x3[1], x3[2]))


def _norm_kernel(x_ref, ada_ref, g_ref, pos_ref, inv_ref, h_ref, cos_ref, sin_ref, plane_ref):
    x = x_ref[0]
    ms = jnp.mean(x * x, axis=-1, keepdims=True)
    y = x * lax.rsqrt(ms + EPS) * g_ref[...]
    shift = ada_ref[0, 0:1, :]
    scale = ada_ref[0, 1:2, :]
    h_ref[...] = (y * (1.0 + scale) + shift).astype(BF16)
    ang = pos_ref[...].astype(F32) * inv_ref[...]
    lane = lax.broadcasted_iota(jnp.int32, ang.shape, 1)
    sign = jnp.where((lane % MLA_ROPE) < MLA_ROPE // 2, -1.0, 1.0).astype(F32)
    cos_ref[...] = jnp.cos(ang)
    sin_ref[...] = jnp.sin(ang) * sign
    pos3 = _bf16_pieces(pos_ref[...].astype(F32))
    piece = _pick3(lane % N_PIECES, pos3)
    plane_ref[...] = jnp.where(lane < N_PAIR_LANES, piece,
                               jnp.where(lane < 2 * N_PAIR_LANES, -piece, 0.0)).astype(BF16)


def _norm(x, ada3, g_pre, pos_col, inv_tab):
    nb, s, d = x.shape
    ts = NORM_TS
    ns = s // ts
    row = lambda b, i: (b * ns + i, 0)
    return pl.pallas_call(
        _norm_kernel,
        grid=(nb, ns),
        in_specs=[pl.BlockSpec((1, ts, d), lambda b, i: (b, i, 0)),
                  pl.BlockSpec((1, 3, d), lambda b, i: (b, 0, 0)),
                  pl.BlockSpec((1, d), lambda b, i: (0, 0)),
                  pl.BlockSpec((ts, 1), row),
                  pl.BlockSpec((1, LANES), lambda b, i: (0, 0))],
        out_specs=[pl.BlockSpec((ts, d), row),
                   pl.BlockSpec((ts, LANES), row),
                   pl.BlockSpec((ts, LANES), row),
                   pl.BlockSpec((ts, LANES), row)],
        out_shape=[jax.ShapeDtypeStruct((nb * s, d), BF16),
                   jax.ShapeDtypeStruct((nb * s, LANES), F32),
                   jax.ShapeDtypeStruct((nb * s, LANES), F32),
                   jax.ShapeDtypeStruct((nb * s, LANES), BF16)],
        compiler_params=_cparams(("arbitrary", "arbitrary")),
        name="prenorm",
    )(x, ada3, g_pre, pos_col, inv_tab)


def _rope_cols(r, cos, sin_signed):
    lane = lax.broadcasted_iota(jnp.int32, r.shape, 1)
    half = MLA_ROPE // 2
    partner = jnp.where((lane % MLA_ROPE) < half,
                        pltpu.roll(r, LANES - half, 1), pltpu.roll(r, half, 1))
    return r * cos + partner * sin_signed


def _qmla_kernel(a_ref, w_ref, cos_ref, sin_ref, o_ref, wb_ref, *, scale):
    hd = MLA_NOPE + MLA_ROPE
    pair_w = 2 * hd
    n_pairs = wb_ref.shape[0] // pair_w

    @pl.when(pl.program_id(1) == 0)
    def _():
        for p in range(n_pairs):
            src, dst = w_ref.at[p * pair_w:(p + 1) * pair_w], wb_ref.at[p * pair_w:(p + 1) * pair_w]
            dst[:MLA_NOPE] = src[:MLA_NOPE].astype(BF16)
            dst[MLA_NOPE:2 * MLA_NOPE] = src[hd:hd + MLA_NOPE].astype(BF16)
            dst[2 * MLA_NOPE:2 * MLA_NOPE + MLA_ROPE] = src[MLA_NOPE:hd].astype(BF16)
            dst[2 * MLA_NOPE + MLA_ROPE:] = src[hd + MLA_NOPE:].astype(BF16)

    acc = lax.dot_general(a_ref[...], wb_ref[...], _NT, preferred_element_type=F32)
    for p in range(n_pairs):
        c0 = p * pair_w
        rr = _rope_cols(acc[:, c0 + 2 * MLA_NOPE:c0 + pair_w], cos_ref[...], sin_ref[...])
        o_ref[:, c0:c0 + 2 * MLA_NOPE] = (acc[:, c0:c0 + 2 * MLA_NOPE] * scale).astype(BF16)
        o_ref[:, c0 + 2 * MLA_NOPE:c0 + pair_w] = (rr * scale).astype(BF16)


def _sigmoid(x):
    return 0.5 * jnp.tanh(0.5 * x) + 0.5


def _rest_kernel(a_ref, w_ref, o_ref, wb_ref, *, tn):
    j = pl.program_id(0)

    @pl.when(pl.program_id(1) == 0)
    def _():
        wb_ref[...] = w_ref[...].astype(BF16)

    def tile(epilogue):
        acc = lax.dot_general(a_ref[...], wb_ref[...], _NT, preferred_element_type=F32)
        o_ref[...] = epilogue(acc).astype(BF16)

    @pl.when(j < KD_OFF // tn)
    def _():
        tile(lambda acc: acc * (DIFF_QK ** -0.5 * LOG2E))

    @pl.when(jnp.logical_and(j >= KD_OFF // tn, j < GM_OFF // tn))
    def _():
        tile(lambda acc: acc)

    @pl.when(jnp.logical_and(j >= GM_OFF // tn, j < MGM_OFF // tn))
    def _():
        tile(lambda acc: acc * _sigmoid(acc))

    @pl.when(j >= MGM_OFF // tn)
    def _():
        tile(_sigmoid)


def _proj(h, w_t, cos_tab, sin_tab):
    m, k_dim = h.shape
    tm = PROJ_TM
    pair_w = 2 * (MLA_NOPE + MLA_ROPE)
    q_scale = (MLA_NOPE + MLA_ROPE) ** -0.5 * LOG2E

    def row_window(rows, offset_fn):
        return pl.BlockSpec((pl.Element(rows), pl.Element(k_dim)),
                            lambda *g: (pl.multiple_of(offset_fn(*g), SUBLANES), 0))

    qtn = QMLA_PAIRS_PER_TILE * pair_w
    qm = pl.pallas_call(
        functools.partial(_qmla_kernel, scale=q_scale),
        grid=(Q_MLA_W // qtn, m // tm),
        in_specs=[pl.BlockSpec((tm, k_dim), lambda j, i: (i, 0)),
                  pl.BlockSpec((qtn, k_dim), lambda j, i: (j, 0)),
                  pl.BlockSpec((tm, LANES), lambda j, i: (i, 0)),
                  pl.BlockSpec((tm, LANES), lambda j, i: (i, 0))],
        out_specs=pl.BlockSpec((tm, qtn), lambda j, i: (i, j)),
        out_shape=jax.ShapeDtypeStruct((m, Q_MLA_W), BF16),
        scratch_shapes=[pltpu.VMEM((qtn, k_dim), BF16)],
        compiler_params=_cparams(("arbitrary", "arbitrary")),
        name="proj_qmla",
    )(h, w_t, cos_tab, sin_tab)
    tn = REST_TN
    rest = pl.pallas_call(
        functools.partial(_rest_kernel, tn=tn),
        grid=(REST_W // tn, m // tm),
        in_specs=[pl.BlockSpec((tm, k_dim), lambda j, i: (i, 0)),
                  row_window(tn, lambda j, i: REST_OFF + tn * j)],
        out_specs=pl.BlockSpec((tm, tn), lambda j, i: (i, j)),
        out_shape=jax.ShapeDtypeStruct((m, REST_W), BF16),
        scratch_shapes=[pltpu.VMEM((tn, k_dim), BF16)],
        compiler_params=_cparams(("arbitrary", "arbitrary")),
        name="proj_rest",
    )(h, w_t)
    return qm, rest


def _kv_kernel(a_ref, wp_ref, g_ref, w_ref, cos_ref, sin_ref, k_ref, v_ref, wpb_ref, wb_ref):
    @pl.when(pl.program_id(0) == 0)
    def _():
        wpb_ref[:KVR_W] = wp_ref[...].astype(BF16)
        wpb_ref[KVR_W:] = jnp.zeros((KVR_PAD - KVR_W, wpb_ref.shape[1]), BF16)
        wb_ref[...] = w_ref[...].astype(BF16)

    kw = MLA_NOPE + MLA_V
    rb = a_ref.shape[0] // EPILOGUE_ROW_SPLIT
    for t in range(EPILOGUE_ROW_SPLIT):
        rows = slice(t * rb, (t + 1) * rb)
        p = lax.dot_general(a_ref[rows, :], wpb_ref[...], _NT, preferred_element_type=F32)
        ckv = p[:, :KV_RANK]
        ms = jnp.mean(ckv * ckv, axis=-1, keepdims=True)
        n = (ckv * lax.rsqrt(ms + EPS) * g_ref[...]).astype(BF16)
        kv = jnp.dot(n, wb_ref[...], preferred_element_type=F32)
        kr_even = _rope_cols(p[:, KV_RANK:], cos_ref[rows, :], sin_ref[rows, :])
        kr_odd = pltpu.roll(kr_even, MLA_ROPE, 1)
        ones_col = jnp.ones(kr_even.shape, BF16)
        for hd in range(MLA_HEADS):
            k_ref[rows, hd * kw:hd * kw + MLA_NOPE] = kv[:, hd * kw:hd * kw + MLA_NOPE].astype(BF16)
            k_ref[rows, hd * kw + MLA_NOPE:(hd + 1) * kw] = (
                kr_even if hd % 2 == 0 else kr_odd).astype(BF16)
            v_ref[rows, hd * kw:hd * kw + MLA_V] = kv[:, hd * kw + MLA_NOPE:(hd + 1) * kw].astype(BF16)
            v_ref[rows, hd * kw + MLA_V:(hd + 1) * kw] = ones_col


def _kv(h, w_t, g_kv, w_ukv, cos_tab, sin_tab):
    m, k_dim = h.shape
    tm = KV_TM
    kw = MLA_HEADS * (MLA_NOPE + MLA_V)
    return pl.pallas_call(
        _kv_kernel,
        grid=(m // tm,),
        in_specs=[pl.BlockSpec((tm, k_dim), lambda i: (i, 0)),
                  pl.BlockSpec((pl.Element(KVR_W), pl.Element(k_dim)),
                               lambda i: (pl.multiple_of(Q_MLA_W + 0 * i, SUBLANES), 0)),
                  pl.BlockSpec((1, KV_RANK), lambda i: (0, 0)),
                  pl.BlockSpec((KV_RANK, kw), lambda i: (0, 0)),
                  pl.BlockSpec((tm, LANES), lambda i: (i, 0)),
                  pl.BlockSpec((tm, LANES), lambda i: (i, 0))],
        out_specs=[pl.BlockSpec((tm, kw), lambda i: (i, 0)),
                   pl.BlockSpec((tm, kw), lambda i: (i, 0))],
        out_shape=[jax.ShapeDtypeStruct((m, kw), BF16),
                   jax.ShapeDtypeStruct((m, kw), BF16)],
        scratch_shapes=[pltpu.VMEM((KVR_PAD, k_dim), BF16), pltpu.VMEM((KV_RANK, kw), BF16)],
        compiler_params=_cparams(("arbitrary",)),
        name="kv_up",
    )(h, w_t, g_kv, w_ukv, cos_tab, sin_tab)


def _flash_pipeline(n_chains, nq, diags, loops, value_fn, finalize_fn,
                    s_ref, acc_ref, mpart_ref, macc_ref, sd_ref=None, mpd_ref=None):
    assert sd_ref is None or nq % 2 == 0
    chunk = s_ref.shape[2]

    def lane_tiles(x, n):
        return jnp.concatenate([x] * n, axis=1)

    def qk_phase(scores, stage):
        s_dst, m_dst = stage
        for ci, s in enumerate(scores):
            s_dst[ci] = s
            part = s[:, :LANES]
            for j in range(1, chunk // LANES):
                part = jnp.maximum(part, s[:, j * LANES:(j + 1) * LANES])
            m_dst[ci] = part

    def pv_phase(kc, stage):
        s_src, m_src = stage
        for ci in range(n_chains):
            m_acc = macc_ref[ci]
            m_run = jnp.maximum(m_acc, jnp.max(m_src[ci], axis=-1, keepdims=True))
            macc_ref[ci] = m_run
            p = jnp.exp2(s_src[ci] - lane_tiles(m_run, chunk // LANES))
            alpha = jnp.exp2(m_acc - m_run)
            pv = jnp.dot(p.astype(BF16), value_fn(ci, kc), preferred_element_type=F32)
            acc_ref[ci] = lane_tiles(alpha, acc_ref.shape[2] // LANES) * acc_ref[ci] + pv

    def reset():
        acc_ref[...] = jnp.zeros_like(acc_ref)
        macc_ref[...] = jnp.full(macc_ref.shape, NEG, F32)

    def tile(qt, stage, next_stage):
        cur = qt
        for trips_fn, score_fn in loops:
            def step(kc, cur, score_fn=score_fn):
                pv_phase(cur, stage)
                qk_phase(score_fn(qt, kc), stage)
                return kc

            cur = lax.fori_loop(0, trips_fn(qt), step, cur)
        nxt = jnp.minimum(qt + 1, nq - 1)

        def transition(_, cur, fn):
            if next_stage is stage:
                pv_phase(cur, stage)
                finalize_fn(qt)
                reset()
                qk_phase(fn(nxt), stage)
            else:
                qk_phase(fn(nxt), next_stage)
                pv_phase(cur, stage)
                finalize_fn(qt)
                reset()
            return cur

        for select_fn, fn in diags:
            if select_fn is None:
                transition(0, cur, fn)
            else:
                lax.fori_loop(0, select_fn(nxt), functools.partial(transition, fn=fn), cur)

    stage_a = (s_ref, mpart_ref)
    reset()
    for select_fn, fn in diags:
        if select_fn is None:
            qk_phase(fn(0), stage_a)
        else:
            lax.fori_loop(0, select_fn(0), lambda _, c, fn=fn: qk_phase(fn(0), stage_a) or c, 0)
    if sd_ref is None:
        lax.fori_loop(0, nq, lambda qt, c: tile(qt, stage_a, stage_a) or c, 0)
    else:
        stage_b = (sd_ref, mpd_ref)

        def tile_pair(j, c):
            tile(2 * j, stage_a, stage_b)
            tile(2 * j + 1, stage_b, stage_a)
            return c

        lax.fori_loop(0, nq // 2, tile_pair, 0)


def _causal_mask(rows, cols, row0):
    row = lax.broadcasted_iota(jnp.int32, (rows, cols), 0) + row0
    col = lax.broadcasted_iota(jnp.int32, (rows, cols), 1)
    return col <= row


def _pad_masked(s, cols):
    if s.shape[1] == cols:
        return s
    return jnp.concatenate([s, jnp.full((s.shape[0], cols - s.shape[1]), NEG, s.dtype)], axis=1)


_NT = (((1,), (1,)), ((), ()))


def _mla_attn_kernel(q_ref, k_ref, v_ref, g_ref, o_ref, acc_ref, qs_ref, s_ref, mpart_ref, macc_ref,
                     *, tq, rs):
    kw = 2 * MLA_NOPE
    tr = tq // rs
    nq = q_ref.shape[0] // tq
    chains = [(hp, r) for hp in range(2) for r in range(rs)]

    def stage_queries(qt):
        rows = pl.ds(pl.multiple_of(qt * tq, tq), tq)
        for hp in range(2):
            qs_ref[hp, :, :MLA_NOPE] = q_ref[rows, hp * MLA_NOPE:(hp + 1) * MLA_NOPE]
            qs_ref[hp, :, MLA_NOPE:] = q_ref[rows, 2 * MLA_NOPE:]

    def scores(qt, kc, diag):
        if diag:
            stage_queries(qt)
        k0 = pl.multiple_of(kc * tq, tq)
        out = []
        for hp, r in chains:
            nk = (r + 1) * tr if diag else tq
            q = qs_ref[hp, r * tr:(r + 1) * tr, :]
            k = k_ref[pl.ds(k0, nk), hp * kw:(hp + 1) * kw]
            s = lax.dot_general(q, k, _NT, preferred_element_type=F32)
            if diag:
                s = _pad_masked(jnp.where(_causal_mask(tr, nk, r * tr), s, NEG), tq)
            out.append(s)
        return out

    def values(ci, kc):
        hp = chains[ci][0]
        return v_ref[pl.ds(pl.multiple_of(kc * tq, tq), tq), hp * kw:(hp + 1) * kw]

    def finalize(qt):
        for ci, (hp, r) in enumerate(chains):
            rows = pl.ds(pl.multiple_of(qt * tq + r * tr, tr), tr)
            acc = acc_ref[ci]
            o = acc[:, :MLA_V] / acc[:, MLA_V:]
            gate = g_ref[rows, hp * MLA_V:(hp + 1) * MLA_V].astype(F32)
            o_ref[rows, hp * MLA_V:(hp + 1) * MLA_V] = (o * gate).astype(BF16)

    _flash_pipeline(len(chains), nq, [(None, lambda qt: scores(qt, qt, True))],
                    [(lambda qt: qt, lambda qt, kc: scores(qt, kc, False))],
                    values, finalize, s_ref, acc_ref, mpart_ref, macc_ref)


def _mla_attn(qm, kk, vv, rest, nb, s):
    tq = MLA_TQ
    pair_w = 2 * (MLA_NOPE + MLA_ROPE)
    kw = 4 * MLA_NOPE
    gate_blk = GM_OFF // (2 * MLA_V)
    rs = MLA_ROW_SPLIT
    return pl.pallas_call(
        functools.partial(_mla_attn_kernel, tq=tq, rs=rs),
        grid=(nb, MLA_HEADS // 2),
        in_specs=[pl.BlockSpec((s, pair_w), lambda b, hh: (b, hh)),
                  pl.BlockSpec((s, kw), lambda b, hh: (b, hh)),
                  pl.BlockSpec((s, kw), lambda b, hh: (b, hh)),
                  pl.BlockSpec((s, 2 * MLA_V), lambda b, hh: (b, gate_blk + hh))],
        out_specs=pl.BlockSpec((s, 2 * MLA_V), lambda b, hh: (b, hh)),
        out_shape=jax.ShapeDtypeStruct((nb * s, MLA_HEADS * MLA_V), BF16),
        scratch_shapes=[pltpu.VMEM((2 * rs, tq // rs, 2 * MLA_V), F32),
                        pltpu.VMEM((2, tq, 2 * MLA_NOPE), BF16),
                        pltpu.VMEM((2 * rs, tq // rs, tq), F32),
                        pltpu.VMEM((2 * rs, tq // rs, LANES), F32),
                        pltpu.VMEM((2 * rs, tq // rs, LANES), F32)],
        compiler_params=_cparams(("arbitrary", "arbitrary")),
        name="mla_attn",
    )(qm, kk, vv, rest)


def _diff_attn_kernel(ord_ref, q_ref, k_ref, v_ref, g_ref, pq_ref, pk_ref, plane_ref, sl_ref, lam_ref,
                      gs_ref, o_ref, vaug_ref, kaug_ref, acc_ref, qf_ref, s_ref, mpart_ref, macc_ref,
                      sd_ref, mpd_ref, *, tq, rs, lambda_init):
    seq = q_ref.shape[0]
    nq = seq // tq
    tr = tq // rs
    chains = [(c, r) for c in range(2) for r in range(rs)]
    lane = lax.broadcasted_iota(jnp.int32, (tq, LANES), 1)
    ones_col = jnp.ones((tq, LANES), BF16)
    slope2 = sl_ref[0, :, 0:1] * LOG2E
    c_pieces = _bf16_pieces(slope2)
    lane_row = lane[:1]
    c_query = jnp.where(lane_row < N_PAIR_LANES, _pick3(lane_row // N_PIECES, c_pieces), 0.0).astype(BF16)
    c_key = jnp.where(jnp.logical_and(lane_row >= N_PAIR_LANES, lane_row < 2 * N_PAIR_LANES),
                      _pick3((lane_row - N_PAIR_LANES) // N_PIECES, c_pieces), 0.0).astype(BF16)

    def bias_lanes(rows, key_side):
        pos_lanes = plane_ref[rows, :]
        if key_side:
            return jnp.where(lane < N_PAIR_LANES, pos_lanes, c_key)
        return jnp.where(lane < N_PAIR_LANES, c_query, pos_lanes)

    def stage_keys(kc, _):
        rows = pl.ds(pl.multiple_of(kc * tq, tq), tq)
        vaug_ref[rows, :DIFF_V] = v_ref[rows, :]
        vaug_ref[rows, DIFF_V:] = ones_col
        kaug_ref[rows, :2 * DIFF_QK] = k_ref[rows, :]
        kaug_ref[rows, 2 * DIFF_QK:] = bias_lanes(rows, True)
        return 0

    lax.fori_loop(0, nq, stage_keys, 0)

    def stage_queries(qt):
        rows = pl.ds(pl.multiple_of(qt * tq, tq), tq)
        q = q_ref[rows, :]
        q_side = bias_lanes(rows, False)
        qf_ref[0, :, :2 * DIFF_QK] = jnp.where(lane < DIFF_QK, q, 0).astype(BF16)
        qf_ref[1, :, :2 * DIFF_QK] = jnp.where(lane >= DIFF_QK, q, 0).astype(BF16)
        qf_ref[0, :, 2 * DIFF_QK:] = q_side
        qf_ref[1, :, 2 * DIFF_QK:] = q_side

    lq = lam_ref[...]
    lam = (jnp.exp(jnp.sum(lq[0:1] * lq[1:2], axis=-1, keepdims=True))
           - jnp.exp(jnp.sum(lq[2:3] * lq[3:4], axis=-1, keepdims=True)) + lambda_init)

    def scores(qt, kc, diag):
        k0 = pl.multiple_of(kc * tq, tq)
        k = k_ref[pl.ds(k0, tq), :]
        pk = pk_ref[0, pl.ds(kc, 1), :]
        out = [None] * len(chains)
        for r in range(rs):
            pq = pq_ref[pl.ds(pl.multiple_of(qt * tq + r * tr, tr), tr), :]
            bias = slope2 * jnp.abs(pq - pk)
            mask = _causal_mask(tr, tq, r * tr) if diag else None
            for c in range(2):
                q = qf_ref[c, r * tr:(r + 1) * tr, :2 * DIFF_QK]
                s = lax.dot_general(q, k, _NT, preferred_element_type=F32) - bias
                out[chains.index((c, r))] = jnp.where(mask, s, NEG) if diag else s
        return out

    def scores_diag(qt):
        stage_queries(qt)
        return scores(qt, qt, True)

    def scores_ordered(qt, kc):
        k = kaug_ref[pl.ds(pl.multiple_of(kc * tq, tq), tq), :]
        return [lax.dot_general(qf_ref[c, r * tr:(r + 1) * tr, :], k, _NT, preferred_element_type=F32)
                for c, r in chains]

    def scores_diag_sorted(qt):
        stage_queries(qt)
        k0 = pl.multiple_of(qt * tq, tq)
        out = []
        for c, r in chains:
            nk = (r + 1) * tr
            s = lax.dot_general(qf_ref[c, r * tr:(r + 1) * tr, :], kaug_ref[pl.ds(k0, nk), :], _NT,
                                preferred_element_type=F32)
            out.append(_pad_masked(jnp.where(_causal_mask(tr, nk, r * tr), s, NEG), tq))
        return out

    batch = pl.program_id(0)

    def trips_ordered(qt):
        return jnp.where((ord_ref[batch, qt] & 1) != 0, qt, 0)

    def tile_sorted(qt):
        return (ord_ref[batch, qt] >> 1) & 1

    def values(ci, kc):
        return vaug_ref[pl.ds(pl.multiple_of(kc * tq, tq), tq), :]

    def finalize(qt):
        for r in range(rs):
            rows = pl.ds(pl.multiple_of(qt * tq + r * tr, tr), tr)
            a1 = acc_ref[chains.index((0, r))]
            a2 = acc_ref[chains.index((1, r))]
            o = a1[:, :DIFF_V] / a1[:, DIFF_V:] - lam * (a2[:, :DIFF_V] / a2[:, DIFF_V:])
            ms_o = jnp.mean(o * o, axis=-1, keepdims=True)
            o = o * lax.rsqrt(ms_o + EPS) * gs_ref[...] * (1.0 - lambda_init)
            o_ref[rows, :] = (o * g_ref[rows, :].astype(F32)).astype(BF16)

    _flash_pipeline(len(chains), nq,
                    [(tile_sorted, scores_diag_sorted), (lambda qt: 1 - tile_sorted(qt), scores_diag)],
                    [(trips_ordered, scores_ordered),
                     (lambda qt: qt - trips_ordered(qt), lambda qt, kc: scores(qt, kc, False))],
                    values, finalize, s_ref, acc_ref, mpart_ref, macc_ref, sd_ref, mpd_ref)


def _diff_attn(rest, pos_col, pos_row, pos_lanes, ordered, slopes, lam_par, g_subln, lambda_init,
               nb, s):
    tq = ATTN_TQ
    nq = s // tq
    rs = DIFF_ROW_SPLIT
    tr = tq // rs
    hw = DIFF_V
    return pl.pallas_call(
        functools.partial(_diff_attn_kernel, tq=tq, rs=rs, lambda_init=lambda_init),
        grid=(nb, DIFF_HEADS),
        in_specs=[pl.BlockSpec(memory_space=pltpu.SMEM),
                  pl.BlockSpec((s, hw), lambda b, hd: (b, QD_OFF // hw + hd)),
                  pl.BlockSpec((s, hw), lambda b, hd: (b, KD_OFF // hw + hd)),
                  pl.BlockSpec((s, hw), lambda b, hd: (b, VD_OFF // hw + hd)),
                  pl.BlockSpec((s, hw), lambda b, hd: (b, GD_OFF // hw + hd)),
                  pl.BlockSpec((s, 1), lambda b, hd: (b, 0), pipeline_mode=pl.Buffered(1)),
                  pl.BlockSpec((1, nq, tq), lambda b, hd: (b, 0, 0)),
                  pl.BlockSpec((s, LANES), lambda b, hd: (b, 0), pipeline_mode=pl.Buffered(1)),
                  pl.BlockSpec((1, 1, LANES), lambda b, hd: (hd, 0, 0)),
                  pl.BlockSpec((4, DIFF_QK), lambda b, hd: (0, 0)),
                  pl.BlockSpec((1, DIFF_V), lambda b, hd: (0, 0))],
        out_specs=pl.BlockSpec((s, hw), lambda b, hd: (b, hd)),
        out_shape=jax.ShapeDtypeStruct((nb * s, DIFF_HEADS * DIFF_V), BF16),
        scratch_shapes=[pltpu.VMEM((s, 2 * DIFF_V), BF16),
                        pltpu.VMEM((s, 4 * DIFF_QK), BF16),
                        pltpu.VMEM((2 * rs, tr, 2 * DIFF_V), F32),
                        pltpu.VMEM((2, tq, 4 * DIFF_QK), BF16),
                        pltpu.VMEM((2 * rs, tr, tq), F32),
                        pltpu.VMEM((2 * rs, tr, LANES), F32),
                        pltpu.VMEM((2 * rs, tr, LANES), F32),
                        pltpu.VMEM((2 * rs, tr, tq), F32),
                        pltpu.VMEM((2 * rs, tr, LANES), F32)],
        compiler_params=_cparams(("arbitrary", "arbitrary")),
        name="diff_attn",
    )(ordered, rest, rest, rest, rest, pos_col, pos_row, pos_lanes, slopes, lam_par, g_subln)


def _cast_kernel(w1_ref, w2_ref, w3_ref, o1_ref, o2_ref, o3_ref):
    o1_ref[...] = w1_ref[...].astype(BF16)
    o2_ref[...] = w2_ref[...].astype(BF16)
    o3_ref[...] = w3_ref[...].astype(BF16)


def _cast_weights(w1, w2, w3):
    steps = CAST_STEPS
    specs = [pl.BlockSpec((w.shape[0] // steps, w.shape[1]), lambda i: (i, 0)) for w in (w1, w2, w3)]
    return pl.pallas_call(
        _cast_kernel,
        grid=(steps,),
        in_specs=specs,
        out_specs=specs,
        out_shape=[jax.ShapeDtypeStruct(w.shape, BF16) for w in (w1, w2, w3)],
        compiler_params=_cparams(("arbitrary",)),
        name="cast_weights",
    )(w1, w2, w3)


def _merge_out_kernel(a1_ref, a2_ref, s1a_ref, s1b_ref, s2a_ref, s2b_ref, w1_ref, w2_ref, wo_ref,
                      x_ref, ada_ref, g_ref, o_ref):
    half = s1a_ref.shape[1]
    rb = a1_ref.shape[0] // EPILOGUE_ROW_SPLIT
    for t in range(EPILOGUE_ROW_SPLIT):
        rows = slice(t * rb, (t + 1) * rb)
        y1 = jnp.dot(a1_ref[rows, :], w1_ref[...], preferred_element_type=F32)
        y2 = jnp.dot(a2_ref[rows, :], w2_ref[...], preferred_element_type=F32)
        merged = jnp.concatenate(
            [s1a_ref[rows, :].astype(F32) * y1[:, :half] + s2a_ref[rows, :].astype(F32) * y2[:, :half],
             s1b_ref[rows, :].astype(F32) * y1[:, half:] + s2b_ref[rows, :].astype(F32) * y2[:, half:]],
            axis=1).astype(BF16)
        y = jnp.dot(merged, wo_ref[...], preferred_element_type=F32)
        ms = jnp.mean(y * y, axis=-1, keepdims=True)
        yn = y * lax.rsqrt(ms + EPS) * g_ref[...]
        o_ref[0, rows, :] = x_ref[0, rows, :] + ada_ref[0, 2:3, :] * yn


def _merge_out(og_mla, og_diff, rest, w1b, w2b, wob, x, ada3, g_post):
    nb, s, d = x.shape
    k_dim = og_mla.shape[1]
    tm = OUT_TM
    ns = s // tm
    half = d // 2
    row = lambda b, i: (b * ns + i, 0)
    gate = lambda blk: pl.BlockSpec((tm, half), lambda b, i: (b * ns + i, blk))
    resident = lambda shape: pl.BlockSpec(shape, lambda b, i: (0, 0), pipeline_mode=pl.Buffered(1))
    return pl.pallas_call(
        _merge_out_kernel,
        grid=(nb, ns),
        in_specs=[pl.BlockSpec((tm, k_dim), row),
                  pl.BlockSpec((tm, k_dim), row),
                  gate(MGM_OFF // half), gate(MGM_OFF // half + 1),
                  gate(MGD_OFF // half), gate(MGD_OFF // half + 1),
                  resident((k_dim, d)), resident((k_dim, d)), resident((d, d)),
                  pl.BlockSpec((1, tm, d), lambda b, i: (b, i, 0)),
                  pl.BlockSpec((1, 3, d), lambda b, i: (b, 0, 0)),
                  pl.BlockSpec((1, d), lambda b, i: (0, 0))],
        out_specs=pl.BlockSpec((1, tm, d), lambda b, i: (b, i, 0)),
        out_shape=jax.ShapeDtypeStruct((nb, s, d), F32),
        compiler_params=_cparams(("arbitrary", "arbitrary")),
        name="merge_out",
    )(og_mla, og_diff, rest, rest, rest, rest, w1b, w2b, wob, x, ada3, g_post)


def kernel(x, c, positions, w_ada, b_ada, g_pre, w_in, g_kv, w_ukv, lambda_q1, lambda_k1,
           lambda_q2, lambda_k2, g_subln, w_o_mla, w_o_diff, w_out, g_post):
    nb, s, d = x.shape
    depth = w_in.shape[0]
    assert w_in.shape[1:] == (d, REST_OFF + REST_W) and w_out.shape[1:] == (d, d)
    assert s % ATTN_TQ == 0 and s % MLA_TQ == 0 and (nb * s) % PROJ_TM == 0
    half = MLA_ROPE // 2
    inv = ROPE_THETA ** (-jnp.arange(half, dtype=F32) / half)
    inv_tab = jnp.tile(inv, LANES // half).reshape(1, LANES)
    slopes = 2.0 ** (-8.0 * jnp.arange(1, DIFF_HEADS + 1, dtype=F32) / DIFF_HEADS)
    slopes = jnp.broadcast_to(slopes.reshape(DIFF_HEADS, 1, 1), (DIFF_HEADS, 1, LANES))
    pos_col = positions.reshape(nb * s, 1)
    pos_colf = pos_col.astype(F32)
    pos_chunks = positions.reshape(nb, s // ATTN_TQ, ATTN_TQ)
    pos_row = pos_chunks.astype(F32)
    run_max = lax.cummax(pos_chunks.max(axis=-1), axis=1)
    prev_max = jnp.concatenate(
        [jnp.full((nb, 1), jnp.iinfo(jnp.int32).min, jnp.int32), run_max[:, :-1]], axis=1)
    tile_sorted = jnp.all(pos_chunks[..., 1:] >= pos_chunks[..., :-1], axis=-1)
    foldable = jnp.max(jnp.abs(pos_chunks), axis=(1, 2), keepdims=False)[:, None] < MAX_FOLD_POS
    earlier_ordered = prev_max <= pos_chunks.min(axis=-1)
    ordered = (jnp.logical_and(foldable, earlier_ordered).astype(jnp.int32)
               + 2 * jnp.logical_and(foldable, tile_sorted).astype(jnp.int32))

    for l in range(depth):
        ada3 = _ada(c, w_ada[l], b_ada[l]).reshape(nb, 3, d)
        h, cos_tab, sin_tab, pos_lanes = _norm(x, ada3, g_pre[l].reshape(1, d), pos_col, inv_tab)
        w_t = jnp.swapaxes(w_in[l], 0, 1)
        qm, rest = _proj(h, w_t, cos_tab, sin_tab)
        kk, vv = _kv(h, w_t, g_kv[l].reshape(1, KV_RANK), w_ukv[l], cos_tab, sin_tab)
        og_mla = _mla_attn(qm, kk, vv, rest, nb, s)
        lam_par = jnp.stack([lambda_q1[l], lambda_k1[l], lambda_q2[l], lambda_k2[l]]).astype(F32)
        og_diff = _diff_attn(rest, pos_colf, pos_row, pos_lanes, ordered, slopes, lam_par,
                             g_subln[l].reshape(1, DIFF_V), 0.8 - 0.6 * math.exp(-0.3 * l), nb, s)
        w1b, w2b, wob = _cast_weights(w_o_mla[l], w_o_diff[l], w_out[l])
        x = _merge_out(og_mla, og_diff, rest, w1b, w2b, wob, x, ada3, g_post[l].reshape(1, d))
    return x
```
